```python
import jax
import jax.numpy as jnp
from jax import lax
import numpy as np

D_MODEL = 2048
BATCH = 1
SEQ = 8192
DEPTH = 2

GRID_W = 64
CTX_LEN = 256
N_BRANCH = 4
MIX_W = D_MODEL // 4
RW_HEAD = 64
RW_HEADS = MIX_W // RW_HEAD
RW_DECAY_RANK = 64
RW_ICLR_RANK = 64
RW_GATE_RANK = 128
RW_GN_EPS = 64e-5
GLA_HEADS = 4
GLA_DV = MIX_W // GLA_HEADS
GLA_DK = GLA_DV // 2
GLA_GATE_RANK = 16
GLA_TAU = 16.0
GLA_CHUNK = 64
GN_EPS = 1e-5
ROPE_BASE = 10000.0
SGU_GROUPS = MIX_W // 64
SGU_CHUNK = 128
NA_HEAD = 64
NA_HEADS = MIX_W // NA_HEAD
NA_WIN_H = 8
NA_WIN_W = 16
FFN_DIM = 7 * D_MODEL // 2
N_EXPERTS = 8
TOP_K = 2
N_DENSE = (DEPTH + 1) // 2
N_MOE = DEPTH // 2
EPS = 1e-6
NEG_INF = -1e30
IN_SPLITS = (
    ('rw_r', MIX_W), ('rw_k', MIX_W), ('rw_v', MIX_W),
    ('rw_wd', 2 * RW_DECAY_RANK), ('rw_ad', 2 * RW_ICLR_RANK), ('rw_gd', RW_GATE_RANK),
    ('gla_q', GLA_HEADS * GLA_DK), ('gla_k', GLA_HEADS * GLA_DK), ('gla_v', MIX_W),
    ('gla_r', MIX_W), ('gla_ad', 2 * GLA_GATE_RANK),
    ('sgu_u', MIX_W), ('sgu_v', MIX_W),
    ('na_q', MIX_W), ('na_k', MIX_W), ('na_v', MIX_W),
    ('gate', N_BRANCH * D_MODEL),
)
N_IN = sum(w for _, w in IN_SPLITS)

kernel_name = 'hybrid_diffusion_prefix_block'


def identity(z):
    return z


def flip_time(z):
    return z[:, ::-1]


def rms_norm(x, g):
    xf = x.astype(jnp.float32)
    y = xf * lax.rsqrt(jnp.mean(xf * xf, axis=-1, keepdims=True) + EPS)
    return (y * g.astype(jnp.float32)).astype(x.dtype)


def modulate(h, shift, scale):
    return h * (1 + scale) + shift


def adaln(cond, ada_w, ada_b):
    m = jax.nn.silu(cond) @ ada_w + ada_b
    return jnp.split(m[..., None, :], 6, axis=-1)


def split_proj(p):
    out, off = {}, 0
    for name, width in IN_SPLITS:
        out[name] = p[..., off:off + width]
        off += width
    return out


def head_norm(y, g, b, eps):
    mu = jnp.mean(y, axis=-1, keepdims=True)
    var = jnp.mean(jnp.square(y - mu), axis=-1, keepdims=True)
    yn = ((y - mu) * lax.rsqrt(var + eps)).reshape(y.shape[:-2] + (-1,)) * g.astype(jnp.float32)
    return yn if b is None else yn + b.astype(jnp.float32)


def token_shift(z, mu):
    prev = jnp.pad(z, ((0, 0), (1, 0), (0, 0)))[:, :-1]
    nxt = jnp.pad(z, ((0, 0), (0, 1), (0, 0)))[:, 1:]
    return z + mu[0] * (prev - z) + mu[1] * (nxt - z)


def axial_rope(x):
    B, T, H, dk = x.shape
    t = jnp.arange(T)
    pos = jnp.stack([t // GRID_W, t % GRID_W], axis=-1).astype(jnp.float32)
    nf = dk // 4
    inv = ROPE_BASE ** (-jnp.arange(nf, dtype=jnp.float32) / nf)
    ang = pos[:, :, None] * inv
    cos = jnp.cos(ang)[None, :, None]
    sin = jnp.sin(ang)[None, :, None]
    xr = x.reshape(B, T, H, 2, nf, 2)
    x1, x2 = xr[..., 0], xr[..., 1]
    out = jnp.stack([x1 * cos - x2 * sin, x1 * sin + x2 * cos], axis=-1)
    return out.reshape(B, T, H, dk)


def rwkv_token_inputs(p, mu, w0, w_up, a0, a_up, k_k, k_a):
    f32 = jnp.float32
    B, T, _ = p['rw_r'].shape
    heads = lambda z: z.reshape(z.shape[:-1] + (RW_HEADS, RW_HEAD))
    r = token_shift(p['rw_r'], mu[0]).astype(f32)
    k = token_shift(p['rw_k'], mu[1]).astype(f32)
    v = token_shift(p['rw_v'], mu[2]).astype(f32)
    wd = jnp.tanh(p['rw_wd'].astype(f32)).reshape(B, T, 2, RW_DECAY_RANK)
    ad = p['rw_ad'].astype(f32).reshape(B, T, 2, RW_ICLR_RANK)
    w_log = -jax.nn.softplus(-(w0 + jnp.einsum('btdr,drc->btdc', wd, w_up))) - 0.5
    decay = jnp.exp(-jnp.exp(w_log))
    a = jax.nn.sigmoid(a0 + jnp.einsum('btdr,drc->btdc', ad, a_up))
    kk = heads(k * k_k)
    kk = kk * lax.rsqrt(jnp.maximum(jnp.sum(kk * kk, axis=-1, keepdims=True), 1e-12))
    k_dir = k[:, :, None, :] * (1.0 + (a - 1.0) * k_a)
    return heads(r), heads(decay), heads(k_dir), heads(v), kk, heads(a)


def rwkv7_scan(r, w, k, v, kk, a, s0):
    def step(s, inp):
        r_t, w_t, k_t, v_t, kk_t, a_t = inp
        sa = jnp.einsum('bhij,bhj->bhi', s, -kk_t)
        s = (s * w_t[:, :, None, :] + sa[..., None] * (kk_t * a_t)[:, :, None, :]
             + v_t[..., None] * k_t[:, :, None, :])
        return s, jnp.einsum('bhij,bhj->bhi', s, r_t)
    xs = tuple(jnp.moveaxis(z, 1, 0) for z in (r, w, k, v, kk, a))
    s_fin, ys = lax.scan(step, s0, xs)
    return jnp.moveaxis(ys, 0, 1), s_fin


def rwkv_select(t, d):
    r, decay, k_dir, v, kk, a = t
    return (r, decay[:, :, d], k_dir[:, :, d], v, kk, a[:, :, d])


def rwkv_branch(pc, pl, mu, w0, w_up, a0, a_up, g_up, k_k, k_a, r_k, ln_g, ln_b, need_ctx):
    tc = rwkv_token_inputs(pc, mu, w0, w_up, a0, a_up, k_k, k_a)
    tl = rwkv_token_inputs(pl, mu, w0, w_up, a0, a_up, k_k, k_a)
    B = pl['rw_r'].shape[0]
    ys_c, ys_l = [], []
    for d in range(2):
        fl = flip_time if d else identity
        s0 = jnp.zeros((B, RW_HEADS, RW_HEAD, RW_HEAD), jnp.float32)
        y_c, s_c = rwkv7_scan(*[fl(z) for z in rwkv_select(tc, d)], s0)
        y_l, _ = rwkv7_scan(*[fl(z) for z in rwkv_select(tl, d)], s_c)
        ys_c.append(fl(y_c))
        ys_l.append(fl(y_l))

    def finish(t, ys, p):
        r, _, k_dir, v, _, _ = t
        bonus = (jnp.sum(r * k_dir[:, :, 0] * r_k, axis=-1, keepdims=True) * v
                 + jnp.sum(r * k_dir[:, :, 1] * r_k, axis=-1, keepdims=True) * v)
        y = head_norm(ys[0] + ys[1], ln_g, ln_b, RW_GN_EPS) + bonus.reshape(bonus.shape[:2] + (MIX_W,))
        g = jax.nn.sigmoid(p['rw_gd'].astype(jnp.float32)) @ g_up
        return (y * g).astype(p['rw_r'].dtype)

    return (finish(tc, ys_c, pc) if need_ctx else None), finish(tl, ys_l, pl)


def gla_token_inputs(p, a_up, a_b, rotate):
    f32 = jnp.float32
    B, T, _ = p['gla_q'].shape
    q = p['gla_q'].astype(f32).reshape(B, T, GLA_HEADS, GLA_DK) * GLA_DK ** -0.5
    k = p['gla_k'].astype(f32).reshape(B, T, GLA_HEADS, GLA_DK)
    if rotate:
        q, k = axial_rope(q), axial_rope(k)
    v = p['gla_v'].astype(f32).reshape(B, T, GLA_HEADS, GLA_DV)
    ad = p['gla_ad'].astype(f32).reshape(B, T, 2, GLA_GATE_RANK)
    g = jax.nn.log_sigmoid(jnp.einsum('btdr,drc->btdc', ad, a_up) + a_b) / GLA_TAU
    return q, k, v, g.reshape(B, T, 2, GLA_HEADS, GLA_DK)


def gla_chunked(q, k, v, g, s0):
    B, T, H, _ = q.shape
    dv = v.shape[-1]
    n = T // GLA_CHUNK
    q, k, v, g = (z.reshape(B, n, GLA_CHUNK, H, z.shape[-1]) for z in (q, k, v, g))
    b = jnp.cumsum(g, axis=2)
    b_last = b[:, :, -1:]
    q_e = q * jnp.exp(b)
    k_e = k * jnp.exp(-b)
    k_end = k * jnp.exp(b_last - b)
    lower = jnp.tril(jnp.ones((GLA_CHUNK, GLA_CHUNK), bool))
    att = jnp.where(lower, jnp.einsum('bnihd,bnjhd->bnhij', q_e, k_e), 0.0)
    o_intra = jnp.einsum('bnhij,bnjhe->bnihe', att, v)
    ds = jnp.einsum('bnjhd,bnjhe->bnhde', k_end, v)
    dec = jnp.exp(b_last[:, :, 0])

    def step(s, inp):
        d_n, ds_n = inp
        return s * d_n[..., None] + ds_n, s
    s_fin, s_start = lax.scan(step, s0, (jnp.moveaxis(dec, 1, 0), jnp.moveaxis(ds, 1, 0)))
    o_inter = jnp.einsum('bnihd,nbhde->bnihe', q_e, s_start)
    return (o_intra + o_inter).reshape(B, T, H, dv), s_fin


def gla_branch(pc, pl, a_up, a_b, gn_g, need_ctx):
    tc = gla_token_inputs(pc, a_up, a_b, False)
    tl = gla_token_inputs(pl, a_up, a_b, True)
    B = pl['gla_q'].shape[0]
    oc, ol = [], []
    for d in range(2):
        fl = flip_time if d else identity
        s0 = jnp.zeros((B, GLA_HEADS, GLA_DK, GLA_DV), jnp.float32)
        o_c, s_c = gla_chunked(fl(tc[0]), fl(tc[1]), fl(tc[2]), fl(tc[3][:, :, d]), s0)
        o_l, _ = gla_chunked(fl(tl[0]), fl(tl[1]), fl(tl[2]), fl(tl[3][:, :, d]), s_c)
        oc.append(fl(o_c))
        ol.append(fl(o_l))

    def finish(o, p):
        y = head_norm(o[0] + o[1], gn_g, None, GN_EPS) * jax.nn.silu(p['gla_r'].astype(jnp.float32))
        return y.astype(p['gla_r'].dtype)

    return (finish(oc, pc) if need_ctx else None), finish(ol, pl)


def sgu_branch(p, ln_g, ln_b, w_s, b_s):
    f32 = jnp.float32
    u = jax.nn.gelu(p['sgu_u'].astype(f32), approximate=False)
    v = jax.nn.gelu(p['sgu_v'].astype(f32), approximate=False)
    mu = jnp.mean(v, axis=-1, keepdims=True)
    var = jnp.mean(jnp.square(v - mu), axis=-1, keepdims=True)
    v = (v - mu) * lax.rsqrt(var + GN_EPS) * ln_g + ln_b
    B, T, C = v.shape
    vg = v.reshape(B, T // SGU_CHUNK, SGU_CHUNK, SGU_GROUPS, C // SGU_GROUPS)
    s = jnp.einsum('gij,bnjgc->bnigc', w_s, vg) + b_s.T[:, :, None]
    return (u * s.reshape(B, T, C)).astype(p['sgu_u'].dtype)


def na_branch(pc, pl, rpb, need_ctx):
    f32 = jnp.float32
    B, T, _ = pl['na_q'].shape
    Lc = pc['na_q'].shape[1]
    rows = T // GRID_W
    wh = min(NA_WIN_H, rows)
    scale = NA_HEAD ** -0.5
    grid = lambda z: z.reshape(B, rows, GRID_W, NA_HEADS, NA_HEAD)
    seqh = lambda z: z.reshape(B, Lc, NA_HEADS, NA_HEAD)
    q, k, v = grid(pl['na_q']), grid(pl['na_k']), grid(pl['na_v'])
    qc, kc, vc = seqh(pc['na_q']), seqh(pc['na_k']), seqh(pc['na_v'])
    r = jnp.arange(rows)
    rs = jnp.clip(r - wh // 2, 0, rows - wh)
    row_idx = rs[:, None] + jnp.arange(wh)[None, :]
    cidx = jnp.arange(GRID_W)
    cs = jnp.clip(cidx - NA_WIN_W // 2, 0, GRID_W - NA_WIN_W)
    col_ok = (cidx[None, :] >= cs[:, None]) & (cidx[None, :] < cs[:, None] + NA_WIN_W)
    kb = k[:, row_idx]
    vb = v[:, row_idx]
    s_win = jnp.einsum('brchd,brwxhd->bhrcwx', q, kb).astype(f32) * scale
    dr = row_idx - r[:, None] + (NA_WIN_H - 1)
    dc = jnp.clip(cidx[None, :] - cidx[:, None] + (NA_WIN_W - 1), 0, 2 * NA_WIN_W - 2)
    bias = rpb[:, dr[:, None, :, None], dc[None, :, None, :]].astype(f32)
    s_win = jnp.where(col_ok[:, None, :], s_win + bias[None], NEG_INF)
    s_ctx = jnp.einsum('brchd,bjhd->bhrcj', q, kc).astype(f32) * scale
    n_win = wh * GRID_W
    s_all = jnp.concatenate([s_win.reshape(B, NA_HEADS, rows, GRID_W, n_win), s_ctx], axis=-1)
    p_all = jax.nn.softmax(s_all, axis=-1).astype(v.dtype)
    p_win = p_all[..., :n_win].reshape(B, NA_HEADS, rows, GRID_W, wh, GRID_W)
    o = (jnp.einsum('bhrcwx,brwxhd->brchd', p_win, vb)
         + jnp.einsum('bhrcj,bjhd->brchd', p_all[..., n_win:], vc))
    o_lat = o.reshape(B, T, MIX_W).astype(pl['na_q'].dtype)
    o_ctx = None
    if need_ctx:
        s_cc = jnp.einsum('bihd,bjhd->bhij', qc, kc).astype(f32) * scale
        o_ctx = jnp.einsum('bhij,bjhd->bihd', jax.nn.softmax(s_cc, axis=-1).astype(vc.dtype), vc)
        o_ctx = o_ctx.reshape(B, Lc, MIX_W).astype(pc['na_q'].dtype)
    return o_ctx, o_lat


def merge_branches(ys, gate_pre, w_br, w_o):
    B, T, _ = gate_pre.shape
    y = jnp.stack(ys, axis=2)
    z = jnp.einsum('btnm,nmd->btnd', y, w_br)
    gate = jax.nn.sigmoid(gate_pre.reshape(B, T, N_BRANCH, D_MODEL))
    return jnp.einsum('btnd,btnd->btd', gate, z) @ w_o


def swiglu(h, w1, w3, w2):
    return (jax.nn.silu(h @ w1) * (h @ w3)) @ w2


def moe_swiglu(h, router, w1, w3, w2):
    logits = (h @ router).astype(jnp.float32)
    top_v, top_i = lax.top_k(logits, TOP_K)
    top_w = jax.nn.softmax(top_v, axis=-1)
    gates = jnp.sum(jax.nn.one_hot(top_i, N_EXPERTS, dtype=jnp.float32) * top_w[..., None], axis=-2).astype(h.dtype)
    out = gates[..., 0:1] * swiglu(h, w1[0], w3[0], w2[0])
    for e in range(1, N_EXPERTS):
        out = out + gates[..., e:e + 1] * swiglu(h, w1[e], w3[e], w2[e])
    return out


def setup_inputs(seed: int = 0) -> dict:
    key = jax.random.key(seed)
    ks = jax.random.split(key, 36)
    f32 = jnp.float32
    L, D, M = DEPTH, D_MODEL, MIX_W
    nrm = lambda i, shape, s: s * jax.random.normal(ks[i], shape, f32)
    return {
        'x': nrm(0, (BATCH, SEQ, D), 1.0),
        'c': nrm(1, (BATCH, D), 1.0),
        'ctx': nrm(2, (BATCH, CTX_LEN, D), 1.0),
        'c_ctx': nrm(3, (D,), 1.0),
        'ada_w': nrm(4, (L, D, 6 * D), 0.5 * D ** -0.5),
        'ada_b': nrm(5, (L, 6 * D), 0.02),
        'norm_g': 1.0 + nrm(6, (L, 4, D), 0.02),
        'w_in': nrm(7, (L, D, N_IN), D ** -0.5),
        'rw_mu': jax.random.uniform(ks[8], (L, 3, 2, M), f32, 0.0, 0.5),
        'rw_w0': jax.random.uniform(ks[9], (L, 2, M), f32, -5.0, -1.0),
        'rw_w_up': nrm(10, (L, 2, RW_DECAY_RANK, M), 0.5 * RW_DECAY_RANK ** -0.5),
        'rw_a0': nrm(11, (L, 2, M), 0.1),
        'rw_a_up': nrm(12, (L, 2, RW_ICLR_RANK, M), 0.5 * RW_ICLR_RANK ** -0.5),
        'rw_g_up': nrm(13, (L, RW_GATE_RANK, M), RW_GATE_RANK ** -0.5),
        'rw_k_k': 0.85 + nrm(14, (L, M), 0.05),
        'rw_k_a': 1.0 + nrm(15, (L, M), 0.05),
        'rw_r_k': nrm(16, (L, RW_HEADS, RW_HEAD), 0.1),
        'rw_ln_g': 1.0 + nrm(17, (L, M), 0.02),
        'rw_ln_b': nrm(18, (L, M), 0.02),
        'gla_a_up': nrm(19, (L, 2, GLA_GATE_RANK, GLA_HEADS * GLA_DK), GLA_GATE_RANK ** -0.5),
        'gla_a_b': nrm(20, (L, 2, GLA_HEADS * GLA_DK), 0.5),
        'gla_gn_g': 1.0 + nrm(21, (L, M), 0.02),
        'sgu_ln_g': 1.0 + nrm(22, (L, M), 0.02),
        'sgu_ln_b': nrm(23, (L, M), 0.02),
        'sgu_w': nrm(24, (L, SGU_GROUPS, SGU_CHUNK, SGU_CHUNK), SGU_CHUNK ** -0.5),
        'sgu_b': 1.0 + nrm(25, (L, SGU_GROUPS, SGU_CHUNK), 0.02),
        'na_rpb': nrm(26, (L, NA_HEADS, 2 * NA_WIN_H - 1, 2 * NA_WIN_W - 1), 0.1),
        'w_br': nrm(27, (L, N_BRANCH, M, D), M ** -0.5),
        'w_o': nrm(28, (L, D, D), D ** -0.5),
        'ffn_w1': nrm(29, (N_DENSE, D, FFN_DIM), D ** -0.5),
        'ffn_w3': nrm(30, (N_DENSE, D, FFN_DIM), D ** -0.5),
        'ffn_w2': nrm(31, (N_DENSE, FFN_DIM, D), FFN_DIM ** -0.5),
        'moe_router': nrm(32, (N_MOE, D, N_EXPERTS), D ** -0.5),
        'moe_w1': nrm(33, (N_MOE, N_EXPERTS, D, FFN_DIM), D ** -0.5),
        'moe_w3': nrm(34, (N_MOE, N_EXPERTS, D, FFN_DIM), D ** -0.5),
        'moe_w2': nrm(35, (N_MOE, N_EXPERTS, FFN_DIM, D), FFN_DIM ** -0.5),
    }


def reference(x, c, ctx, c_ctx, ada_w, ada_b, norm_g, w_in, rw_mu, rw_w0, rw_w_up, rw_a0, rw_a_up, rw_g_up,
              rw_k_k, rw_k_a, rw_r_k, rw_ln_g, rw_ln_b, gla_a_up, gla_a_b, gla_gn_g, sgu_ln_g, sgu_ln_b, sgu_w,
              sgu_b, na_rpb, w_br, w_o, ffn_w1, ffn_w3, ffn_w2, moe_router, moe_w1, moe_w3, moe_w2):
    xl, xc = x, ctx
    for i in range(DEPTH):
        need_ctx = i < DEPTH - 1
        ml = adaln(c, ada_w[i], ada_b[i])
        mc = adaln(c_ctx, ada_w[i], ada_b[i])
        hl = modulate(rms_norm(xl, norm_g[i, 0]), ml[0], ml[1])
        hc = modulate(rms_norm(xc, norm_g[i, 0]), mc[0], mc[1])
        pl = split_proj(hl @ w_in[i])
        pc = split_proj(hc @ w_in[i])
        a_c, a_l = rwkv_branch(pc, pl, rw_mu[i], rw_w0[i], rw_w_up[i], rw_a0[i], rw_a_up[i], rw_g_up[i],
                               rw_k_k[i], rw_k_a[i], rw_r_k[i], rw_ln_g[i], rw_ln_b[i], need_ctx)
        b_c, b_l = gla_branch(pc, pl, gla_a_up[i], gla_a_b[i], gla_gn_g[i], need_ctx)
        s_l = sgu_branch(pl, sgu_ln_g[i], sgu_ln_b[i], sgu_w[i], sgu_b[i])
        d_c, d_l = na_branch(pc, pl, na_rpb[i], need_ctx)
        y_l = merge_branches([a_l, b_l, s_l, d_l], pl['gate'], w_br[i], w_o[i])
        xl = xl + ml[2] * rms_norm(y_l, norm_g[i, 1])
        if need_ctx:
            s_c = sgu_branch(pc, sgu_ln_g[i], sgu_ln_b[i], sgu_w[i], sgu_b[i])
            y_c = merge_branches([a_c, b_c, s_c, d_c], pc['gate'], w_br[i], w_o[i])
            xc = xc + mc[2] * rms_norm(y_c, norm_g[i, 1])
        j = i // 2
        if i % 2 == 0:
            ffn = lambda h: swiglu(h, ffn_w1[j], ffn_w3[j], ffn_w2[j])
        else:
            ffn = lambda h: moe_swiglu(h, moe_router[j], moe_w1[j], moe_w3[j], moe_w2[j])
        f_l = ffn(modulate(rms_norm(xl, norm_g[i, 2]), ml[3], ml[4]))
        xl = xl + ml[5] * rms_norm(f_l, norm_g[i, 3])
        if need_ctx:
            f_c = ffn(modulate(rms_norm(xc, norm_g[i, 2]), mc[3], mc[4]))
            xc = xc + mc[5] * rms_norm(f_c, norm_g[i, 3])
    return xl
```

```python
import functools

import jax
import jax.numpy as jnp
from jax import lax
from jax.experimental import pallas as pl
from jax.experimental.pallas import tpu as pltpu

F32 = jnp.float32
BF16 = jnp.bfloat16

D_MODEL = 2048
GRID_W = 64
N_BRANCH = 4
MIX_W = D_MODEL // 4
RW_HEAD = 64
RW_HEADS = MIX_W // RW_HEAD
RW_DECAY_RANK = 64
RW_ICLR_RANK = 64
RW_GATE_RANK = 128
RW_GN_EPS = 64e-5
GLA_HEADS = 4
GLA_DV = MIX_W // GLA_HEADS
GLA_DK = GLA_DV // 2
GLA_GATE_RANK = 16
GLA_TAU = 16.0
GN_EPS = 1e-5
ROPE_BASE = 10000.0
SGU_GROUPS = MIX_W // 64
SGU_CHUNK = 128
NA_HEAD = 64
NA_HEADS = MIX_W // NA_HEAD
NA_WIN_H = 8
NA_WIN_W = 16
FFN_DIM = 7 * D_MODEL // 2
N_EXPERTS = 8
TOP_K = 2
EPS = 1e-6
NEG_INF = -1e30

CHUNK = 64
LANE = 128
VMEM_LIMIT = 56 * 2 ** 20

_IN_ORDER = (
    ('rw_r', 512), ('rw_k', 512), ('rw_v', 512), ('gla_v', 512), ('gla_r', 512),
    ('sgu_u', 512), ('sgu_v', 512), ('na_q', 512), ('na_k', 512), ('na_v', 512),
    ('gla_q', 256), ('gla_k', 256), ('rw_wd', 128), ('rw_ad', 128), ('rw_gd', 128),
    ('gla_ad', 32), ('pad', 96), ('gate', N_BRANCH * D_MODEL),
)
_REF_SPLITS = (
    ('rw_r', MIX_W), ('rw_k', MIX_W), ('rw_v', MIX_W),
    ('rw_wd', 2 * RW_DECAY_RANK), ('rw_ad', 2 * RW_ICLR_RANK), ('rw_gd', RW_GATE_RANK),
    ('gla_q', GLA_HEADS * GLA_DK), ('gla_k', GLA_HEADS * GLA_DK), ('gla_v', MIX_W),
    ('gla_r', MIX_W), ('gla_ad', 2 * GLA_GATE_RANK),
    ('sgu_u', MIX_W), ('sgu_v', MIX_W),
    ('na_q', MIX_W), ('na_k', MIX_W), ('na_v', MIX_W),
    ('gate', N_BRANCH * D_MODEL),
)
N_IN_P = sum(w for _, w in _IN_ORDER)
CB_GLA_V, CB_GLA_R, CB_SGU_U, CB_SGU_V, CB_NA_Q, CB_NA_K, CB_NA_V, CB_GLA_QK, CB_SMALL = 3, 4, 5, 6, 7, 8, 9, 10, 11
GATE_OFF = 6144


def _cparams(sem):
    return pltpu.CompilerParams(dimension_semantics=sem, vmem_limit_bytes=VMEM_LIMIT)


def _pick(m, cands):
    for c in cands:
        if m % c == 0:
            return c
    raise ValueError(f'no tile for {m}')


def _mm(a, b):
    return jnp.dot(a.astype(BF16), b.astype(BF16), preferred_element_type=F32)


def _mm_nt(a, b):
    return lax.dot_general(a.astype(BF16), b.astype(BF16), (((1,), (1,)), ((), ())),
                           preferred_element_type=F32)


def _mm_tn(a, b):
    return lax.dot_general(a.astype(BF16), b.astype(BF16), (((0,), (0,)), ((), ())),
                           preferred_element_type=F32)


def _split2(x):
    hi = x.astype(BF16)
    lo = (x - hi.astype(F32)).astype(BF16)
    return hi, lo


def _mm_xw(x, w_bf):
    hi, lo = _split2(x)
    return (jnp.dot(hi, w_bf, preferred_element_type=F32)
            + jnp.dot(lo, w_bf, preferred_element_type=F32))


def _mm_wx(w_bf, x):
    hi, lo = _split2(x)
    return (jnp.dot(w_bf, hi, preferred_element_type=F32)
            + jnp.dot(w_bf, lo, preferred_element_type=F32))


def _mm3(a, b):
    ah, al = _split2(a)
    bh, bl = _split2(b)
    return (jnp.dot(ah, bh, preferred_element_type=F32)
            + jnp.dot(ah, bl, preferred_element_type=F32)
            + jnp.dot(al, bh, preferred_element_type=F32))


def _sigmoid(x):
    return 1.0 / (1.0 + jnp.exp(-x))


def _softplus(x):
    return jnp.maximum(x, 0.0) + jnp.log1p(jnp.exp(-jnp.abs(x)))


def _gelu(x):
    return 0.5 * x * (1.0 + lax.erf(x * (0.5 ** 0.5)))


def _order_masks(d):
    t = lax.broadcasted_iota(jnp.int32, (CHUNK, CHUNK), 0)
    s = lax.broadcasted_iota(jnp.int32, (CHUNK, CHUNK), 1)
    diff = (t - s) * jnp.where(d == 0, 1, -1)
    return diff > 0, diff >= 0


def _chunk_of(d, n, nl, nc):
    fwd = jnp.where(n < nc, nl + n, n - nc)
    bwd = jnp.where(n < nc, nl + nc - 1 - n, nl - 1 - (n - nc))
    return jnp.where(d == 0, fwd, bwd)


def _ada_kernel(c_ref, w_ref, b_ref, o_ref):
    cnd = c_ref[...]
    a = cnd * _sigmoid(cnd)
    o_ref[...] = _mm(a, w_ref[...]) + b_ref[...]


def _adaln(cond8, ada_w, ada_b):
    n = ada_w.shape[1]
    bn = 1536
    return pl.pallas_call(
        _ada_kernel,
        grid=(n // bn,),
        in_specs=[pl.BlockSpec((8, D_MODEL), lambda j: (0, 0)),
                  pl.BlockSpec((D_MODEL, bn), lambda j: (0, j)),
                  pl.BlockSpec((1, bn), lambda j: (0, j))],
        out_specs=pl.BlockSpec((8, bn), lambda j: (0, j)),
        out_shape=jax.ShapeDtypeStruct((8, n), F32),
        compiler_params=_cparams(('arbitrary',)),
        name='adaln',
    )(cond8, ada_w, ada_b.reshape(1, n))


def _norm_mod(x, g, m, k0):
    y = x * lax.rsqrt(jnp.mean(x * x, axis=-1, keepdims=True) + EPS) * g
    return y * (1.0 + m[k0 + 1:k0 + 2]) + m[k0:k0 + 1]


def _norm_mod_kernel(x_ref, g_ref, m_ref, o_ref, *, k0):
    o_ref[...] = _norm_mod(x_ref[...], g_ref[...], m_ref[0], k0).astype(BF16)


def _norm_mod_route_kernel(x_ref, g_ref, m_ref, r_ref, o_ref, gate_ref, *, k0):
    h = _norm_mod(x_ref[...], g_ref[...], m_ref[0], k0)
    o_ref[...] = h.astype(BF16)
    logits = _mm3(h, r_ref[...])
    lane = lax.broadcasted_iota(jnp.int32, logits.shape, 1)
    neg = jnp.float32(-jnp.inf)
    l1 = jnp.where(lane < N_EXPERTS, logits, neg)
    m1 = jnp.max(l1, axis=-1, keepdims=True)
    i1 = jnp.min(jnp.where(l1 == m1, lane, LANE), axis=-1, keepdims=True)
    l2 = jnp.where(lane == i1, neg, l1)
    m2 = jnp.max(l2, axis=-1, keepdims=True)
    i2 = jnp.min(jnp.where(l2 == m2, lane, LANE), axis=-1, keepdims=True)
    e2 = jnp.exp(m2 - m1)
    den = 1.0 + e2
    gate_ref[...] = jnp.where(lane == i1, 1.0 / den, 0.0) + jnp.where(lane == i2, e2 / den, 0.0)


def _row_tile(t, lc):
    return _pick(lc, (256, 128))


def _norm_mod_call(x, g, mods, k0, t, router=None):
    tt = x.shape[0]
    bm = _row_tile(t, tt - t)
    nlt = t // bm
    xs = pl.BlockSpec((bm, D_MODEL), lambda i: (i, 0))
    gs = pl.BlockSpec((1, D_MODEL), lambda i: (0, 0))
    ms = pl.BlockSpec((1, 6, D_MODEL), lambda i: (jnp.where(i >= nlt, 1, 0), 0, 0))
    if router is None:
        return pl.pallas_call(
            functools.partial(_norm_mod_kernel, k0=k0),
            grid=(tt // bm,), in_specs=[xs, gs, ms], out_specs=xs,
            out_shape=jax.ShapeDtypeStruct((tt, D_MODEL), BF16),
            compiler_params=_cparams(('parallel',)), name='norm_mod',
        )(x, g.reshape(1, D_MODEL), mods)
    rpad = jnp.pad(router, ((0, 0), (0, LANE - N_EXPERTS)))
    return pl.pallas_call(
        functools.partial(_norm_mod_route_kernel, k0=k0),
        grid=(tt // bm,),
        in_specs=[xs, gs, ms, pl.BlockSpec((D_MODEL, LANE), lambda i: (0, 0))],
        out_specs=[xs, pl.BlockSpec((bm, LANE), lambda i: (i, 0))],
        out_shape=[jax.ShapeDtypeStruct((tt, D_MODEL), BF16), jax.ShapeDtypeStruct((tt, LANE), F32)],
        compiler_params=_cparams(('parallel',)), name='norm_mod_route',
    )(x, g.reshape(1, D_MODEL), mods, rpad)


def _resid_kernel(x_ref, y_ref, g_ref, m_ref, o_ref, *, kg):
    y = y_ref[...]
    n = y * lax.rsqrt(jnp.mean(y * y, axis=-1, keepdims=True) + EPS) * g_ref[...]
    o_ref[...] = x_ref[...] + m_ref[0][kg:kg + 1] * n


def _resid_call(x, y, g, mods, kg, t):
    tt = x.shape[0]
    bm = _row_tile(t, tt - t)
    nlt = t // bm
    xs = pl.BlockSpec((bm, D_MODEL), lambda i: (i, 0))
    return pl.pallas_call(
        functools.partial(_resid_kernel, kg=kg),
        grid=(tt // bm,),
        in_specs=[xs, xs, pl.BlockSpec((1, D_MODEL), lambda i: (0, 0)),
                  pl.BlockSpec((1, 6, D_MODEL), lambda i: (jnp.where(i >= nlt, 1, 0), 0, 0))],
        out_specs=xs,
        out_shape=jax.ShapeDtypeStruct((tt, D_MODEL), F32),
        compiler_params=_cparams(('parallel',)), name='resid_norm',
    )(x, y, g.reshape(1, D_MODEL), mods)


def _mm_kernel(a_ref, w_ref, o_ref, wb_ref):
    @pl.when(pl.program_id(1) == 0)
    def _():
        wb_ref[...] = w_ref[...].astype(BF16)
    o_ref[...] = jnp.dot(a_ref[...], wb_ref[...], preferred_element_type=F32).astype(o_ref.dtype)


def _matmul(a, w, bn, bm_cands, out_dtype=F32, name='matmul'):
    m, k = a.shape
    n = w.shape[1]
    bm = _pick(m, bm_cands)
    return pl.pallas_call(
        _mm_kernel,
        grid=(n // bn, m // bm),
        in_specs=[pl.BlockSpec((bm, k), lambda j, i: (i, 0)),
                  pl.BlockSpec((k, bn), lambda j, i: (0, j))],
        out_specs=pl.BlockSpec((bm, bn), lambda j, i: (i, j)),
        out_shape=jax.ShapeDtypeStruct((m, n), out_dtype),
        scratch_shapes=[pltpu.VMEM((k, bn), BF16)],
        compiler_params=_cparams(('arbitrary', 'arbitrary')), name=name,
    )(a, w)


def _ffn_up_kernel(a_ref, w1_ref, w3_ref, o_ref, w1b, w3b):
    @pl.when(pl.program_id(1) == 0)
    def _():
        w1b[...] = w1_ref[...].astype(BF16)
        w3b[...] = w3_ref[...].astype(BF16)
    a = a_ref[...]
    h1 = jnp.dot(a, w1b[...], preferred_element_type=F32)
    h3 = jnp.dot(a, w3b[...], preferred_element_type=F32)
    o_ref[...] = (h1 * _sigmoid(h1) * h3).astype(BF16)


def _ffn_up(h, w1, w3, e):
    m = h.shape[0]
    bn = 512
    bm = _pick(m, (768, 384, 128))
    ws = pl.BlockSpec((None, D_MODEL, bn), lambda j, i: (e, 0, j))
    return pl.pallas_call(
        _ffn_up_kernel,
        grid=(FFN_DIM // bn, m // bm),
        in_specs=[pl.BlockSpec((bm, D_MODEL), lambda j, i: (i, 0)), ws, ws],
        out_specs=pl.BlockSpec((bm, bn), lambda j, i: (i, j)),
        out_shape=jax.ShapeDtypeStruct((m, FFN_DIM), BF16),
        scratch_shapes=[pltpu.VMEM((D_MODEL, bn), BF16), pltpu.VMEM((D_MODEL, bn), BF16)],
        compiler_params=_cparams(('arbitrary', 'arbitrary')), name='ffn_up',
    )(h, w1, w3)


def _ffn_down_kernel(a_ref, w_ref, o_ref, wb_ref):
    @pl.when(pl.program_id(1) == 0)
    def _():
        wb_ref[...] = w_ref[...].astype(BF16)
    o_ref[...] = jnp.dot(a_ref[...], wb_ref[...], preferred_element_type=F32)


def _moe_down_kernel(a_ref, w_ref, g_ref, p_ref, o_ref, wb_ref, *, e):
    @pl.when(pl.program_id(1) == 0)
    def _():
        wb_ref[...] = w_ref[...].astype(BF16)
    acc = jnp.dot(a_ref[...], wb_ref[...], preferred_element_type=F32)
    o_ref[...] = p_ref[...] + g_ref[:, e:e + 1] * acc


def _ffn_down(u, w2, e, gates=None, prev=None):
    m = u.shape[0]
    bn = 256
    bm = _pick(m, (384, 128))
    a_s = pl.BlockSpec((bm, FFN_DIM), lambda j, i: (i, 0))
    w_s = pl.BlockSpec((None, FFN_DIM, bn), lambda j, i: (e, 0, j))
    o_s = pl.BlockSpec((bm, bn), lambda j, i: (i, j))
    common = dict(
        grid=(D_MODEL // bn, m // bm), out_specs=o_s,
        out_shape=jax.ShapeDtypeStruct((m, D_MODEL), F32),
        scratch_shapes=[pltpu.VMEM((FFN_DIM, bn), BF16)],
        compiler_params=_cparams(('arbitrary', 'arbitrary')))
    if gates is None:
        return pl.pallas_call(_ffn_down_kernel, in_specs=[a_s, w_s], name='ffn_down', **common)(u, w2)
    return pl.pallas_call(
        functools.partial(_moe_down_kernel, e=e),
        in_specs=[a_s, w_s, pl.BlockSpec((bm, LANE), lambda j, i: (i, 0)), o_s],
        input_output_aliases={3: 0}, name='moe_down', **common)(u, w2, gates, prev)


def _rwkv_kernel(cur_ref, prv_ref, nxt_ref, sm_ref, mu_ref, w0_ref, wup_ref, a0_ref, aup_ref,
                 kk_ref, ka_ref, rk_ref, e_ref, y_ref, bonus_ref, s_ref, *, nl, nc):
    d = pl.program_id(0)
    n = pl.program_id(1)
    ch = _chunk_of(d, n, nl, nc)

    @pl.when(n == 0)
    def _():
        s_ref[...] = jnp.zeros_like(s_ref)

    x = cur_ref[...]
    first = jnp.logical_or(ch == 0, ch == nl)
    last = jnp.logical_or(ch == nl - 1, ch == nl + nc - 1)
    p_row = jnp.where(first, 0.0, prv_ref[7:8, :])
    n_row = jnp.where(last, 0.0, nxt_ref[0:1, :])
    row = lax.broadcasted_iota(jnp.int32, (CHUNK, 1), 0)
    prev = jnp.where(row == 0, p_row, pltpu.roll(x, 1, axis=0))
    nxt = jnp.where(row == CHUNK - 1, n_row, pltpu.roll(x, CHUNK - 1, axis=0))
    z = x + mu_ref[0:1, :] * (prev - x) + mu_ref[1:2, :] * (nxt - x)
    r, k, v = z[:, 0:MIX_W], z[:, MIX_W:2 * MIX_W], z[:, 2 * MIX_W:3 * MIX_W]

    sm = sm_ref[...]
    e_bf = e_ref[...]
    wd = jnp.tanh(sm[:, 0:128])
    w_log = -_softplus(-(w0_ref[...] + _mm3(wd, wup_ref[...]))) - 0.5
    logw = -jnp.exp(w_log)
    a = _sigmoid(a0_ref[...] + _mm3(sm[:, 128:256], aup_ref[...]))
    kkr = k * kk_ref[...]
    kk = kkr * lax.rsqrt(jnp.maximum(_mm_xw(kkr * kkr, e_bf), 1e-12))
    k_dir = k * (1.0 + (a - 1.0) * ka_ref[...])
    bonus_ref[...] = _mm_xw(r * k_dir * rk_ref[...], e_bf) * v

    strict, incl = _order_masks(d)
    b_inc = _mm_wx(jnp.where(incl, 1.0, 0.0).astype(BF16), logw)
    b_exc = b_inc - logw
    b_last = jnp.sum(logw, axis=0, keepdims=True)
    beta = kk * a
    ea = -kk * jnp.exp(b_exc)
    er = r * jnp.exp(b_inc)
    ninv = jnp.exp(-b_inc)
    eb = beta * ninv
    ek = k_dir * ninv
    eend = jnp.exp(b_last - b_inc)
    hb = beta * eend
    hk = k_dir * eend
    gam = jnp.exp(b_last)

    zeros = jnp.zeros((CHUNK, RW_HEAD), F32)
    for h in range(RW_HEADS):
        sl = slice(h * RW_HEAD, (h + 1) * RW_HEAD)
        v_h = v[:, sl]
        left = jnp.concatenate([ea[:, sl], er[:, sl]], axis=0)
        right = jnp.concatenate([eb[:, sl], ek[:, sl]], axis=0)
        amat = _mm_nt(left, right)
        m_ab = jnp.where(strict, amat[0:CHUNK, 0:CHUNK], 0.0)
        m_ak = jnp.where(strict, amat[0:CHUNK, CHUNK:], 0.0)
        n_rb = jnp.where(incl, amat[CHUNK:, 0:CHUNK], 0.0)
        n_rk = jnp.where(incl, amat[CHUNK:, CHUNK:], 0.0)
        xs = jnp.concatenate([ea[:, sl], _mm(m_ak, v_h)], axis=1)
        mp = m_ab
        for it in range(6):
            xs = xs + _mm(mp, xs)
            if it < 5:
                mp = _mm(mp, mp)
        zmat = jnp.concatenate([xs, jnp.concatenate([zeros, v_h], axis=1)], axis=0)
        qy = _mm(jnp.concatenate([n_rb, n_rk], axis=1), zmat)
        q_hat = er[:, sl] + qy[:, 0:RW_HEAD]
        x1 = jnp.concatenate([hb[:, sl], hk[:, sl]], axis=0)
        gbt = _mm_tn(zmat, x1)
        s0 = s_ref[h]
        y_ref[:, sl] = _mm_nt(q_hat, s0) + qy[:, RW_HEAD:]
        s_ref[h] = s0 * gam[:, sl] + _mm(s0, gbt[0:RW_HEAD]) + gbt[RW_HEAD:]


def _rwkv_scan(p, t, lc, mu2, w0, wup_pad, a0, aup_pad, k_k, k_a, r_k, e64):
    tt = t + lc
    nl, nc = t // CHUNK, lc // CHUNK
    nch = nl + nc
    last8 = tt // 8 - 1
    ch = functools.partial(_chunk_of, nl=nl, nc=nc)
    vec = pl.BlockSpec((1, MIX_W), lambda d, n: (0, 0))
    out_s = pl.BlockSpec((None, CHUNK, MIX_W), lambda d, n: (d, ch(d, n), 0))
    return pl.pallas_call(
        functools.partial(_rwkv_kernel, nl=nl, nc=nc),
        grid=(2, nch),
        in_specs=[
            pl.BlockSpec((CHUNK, 3 * MIX_W), lambda d, n: (ch(d, n), 0)),
            pl.BlockSpec((8, 3 * MIX_W), lambda d, n: (jnp.maximum(ch(d, n) * 8 - 1, 0), 0)),
            pl.BlockSpec((8, 3 * MIX_W), lambda d, n: (jnp.minimum(ch(d, n) * 8 + 8, last8), 0)),
            pl.BlockSpec((CHUNK, MIX_W), lambda d, n: (ch(d, n), CB_SMALL)),
            pl.BlockSpec((2, 3 * MIX_W), lambda d, n: (0, 0)),
            pl.BlockSpec((None, 1, MIX_W), lambda d, n: (d, 0, 0)),
            pl.BlockSpec((None, LANE, MIX_W), lambda d, n: (d, 0, 0)),
            pl.BlockSpec((None, 1, MIX_W), lambda d, n: (d, 0, 0)),
            pl.BlockSpec((None, LANE, MIX_W), lambda d, n: (d, 0, 0)),
            vec, vec, vec,
            pl.BlockSpec((MIX_W, MIX_W), lambda d, n: (0, 0)),
        ],
        out_specs=[out_s, out_s],
        out_shape=[jax.ShapeDtypeStruct((2, tt, MIX_W), F32)] * 2,
        scratch_shapes=[pltpu.VMEM((RW_HEADS, RW_HEAD, RW_HEAD), F32)],
        compiler_params=_cparams(('arbitrary', 'arbitrary')), name='rwkv_scan',
    )(p, p, p, p, mu2, w0, wup_pad, a0, aup_pad, k_k, k_a, r_k, e64)


def _gla_kernel(qk_ref, v_ref, sm_ref, cos_ref, sin_ref, aup_ref, ab_ref, o_ref, s_ref, *, nl, nc):
    d = pl.program_id(0)
    n = pl.program_id(1)

    @pl.when(n == 0)
    def _():
        s_ref[...] = jnp.zeros_like(s_ref)

    qk = qk_ref[...]
    lane = lax.broadcasted_iota(jnp.int32, (CHUNK, 2 * GLA_HEADS * GLA_DK), 1)
    partner = jnp.where((lane & 1) == 0, pltpu.roll(qk, 2 * GLA_HEADS * GLA_DK - 1, axis=1),
                        pltpu.roll(qk, 1, axis=1))
    qk = qk * cos_ref[...] + partner * sin_ref[...]
    hk = GLA_HEADS * GLA_DK
    q = qk[:, 0:hk] * (GLA_DK ** -0.5)
    k = qk[:, hk:]
    v = v_ref[...]
    g = -_softplus(-(_mm3(sm_ref[:, 384:512], aup_ref[...]) + ab_ref[...])) / GLA_TAU
    _, incl = _order_masks(d)
    b = _mm_wx(jnp.where(incl, 1.0, 0.0).astype(BF16), g)
    b_last = jnp.sum(g, axis=0, keepdims=True)
    q_e = q * jnp.exp(b)
    k_e = k * jnp.exp(-b)
    k_end = k * jnp.exp(b_last - b)
    dec = jnp.exp(b_last)
    for h in range(GLA_HEADS):
        sk = slice(h * GLA_DK, (h + 1) * GLA_DK)
        sv = slice(h * GLA_DV, (h + 1) * GLA_DV)
        att = jnp.where(incl, _mm_nt(q_e[:, sk], k_e[:, sk]), 0.0)
        st = s_ref[h]
        o_ref[:, sv] = _mm(att, v[:, sv]) + _mm_nt(q_e[:, sk], st)
        s_ref[h] = st * dec[:, sk] + _mm_tn(v[:, sv], k_end[:, sk])


def _gla_scan(p, t, lc, cos_t, sin_t, aup_pad, a_b):
    tt = t + lc
    nl, nc = t // CHUNK, lc // CHUNK
    ch = functools.partial(_chunk_of, nl=nl, nc=nc)
    blk = lambda cb: pl.BlockSpec((CHUNK, MIX_W), lambda d, n: (ch(d, n), cb))
    return pl.pallas_call(
        functools.partial(_gla_kernel, nl=nl, nc=nc),
        grid=(2, nl + nc),
        in_specs=[blk(CB_GLA_QK), blk(CB_GLA_V), blk(CB_SMALL), blk(0), blk(0),
                  pl.BlockSpec((None, LANE, 2 * LANE), lambda d, n: (d, 0, 0)),
                  pl.BlockSpec((None, 1, 2 * LANE), lambda d, n: (d, 0, 0))],
        out_specs=pl.BlockSpec((None, CHUNK, MIX_W), lambda d, n: (d, ch(d, n), 0)),
        out_shape=jax.ShapeDtypeStruct((2, tt, MIX_W), F32),
        scratch_shapes=[pltpu.VMEM((GLA_HEADS, GLA_DV, GLA_DK), F32)],
        compiler_params=_cparams(('arbitrary', 'arbitrary')), name='gla_scan',
    )(p, p, p, cos_t, sin_t, aup_pad, a_b)


def _head_norm(y, e_bf, width, eps):
    mu = _mm_xw(y, e_bf) * (1.0 / width)
    dl = y - mu
    var = _mm_xw(dl * dl, e_bf) * (1.0 / width)
    return dl * lax.rsqrt(var + eps)


def _mix_finish_kernel(y0_ref, y1_ref, b0_ref, b1_ref, sm_ref, gup_ref, lng_ref, lnb_ref, e64_ref,
                       o0_ref, o1_ref, gr_ref, gng_ref, e128_ref, a_ref, b_ref):
    yn = _head_norm(y0_ref[...] + y1_ref[...], e64_ref[...], RW_HEAD, RW_GN_EPS)
    yn = yn * lng_ref[...] + lnb_ref[...] + b0_ref[...] + b1_ref[...]
    gate = _mm(_sigmoid(sm_ref[:, 256:384]), gup_ref[...])
    a_ref[...] = (yn * gate).astype(BF16)
    on = _head_norm(o0_ref[...] + o1_ref[...], e128_ref[...], GLA_DV, GN_EPS) * gng_ref[...]
    gr = gr_ref[...]
    b_ref[...] = (on * (gr * _sigmoid(gr))).astype(BF16)


def _mix_finish(p, rw_y, rw_bonus, gla_o, g_up, ln_g, ln_b, gn_g, e64, e128, t):
    tt = p.shape[0]
    bm = _row_tile(t, tt - t)
    dblk = lambda d: pl.BlockSpec((None, bm, MIX_W), lambda i: (d, i, 0))
    vec = pl.BlockSpec((1, MIX_W), lambda i: (0, 0))
    mat = pl.BlockSpec((MIX_W, MIX_W), lambda i: (0, 0))
    out = pl.BlockSpec((bm, MIX_W), lambda i: (i, 0))
    return pl.pallas_call(
        _mix_finish_kernel,
        grid=(tt // bm,),
        in_specs=[dblk(0), dblk(1), dblk(0), dblk(1),
                  pl.BlockSpec((bm, MIX_W), lambda i: (i, CB_SMALL)),
                  pl.BlockSpec((RW_GATE_RANK, MIX_W), lambda i: (0, 0)), vec, vec, mat,
                  dblk(0), dblk(1), pl.BlockSpec((bm, MIX_W), lambda i: (i, CB_GLA_R)), vec, mat],
        out_specs=[out, out],
        out_shape=[jax.ShapeDtypeStruct((tt, MIX_W), BF16)] * 2,
        compiler_params=_cparams(('parallel',)), name='mix_finish',
    )(rw_y, rw_y, rw_bonus, rw_bonus, p, g_up, ln_g, ln_b, e64, gla_o, gla_o, p, gn_g, e128)


def _sgu_kernel(u_ref, v_ref, lng_ref, lnb_ref, ws_ref, bs_ref, o_ref):
    u = _gelu(u_ref[...])
    v = _gelu(v_ref[...])
    mu = jnp.mean(v, axis=-1, keepdims=True)
    dl = v - mu
    var = jnp.mean(dl * dl, axis=-1, keepdims=True)
    vn = (dl * lax.rsqrt(var + GN_EPS) * lng_ref[...] + lnb_ref[...]).astype(BF16)
    lane = lax.broadcasted_iota(jnp.int32, (1, MIX_W), 1)
    s = bs_ref[...]
    for g in range(SGU_GROUPS):
        s = s + jnp.where((lane >> 6) == g, jnp.dot(ws_ref[g].astype(BF16), vn, preferred_element_type=F32), 0.0)
    o_ref[...] = (u * s).astype(BF16)


def _sgu(p, ln_g, ln_b, w_s, b_full):
    tt = p.shape[0]
    vec = pl.BlockSpec((1, MIX_W), lambda i: (0, 0))
    return pl.pallas_call(
        _sgu_kernel,
        grid=(tt // SGU_CHUNK,),
        in_specs=[pl.BlockSpec((SGU_CHUNK, MIX_W), lambda i: (i, CB_SGU_U)),
                  pl.BlockSpec((SGU_CHUNK, MIX_W), lambda i: (i, CB_SGU_V)), vec, vec,
                  pl.BlockSpec((SGU_GROUPS, SGU_CHUNK, SGU_CHUNK), lambda i: (0, 0, 0)),
                  pl.BlockSpec((SGU_CHUNK, MIX_W), lambda i: (0, 0))],
        out_specs=pl.BlockSpec((SGU_CHUNK, MIX_W), lambda i: (i, 0)),
        out_shape=jax.ShapeDtypeStruct((tt, MIX_W), BF16),
        compiler_params=_cparams(('parallel',)), name='sgu',
    )(p, p, ln_g, ln_b, w_s, b_full)


def _na_bias_kernel(rpb_ref, o_ref, *, wh):
    dr0 = pl.program_id(0)
    h = pl.program_id(1)
    n_dc = 2 * NA_WIN_W - 1
    shape = (GRID_W, wh * GRID_W)
    c = lax.broadcasted_iota(jnp.int32, shape, 0)
    lane = lax.broadcasted_iota(jnp.int32, shape, 1)
    x = lane & (GRID_W - 1)
    w = lane >> 6
    dc = jnp.clip(x - c + (NA_WIN_W - 1), 0, 2 * NA_WIN_W - 2)
    key = w * n_dc + dc
    base = h * ((2 * NA_WIN_H - 1) * n_dc) + dr0 * n_dc

    def body(j, acc):
        return jnp.where(key == j, rpb_ref[base + j], acc)
    tile = lax.fori_loop(0, wh * n_dc, body, jnp.zeros(shape, F32))
    cs = jnp.clip(c - NA_WIN_W // 2, 0, GRID_W - NA_WIN_W)
    ok = jnp.logical_and(x >= cs, x < cs + NA_WIN_W)
    o_ref[0, 0] = jnp.where(ok, tile, NEG_INF)


def _na_bias(rpb, wh):
    n_dr0 = 2 * NA_WIN_H - wh
    return pl.pallas_call(
        functools.partial(_na_bias_kernel, wh=wh),
        grid=(n_dr0, NA_HEADS),
        in_specs=[pl.BlockSpec(memory_space=pltpu.SMEM)],
        out_specs=pl.BlockSpec((1, 1, GRID_W, wh * GRID_W), lambda a, h: (a, h, 0, 0)),
        out_shape=jax.ShapeDtypeStruct((n_dr0, NA_HEADS, GRID_W, wh * GRID_W), F32),
        compiler_params=_cparams(('parallel', 'parallel')), name='na_bias',
    )(rpb.reshape(-1))


def _na_kernel(*refs, wh):
    q_ref = refs[0]
    k_refs = refs[1:1 + wh]
    v_refs = refs[1 + wh:1 + 2 * wh]
    kc_ref, vc_ref, bias_ref, o_ref = refs[1 + 2 * wh:]
    q = q_ref[...] * (NA_HEAD ** -0.5)
    kw = jnp.concatenate([r[...].astype(BF16) for r in k_refs], axis=0)
    vw = jnp.concatenate([r[...].astype(BF16) for r in v_refs], axis=0)
    kc = kc_ref[...].astype(BF16)
    vc = vc_ref[...].astype(BF16)
    lane = lax.broadcasted_iota(jnp.int32, (1, MIX_W), 1)
    o = jnp.zeros((GRID_W, MIX_W), F32)
    for h in range(NA_HEADS):
        hm = (lane >> 6) == h
        qh = jnp.where(hm, q, 0.0).astype(BF16)
        sw = _mm_nt(qh, kw) + bias_ref[0, h]
        sc = _mm_nt(qh, kc)
        m = jnp.maximum(jnp.max(sw, axis=-1, keepdims=True), jnp.max(sc, axis=-1, keepdims=True))
        ew = jnp.exp(sw - m)
        ec = jnp.exp(sc - m)
        den = jnp.sum(ew, axis=-1, keepdims=True) + jnp.sum(ec, axis=-1, keepdims=True)
        oh = (_mm(ew, vw) + _mm(ec, vc)) / den
        o = jnp.where(hm, oh, o)
    o_ref[...] = o.astype(BF16)


def _na_ctx_kernel(q_ref, k_ref, v_ref, o_ref):
    q = q_ref[...] * (NA_HEAD ** -0.5)
    kc = k_ref[...].astype(BF16)
    vc = v_ref[...].astype(BF16)
    lane = lax.broadcasted_iota(jnp.int32, (1, MIX_W), 1)
    o = jnp.zeros(q.shape, F32)
    for h in range(NA_HEADS):
        hm = (lane >> 6) == h
        s = _mm_nt(jnp.where(hm, q, 0.0), kc)
        e = jnp.exp(s - jnp.max(s, axis=-1, keepdims=True))
        o = jnp.where(hm, _mm(e, vc) / jnp.sum(e, axis=-1, keepdims=True), o)
    o_ref[...] = o.astype(BF16)


def _na(p, bias, t, lc):
    rows = t // GRID_W
    wh = min(NA_WIN_H, rows)
    rs = lambda r: jnp.clip(r - wh // 2, 0, rows - wh)
    ctx_blk = t // lc
    kv = lambda cb: [pl.BlockSpec((GRID_W, MIX_W), functools.partial(lambda r, w, cb: (rs(r) + w, cb), w=w, cb=cb))
                     for w in range(wh)]
    lat = pl.pallas_call(
        functools.partial(_na_kernel, wh=wh),
        grid=(rows,),
        in_specs=([pl.BlockSpec((GRID_W, MIX_W), lambda r: (r, CB_NA_Q))] + kv(CB_NA_K) + kv(CB_NA_V)
                  + [pl.BlockSpec((lc, MIX_W), lambda r: (ctx_blk, CB_NA_K)),
                     pl.BlockSpec((lc, MIX_W), lambda r: (ctx_blk, CB_NA_V)),
                     pl.BlockSpec((1, NA_HEADS, GRID_W, wh * GRID_W),
                                  lambda r: (rs(r) - r + (NA_WIN_H - 1), 0, 0, 0))]),
        out_specs=pl.BlockSpec((GRID_W, MIX_W), lambda r: (r, 0)),
        out_shape=jax.ShapeDtypeStruct((t, MIX_W), BF16),
        compiler_params=_cparams(('parallel',)), name='na_latent',
    )(*([p] * (1 + 2 * wh + 2)), bias)
    cblk = lambda cb: pl.BlockSpec((lc, MIX_W), lambda i: (ctx_blk, cb))
    ctx = pl.pallas_call(
        _na_ctx_kernel,
        grid=(1,),
        in_specs=[cblk(CB_NA_Q), cblk(CB_NA_K), cblk(CB_NA_V)],
        out_specs=pl.BlockSpec((lc, MIX_W), lambda i: (0, 0)),
        out_shape=jax.ShapeDtypeStruct((lc, MIX_W), BF16),
        compiler_params=_cparams(('arbitrary',)), name='na_ctx',
    )(p, p, p)
    return jnp.concatenate([lat, ctx], axis=0)


def _merge_kernel(a0, a1, a2, a3, g0, g1, g2, g3, w_ref, o_ref, wb_ref):
    @pl.when(pl.program_id(1) == 0)
    def _():
        wb_ref[...] = w_ref[...].astype(BF16)
    acc = None
    for n, (a_ref, g_ref) in enumerate(((a0, g0), (a1, g1), (a2, g2), (a3, g3))):
        zn = jnp.dot(a_ref[...], wb_ref[n], preferred_element_type=F32) * _sigmoid(g_ref[...])
        acc = zn if acc is None else acc + zn
    o_ref[...] = acc.astype(BF16)


def _merge(ys, p, w_br):
    tt = p.shape[0]
    bn = 512
    bm = _pick(tt, (768, 384, 128))
    a_s = pl.BlockSpec((bm, MIX_W), lambda j, i: (i, 0))
    gs = [pl.BlockSpec((bm, bn), functools.partial(lambda j, i, n: (i, (GATE_OFF + n * D_MODEL) // bn + j), n=n))
          for n in range(N_BRANCH)]
    return pl.pallas_call(
        _merge_kernel,
        grid=(D_MODEL // bn, tt // bm),
        in_specs=[a_s] * 4 + gs + [pl.BlockSpec((N_BRANCH, MIX_W, bn), lambda j, i: (0, 0, j))],
        out_specs=pl.BlockSpec((bm, bn), lambda j, i: (i, j)),
        out_shape=jax.ShapeDtypeStruct((tt, D_MODEL), BF16),
        scratch_shapes=[pltpu.VMEM((N_BRANCH, MIX_W, bn), BF16)],
        compiler_params=_cparams(('arbitrary', 'arbitrary')), name='merge',
    )(*ys, p, p, p, p, w_br)


def _reorder_w_in(w):
    off, pos = 0, {}
    for name, width in _REF_SPLITS:
        pos[name] = (off, width)
        off += width
    cols = []
    for name, width in _IN_ORDER:
        if name == 'pad':
            cols.append(jnp.zeros((w.shape[0], width), w.dtype))
        else:
            o, wd = pos[name]
            assert wd == width
            cols.append(w[:, o:o + wd])
    return jnp.concatenate(cols, axis=1)


def _block_diag_ones(width):
    i = jnp.arange(MIX_W) // width
    return (i[:, None] == i[None, :]).astype(BF16)


def _rope_tables(t, lc):
    tok = jnp.arange(t)
    pos = jnp.stack([tok // GRID_W, tok % GRID_W], axis=-1).astype(F32)
    nf = GLA_DK // 4
    inv = ROPE_BASE ** (-jnp.arange(nf, dtype=F32) / nf)
    ang = pos[:, :, None] * inv
    cos = jnp.repeat(jnp.cos(ang), 2, axis=-1).reshape(t, GLA_DK)
    sin = jnp.sin(ang)
    sin = jnp.stack([-sin, sin], axis=-1).reshape(t, GLA_DK)
    reps = 2 * GLA_HEADS
    cos = jnp.concatenate([jnp.tile(cos, (1, reps)), jnp.ones((lc, reps * GLA_DK), F32)], axis=0)
    sin = jnp.concatenate([jnp.tile(sin, (1, reps)), jnp.zeros((lc, reps * GLA_DK), F32)], axis=0)
    return cos, sin


def _pad_rank_rows(w_up, rank, rows):
    out = jnp.zeros((2, rows, w_up.shape[-1]), w_up.dtype)
    for d in range(2):
        out = out.at[d, d * rank:(d + 1) * rank].set(w_up[d])
    return out


def kernel(x, c, ctx, c_ctx, ada_w, ada_b, norm_g, w_in, rw_mu, rw_w0, rw_w_up, rw_a0, rw_a_up, rw_g_up, rw_k_k, rw_k_a, rw_r_k, rw_ln_g, rw_ln_b, gla_a_up, gla_a_b, gla_gn_g, sgu_ln_g, sgu_ln_b, sgu_w, sgu_b, na_rpb, w_br, w_o, ffn_w1, ffn_w3, ffn_w2, moe_router, moe_w1, moe_w3, moe_w2):
    assert x.shape[0] == 1 and x.shape[2] == D_MODEL
    t, lc = x.shape[1], ctx.shape[1]
    depth = ada_w.shape[0]
    assert t % max(lc, SGU_CHUNK) == 0 and lc % SGU_CHUNK == 0 and t % GRID_W == 0
    rows = t // GRID_W
    wh = min(NA_WIN_H, rows)
    xs = jnp.concatenate([x[0], ctx[0]], axis=0)
    cond8 = jnp.zeros((8, D_MODEL), F32).at[0].set(c[0]).at[1].set(c_ctx)
    e64 = _block_diag_ones(RW_HEAD)
    e128 = _block_diag_ones(GLA_DV)
    cos_t, sin_t = _rope_tables(t, lc)
    row1 = lambda v: v.reshape(1, -1)

    for i in range(depth):
        mods = _adaln(cond8, ada_w[i], ada_b[i])[0:2].reshape(2, 6, D_MODEL)
        h = _norm_mod_call(xs, norm_g[i, 0], mods, 0, t)
        p = _matmul(h, _reorder_w_in(w_in[i]), 1024, (768, 384, 128), name='in_proj')
        mu2 = jnp.stack([rw_mu[i, :, 0].reshape(-1), rw_mu[i, :, 1].reshape(-1)])
        rw_y, rw_bonus = _rwkv_scan(
            p, t, lc, mu2, rw_w0[i].reshape(2, 1, MIX_W), _pad_rank_rows(rw_w_up[i], RW_DECAY_RANK, LANE),
            rw_a0[i].reshape(2, 1, MIX_W), _pad_rank_rows(rw_a_up[i], RW_ICLR_RANK, LANE),
            row1(rw_k_k[i]), row1(rw_k_a[i]), row1(rw_r_k[i]), e64)
        gla_o = _gla_scan(p, t, lc, cos_t, sin_t, _pad_rank_rows(gla_a_up[i], GLA_GATE_RANK, LANE),
                          gla_a_b[i].reshape(2, 1, 2 * LANE))
        y_a, y_b = _mix_finish(p, rw_y, rw_bonus, gla_o, rw_g_up[i], row1(rw_ln_g[i]), row1(rw_ln_b[i]),
                               row1(gla_gn_g[i]), e64, e128, t)
        y_s = _sgu(p, row1(sgu_ln_g[i]), row1(sgu_ln_b[i]), sgu_w[i], jnp.repeat(sgu_b[i].T, 64, axis=1))
        y_d = _na(p, _na_bias(na_rpb[i], wh), t, lc)
        z = _merge((y_a, y_b, y_s, y_d), p, w_br[i])
        y = _matmul(z, w_o[i], 1024, (768, 384, 128), name='out_proj')
        xs = _resid_call(xs, y, norm_g[i, 1], mods, 2, t)
        j = i // 2
        if i % 2 == 0:
            h = _norm_mod_call(xs, norm_g[i, 2], mods, 3, t)
            f = _ffn_down(_ffn_up(h, ffn_w1, ffn_w3, j), ffn_w2, j)
        else:
            h, gates = _norm_mod_call(xs, norm_g[i, 2], mods, 3, t, router=moe_router[j])
            f = jnp.zeros_like(xs)
            for e in range(N_EXPERTS):
                f = _ffn_down(_ffn_up(h, moe_w1[j], moe_w3[j], e), moe_w2[j], e, gates=gates, prev=f)
        xs = _resid_call(xs, f, norm_g[i, 3], mods, 5, t)
    return xs[0:t][None]
```

```python
import functools

import jax
import jax.numpy as jnp
from jax import lax
from jax.experimental import pallas as pl
from jax.experimental.pallas import tpu as pltpu

F32 = jnp.float32
BF16 = jnp.bfloat16

D_MODEL = 2048
GRID_W = 64
N_BRANCH = 4
MIX_W = D_MODEL // 4
RW_HEAD = 64
RW_HEADS = MIX_W // RW_HEAD
RW_DECAY_RANK = 64
RW_ICLR_RANK = 64
RW_GATE_RANK = 128
RW_GN_EPS = 64e-5
GLA_HEADS = 4
GLA_DV = MIX_W // GLA_HEADS
GLA_DK = GLA_DV // 2
GLA_GATE_RANK = 16
GLA_TAU = 16.0
GN_EPS = 1e-5
ROPE_BASE = 10000.0
SGU_GROUPS = MIX_W // 64
SGU_CHUNK = 128
NA_HEAD = 64
NA_HEADS = MIX_W // NA_HEAD
NA_WIN_H = 8
NA_WIN_W = 16
FFN_DIM = 7 * D_MODEL // 2
N_EXPERTS = 8
TOP_K = 2
EPS = 1e-6
NEG_INF = -1e30

RT_G1, RT_G2, RT_I1, RT_I2 = 8, 9, 10, 11
MOE_BM = 512
CHUNK = 64
LANE = 128
VMEM_LIMIT = 56 * 2 ** 20

_IN_ORDER = (
    ('rw_r', 512), ('rw_k', 512), ('rw_v', 512), ('gla_v', 512), ('gla_r', 512),
    ('sgu_u', 512), ('sgu_v', 512), ('na_q', 512), ('na_k', 512), ('na_v', 512),
    ('gla_q', 256), ('gla_k', 256), ('rw_wd', 128), ('rw_ad', 128), ('rw_gd', 128),
    ('gla_ad', 32), ('pad', 96), ('gate', N_BRANCH * D_MODEL),
)
_REF_SPLITS = (
    ('rw_r', MIX_W), ('rw_k', MIX_W), ('rw_v', MIX_W),
    ('rw_wd', 2 * RW_DECAY_RANK), ('rw_ad', 2 * RW_ICLR_RANK), ('rw_gd', RW_GATE_RANK),
    ('gla_q', GLA_HEADS * GLA_DK), ('gla_k', GLA_HEADS * GLA_DK), ('gla_v', MIX_W),
    ('gla_r', MIX_W), ('gla_ad', 2 * GLA_GATE_RANK),
    ('sgu_u', MIX_W), ('sgu_v', MIX_W),
    ('na_q', MIX_W), ('na_k', MIX_W), ('na_v', MIX_W),
    ('gate', N_BRANCH * D_MODEL),
)
N_IN_P = sum(w for _, w in _IN_ORDER)
CB_GLA_V, CB_GLA_R, CB_SGU_U, CB_SGU_V, CB_NA_Q, CB_NA_K, CB_NA_V, CB_GLA_QK, CB_SMALL = 3, 4, 5, 6, 7, 8, 9, 10, 11
GATE_OFF = 6144


def _cparams(sem):
    return pltpu.CompilerParams(dimension_semantics=sem, vmem_limit_bytes=VMEM_LIMIT)


def _pick(m, cands):
    for c in cands:
        if m % c == 0:
            return c
    raise ValueError(f'no tile for {m}')


def _mm(a, b):
    return jnp.dot(a.astype(BF16), b.astype(BF16), preferred_element_type=F32)


def _mm_nt(a, b):
    return lax.dot_general(a.astype(BF16), b.astype(BF16), (((1,), (1,)), ((), ())),
                           preferred_element_type=F32)


def _mm_tn(a, b):
    return lax.dot_general(a.astype(BF16), b.astype(BF16), (((0,), (0,)), ((), ())),
                           preferred_element_type=F32)


def _split2(x):
    hi = x.astype(BF16)
    lo = (x - hi.astype(F32)).astype(BF16)
    return hi, lo


def _mm_xw(x, w_bf):
    hi, lo = _split2(x)
    return (jnp.dot(hi, w_bf, preferred_element_type=F32)
            + jnp.dot(lo, w_bf, preferred_element_type=F32))


def _mm_wx(w_bf, x):
    hi, lo = _split2(x)
    return (jnp.dot(w_bf, hi, preferred_element_type=F32)
            + jnp.dot(w_bf, lo, preferred_element_type=F32))


def _mm3(a, b):
    ah, al = _split2(a)
    bh, bl = _split2(b)
    return (jnp.dot(ah, bh, preferred_element_type=F32)
            + jnp.dot(ah, bl, preferred_element_type=F32)
            + jnp.dot(al, bh, preferred_element_type=F32))


def _sigmoid(x):
    return 1.0 / (1.0 + jnp.exp(-x))


def _softplus(x):
    return jnp.maximum(x, 0.0) + jnp.log1p(jnp.exp(-jnp.abs(x)))


def _gelu(x):
    return 0.5 * x * (1.0 + lax.erf(x * (0.5 ** 0.5)))


def _order_masks(d):
    t = lax.broadcasted_iota(jnp.int32, (CHUNK, CHUNK), 0)
    s = lax.broadcasted_iota(jnp.int32, (CHUNK, CHUNK), 1)
    diff = (t - s) * jnp.where(d == 0, 1, -1)
    return diff > 0, diff >= 0


def _chunk_of(d, n, nl, nc):
    fwd = jnp.where(n < nc, nl + n, n - nc)
    bwd = jnp.where(n < nc, nl + nc - 1 - n, nl - 1 - (n - nc))
    return jnp.where(d == 0, fwd, bwd)


def _ada_kernel(c_ref, w_ref, b_ref, o_ref):
    cnd = c_ref[...]
    a = cnd * _sigmoid(cnd)
    o_ref[...] = _mm(a, w_ref[...]) + b_ref[...]


def _adaln(cond8, ada_w, ada_b):
    n = ada_w.shape[1]
    bn = 1536
    return pl.pallas_call(
        _ada_kernel,
        grid=(n // bn,),
        in_specs=[pl.BlockSpec((8, D_MODEL), lambda j: (0, 0)),
                  pl.BlockSpec((D_MODEL, bn), lambda j: (0, j)),
                  pl.BlockSpec((1, bn), lambda j: (0, j))],
        out_specs=pl.BlockSpec((8, bn), lambda j: (0, j)),
        out_shape=jax.ShapeDtypeStruct((8, n), F32),
        compiler_params=_cparams(('arbitrary',)),
        name='adaln',
    )(cond8, ada_w, ada_b.reshape(1, n))


def _norm_mod(x, g, m, k0):
    y = x * lax.rsqrt(jnp.mean(x * x, axis=-1, keepdims=True) + EPS) * g
    return y * (1.0 + m[k0 + 1:k0 + 2]) + m[k0:k0 + 1]


def _norm_mod_kernel(x_ref, g_ref, m_ref, o_ref, *, k0):
    o_ref[...] = _norm_mod(x_ref[...], g_ref[...], m_ref[0], k0).astype(BF16)


def _norm_mod_route_kernel(x_ref, g_ref, m_ref, r_ref, o_ref, gate_ref, *, k0):
    h = _norm_mod(x_ref[...], g_ref[...], m_ref[0], k0)
    o_ref[...] = h.astype(BF16)
    logits = _mm3(h, r_ref[...])
    lane = lax.broadcasted_iota(jnp.int32, logits.shape, 1)
    neg = jnp.float32(-jnp.inf)
    l1 = jnp.where(lane < N_EXPERTS, logits, neg)
    m1 = jnp.max(l1, axis=-1, keepdims=True)
    i1 = jnp.min(jnp.where(l1 == m1, lane, LANE), axis=-1, keepdims=True)
    l2 = jnp.where(lane == i1, neg, l1)
    m2 = jnp.max(l2, axis=-1, keepdims=True)
    i2 = jnp.min(jnp.where(l2 == m2, lane, LANE), axis=-1, keepdims=True)
    e2 = jnp.exp(m2 - m1)
    den = 1.0 + e2
    sel = jnp.where(jnp.logical_or(lane == i1, lane == i2), 1.0, 0.0)
    sel = jnp.where(lane == RT_G1, 1.0 / den, jnp.where(lane == RT_G2, e2 / den, sel))
    gate_ref[...] = jnp.where(lane == RT_I1, i1.astype(F32), jnp.where(lane == RT_I2, i2.astype(F32), sel))


def _row_tile(t, lc):
    return _pick(lc, (256, 128))


def _norm_mod_call(x, g, mods, k0, t, router=None):
    tt = x.shape[0]
    bm = _row_tile(t, tt - t)
    nlt = t // bm
    xs = pl.BlockSpec((bm, D_MODEL), lambda i: (i, 0))
    gs = pl.BlockSpec((1, D_MODEL), lambda i: (0, 0))
    ms = pl.BlockSpec((1, 6, D_MODEL), lambda i: (jnp.where(i >= nlt, 1, 0), 0, 0))
    if router is None:
        return pl.pallas_call(
            functools.partial(_norm_mod_kernel, k0=k0),
            grid=(tt // bm,), in_specs=[xs, gs, ms], out_specs=xs,
            out_shape=jax.ShapeDtypeStruct((tt, D_MODEL), BF16),
            compiler_params=_cparams(('parallel',)), name='norm_mod',
        )(x, g.reshape(1, D_MODEL), mods)
    rpad = jnp.pad(router, ((0, 0), (0, LANE - N_EXPERTS)))
    return pl.pallas_call(
        functools.partial(_norm_mod_route_kernel, k0=k0),
        grid=(tt // bm,),
        in_specs=[xs, gs, ms, pl.BlockSpec((D_MODEL, LANE), lambda i: (0, 0))],
        out_specs=[xs, pl.BlockSpec((bm, LANE), lambda i: (i, 0))],
        out_shape=[jax.ShapeDtypeStruct((tt, D_MODEL), BF16), jax.ShapeDtypeStruct((tt, LANE), F32)],
        compiler_params=_cparams(('parallel',)), name='norm_mod_route',
    )(x, g.reshape(1, D_MODEL), mods, rpad)


def _resid_kernel(x_ref, y_ref, g_ref, m_ref, o_ref, *, kg):
    y = y_ref[...]
    n = y * lax.rsqrt(jnp.mean(y * y, axis=-1, keepdims=True) + EPS) * g_ref[...]
    o_ref[...] = x_ref[...] + m_ref[0][kg:kg + 1] * n


def _resid_call(x, y, g, mods, kg, t):
    tt = x.shape[0]
    bm = _row_tile(t, tt - t)
    nlt = t // bm
    xs = pl.BlockSpec((bm, D_MODEL), lambda i: (i, 0))
    return pl.pallas_call(
        functools.partial(_resid_kernel, kg=kg),
        grid=(tt // bm,),
        in_specs=[xs, xs, pl.BlockSpec((1, D_MODEL), lambda i: (0, 0)),
                  pl.BlockSpec((1, 6, D_MODEL), lambda i: (jnp.where(i >= nlt, 1, 0), 0, 0))],
        out_specs=xs,
        out_shape=jax.ShapeDtypeStruct((tt, D_MODEL), F32),
        compiler_params=_cparams(('parallel',)), name='resid_norm',
    )(x, y, g.reshape(1, D_MODEL), mods)


def _mm_kernel(a_ref, w_ref, o_ref, wb_ref):
    @pl.when(pl.program_id(1) == 0)
    def _():
        wb_ref[...] = w_ref[...].astype(BF16)
    o_ref[...] = jnp.dot(a_ref[...], wb_ref[...], preferred_element_type=F32).astype(o_ref.dtype)


def _matmul(a, w, bn, bm_cands, out_dtype=F32, name='matmul'):
    m, k = a.shape
    n = w.shape[1]
    bm = _pick(m, bm_cands)
    return pl.pallas_call(
        _mm_kernel,
        grid=(n // bn, m // bm),
        in_specs=[pl.BlockSpec((bm, k), lambda j, i: (i, 0)),
                  pl.BlockSpec((k, bn), lambda j, i: (0, j))],
        out_specs=pl.BlockSpec((bm, bn), lambda j, i: (i, j)),
        out_shape=jax.ShapeDtypeStruct((m, n), out_dtype),
        scratch_shapes=[pltpu.VMEM((k, bn), BF16)],
        compiler_params=_cparams(('arbitrary', 'arbitrary')), name=name,
    )(a, w)


def _ffn_up_kernel(a_ref, w1_ref, w3_ref, o_ref, w1b, w3b):
    @pl.when(pl.program_id(1) == 0)
    def _():
        w1b[...] = w1_ref[...].astype(BF16)
        w3b[...] = w3_ref[...].astype(BF16)
    a = a_ref[...]
    h1 = jnp.dot(a, w1b[...], preferred_element_type=F32)
    h3 = jnp.dot(a, w3b[...], preferred_element_type=F32)
    o_ref[...] = (h1 * _sigmoid(h1) * h3).astype(BF16)


def _ffn_up(h, w1, w3, e):
    m = h.shape[0]
    bn = 512
    bm = _pick(m, (768, 384, 128))
    ws = pl.BlockSpec((None, D_MODEL, bn), lambda j, i: (e, 0, j))
    return pl.pallas_call(
        _ffn_up_kernel,
        grid=(FFN_DIM // bn, m // bm),
        in_specs=[pl.BlockSpec((bm, D_MODEL), lambda j, i: (i, 0)), ws, ws],
        out_specs=pl.BlockSpec((bm, bn), lambda j, i: (i, j)),
        out_shape=jax.ShapeDtypeStruct((m, FFN_DIM), BF16),
        scratch_shapes=[pltpu.VMEM((D_MODEL, bn), BF16), pltpu.VMEM((D_MODEL, bn), BF16)],
        compiler_params=_cparams(('arbitrary', 'arbitrary')), name='ffn_up',
    )(h, w1, w3)


def _ffn_down_kernel(a_ref, w_ref, o_ref, wb_ref):
    @pl.when(pl.program_id(1) == 0)
    def _():
        wb_ref[...] = w_ref[...].astype(BF16)
    o_ref[...] = jnp.dot(a_ref[...], wb_ref[...], preferred_element_type=F32)


def _ffn_down(u, w2, e):
    m = u.shape[0]
    bn = 256
    bm = _pick(m, (384, 128))
    return pl.pallas_call(
        _ffn_down_kernel,
        grid=(D_MODEL // bn, m // bm),
        in_specs=[pl.BlockSpec((bm, FFN_DIM), lambda j, i: (i, 0)),
                  pl.BlockSpec((None, FFN_DIM, bn), lambda j, i: (e, 0, j))],
        out_specs=pl.BlockSpec((bm, bn), lambda j, i: (i, j)),
        out_shape=jax.ShapeDtypeStruct((m, D_MODEL), F32),
        scratch_shapes=[pltpu.VMEM((FFN_DIM, bn), BF16)],
        compiler_params=_cparams(('arbitrary', 'arbitrary')), name='ffn_down',
    )(u, w2)


def _moe_rank_kernel(r_ref, dest_ref, cnt_ref, carry_ref, tot_ref):
    ph = pl.program_id(0)
    i = pl.program_id(1)
    r = r_ref[...]
    bm = r.shape[0]
    lane = lax.broadcasted_iota(jnp.int32, (1, LANE), 1)
    oh = jnp.where(lane < N_EXPERTS, r, 0.0)
    colsum = jnp.sum(oh, axis=0, keepdims=True)

    @pl.when(jnp.logical_and(ph == 0, i == 0))
    def _():
        carry_ref[...] = jnp.zeros_like(carry_ref)

    @pl.when(jnp.logical_and(ph == 1, i == 0))
    def _():
        tot_ref[...] = carry_ref[...]
        carry_ref[...] = jnp.zeros_like(carry_ref)

    @pl.when(ph == 1)
    def _():
        tot = tot_ref[...]
        padded = jnp.floor((tot + (MOE_BM - 1)) * (1.0 / MOE_BM)) * MOE_BM
        a = lax.broadcasted_iota(jnp.int32, (LANE, LANE), 0)
        b = lax.broadcasted_iota(jnp.int32, (LANE, LANE), 1)
        upper = jnp.where(a < b, 1.0, 0.0).astype(BF16)
        offs = _mm_xw(jnp.broadcast_to(padded, (8, LANE)), upper)[0:1]
        tr = lax.broadcasted_iota(jnp.int32, (bm, bm), 0)
        ts = lax.broadcasted_iota(jnp.int32, (bm, bm), 1)
        before = jnp.dot(jnp.where(ts < tr, 1.0, 0.0).astype(BF16), oh.astype(BF16),
                         preferred_element_type=F32)
        slot = before + carry_ref[...] + offs
        lane_f = lane.astype(F32)
        d1 = jnp.sum(jnp.where(lane_f == r[:, RT_I1:RT_I1 + 1], slot, 0.0), axis=-1, keepdims=True)
        d2 = jnp.sum(jnp.where(lane_f == r[:, RT_I2:RT_I2 + 1], slot, 0.0), axis=-1, keepdims=True)
        dest_ref[...] = jnp.where(lane == 0, d1, jnp.where(lane == 1, d2, 0.0)).astype(jnp.int32)
        cnt_ref[...] = jnp.broadcast_to(tot, (8, LANE))

    carry_ref[...] = carry_ref[...] + colsum


def _moe_rank(route, bm):
    tt = route.shape[0]
    return pl.pallas_call(
        _moe_rank_kernel,
        grid=(2, tt // bm),
        in_specs=[pl.BlockSpec((bm, LANE), lambda ph, i: (i, 0))],
        out_specs=[pl.BlockSpec((bm, LANE), lambda ph, i: (i * ph, 0)),
                   pl.BlockSpec((8, LANE), lambda ph, i: (0, 0))],
        out_shape=[jax.ShapeDtypeStruct((tt, LANE), jnp.int32), jax.ShapeDtypeStruct((8, LANE), F32)],
        scratch_shapes=[pltpu.VMEM((1, LANE), F32), pltpu.VMEM((1, LANE), F32)],
        compiler_params=_cparams(('arbitrary', 'arbitrary')), name='moe_rank',
    )(route)


def _row_copy(src, dst, sem):
    return pltpu.make_async_copy(src, dst, sem)


def _moe_scatter_kernel(dest_ref, h_hbm, init_hbm, xs_hbm, sem, *, bm):
    del init_hbm
    base = pl.program_id(0) * bm

    def issue(r, carry):
        t = base + r
        for kk in range(TOP_K):
            _row_copy(h_hbm.at[pl.ds(t, 1)], xs_hbm.at[pl.ds(dest_ref[TOP_K * t + kk], 1)], sem).start()
        return carry
    lax.fori_loop(0, bm, issue, 0)

    def drain(r, carry):
        for kk in range(TOP_K):
            _row_copy(h_hbm.at[pl.ds(0, 1)], xs_hbm.at[pl.ds(0, 1)], sem).wait()
        return carry
    lax.fori_loop(0, bm, drain, 0)


def _moe_scatter(dest_flat, h32, ns, bm):
    tt, w = h32.shape
    return pl.pallas_call(
        functools.partial(_moe_scatter_kernel, bm=bm),
        grid_spec=pltpu.PrefetchScalarGridSpec(
            num_scalar_prefetch=1, grid=(tt // bm,),
            in_specs=[pl.BlockSpec(memory_space=pl.ANY), pl.BlockSpec(memory_space=pl.ANY)],
            out_specs=pl.BlockSpec(memory_space=pl.ANY),
            scratch_shapes=[pltpu.SemaphoreType.DMA(())]),
        out_shape=jax.ShapeDtypeStruct((ns, w), h32.dtype),
        input_output_aliases={2: 0},
        compiler_params=_cparams(('arbitrary',)), name='moe_scatter',
    )(dest_flat, h32, jnp.zeros((ns, w), h32.dtype))


def _moe_up_kernel(te_ref, nv_ref, a_ref, w1_ref, w3_ref, o_ref, w1b, w3b):
    m = pl.program_id(1)

    @pl.when(jnp.logical_or(m == 0, te_ref[m] != te_ref[jnp.maximum(m - 1, 0)]))
    def _():
        w1b[...] = w1_ref[...].astype(BF16)
        w3b[...] = w3_ref[...].astype(BF16)

    @pl.when(m < nv_ref[0])
    def _():
        a = a_ref[...]
        h1 = jnp.dot(a, w1b[...], preferred_element_type=F32)
        h3 = jnp.dot(a, w3b[...], preferred_element_type=F32)
        o_ref[...] = (h1 * _sigmoid(h1) * h3).astype(BF16)

    @pl.when(m >= nv_ref[0])
    def _():
        o_ref[...] = jnp.zeros_like(o_ref)


def _moe_down_kernel(te_ref, nv_ref, a_ref, w_ref, o_ref, wb_ref):
    m = pl.program_id(1)

    @pl.when(jnp.logical_or(m == 0, te_ref[m] != te_ref[jnp.maximum(m - 1, 0)]))
    def _():
        wb_ref[...] = w_ref[...].astype(BF16)

    @pl.when(m < nv_ref[0])
    def _():
        o_ref[...] = jnp.dot(a_ref[...], wb_ref[...], preferred_element_type=F32)

    @pl.when(m >= nv_ref[0])
    def _():
        o_ref[...] = jnp.zeros_like(o_ref)


def _moe_grouped_ffn(tile_expert, n_valid, xs, w1, w3, w2):
    ns = xs.shape[0]
    n_tiles = ns // MOE_BM
    bn = 512
    ws = pl.BlockSpec((None, D_MODEL, bn), lambda j, m, te, nv: (te[m], 0, j))
    u = pl.pallas_call(
        _moe_up_kernel,
        grid_spec=pltpu.PrefetchScalarGridSpec(
            num_scalar_prefetch=2, grid=(FFN_DIM // bn, n_tiles),
            in_specs=[pl.BlockSpec((MOE_BM, D_MODEL), lambda j, m, te, nv: (m, 0)), ws, ws],
            out_specs=pl.BlockSpec((MOE_BM, bn), lambda j, m, te, nv: (m, j)),
            scratch_shapes=[pltpu.VMEM((D_MODEL, bn), BF16), pltpu.VMEM((D_MODEL, bn), BF16)]),
        out_shape=jax.ShapeDtypeStruct((ns, FFN_DIM), BF16),
        compiler_params=_cparams(('arbitrary', 'arbitrary')), name='moe_up',
    )(tile_expert, n_valid, xs, w1, w3)
    bn = 256
    return pl.pallas_call(
        _moe_down_kernel,
        grid_spec=pltpu.PrefetchScalarGridSpec(
            num_scalar_prefetch=2, grid=(D_MODEL // bn, n_tiles),
            in_specs=[pl.BlockSpec((MOE_BM, FFN_DIM), lambda j, m, te, nv: (m, 0)),
                      pl.BlockSpec((None, FFN_DIM, bn), lambda j, m, te, nv: (te[m], 0, j))],
            out_specs=pl.BlockSpec((MOE_BM, bn), lambda j, m, te, nv: (m, j)),
            scratch_shapes=[pltpu.VMEM((FFN_DIM, bn), BF16)]),
        out_shape=jax.ShapeDtypeStruct((ns, D_MODEL), F32),
        compiler_params=_cparams(('arbitrary', 'arbitrary')), name='moe_down',
    )(tile_expert, n_valid, u, w2)


def _moe_combine_kernel(dest_ref, ys_hbm, r_ref, x_ref, g_ref, m_ref, o_ref, buf, sem, *, bm, kg):
    base = pl.program_id(0) * bm

    def issue(r, carry):
        t = base + r
        for kk in range(TOP_K):
            _row_copy(ys_hbm.at[pl.ds(dest_ref[TOP_K * t + kk], 1)], buf.at[kk, pl.ds(r, 1)], sem).start()
        return carry
    lax.fori_loop(0, bm, issue, 0)

    def drain(r, carry):
        for kk in range(TOP_K):
            _row_copy(ys_hbm.at[pl.ds(0, 1)], buf.at[0, pl.ds(0, 1)], sem).wait()
        return carry
    lax.fori_loop(0, bm, drain, 0)

    r = r_ref[...]
    f = r[:, RT_G1:RT_G1 + 1] * buf[0] + r[:, RT_G2:RT_G2 + 1] * buf[1]
    n = f * lax.rsqrt(jnp.mean(f * f, axis=-1, keepdims=True) + EPS) * g_ref[...]
    o_ref[...] = x_ref[...] + m_ref[0][kg:kg + 1] * n


def _moe_combine(dest_flat, ys, route, x, g, mods, kg, t, bm):
    tt = x.shape[0]
    nlt = t // bm
    xs = pl.BlockSpec((bm, D_MODEL), lambda i, d: (i, 0))
    return pl.pallas_call(
        functools.partial(_moe_combine_kernel, bm=bm, kg=kg),
        grid_spec=pltpu.PrefetchScalarGridSpec(
            num_scalar_prefetch=1, grid=(tt // bm,),
            in_specs=[pl.BlockSpec(memory_space=pl.ANY),
                      pl.BlockSpec((bm, LANE), lambda i, d: (i, 0)), xs,
                      pl.BlockSpec((1, D_MODEL), lambda i, d: (0, 0)),
                      pl.BlockSpec((1, 6, D_MODEL), lambda i, d: (jnp.where(i >= nlt, 1, 0), 0, 0))],
            out_specs=xs,
            scratch_shapes=[pltpu.VMEM((TOP_K, bm, D_MODEL), F32), pltpu.SemaphoreType.DMA(())]),
        out_shape=jax.ShapeDtypeStruct((tt, D_MODEL), F32),
        compiler_params=_cparams(('arbitrary',)), name='moe_combine',
    )(dest_flat, ys, route, x, g.reshape(1, D_MODEL), mods)


def _moe_layer(x, h, route, w1, w3, w2, g, mods, kg, t):
    tt = x.shape[0]
    bm = _row_tile(t, tt - t)
    dest, cnt = _moe_rank(route, bm)
    n_tiles = -(-TOP_K * tt // MOE_BM) + N_EXPERTS
    ns = n_tiles * MOE_BM
    cum = jnp.cumsum((cnt[0, 0:N_EXPERTS].astype(jnp.int32) + (MOE_BM - 1)) // MOE_BM)
    n_valid = cum[-1]
    tile = jnp.arange(n_tiles, dtype=jnp.int32)
    tile_expert = jnp.searchsorted(cum, jnp.minimum(tile, n_valid - 1), side='right').astype(jnp.int32)
    dest_flat = dest[:, 0:TOP_K].reshape(-1)
    pack = 4 // h.dtype.itemsize
    h32 = lax.bitcast_convert_type(h.reshape(tt, D_MODEL // pack, pack), jnp.uint32).reshape(tt, D_MODEL // pack)
    xs32 = _moe_scatter(dest_flat, h32, ns, bm)
    xs = lax.bitcast_convert_type(xs32.reshape(ns, D_MODEL // pack, 1), h.dtype).reshape(ns, D_MODEL)
    ys = _moe_grouped_ffn(tile_expert, n_valid.reshape(1), xs, w1, w3, w2)
    return _moe_combine(dest_flat, ys, route, x, g, mods, kg, t, bm)


def _rwkv_kernel(cur_ref, prv_ref, nxt_ref, sm_ref, mu_ref, w0_ref, wup_ref, a0_ref, aup_ref,
                 kk_ref, ka_ref, rk_ref, e_ref, y_ref, bonus_ref, s_ref, *, nl, nc):
    d = pl.program_id(0)
    n = pl.program_id(1)
    ch = _chunk_of(d, n, nl, nc)

    @pl.when(n == 0)
    def _():
        s_ref[...] = jnp.zeros_like(s_ref)

    x = cur_ref[...]
    first = jnp.logical_or(ch == 0, ch == nl)
    last = jnp.logical_or(ch == nl - 1, ch == nl + nc - 1)
    p_row = jnp.where(first, 0.0, prv_ref[7:8, :])
    n_row = jnp.where(last, 0.0, nxt_ref[0:1, :])
    row = lax.broadcasted_iota(jnp.int32, (CHUNK, 1), 0)
    prev = jnp.where(row == 0, p_row, pltpu.roll(x, 1, axis=0))
    nxt = jnp.where(row == CHUNK - 1, n_row, pltpu.roll(x, CHUNK - 1, axis=0))
    z = x + mu_ref[0:1, :] * (prev - x) + mu_ref[1:2, :] * (nxt - x)
    r, k, v = z[:, 0:MIX_W], z[:, MIX_W:2 * MIX_W], z[:, 2 * MIX_W:3 * MIX_W]

    sm = sm_ref[...]
    e_bf = e_ref[...]
    wd = jnp.tanh(sm[:, 0:128])
    w_log = -_softplus(-(w0_ref[...] + _mm3(wd, wup_ref[...]))) - 0.5
    logw = -jnp.exp(w_log)
    a = _sigmoid(a0_ref[...] + _mm3(sm[:, 128:256], aup_ref[...]))
    kkr = k * kk_ref[...]
    kk = kkr * lax.rsqrt(jnp.maximum(_mm_xw(kkr * kkr, e_bf), 1e-12))
    k_dir = k * (1.0 + (a - 1.0) * ka_ref[...])
    bonus_ref[...] = _mm_xw(r * k_dir * rk_ref[...], e_bf) * v

    strict, incl = _order_masks(d)
    b_inc = _mm_wx(jnp.where(incl, 1.0, 0.0).astype(BF16), logw)
    b_exc = b_inc - logw
    b_last = jnp.sum(logw, axis=0, keepdims=True)
    beta = kk * a
    ea = -kk * jnp.exp(b_exc)
    er = r * jnp.exp(b_inc)
    ninv = jnp.exp(-b_inc)
    eb = beta * ninv
    ek = k_dir * ninv
    eend = jnp.exp(b_last - b_inc)
    hb = beta * eend
    hk = k_dir * eend
    gam = jnp.exp(b_last)

    zeros = jnp.zeros((CHUNK, RW_HEAD), F32)
    for h in range(RW_HEADS):
        sl = slice(h * RW_HEAD, (h + 1) * RW_HEAD)
        v_h = v[:, sl]
        left = jnp.concatenate([ea[:, sl], er[:, sl]], axis=0)
        right = jnp.concatenate([eb[:, sl], ek[:, sl]], axis=0)
        amat = _mm_nt(left, right)
        m_ab = jnp.where(strict, amat[0:CHUNK, 0:CHUNK], 0.0)
        m_ak = jnp.where(strict, amat[0:CHUNK, CHUNK:], 0.0)
        n_rb = jnp.where(incl, amat[CHUNK:, 0:CHUNK], 0.0)
        n_rk = jnp.where(incl, amat[CHUNK:, CHUNK:], 0.0)
        xs = jnp.concatenate([ea[:, sl], _mm(m_ak, v_h)], axis=1)
        mp = m_ab
        for it in range(6):
            xs = xs + _mm(mp, xs)
            if it < 5:
                mp = _mm(mp, mp)
        zmat = jnp.concatenate([xs, jnp.concatenate([zeros, v_h], axis=1)], axis=0)
        qy = _mm(jnp.concatenate([n_rb, n_rk], axis=1), zmat)
        q_hat = er[:, sl] + qy[:, 0:RW_HEAD]
        x1 = jnp.concatenate([hb[:, sl], hk[:, sl]], axis=0)
        gbt = _mm_tn(zmat, x1)
        s0 = s_ref[h]
        y_ref[:, sl] = _mm_nt(q_hat, s0) + qy[:, RW_HEAD:]
        s_ref[h] = s0 * gam[:, sl] + _mm(s0, gbt[0:RW_HEAD]) + gbt[RW_HEAD:]


def _rwkv_scan(p, t, lc, mu2, w0, wup_pad, a0, aup_pad, k_k, k_a, r_k, e64):
    tt = t + lc
    nl, nc = t // CHUNK, lc // CHUNK
    nch = nl + nc
    last8 = tt // 8 - 1
    ch = functools.partial(_chunk_of, nl=nl, nc=nc)
    vec = pl.BlockSpec((1, MIX_W), lambda d, n: (0, 0))
    out_s = pl.BlockSpec((None, CHUNK, MIX_W), lambda d, n: (d, ch(d, n), 0))
    return pl.pallas_call(
        functools.partial(_rwkv_kernel, nl=nl, nc=nc),
        grid=(2, nch),
        in_specs=[
            pl.BlockSpec((CHUNK, 3 * MIX_W), lambda d, n: (ch(d, n), 0)),
            pl.BlockSpec((8, 3 * MIX_W), lambda d, n: (jnp.maximum(ch(d, n) * 8 - 1, 0), 0)),
            pl.BlockSpec((8, 3 * MIX_W), lambda d, n: (jnp.minimum(ch(d, n) * 8 + 8, last8), 0)),
            pl.BlockSpec((CHUNK, MIX_W), lambda d, n: (ch(d, n), CB_SMALL)),
            pl.BlockSpec((2, 3 * MIX_W), lambda d, n: (0, 0)),
            pl.BlockSpec((None, 1, MIX_W), lambda d, n: (d, 0, 0)),
            pl.BlockSpec((None, LANE, MIX_W), lambda d, n: (d, 0, 0)),
            pl.BlockSpec((None, 1, MIX_W), lambda d, n: (d, 0, 0)),
            pl.BlockSpec((None, LANE, MIX_W), lambda d, n: (d, 0, 0)),
            vec, vec, vec,
            pl.BlockSpec((MIX_W, MIX_W), lambda d, n: (0, 0)),
        ],
        out_specs=[out_s, out_s],
        out_shape=[jax.ShapeDtypeStruct((2, tt, MIX_W), F32)] * 2,
        scratch_shapes=[pltpu.VMEM((RW_HEADS, RW_HEAD, RW_HEAD), F32)],
        compiler_params=_cparams(('arbitrary', 'arbitrary')), name='rwkv_scan',
    )(p, p, p, p, mu2, w0, wup_pad, a0, aup_pad, k_k, k_a, r_k, e64)


def _gla_kernel(qk_ref, v_ref, sm_ref, cos_ref, sin_ref, aup_ref, ab_ref, o_ref, s_ref, *, nl, nc):
    d = pl.program_id(0)
    n = pl.program_id(1)

    @pl.when(n == 0)
    def _():
        s_ref[...] = jnp.zeros_like(s_ref)

    qk = qk_ref[...]
    lane = lax.broadcasted_iota(jnp.int32, (CHUNK, 2 * GLA_HEADS * GLA_DK), 1)
    partner = jnp.where((lane & 1) == 0, pltpu.roll(qk, 2 * GLA_HEADS * GLA_DK - 1, axis=1),
                        pltpu.roll(qk, 1, axis=1))
    qk = qk * cos_ref[...] + partner * sin_ref[...]
    hk = GLA_HEADS * GLA_DK
    q = qk[:, 0:hk] * (GLA_DK ** -0.5)
    k = qk[:, hk:]
    v = v_ref[...]
    g = -_softplus(-(_mm3(sm_ref[:, 384:512], aup_ref[...]) + ab_ref[...])) / GLA_TAU
    _, incl = _order_masks(d)
    b = _mm_wx(jnp.where(incl, 1.0, 0.0).astype(BF16), g)
    b_last = jnp.sum(g, axis=0, keepdims=True)
    q_e = q * jnp.exp(b)
    k_e = k * jnp.exp(-b)
    k_end = k * jnp.exp(b_last - b)
    dec = jnp.exp(b_last)
    for h in range(GLA_HEADS):
        sk = slice(h * GLA_DK, (h + 1) * GLA_DK)
        sv = slice(h * GLA_DV, (h + 1) * GLA_DV)
        att = jnp.where(incl, _mm_nt(q_e[:, sk], k_e[:, sk]), 0.0)
        st = s_ref[h]
        o_ref[:, sv] = _mm(att, v[:, sv]) + _mm_nt(q_e[:, sk], st)
        s_ref[h] = st * dec[:, sk] + _mm_tn(v[:, sv], k_end[:, sk])


def _gla_scan(p, t, lc, cos_t, sin_t, aup_pad, a_b):
    tt = t + lc
    nl, nc = t // CHUNK, lc // CHUNK
    ch = functools.partial(_chunk_of, nl=nl, nc=nc)
    blk = lambda cb: pl.BlockSpec((CHUNK, MIX_W), lambda d, n: (ch(d, n), cb))
    return pl.pallas_call(
        functools.partial(_gla_kernel, nl=nl, nc=nc),
        grid=(2, nl + nc),
        in_specs=[blk(CB_GLA_QK), blk(CB_GLA_V), blk(CB_SMALL), blk(0), blk(0),
                  pl.BlockSpec((None, LANE, 2 * LANE), lambda d, n: (d, 0, 0)),
                  pl.BlockSpec((None, 1, 2 * LANE), lambda d, n: (d, 0, 0))],
        out_specs=pl.BlockSpec((None, CHUNK, MIX_W), lambda d, n: (d, ch(d, n), 0)),
        out_shape=jax.ShapeDtypeStruct((2, tt, MIX_W), F32),
        scratch_shapes=[pltpu.VMEM((GLA_HEADS, GLA_DV, GLA_DK), F32)],
        compiler_params=_cparams(('arbitrary', 'arbitrary')), name='gla_scan',
    )(p, p, p, cos_t, sin_t, aup_pad, a_b)


def _head_norm(y, e_bf, width, eps):
    mu = _mm_xw(y, e_bf) * (1.0 / width)
    dl = y - mu
    var = _mm_xw(dl * dl, e_bf) * (1.0 / width)
    return dl * lax.rsqrt(var + eps)


def _mix_finish_kernel(y0_ref, y1_ref, b0_ref, b1_ref, sm_ref, gup_ref, lng_ref, lnb_ref, e64_ref,
                       o0_ref, o1_ref, gr_ref, gng_ref, e128_ref, a_ref, b_ref):
    yn = _head_norm(y0_ref[...] + y1_ref[...], e64_ref[...], RW_HEAD, RW_GN_EPS)
    yn = yn * lng_ref[...] + lnb_ref[...] + b0_ref[...] + b1_ref[...]
    gate = _mm(_sigmoid(sm_ref[:, 256:384]), gup_ref[...])
    a_ref[...] = (yn * gate).astype(BF16)
    on = _head_norm(o0_ref[...] + o1_ref[...], e128_ref[...], GLA_DV, GN_EPS) * gng_ref[...]
    gr = gr_ref[...]
    b_ref[...] = (on * (gr * _sigmoid(gr))).astype(BF16)


def _mix_finish(p, rw_y, rw_bonus, gla_o, g_up, ln_g, ln_b, gn_g, e64, e128, t):
    tt = p.shape[0]
    bm = _row_tile(t, tt - t)
    dblk = lambda d: pl.BlockSpec((None, bm, MIX_W), lambda i: (d, i, 0))
    vec = pl.BlockSpec((1, MIX_W), lambda i: (0, 0))
    mat = pl.BlockSpec((MIX_W, MIX_W), lambda i: (0, 0))
    out = pl.BlockSpec((bm, MIX_W), lambda i: (i, 0))
    return pl.pallas_call(
        _mix_finish_kernel,
        grid=(tt // bm,),
        in_specs=[dblk(0), dblk(1), dblk(0), dblk(1),
                  pl.BlockSpec((bm, MIX_W), lambda i: (i, CB_SMALL)),
                  pl.BlockSpec((RW_GATE_RANK, MIX_W), lambda i: (0, 0)), vec, vec, mat,
                  dblk(0), dblk(1), pl.BlockSpec((bm, MIX_W), lambda i: (i, CB_GLA_R)), vec, mat],
        out_specs=[out, out],
        out_shape=[jax.ShapeDtypeStruct((tt, MIX_W), BF16)] * 2,
        compiler_params=_cparams(('parallel',)), name='mix_finish',
    )(rw_y, rw_y, rw_bonus, rw_bonus, p, g_up, ln_g, ln_b, e64, gla_o, gla_o, p, gn_g, e128)


def _sgu_kernel(u_ref, v_ref, lng_ref, lnb_ref, ws_ref, bs_ref, o_ref):
    u = _gelu(u_ref[...])
    v = _gelu(v_ref[...])
    mu = jnp.mean(v, axis=-1, keepdims=True)
    dl = v - mu
    var = jnp.mean(dl * dl, axis=-1, keepdims=True)
    vn = (dl * lax.rsqrt(var + GN_EPS) * lng_ref[...] + lnb_ref[...]).astype(BF16)
    lane = lax.broadcasted_iota(jnp.int32, (1, MIX_W), 1)
    s = bs_ref[...]
    for g in range(SGU_GROUPS):
        s = s + jnp.where((lane >> 6) == g, jnp.dot(ws_ref[g].astype(BF16), vn, preferred_element_type=F32), 0.0)
    o_ref[...] = (u * s).astype(BF16)


def _sgu(p, ln_g, ln_b, w_s, b_full):
    tt = p.shape[0]
    vec = pl.BlockSpec((1, MIX_W), lambda i: (0, 0))
    return pl.pallas_call(
        _sgu_kernel,
        grid=(tt // SGU_CHUNK,),
        in_specs=[pl.BlockSpec((SGU_CHUNK, MIX_W), lambda i: (i, CB_SGU_U)),
                  pl.BlockSpec((SGU_CHUNK, MIX_W), lambda i: (i, CB_SGU_V)), vec, vec,
                  pl.BlockSpec((SGU_GROUPS, SGU_CHUNK, SGU_CHUNK), lambda i: (0, 0, 0)),
                  pl.BlockSpec((SGU_CHUNK, MIX_W), lambda i: (0, 0))],
        out_specs=pl.BlockSpec((SGU_CHUNK, MIX_W), lambda i: (i, 0)),
        out_shape=jax.ShapeDtypeStruct((tt, MIX_W), BF16),
        compiler_params=_cparams(('parallel',)), name='sgu',
    )(p, p, ln_g, ln_b, w_s, b_full)


def _na_bias_kernel(rpb_ref, o_ref, *, wh):
    h = pl.program_id(0)
    n_dc = 2 * NA_WIN_W - 1
    n_dr = 2 * NA_WIN_H - 1
    shape = (GRID_W, 2 * GRID_W)
    c = lax.broadcasted_iota(jnp.int32, shape, 0)
    lane = lax.broadcasted_iota(jnp.int32, shape, 1)
    x = lane & (GRID_W - 1)
    dc = jnp.clip(x - c + (NA_WIN_W - 1), 0, 2 * NA_WIN_W - 2)
    key = (lane >> 6) * n_dc + dc
    cs = jnp.clip(c - NA_WIN_W // 2, 0, GRID_W - NA_WIN_W)
    ok = jnp.logical_and(x >= cs, x < cs + NA_WIN_W)
    pairs = []
    for dr in range(n_dr - 1):
        base = h * (n_dr * n_dc) + dr * n_dc
        tile = lax.fori_loop(0, 2 * n_dc, lambda j, acc: jnp.where(key == j, rpb_ref[base + j], acc),
                             jnp.zeros(shape, F32))
        pairs.append(jnp.where(ok, tile, NEG_INF))
    for dr0 in range(n_dr - wh + 1):
        for jj in range(wh // 2):
            o_ref[dr0, 0, :, jj * 2 * GRID_W:(jj + 1) * 2 * GRID_W] = pairs[dr0 + 2 * jj]


def _na_bias(rpb, wh):
    n_dr0 = 2 * NA_WIN_H - wh
    return pl.pallas_call(
        functools.partial(_na_bias_kernel, wh=wh),
        grid=(NA_HEADS,),
        in_specs=[pl.BlockSpec(memory_space=pltpu.SMEM)],
        out_specs=pl.BlockSpec((n_dr0, 1, GRID_W, wh * GRID_W), lambda h: (0, h, 0, 0)),
        out_shape=jax.ShapeDtypeStruct((n_dr0, NA_HEADS, GRID_W, wh * GRID_W), F32),
        compiler_params=_cparams(('parallel',)), name='na_bias',
    )(rpb.reshape(-1))


def _na_kernel(*refs, wh):
    q_ref = refs[0]
    k_refs = refs[1:1 + wh]
    v_refs = refs[1 + wh:1 + 2 * wh]
    kc_ref, vc_ref, bias_ref, o_ref = refs[1 + 2 * wh:]
    q = q_ref[...] * (NA_HEAD ** -0.5)
    kw = jnp.concatenate([r[...].astype(BF16) for r in k_refs], axis=0)
    vw = jnp.concatenate([r[...].astype(BF16) for r in v_refs], axis=0)
    kc = kc_ref[...].astype(BF16)
    vc = vc_ref[...].astype(BF16)
    lane = lax.broadcasted_iota(jnp.int32, (1, MIX_W), 1)
    o = jnp.zeros((GRID_W, MIX_W), F32)
    for h in range(NA_HEADS):
        hm = (lane >> 6) == h
        qh = jnp.where(hm, q, 0.0).astype(BF16)
        sw = _mm_nt(qh, kw) + bias_ref[0, h]
        sc = _mm_nt(qh, kc)
        m = jnp.maximum(jnp.max(sw, axis=-1, keepdims=True), jnp.max(sc, axis=-1, keepdims=True))
        ew = jnp.exp(sw - m)
        ec = jnp.exp(sc - m)
        den = jnp.sum(ew, axis=-1, keepdims=True) + jnp.sum(ec, axis=-1, keepdims=True)
        oh = (_mm(ew, vw) + _mm(ec, vc)) / den
        o = jnp.where(hm, oh, o)
    o_ref[...] = o.astype(BF16)


def _na_ctx_kernel(q_ref, k_ref, v_ref, o_ref):
    q = q_ref[...] * (NA_HEAD ** -0.5)
    kc = k_ref[...].astype(BF16)
    vc = v_ref[...].astype(BF16)
    lane = lax.broadcasted_iota(jnp.int32, (1, MIX_W), 1)
    o = jnp.zeros(q.shape, F32)
    for h in range(NA_HEADS):
        hm = (lane >> 6) == h
        s = _mm_nt(jnp.where(hm, q, 0.0), kc)
        e = jnp.exp(s - jnp.max(s, axis=-1, keepdims=True))
        o = jnp.where(hm, _mm(e, vc) / jnp.sum(e, axis=-1, keepdims=True), o)
    o_ref[...] = o.astype(BF16)


def _na(p, bias, t, lc):
    rows = t // GRID_W
    wh = min(NA_WIN_H, rows)
    rs = lambda r: jnp.clip(r - wh // 2, 0, rows - wh)
    ctx_blk = t // lc
    kv = lambda cb: [pl.BlockSpec((GRID_W, MIX_W), functools.partial(lambda r, w, cb: (rs(r) + w, cb), w=w, cb=cb))
                     for w in range(wh)]
    lat = pl.pallas_call(
        functools.partial(_na_kernel, wh=wh),
        grid=(rows,),
        in_specs=([pl.BlockSpec((GRID_W, MIX_W), lambda r: (r, CB_NA_Q))] + kv(CB_NA_K) + kv(CB_NA_V)
                  + [pl.BlockSpec((lc, MIX_W), lambda r: (ctx_blk, CB_NA_K)),
                     pl.BlockSpec((lc, MIX_W), lambda r: (ctx_blk, CB_NA_V)),
                     pl.BlockSpec((1, NA_HEADS, GRID_W, wh * GRID_W),
                                  lambda r: (rs(r) - r + (NA_WIN_H - 1), 0, 0, 0))]),
        out_specs=pl.BlockSpec((GRID_W, MIX_W), lambda r: (r, 0)),
        out_shape=jax.ShapeDtypeStruct((t, MIX_W), BF16),
        compiler_params=_cparams(('parallel',)), name='na_latent',
    )(*([p] * (1 + 2 * wh + 2)), bias)
    cblk = lambda cb: pl.BlockSpec((lc, MIX_W), lambda i: (ctx_blk, cb))
    ctx = pl.pallas_call(
        _na_ctx_kernel,
        grid=(1,),
        in_specs=[cblk(CB_NA_Q), cblk(CB_NA_K), cblk(CB_NA_V)],
        out_specs=pl.BlockSpec((lc, MIX_W), lambda i: (0, 0)),
        out_shape=jax.ShapeDtypeStruct((lc, MIX_W), BF16),
        compiler_params=_cparams(('arbitrary',)), name='na_ctx',
    )(p, p, p)
    return jnp.concatenate([lat, ctx], axis=0)


def _merge_kernel(a0, a1, a2, a3, g0, g1, g2, g3, w_ref, o_ref, wb_ref):
    @pl.when(pl.program_id(1) == 0)
    def _():
        wb_ref[...] = w_ref[...].astype(BF16)
    acc = None
    for n, (a_ref, g_ref) in enumerate(((a0, g0), (a1, g1), (a2, g2), (a3, g3))):
        zn = jnp.dot(a_ref[...], wb_ref[n], preferred_element_type=F32) * _sigmoid(g_ref[...])
        acc = zn if acc is None else acc + zn
    o_ref[...] = acc.astype(BF16)


def _merge(ys, p, w_br):
    tt = p.shape[0]
    bn = 512
    bm = _pick(tt, (768, 384, 128))
    a_s = pl.BlockSpec((bm, MIX_W), lambda j, i: (i, 0))
    gs = [pl.BlockSpec((bm, bn), functools.partial(lambda j, i, n: (i, (GATE_OFF + n * D_MODEL) // bn + j), n=n))
          for n in range(N_BRANCH)]
    return pl.pallas_call(
        _merge_kernel,
        grid=(D_MODEL // bn, tt // bm),
        in_specs=[a_s] * 4 + gs + [pl.BlockSpec((N_BRANCH, MIX_W, bn), lambda j, i: (0, 0, j))],
        out_specs=pl.BlockSpec((bm, bn), lambda j, i: (i, j)),
        out_shape=jax.ShapeDtypeStruct((tt, D_MODEL), BF16),
        scratch_shapes=[pltpu.VMEM((N_BRANCH, MIX_W, bn), BF16)],
        compiler_params=_cparams(('arbitrary', 'arbitrary')), name='merge',
    )(*ys, p, p, p, p, w_br)


def _reorder_w_in(w):
    off, pos = 0, {}
    for name, width in _REF_SPLITS:
        pos[name] = (off, width)
        off += width
    cols = []
    for name, width in _IN_ORDER:
        if name == 'pad':
            cols.append(jnp.zeros((w.shape[0], width), w.dtype))
        else:
            o, wd = pos[name]
            assert wd == width
            cols.append(w[:, o:o + wd])
    return jnp.concatenate(cols, axis=1)


def _block_diag_ones(width):
    i = jnp.arange(MIX_W) // width
    return (i[:, None] == i[None, :]).astype(BF16)


def _rope_tables(t, lc):
    tok = jnp.arange(t)
    pos = jnp.stack([tok // GRID_W, tok % GRID_W], axis=-1).astype(F32)
    nf = GLA_DK // 4
    inv = ROPE_BASE ** (-jnp.arange(nf, dtype=F32) / nf)
    ang = pos[:, :, None] * inv
    cos = jnp.repeat(jnp.cos(ang), 2, axis=-1).reshape(t, GLA_DK)
    sin = jnp.sin(ang)
    sin = jnp.stack([-sin, sin], axis=-1).reshape(t, GLA_DK)
    reps = 2 * GLA_HEADS
    cos = jnp.concatenate([jnp.tile(cos, (1, reps)), jnp.ones((lc, reps * GLA_DK), F32)], axis=0)
    sin = jnp.concatenate([jnp.tile(sin, (1, reps)), jnp.zeros((lc, reps * GLA_DK), F32)], axis=0)
    return cos, sin


def _pad_rank_rows(w_up, rank, rows):
    out = jnp.zeros((2, rows, w_up.shape[-1]), w_up.dtype)
    for d in range(2):
        out = out.at[d, d * rank:(d + 1) * rank].set(w_up[d])
    return out


def kernel(x, c, ctx, c_ctx, ada_w, ada_b, norm_g, w_in, rw_mu, rw_w0, rw_w_up, rw_a0, rw_a_up, rw_g_up, rw_k_k, rw_k_a, rw_r_k, rw_ln_g, rw_ln_b, gla_a_up, gla_a_b, gla_gn_g, sgu_ln_g, sgu_ln_b, sgu_w, sgu_b, na_rpb, w_br, w_o, ffn_w1, ffn_w3, ffn_w2, moe_router, moe_w1, moe_w3, moe_w2):
    assert x.shape[0] == 1 and x.shape[2] == D_MODEL
    t, lc = x.shape[1], ctx.shape[1]
    depth = ada_w.shape[0]
    assert t % max(lc, SGU_CHUNK) == 0 and lc % SGU_CHUNK == 0 and t % GRID_W == 0
    rows = t // GRID_W
    assert rows >= NA_WIN_H
    wh = NA_WIN_H
    xs = jnp.concatenate([x[0], ctx[0]], axis=0)
    cond8 = jnp.zeros((8, D_MODEL), F32).at[0].set(c[0]).at[1].set(c_ctx)
    e64 = _block_diag_ones(RW_HEAD)
    e128 = _block_diag_ones(GLA_DV)
    cos_t, sin_t = _rope_tables(t, lc)
    row1 = lambda v: v.reshape(1, -1)

    for i in range(depth):
        mods = _adaln(cond8, ada_w[i], ada_b[i])[0:2].reshape(2, 6, D_MODEL)
        h = _norm_mod_call(xs, norm_g[i, 0], mods, 0, t)
        p = _matmul(h, _reorder_w_in(w_in[i]), 1024, (768, 384, 128), name='in_proj')
        mu2 = jnp.stack([rw_mu[i, :, 0].reshape(-1), rw_mu[i, :, 1].reshape(-1)])
        rw_y, rw_bonus = _rwkv_scan(
            p, t, lc, mu2, rw_w0[i].reshape(2, 1, MIX_W), _pad_rank_rows(rw_w_up[i], RW_DECAY_RANK, LANE),
            rw_a0[i].reshape(2, 1, MIX_W), _pad_rank_rows(rw_a_up[i], RW_ICLR_RANK, LANE),
            row1(rw_k_k[i]), row1(rw_k_a[i]), row1(rw_r_k[i]), e64)
        gla_o = _gla_scan(p, t, lc, cos_t, sin_t, _pad_rank_rows(gla_a_up[i], GLA_GATE_RANK, LANE),
                          gla_a_b[i].reshape(2, 1, 2 * LANE))
        y_a, y_b = _mix_finish(p, rw_y, rw_bonus, gla_o, rw_g_up[i], row1(rw_ln_g[i]), row1(rw_ln_b[i]),
                               row1(gla_gn_g[i]), e64, e128, t)
        y_s = _sgu(p, row1(sgu_ln_g[i]), row1(sgu_ln_b[i]), sgu_w[i], jnp.repeat(sgu_b[i].T, 64, axis=1))
        y_d = _na(p, _na_bias(na_rpb[i], wh), t, lc)
        z = _merge((y_a, y_b, y_s, y_d), p, w_br[i])
        y = _matmul(z, w_o[i], 1024, (768, 384, 128), name='out_proj')
        xs = _resid_call(xs, y, norm_g[i, 1], mods, 2, t)
        j = i // 2
        if i % 2 == 0:
            h = _norm_mod_call(xs, norm_g[i, 2], mods, 3, t)
            f = _ffn_down(_ffn_up(h, ffn_w1, ffn_w3, j), ffn_w2, j)
            xs = _resid_call(xs, f, norm_g[i, 3], mods, 5, t)
        else:
            h, route = _norm_mod_call(xs, norm_g[i, 2], mods, 3, t, router=moe_router[j])
            xs = _moe_layer(xs, h, route, moe_w1[j], moe_w3[j], moe_w2[j], norm_g[i, 3], mods, 5, t)
    return xs[0:t][None]
```

```python
import functools

import jax
import jax.numpy as jnp
from jax import lax
from jax.experimental import pallas as pl
from jax.experimental.pallas import tpu as pltpu

F32 = jnp.float32
BF16 = jnp.bfloat16

D_MODEL = 2048
GRID_W = 64
N_BRANCH = 4
MIX_W = D_MODEL // 4
RW_HEAD = 64
RW_HEADS = MIX_W // RW_HEAD
RW_GROUP = 4
RW_DECAY_RANK = 64
RW_ICLR_RANK = 64
RW_GATE_RANK = 128
RW_GN_EPS = 64e-5
GLA_HEADS = 4
GLA_DV = MIX_W // GLA_HEADS
GLA_DK = GLA_DV // 2
GLA_GATE_RANK = 16
GLA_TAU = 16.0
GN_EPS = 1e-5
ROPE_BASE = 10000.0
SGU_GROUPS = MIX_W // 64
SGU_CHUNK = 128
NA_HEAD = 64
NA_HEADS = MIX_W // NA_HEAD
NA_WIN_H = 8
NA_WIN_W = 16
FFN_DIM = 7 * D_MODEL // 2
N_EXPERTS = 8
TOP_K = 2
EPS = 1e-6
NEG_INF = -1e30

RT_G1, RT_G2, RT_I1, RT_I2 = 8, 9, 10, 11
MOE_BM = 512
CHUNK = 64
LANE = 128
VMEM_LIMIT = 56 * 2 ** 20

_IN_ORDER = (
    ('rw_r', 512), ('rw_k', 512), ('rw_v', 512), ('gla_v', 512), ('gla_r', 512),
    ('sgu_u', 512), ('sgu_v', 512), ('na_q', 512), ('na_k', 512), ('na_v', 512),
    ('gla_q', 256), ('gla_k', 256), ('rw_wd', 128), ('rw_ad', 128), ('rw_gd', 128),
    ('gla_ad', 32), ('pad', 96), ('gate', N_BRANCH * D_MODEL),
)
_REF_SPLITS = (
    ('rw_r', MIX_W), ('rw_k', MIX_W), ('rw_v', MIX_W),
    ('rw_wd', 2 * RW_DECAY_RANK), ('rw_ad', 2 * RW_ICLR_RANK), ('rw_gd', RW_GATE_RANK),
    ('gla_q', GLA_HEADS * GLA_DK), ('gla_k', GLA_HEADS * GLA_DK), ('gla_v', MIX_W),
    ('gla_r', MIX_W), ('gla_ad', 2 * GLA_GATE_RANK),
    ('sgu_u', MIX_W), ('sgu_v', MIX_W),
    ('na_q', MIX_W), ('na_k', MIX_W), ('na_v', MIX_W),
    ('gate', N_BRANCH * D_MODEL),
)
N_IN_P = sum(w for _, w in _IN_ORDER)
CB_GLA_V, CB_GLA_R, CB_SGU_U, CB_SGU_V, CB_NA_Q, CB_NA_K, CB_NA_V, CB_GLA_QK, CB_SMALL = 3, 4, 5, 6, 7, 8, 9, 10, 11
GATE_OFF = 6144


def _cparams(sem):
    return pltpu.CompilerParams(dimension_semantics=sem, vmem_limit_bytes=VMEM_LIMIT)


def _pick(m, cands):
    for c in cands:
        if m % c == 0:
            return c
    raise ValueError(f'no tile for {m}')


def _mm(a, b):
    return jnp.dot(a.astype(BF16), b.astype(BF16), preferred_element_type=F32)


def _mm_nt(a, b):
    return lax.dot_general(a.astype(BF16), b.astype(BF16), (((1,), (1,)), ((), ())),
                           preferred_element_type=F32)


def _mm_tn(a, b):
    return lax.dot_general(a.astype(BF16), b.astype(BF16), (((0,), (0,)), ((), ())),
                           preferred_element_type=F32)


def _split2(x):
    hi = x.astype(BF16)
    lo = (x - hi.astype(F32)).astype(BF16)
    return hi, lo


def _mm_xw(x, w_bf):
    hi, lo = _split2(x)
    return (jnp.dot(hi, w_bf, preferred_element_type=F32)
            + jnp.dot(lo, w_bf, preferred_element_type=F32))


def _mm_wx(w_bf, x):
    hi, lo = _split2(x)
    return (jnp.dot(w_bf, hi, preferred_element_type=F32)
            + jnp.dot(w_bf, lo, preferred_element_type=F32))


def _mm3(a, b):
    ah, al = _split2(a)
    bh, bl = _split2(b)
    return (jnp.dot(ah, bh, preferred_element_type=F32)
            + jnp.dot(ah, bl, preferred_element_type=F32)
            + jnp.dot(al, bh, preferred_element_type=F32))


def _sigmoid(x):
    return 1.0 / (1.0 + jnp.exp(-x))


def _softplus(x):
    return jnp.maximum(x, 0.0) + jnp.log1p(jnp.exp(-jnp.abs(x)))


def _gelu(x):
    return 0.5 * x * (1.0 + lax.erf(x * (0.5 ** 0.5)))


def _order_masks(d):
    t = lax.broadcasted_iota(jnp.int32, (CHUNK, CHUNK), 0)
    s = lax.broadcasted_iota(jnp.int32, (CHUNK, CHUNK), 1)
    diff = (t - s) * jnp.where(d == 0, 1, -1)
    return diff > 0, diff >= 0


def _chunk_of(d, n, nl, nc):
    fwd = jnp.where(n < nc, nl + n, n - nc)
    bwd = jnp.where(n < nc, nl + nc - 1 - n, nl - 1 - (n - nc))
    return jnp.where(d == 0, fwd, bwd)


def _ada_kernel(c_ref, w_ref, b_ref, o_ref):
    cnd = c_ref[...]
    a = cnd * _sigmoid(cnd)
    o_ref[...] = _mm(a, w_ref[...]) + b_ref[...]


def _adaln(cond8, ada_w, ada_b, lyr):
    depth, _, n = ada_w.shape
    bn = 1536
    return pl.pallas_call(
        _ada_kernel,
        grid=(n // bn,),
        in_specs=[pl.BlockSpec((8, D_MODEL), lambda j: (0, 0)),
                  pl.BlockSpec((None, D_MODEL, bn), lambda j: (lyr, 0, j)),
                  pl.BlockSpec((None, 1, bn), lambda j: (lyr, 0, j))],
        out_specs=pl.BlockSpec((8, bn), lambda j: (0, j)),
        out_shape=jax.ShapeDtypeStruct((8, n), F32),
        compiler_params=_cparams(('arbitrary',)),
        name='adaln',
    )(cond8, ada_w, ada_b.reshape(depth, 1, n))


def _norm_mod(x, g, m, k0):
    y = x * lax.rsqrt(jnp.mean(x * x, axis=-1, keepdims=True) + EPS) * g
    return y * (1.0 + m[k0 + 1:k0 + 2]) + m[k0:k0 + 1]


def _pack_halves(h):
    n = h.shape[1] // 2
    bits = lax.bitcast_convert_type(h.astype(BF16).astype(F32), jnp.uint32)
    return (bits[:, 0:n] >> 16) | bits[:, n:]


def _unpack_halves(w):
    lo = lax.bitcast_convert_type(w << 16, F32)
    hi = lax.bitcast_convert_type(w & jnp.uint32(0xFFFF0000), F32)
    return jnp.concatenate([lo, hi], axis=1).astype(BF16)


def _norm_mod_kernel(x_ref, g_ref, m_ref, o_ref, *, k0):
    o_ref[...] = _norm_mod(x_ref[...], g_ref[...], m_ref[0], k0).astype(BF16)


def _norm_mod_route_kernel(x_ref, g_ref, m_ref, r_ref, o_ref, gate_ref, *, k0):
    h = _norm_mod(x_ref[...], g_ref[...], m_ref[0], k0)
    o_ref[...] = _pack_halves(h)
    logits = _mm3(h, r_ref[...])
    lane = lax.broadcasted_iota(jnp.int32, logits.shape, 1)
    neg = jnp.float32(-jnp.inf)
    l1 = jnp.where(lane < N_EXPERTS, logits, neg)
    m1 = jnp.max(l1, axis=-1, keepdims=True)
    i1 = jnp.min(jnp.where(l1 == m1, lane, LANE), axis=-1, keepdims=True)
    l2 = jnp.where(lane == i1, neg, l1)
    m2 = jnp.max(l2, axis=-1, keepdims=True)
    i2 = jnp.min(jnp.where(l2 == m2, lane, LANE), axis=-1, keepdims=True)
    e2 = jnp.exp(m2 - m1)
    den = 1.0 + e2
    sel = jnp.where(jnp.logical_or(lane == i1, lane == i2), 1.0, 0.0)
    sel = jnp.where(lane == RT_G1, 1.0 / den, jnp.where(lane == RT_G2, e2 / den, sel))
    gate_ref[...] = jnp.where(lane == RT_I1, i1.astype(F32), jnp.where(lane == RT_I2, i2.astype(F32), sel))


def _row_tile(t, lc):
    return _pick(lc, (256, 128))


def _norm_mod_call(x, g, mods, k0, t, router=None):
    tt = x.shape[0]
    bm = _row_tile(t, tt - t)
    nlt = t // bm
    xs = pl.BlockSpec((bm, D_MODEL), lambda i: (i, 0))
    gs = pl.BlockSpec((1, D_MODEL), lambda i: (0, 0))
    ms = pl.BlockSpec((1, 6, D_MODEL), lambda i: (jnp.where(i >= nlt, 1, 0), 0, 0))
    if router is None:
        return pl.pallas_call(
            functools.partial(_norm_mod_kernel, k0=k0),
            grid=(tt // bm,), in_specs=[xs, gs, ms], out_specs=xs,
            out_shape=jax.ShapeDtypeStruct((tt, D_MODEL), BF16),
            compiler_params=_cparams(('parallel',)), name='norm_mod',
        )(x, g.reshape(1, D_MODEL), mods)
    rpad = jnp.pad(router, ((0, 0), (0, LANE - N_EXPERTS)))
    return pl.pallas_call(
        functools.partial(_norm_mod_route_kernel, k0=k0),
        grid=(tt // bm,),
        in_specs=[xs, gs, ms, pl.BlockSpec((D_MODEL, LANE), lambda i: (0, 0))],
        out_specs=[pl.BlockSpec((bm, D_MODEL // 2), lambda i: (i, 0)), pl.BlockSpec((bm, LANE), lambda i: (i, 0))],
        out_shape=[jax.ShapeDtypeStruct((tt, D_MODEL // 2), jnp.uint32), jax.ShapeDtypeStruct((tt, LANE), F32)],
        compiler_params=_cparams(('parallel',)), name='norm_mod_route',
    )(x, g.reshape(1, D_MODEL), mods, rpad)


def _resid_kernel(x_ref, y_ref, g_ref, m_ref, o_ref, *, kg):
    y = y_ref[...]
    n = y * lax.rsqrt(jnp.mean(y * y, axis=-1, keepdims=True) + EPS) * g_ref[...]
    o_ref[...] = x_ref[...] + m_ref[0][kg:kg + 1] * n


def _resid_call(x, y, g, mods, kg, t):
    tt = x.shape[0]
    bm = _row_tile(t, tt - t)
    nlt = t // bm
    xs = pl.BlockSpec((bm, D_MODEL), lambda i: (i, 0))
    return pl.pallas_call(
        functools.partial(_resid_kernel, kg=kg),
        grid=(tt // bm,),
        in_specs=[xs, xs, pl.BlockSpec((1, D_MODEL), lambda i: (0, 0)),
                  pl.BlockSpec((1, 6, D_MODEL), lambda i: (jnp.where(i >= nlt, 1, 0), 0, 0))],
        out_specs=xs,
        out_shape=jax.ShapeDtypeStruct((tt, D_MODEL), F32),
        compiler_params=_cparams(('parallel',)), name='resid_norm',
    )(x, y, g.reshape(1, D_MODEL), mods)


def _mm_kernel(a_ref, w_ref, o_ref, wb_ref):
    @pl.when(pl.program_id(1) == 0)
    def _():
        wb_ref[...] = w_ref[...].astype(BF16)
    o_ref[...] = jnp.dot(a_ref[...], wb_ref[...], preferred_element_type=F32).astype(o_ref.dtype)


def _matmul(a, w, lyr, bn, bm_cands, out_dtype=F32, name='matmul'):
    m, k = a.shape
    n = w.shape[2]
    bm = _pick(m, bm_cands)
    return pl.pallas_call(
        _mm_kernel,
        grid=(n // bn, m // bm),
        in_specs=[pl.BlockSpec((bm, k), lambda j, i: (i, 0)),
                  pl.BlockSpec((None, k, bn), lambda j, i: (lyr, 0, j))],
        out_specs=pl.BlockSpec((bm, bn), lambda j, i: (i, j)),
        out_shape=jax.ShapeDtypeStruct((m, n), out_dtype),
        scratch_shapes=[pltpu.VMEM((k, bn), BF16)],
        compiler_params=_cparams(('arbitrary', 'arbitrary')), name=name,
    )(a, w)


def _ffn_up_kernel(a_ref, w1_ref, w3_ref, o_ref, w1b, w3b):
    @pl.when(pl.program_id(1) == 0)
    def _():
        w1b[...] = w1_ref[...].astype(BF16)
        w3b[...] = w3_ref[...].astype(BF16)
    a = a_ref[...]
    h1 = jnp.dot(a, w1b[...], preferred_element_type=F32)
    h3 = jnp.dot(a, w3b[...], preferred_element_type=F32)
    o_ref[...] = (h1 * _sigmoid(h1) * h3).astype(BF16)


def _ffn_up(h, w1, w3, e):
    m = h.shape[0]
    bn = 512
    bm = _pick(m, (768, 384, 128))
    ws = pl.BlockSpec((None, D_MODEL, bn), lambda j, i: (e, 0, j))
    return pl.pallas_call(
        _ffn_up_kernel,
        grid=(FFN_DIM // bn, m // bm),
        in_specs=[pl.BlockSpec((bm, D_MODEL), lambda j, i: (i, 0)), ws, ws],
        out_specs=pl.BlockSpec((bm, bn), lambda j, i: (i, j)),
        out_shape=jax.ShapeDtypeStruct((m, FFN_DIM), BF16),
        scratch_shapes=[pltpu.VMEM((D_MODEL, bn), BF16), pltpu.VMEM((D_MODEL, bn), BF16)],
        compiler_params=_cparams(('arbitrary', 'arbitrary')), name='ffn_up',
    )(h, w1, w3)


def _ffn_down_kernel(a_ref, w_ref, o_ref, wb_ref):
    @pl.when(pl.program_id(1) == 0)
    def _():
        wb_ref[...] = w_ref[...].astype(BF16)
    o_ref[...] = jnp.dot(a_ref[...], wb_ref[...], preferred_element_type=F32)


def _ffn_down(u, w2, e):
    m = u.shape[0]
    bn = 256
    bm = _pick(m, (384, 128))
    return pl.pallas_call(
        _ffn_down_kernel,
        grid=(D_MODEL // bn, m // bm),
        in_specs=[pl.BlockSpec((bm, FFN_DIM), lambda j, i: (i, 0)),
                  pl.BlockSpec((None, FFN_DIM, bn), lambda j, i: (e, 0, j))],
        out_specs=pl.BlockSpec((bm, bn), lambda j, i: (i, j)),
        out_shape=jax.ShapeDtypeStruct((m, D_MODEL), F32),
        scratch_shapes=[pltpu.VMEM((FFN_DIM, bn), BF16)],
        compiler_params=_cparams(('arbitrary', 'arbitrary')), name='ffn_down',
    )(u, w2)


def _moe_rank_kernel(r_ref, dest_ref, cnt_ref, carry_ref, tot_ref):
    ph = pl.program_id(0)
    i = pl.program_id(1)
    r = r_ref[...]
    bm = r.shape[0]
    lane = lax.broadcasted_iota(jnp.int32, (1, LANE), 1)
    oh = jnp.where(lane < N_EXPERTS, r, 0.0)
    colsum = jnp.sum(oh, axis=0, keepdims=True)

    @pl.when(jnp.logical_and(ph == 0, i == 0))
    def _():
        carry_ref[...] = jnp.zeros_like(carry_ref)

    @pl.when(jnp.logical_and(ph == 1, i == 0))
    def _():
        tot_ref[...] = carry_ref[...]
        carry_ref[...] = jnp.zeros_like(carry_ref)

    @pl.when(ph == 1)
    def _():
        tot = tot_ref[...]
        padded = jnp.floor((tot + (MOE_BM - 1)) * (1.0 / MOE_BM)) * MOE_BM
        a = lax.broadcasted_iota(jnp.int32, (LANE, LANE), 0)
        b = lax.broadcasted_iota(jnp.int32, (LANE, LANE), 1)
        upper = jnp.where(a < b, 1.0, 0.0).astype(BF16)
        offs = _mm_xw(jnp.broadcast_to(padded, (8, LANE)), upper)[0:1]
        tr = lax.broadcasted_iota(jnp.int32, (bm, bm), 0)
        ts = lax.broadcasted_iota(jnp.int32, (bm, bm), 1)
        before = jnp.dot(jnp.where(ts < tr, 1.0, 0.0).astype(BF16), oh.astype(BF16),
                         preferred_element_type=F32)
        slot = before + carry_ref[...] + offs
        lane_f = lane.astype(F32)
        d1 = jnp.sum(jnp.where(lane_f == r[:, RT_I1:RT_I1 + 1], slot, 0.0), axis=-1, keepdims=True)
        d2 = jnp.sum(jnp.where(lane_f == r[:, RT_I2:RT_I2 + 1], slot, 0.0), axis=-1, keepdims=True)
        dest_ref[...] = jnp.where(lane == 0, d1, jnp.where(lane == 1, d2, 0.0)).astype(jnp.int32)
        cnt_ref[...] = jnp.broadcast_to(tot, (8, LANE))

    carry_ref[...] = carry_ref[...] + colsum


def _moe_rank(route, bm):
    tt = route.shape[0]
    return pl.pallas_call(
        _moe_rank_kernel,
        grid=(2, tt // bm),
        in_specs=[pl.BlockSpec((bm, LANE), lambda ph, i: (i, 0))],
        out_specs=[pl.BlockSpec((bm, LANE), lambda ph, i: (i * ph, 0)),
                   pl.BlockSpec((8, LANE), lambda ph, i: (0, 0))],
        out_shape=[jax.ShapeDtypeStruct((tt, LANE), jnp.int32), jax.ShapeDtypeStruct((8, LANE), F32)],
        scratch_shapes=[pltpu.VMEM((1, LANE), F32), pltpu.VMEM((1, LANE), F32)],
        compiler_params=_cparams(('arbitrary', 'arbitrary')), name='moe_rank',
    )(route)


def _row_copy(src, dst, sem):
    return pltpu.make_async_copy(src, dst, sem)


def _moe_scatter_kernel(dest_ref, h_ref, init_hbm, xs_hbm, sem, *, bm):
    del init_hbm
    base = pl.program_id(0) * bm

    def issue(r, carry):
        t = base + r
        for kk in range(TOP_K):
            _row_copy(h_ref.at[pl.ds(r, 1)], xs_hbm.at[pl.ds(dest_ref[TOP_K * t + kk], 1)], sem).start()
        return carry
    lax.fori_loop(0, bm, issue, 0)

    def drain(r, carry):
        for kk in range(TOP_K):
            _row_copy(h_ref.at[pl.ds(0, 1)], xs_hbm.at[pl.ds(0, 1)], sem).wait()
        return carry
    lax.fori_loop(0, bm, drain, 0)


def _moe_scatter(dest_flat, h32, ns, bm):
    tt, w = h32.shape
    return pl.pallas_call(
        functools.partial(_moe_scatter_kernel, bm=bm),
        grid_spec=pltpu.PrefetchScalarGridSpec(
            num_scalar_prefetch=1, grid=(tt // bm,),
            in_specs=[pl.BlockSpec((bm, w), lambda i, d: (i, 0)), pl.BlockSpec(memory_space=pl.ANY)],
            out_specs=pl.BlockSpec(memory_space=pl.ANY),
            scratch_shapes=[pltpu.SemaphoreType.DMA(())]),
        out_shape=jax.ShapeDtypeStruct((ns, w), h32.dtype),
        input_output_aliases={2: 0},
        compiler_params=_cparams(('arbitrary',)), name='moe_scatter',
    )(dest_flat, h32, jnp.zeros((ns, w), h32.dtype))


def _moe_up_kernel(te_ref, nv_ref, a_ref, w1_ref, w3_ref, o_ref, w1b, w3b):
    m = pl.program_id(1)

    @pl.when(jnp.logical_or(m == 0, te_ref[m] != te_ref[jnp.maximum(m - 1, 0)]))
    def _():
        w1b[...] = w1_ref[...].astype(BF16)
        w3b[...] = w3_ref[...].astype(BF16)

    @pl.when(m < nv_ref[0])
    def _():
        a = _unpack_halves(a_ref[...])
        h1 = jnp.dot(a, w1b[...], preferred_element_type=F32)
        h3 = jnp.dot(a, w3b[...], preferred_element_type=F32)
        o_ref[...] = (h1 * _sigmoid(h1) * h3).astype(BF16)

    @pl.when(m >= nv_ref[0])
    def _():
        o_ref[...] = jnp.zeros_like(o_ref)


def _moe_down_kernel(te_ref, nv_ref, a_ref, w_ref, o_ref, wb_ref):
    m = pl.program_id(1)

    @pl.when(jnp.logical_or(m == 0, te_ref[m] != te_ref[jnp.maximum(m - 1, 0)]))
    def _():
        wb_ref[...] = w_ref[...].astype(BF16)

    @pl.when(m < nv_ref[0])
    def _():
        o_ref[...] = jnp.dot(a_ref[...], wb_ref[...], preferred_element_type=F32)

    @pl.when(m >= nv_ref[0])
    def _():
        o_ref[...] = jnp.zeros_like(o_ref)


def _moe_grouped_ffn(tile_expert, n_valid, xs, w1, w3, w2, lyr):
    ns = xs.shape[0]
    n_tiles = ns // MOE_BM
    bn = 512
    ws = pl.BlockSpec((None, None, D_MODEL, bn), lambda j, m, te, nv: (lyr, te[m], 0, j))
    u = pl.pallas_call(
        _moe_up_kernel,
        grid_spec=pltpu.PrefetchScalarGridSpec(
            num_scalar_prefetch=2, grid=(FFN_DIM // bn, n_tiles),
            in_specs=[pl.BlockSpec((MOE_BM, D_MODEL // 2), lambda j, m, te, nv: (m, 0)), ws, ws],
            out_specs=pl.BlockSpec((MOE_BM, bn), lambda j, m, te, nv: (m, j)),
            scratch_shapes=[pltpu.VMEM((D_MODEL, bn), BF16), pltpu.VMEM((D_MODEL, bn), BF16)]),
        out_shape=jax.ShapeDtypeStruct((ns, FFN_DIM), BF16),
        compiler_params=_cparams(('arbitrary', 'arbitrary')), name='moe_up',
    )(tile_expert, n_valid, xs, w1, w3)
    bn = 256
    return pl.pallas_call(
        _moe_down_kernel,
        grid_spec=pltpu.PrefetchScalarGridSpec(
            num_scalar_prefetch=2, grid=(D_MODEL // bn, n_tiles),
            in_specs=[pl.BlockSpec((MOE_BM, FFN_DIM), lambda j, m, te, nv: (m, 0)),
                      pl.BlockSpec((None, None, FFN_DIM, bn), lambda j, m, te, nv: (lyr, te[m], 0, j))],
            out_specs=pl.BlockSpec((MOE_BM, bn), lambda j, m, te, nv: (m, j)),
            scratch_shapes=[pltpu.VMEM((FFN_DIM, bn), BF16)]),
        out_shape=jax.ShapeDtypeStruct((ns, D_MODEL), F32),
        compiler_params=_cparams(('arbitrary', 'arbitrary')), name='moe_down',
    )(tile_expert, n_valid, u, w2)


def _moe_combine_kernel(dest_ref, ys_hbm, r_ref, x_ref, g_ref, m_ref, o_ref, buf, sem, *, bm, kg):
    base = pl.program_id(0) * bm

    def issue(r, carry):
        t = base + r
        for kk in range(TOP_K):
            _row_copy(ys_hbm.at[pl.ds(dest_ref[TOP_K * t + kk], 1)], buf.at[kk, pl.ds(r, 1)], sem).start()
        return carry
    lax.fori_loop(0, bm, issue, 0)

    def drain(r, carry):
        for kk in range(TOP_K):
            _row_copy(ys_hbm.at[pl.ds(0, 1)], buf.at[0, pl.ds(0, 1)], sem).wait()
        return carry
    lax.fori_loop(0, bm, drain, 0)

    r = r_ref[...]
    f = r[:, RT_G1:RT_G1 + 1] * buf[0] + r[:, RT_G2:RT_G2 + 1] * buf[1]
    n = f * lax.rsqrt(jnp.mean(f * f, axis=-1, keepdims=True) + EPS) * g_ref[...]
    o_ref[...] = x_ref[...] + m_ref[0][kg:kg + 1] * n


def _moe_combine(dest_flat, ys, route, x, g, mods, kg, t, bm):
    tt = x.shape[0]
    nlt = t // bm
    xs = pl.BlockSpec((bm, D_MODEL), lambda i, d: (i, 0))
    return pl.pallas_call(
        functools.partial(_moe_combine_kernel, bm=bm, kg=kg),
        grid_spec=pltpu.PrefetchScalarGridSpec(
            num_scalar_prefetch=1, grid=(tt // bm,),
            in_specs=[pl.BlockSpec(memory_space=pl.ANY),
                      pl.BlockSpec((bm, LANE), lambda i, d: (i, 0)), xs,
                      pl.BlockSpec((1, D_MODEL), lambda i, d: (0, 0)),
                      pl.BlockSpec((1, 6, D_MODEL), lambda i, d: (jnp.where(i >= nlt, 1, 0), 0, 0))],
            out_specs=xs,
            scratch_shapes=[pltpu.VMEM((TOP_K, bm, D_MODEL), F32), pltpu.SemaphoreType.DMA(())]),
        out_shape=jax.ShapeDtypeStruct((tt, D_MODEL), F32),
        compiler_params=_cparams(('arbitrary',)), name='moe_combine',
    )(dest_flat, ys, route, x, g.reshape(1, D_MODEL), mods)


def _moe_layer(x, h32, route, w1, w3, w2, lyr, g, mods, kg, t):
    tt = x.shape[0]
    bm = _row_tile(t, tt - t)
    dest, cnt = _moe_rank(route, bm)
    n_tiles = -(-TOP_K * tt // MOE_BM) + N_EXPERTS
    ns = n_tiles * MOE_BM
    cum = jnp.cumsum((cnt[0, 0:N_EXPERTS].astype(jnp.int32) + (MOE_BM - 1)) // MOE_BM)
    n_valid = cum[-1]
    tile = jnp.arange(n_tiles, dtype=jnp.int32)
    tile_expert = jnp.searchsorted(cum, jnp.minimum(tile, n_valid - 1), side='right').astype(jnp.int32)
    dest_flat = dest[:, 0:TOP_K].reshape(-1)
    xs32 = _moe_scatter(dest_flat, h32, ns, bm)
    ys = _moe_grouped_ffn(tile_expert, n_valid.reshape(1), xs32, w1, w3, w2, lyr)
    return _moe_combine(dest_flat, ys, route, x, g, mods, kg, t, bm)


def _rwkv_kernel(cur_ref, prv_ref, nxt_ref, sm_ref, mu_ref, w0_ref, wup_ref, a0_ref, aup_ref,
                 kk_ref, ka_ref, rk_ref, e_ref, tri_ref, y_ref, bonus_ref, s_ref, *, nl, nc):
    d = pl.program_id(0)
    n = pl.program_id(1)
    ch = _chunk_of(d, n, nl, nc)

    @pl.when(n == 0)
    def _():
        s_ref[...] = jnp.zeros_like(s_ref)

    x = cur_ref[...]
    first = jnp.logical_or(ch == 0, ch == nl)
    last = jnp.logical_or(ch == nl - 1, ch == nl + nc - 1)
    p_row = jnp.where(first, 0.0, prv_ref[7:8, :])
    n_row = jnp.where(last, 0.0, nxt_ref[0:1, :])
    row = lax.broadcasted_iota(jnp.int32, (CHUNK, 1), 0)
    prev = jnp.where(row == 0, p_row, pltpu.roll(x, 1, axis=0))
    nxt = jnp.where(row == CHUNK - 1, n_row, pltpu.roll(x, CHUNK - 1, axis=0))
    z = x + mu_ref[0:1, :] * (prev - x) + mu_ref[1:2, :] * (nxt - x)
    r, k, v = z[:, 0:MIX_W], z[:, MIX_W:2 * MIX_W], z[:, 2 * MIX_W:3 * MIX_W]

    sm = sm_ref[...]
    e_bf = e_ref[...]
    wd = jnp.tanh(sm[:, 0:128])
    w_log = -_softplus(-(w0_ref[...] + _mm3(wd, wup_ref[...]))) - 0.5
    logw = -jnp.exp(w_log)
    a = _sigmoid(a0_ref[...] + _mm3(sm[:, 128:256], aup_ref[...]))
    kkr = k * kk_ref[...]
    kk = kkr * lax.rsqrt(jnp.maximum(_mm_xw(kkr * kkr, e_bf), 1e-12))
    k_dir = k * (1.0 + (a - 1.0) * ka_ref[...])
    bonus_ref[...] = _mm_xw(r * k_dir * rk_ref[...], e_bf) * v

    m_strict = tri_ref[0]
    m_incl = tri_ref[1]
    b_inc = _mm_wx(m_incl[0:CHUNK, 0:CHUNK].astype(BF16), logw)
    b_exc = b_inc - logw
    b_last = jnp.sum(logw, axis=0, keepdims=True)
    beta = kk * a
    ea = -kk * jnp.exp(b_exc)
    er = r * jnp.exp(b_inc)
    ninv = jnp.exp(-b_inc)
    eb = beta * ninv
    ek = k_dir * ninv
    eend = jnp.exp(b_last - b_inc)
    hb = beta * eend
    hk = k_dir * eend
    gam = jnp.exp(b_last)

    gw = RW_GROUP * RW_HEAD
    lane_head = lax.broadcasted_iota(jnp.int32, (1, gw), 1) >> 6

    def spread(xg):
        return jnp.concatenate([jnp.where(lane_head == h, xg, 0.0) for h in range(RW_GROUP)], axis=0)

    for g in range(RW_HEADS // RW_GROUP):
        sl = slice(g * gw, (g + 1) * gw)
        la, lr = spread(ea[:, sl]), spread(er[:, sl])
        rb, rk = spread(eb[:, sl]), spread(ek[:, sl])
        vb = spread(v[:, sl]).astype(BF16)
        amat = _mm_nt(jnp.concatenate([la, lr], axis=0), jnp.concatenate([rb, rk], axis=0))
        m_ab = amat[0:gw, 0:gw] * m_strict
        m_ak = amat[0:gw, gw:] * m_strict
        n_rb = (amat[gw:, 0:gw] * m_incl).astype(BF16)
        n_rk = (amat[gw:, gw:] * m_incl).astype(BF16)
        xs = jnp.concatenate([la, _mm(m_ak, vb)], axis=1)
        mp = m_ab
        for it in range(6):
            xs = xs + _mm(mp, xs)
            if it < 5:
                mp = _mm(mp, mp)
        xs_bf = xs.astype(BF16)
        qy = _mm(n_rb, xs_bf)
        q_hat = lr + qy[:, 0:gw]
        y_loc = qy[:, gw:] + _mm(n_rk, vb)
        hb_bd = spread(hb[:, sl]).astype(BF16)
        gbt = _mm_tn(xs_bf, hb_bd)
        g_bot = gbt[gw:] + _mm_tn(vb, spread(hk[:, sl]))
        s0 = s_ref[g]
        y_bd = _mm_nt(q_hat, s0) + y_loc
        y_ref[:, sl] = (y_bd[0:CHUNK] + y_bd[CHUNK:2 * CHUNK]
                        + y_bd[2 * CHUNK:3 * CHUNK] + y_bd[3 * CHUNK:4 * CHUNK])
        s_ref[g] = s0 * gam[:, sl] + _mm(s0, gbt[0:gw]) + g_bot


def _rwkv_order_masks():
    i = jnp.arange(RW_GROUP * RW_HEAD)
    same = (i[:, None] // CHUNK) == (i[None, :] // CHUNK)
    diff = (i[:, None] % CHUNK) - (i[None, :] % CHUNK)
    per_dir = [jnp.stack([same & (sg * diff > 0), same & (sg * diff >= 0)]) for sg in (1, -1)]
    return jnp.stack(per_dir).astype(F32)


def _rwkv_scan(p, t, lc, mu2, w0, wup_pad, a0, aup_pad, k_k, k_a, r_k, e64):
    tt = t + lc
    gw = RW_GROUP * RW_HEAD
    nl, nc = t // CHUNK, lc // CHUNK
    nch = nl + nc
    last8 = tt // 8 - 1
    ch = functools.partial(_chunk_of, nl=nl, nc=nc)
    vec = pl.BlockSpec((1, MIX_W), lambda d, n: (0, 0))
    out_s = pl.BlockSpec((None, CHUNK, MIX_W), lambda d, n: (d, ch(d, n), 0))
    return pl.pallas_call(
        functools.partial(_rwkv_kernel, nl=nl, nc=nc),
        grid=(2, nch),
        in_specs=[
            pl.BlockSpec((CHUNK, 3 * MIX_W), lambda d, n: (ch(d, n), 0)),
            pl.BlockSpec((8, 3 * MIX_W), lambda d, n: (jnp.maximum(ch(d, n) * 8 - 1, 0), 0)),
            pl.BlockSpec((8, 3 * MIX_W), lambda d, n: (jnp.minimum(ch(d, n) * 8 + 8, last8), 0)),
            pl.BlockSpec((CHUNK, MIX_W), lambda d, n: (ch(d, n), CB_SMALL)),
            pl.BlockSpec((2, 3 * MIX_W), lambda d, n: (0, 0)),
            pl.BlockSpec((None, 1, MIX_W), lambda d, n: (d, 0, 0)),
            pl.BlockSpec((None, LANE, MIX_W), lambda d, n: (d, 0, 0)),
            pl.BlockSpec((None, 1, MIX_W), lambda d, n: (d, 0, 0)),
            pl.BlockSpec((None, LANE, MIX_W), lambda d, n: (d, 0, 0)),
            vec, vec, vec,
            pl.BlockSpec((MIX_W, MIX_W), lambda d, n: (0, 0)),
            pl.BlockSpec((None, 2, gw, gw), lambda d, n: (d, 0, 0, 0)),
        ],
        out_specs=[out_s, out_s],
        out_shape=[jax.ShapeDtypeStruct((2, tt, MIX_W), F32)] * 2,
        scratch_shapes=[pltpu.VMEM((RW_HEADS // RW_GROUP, gw, gw), F32)],
        compiler_params=_cparams(('arbitrary', 'arbitrary')), name='rwkv_scan',
    )(p, p, p, p, mu2, w0, wup_pad, a0, aup_pad, k_k, k_a, r_k, e64, _rwkv_order_masks())


def _gla_kernel(qk_ref, v_ref, sm_ref, cos_ref, sin_ref, aup_ref, ab_ref, o_ref, s_ref, *, nl, nc):
    d = pl.program_id(0)
    n = pl.program_id(1)

    @pl.when(n == 0)
    def _():
        s_ref[...] = jnp.zeros_like(s_ref)

    qk = qk_ref[...]
    lane = lax.broadcasted_iota(jnp.int32, (CHUNK, 2 * GLA_HEADS * GLA_DK), 1)
    partner = jnp.where((lane & 1) == 0, pltpu.roll(qk, 2 * GLA_HEADS * GLA_DK - 1, axis=1),
                        pltpu.roll(qk, 1, axis=1))
    qk = qk * cos_ref[...] + partner * sin_ref[...]
    hk = GLA_HEADS * GLA_DK
    q = qk[:, 0:hk] * (GLA_DK ** -0.5)
    k = qk[:, hk:]
    v = v_ref[...]
    g = -_softplus(-(_mm3(sm_ref[:, 384:512], aup_ref[...]) + ab_ref[...])) / GLA_TAU
    _, incl = _order_masks(d)
    b = _mm_wx(jnp.where(incl, 1.0, 0.0).astype(BF16), g)
    b_last = jnp.sum(g, axis=0, keepdims=True)
    q_e = q * jnp.exp(b)
    k_e = k * jnp.exp(-b)
    k_end = k * jnp.exp(b_last - b)
    dec = jnp.exp(b_last)
    for h in range(GLA_HEADS):
        sk = slice(h * GLA_DK, (h + 1) * GLA_DK)
        sv = slice(h * GLA_DV, (h + 1) * GLA_DV)
        att = jnp.where(incl, _mm_nt(q_e[:, sk], k_e[:, sk]), 0.0)
        st = s_ref[h]
        o_ref[:, sv] = _mm(att, v[:, sv]) + _mm_nt(q_e[:, sk], st)
        s_ref[h] = st * dec[:, sk] + _mm_tn(v[:, sv], k_end[:, sk])


def _gla_scan(p, t, lc, cos_t, sin_t, aup_pad, a_b):
    tt = t + lc
    nl, nc = t // CHUNK, lc // CHUNK
    ch = functools.partial(_chunk_of, nl=nl, nc=nc)
    blk = lambda cb: pl.BlockSpec((CHUNK, MIX_W), lambda d, n: (ch(d, n), cb))
    return pl.pallas_call(
        functools.partial(_gla_kernel, nl=nl, nc=nc),
        grid=(2, nl + nc),
        in_specs=[blk(CB_GLA_QK), blk(CB_GLA_V), blk(CB_SMALL), blk(0), blk(0),
                  pl.BlockSpec((None, LANE, 2 * LANE), lambda d, n: (d, 0, 0)),
                  pl.BlockSpec((None, 1, 2 * LANE), lambda d, n: (d, 0, 0))],
        out_specs=pl.BlockSpec((None, CHUNK, MIX_W), lambda d, n: (d, ch(d, n), 0)),
        out_shape=jax.ShapeDtypeStruct((2, tt, MIX_W), F32),
        scratch_shapes=[pltpu.VMEM((GLA_HEADS, GLA_DV, GLA_DK), F32)],
        compiler_params=_cparams(('arbitrary', 'arbitrary')), name='gla_scan',
    )(p, p, p, cos_t, sin_t, aup_pad, a_b)


def _head_norm(y, e_bf, width, eps):
    mu = _mm_xw(y, e_bf) * (1.0 / width)
    dl = y - mu
    var = _mm_xw(dl * dl, e_bf) * (1.0 / width)
    return dl * lax.rsqrt(var + eps)


def _mix_finish_kernel(y0_ref, y1_ref, b0_ref, b1_ref, sm_ref, gup_ref, lng_ref, lnb_ref, e64_ref,
                       o0_ref, o1_ref, gr_ref, gng_ref, e128_ref, a_ref, b_ref):
    yn = _head_norm(y0_ref[...] + y1_ref[...], e64_ref[...], RW_HEAD, RW_GN_EPS)
    yn = yn * lng_ref[...] + lnb_ref[...] + b0_ref[...] + b1_ref[...]
    gate = _mm(_sigmoid(sm_ref[:, 256:384]), gup_ref[...])
    a_ref[...] = (yn * gate).astype(BF16)
    on = _head_norm(o0_ref[...] + o1_ref[...], e128_ref[...], GLA_DV, GN_EPS) * gng_ref[...]
    gr = gr_ref[...]
    b_ref[...] = (on * (gr * _sigmoid(gr))).astype(BF16)


def _mix_finish(p, rw_y, rw_bonus, gla_o, g_up, ln_g, ln_b, gn_g, e64, e128, t):
    tt = p.shape[0]
    bm = _row_tile(t, tt - t)
    dblk = lambda d: pl.BlockSpec((None, bm, MIX_W), lambda i: (d, i, 0))
    vec = pl.BlockSpec((1, MIX_W), lambda i: (0, 0))
    mat = pl.BlockSpec((MIX_W, MIX_W), lambda i: (0, 0))
    out = pl.BlockSpec((bm, MIX_W), lambda i: (i, 0))
    return pl.pallas_call(
        _mix_finish_kernel,
        grid=(tt // bm,),
        in_specs=[dblk(0), dblk(1), dblk(0), dblk(1),
                  pl.BlockSpec((bm, MIX_W), lambda i: (i, CB_SMALL)),
                  pl.BlockSpec((RW_GATE_RANK, MIX_W), lambda i: (0, 0)), vec, vec, mat,
                  dblk(0), dblk(1), pl.BlockSpec((bm, MIX_W), lambda i: (i, CB_GLA_R)), vec, mat],
        out_specs=[out, out],
        out_shape=[jax.ShapeDtypeStruct((tt, MIX_W), BF16)] * 2,
        compiler_params=_cparams(('parallel',)), name='mix_finish',
    )(rw_y, rw_y, rw_bonus, rw_bonus, p, g_up, ln_g, ln_b, e64, gla_o, gla_o, p, gn_g, e128)


def _sgu_kernel(u_ref, v_ref, lng_ref, lnb_ref, ws_ref, bs_ref, o_ref):
    u = _gelu(u_ref[...])
    v = _gelu(v_ref[...])
    mu = jnp.mean(v, axis=-1, keepdims=True)
    dl = v - mu
    var = jnp.mean(dl * dl, axis=-1, keepdims=True)
    vn = (dl * lax.rsqrt(var + GN_EPS) * lng_ref[...] + lnb_ref[...]).astype(BF16)
    lane = lax.broadcasted_iota(jnp.int32, (1, MIX_W), 1)
    s = bs_ref[...]
    for g in range(SGU_GROUPS):
        s = s + jnp.where((lane >> 6) == g, jnp.dot(ws_ref[g].astype(BF16), vn, preferred_element_type=F32), 0.0)
    o_ref[...] = (u * s).astype(BF16)


def _sgu(p, ln_g, ln_b, w_s, b_full):
    tt = p.shape[0]
    vec = pl.BlockSpec((1, MIX_W), lambda i: (0, 0))
    return pl.pallas_call(
        _sgu_kernel,
        grid=(tt // SGU_CHUNK,),
        in_specs=[pl.BlockSpec((SGU_CHUNK, MIX_W), lambda i: (i, CB_SGU_U)),
                  pl.BlockSpec((SGU_CHUNK, MIX_W), lambda i: (i, CB_SGU_V)), vec, vec,
                  pl.BlockSpec((SGU_GROUPS, SGU_CHUNK, SGU_CHUNK), lambda i: (0, 0, 0)),
                  pl.BlockSpec((SGU_CHUNK, MIX_W), lambda i: (0, 0))],
        out_specs=pl.BlockSpec((SGU_CHUNK, MIX_W), lambda i: (i, 0)),
        out_shape=jax.ShapeDtypeStruct((tt, MIX_W), BF16),
        compiler_params=_cparams(('parallel',)), name='sgu',
    )(p, p, ln_g, ln_b, w_s, b_full)


def _na_bias_kernel(rpb_ref, o_ref, *, wh):
    h = pl.program_id(0)
    n_dc = 2 * NA_WIN_W - 1
    n_dr = 2 * NA_WIN_H - 1
    shape = (GRID_W, 2 * GRID_W)
    c = lax.broadcasted_iota(jnp.int32, shape, 0)
    lane = lax.broadcasted_iota(jnp.int32, shape, 1)
    x = lane & (GRID_W - 1)
    dc = jnp.clip(x - c + (NA_WIN_W - 1), 0, 2 * NA_WIN_W - 2)
    key = (lane >> 6) * n_dc + dc
    cs = jnp.clip(c - NA_WIN_W // 2, 0, GRID_W - NA_WIN_W)
    ok = jnp.logical_and(x >= cs, x < cs + NA_WIN_W)
    pairs = []
    for dr in range(n_dr - 1):
        base = h * (n_dr * n_dc) + dr * n_dc
        tile = lax.fori_loop(0, 2 * n_dc, lambda j, acc: jnp.where(key == j, rpb_ref[base + j], acc),
                             jnp.zeros(shape, F32))
        pairs.append(jnp.where(ok, tile, NEG_INF))
    for dr0 in range(n_dr - wh + 1):
        for jj in range(wh // 2):
            o_ref[dr0, 0, :, jj * 2 * GRID_W:(jj + 1) * 2 * GRID_W] = pairs[dr0 + 2 * jj]


def _na_bias(rpb, wh):
    n_dr0 = 2 * NA_WIN_H - wh
    return pl.pallas_call(
        functools.partial(_na_bias_kernel, wh=wh),
        grid=(NA_HEADS,),
        in_specs=[pl.BlockSpec(memory_space=pltpu.SMEM)],
        out_specs=pl.BlockSpec((n_dr0, 1, GRID_W, wh * GRID_W), lambda h: (0, h, 0, 0)),
        out_shape=jax.ShapeDtypeStruct((n_dr0, NA_HEADS, GRID_W, wh * GRID_W), F32),
        compiler_params=_cparams(('parallel',)), name='na_bias',
    )(rpb.reshape(-1))


def _na_kernel(*refs, wh):
    q_ref = refs[0]
    k_refs = refs[1:1 + wh]
    v_refs = refs[1 + wh:1 + 2 * wh]
    kc_ref, vc_ref, bias_ref, o_ref = refs[1 + 2 * wh:]
    q = q_ref[...] * (NA_HEAD ** -0.5)
    kw = jnp.concatenate([r[...].astype(BF16) for r in k_refs], axis=0)
    vw = jnp.concatenate([r[...].astype(BF16) for r in v_refs], axis=0)
    kc = kc_ref[...].astype(BF16)
    vc = vc_ref[...].astype(BF16)
    lane = lax.broadcasted_iota(jnp.int32, (1, MIX_W), 1)
    o = jnp.zeros((GRID_W, MIX_W), F32)
    for h in range(NA_HEADS):
        hm = (lane >> 6) == h
        qh = jnp.where(hm, q, 0.0).astype(BF16)
        sw = _mm_nt(qh, kw) + bias_ref[0, h]
        sc = _mm_nt(qh, kc)
        m = jnp.maximum(jnp.max(sw, axis=-1, keepdims=True), jnp.max(sc, axis=-1, keepdims=True))
        ew = jnp.exp(sw - m)
        ec = jnp.exp(sc - m)
        den = jnp.sum(ew, axis=-1, keepdims=True) + jnp.sum(ec, axis=-1, keepdims=True)
        oh = (_mm(ew, vw) + _mm(ec, vc)) / den
        o = jnp.where(hm, oh, o)
    o_ref[...] = o.astype(BF16)


def _na_ctx_kernel(q_ref, k_ref, v_ref, o_ref):
    q = q_ref[...] * (NA_HEAD ** -0.5)
    kc = k_ref[...].astype(BF16)
    vc = v_ref[...].astype(BF16)
    lane = lax.broadcasted_iota(jnp.int32, (1, MIX_W), 1)
    o = jnp.zeros(q.shape, F32)
    for h in range(NA_HEADS):
        hm = (lane >> 6) == h
        s = _mm_nt(jnp.where(hm, q, 0.0), kc)
        e = jnp.exp(s - jnp.max(s, axis=-1, keepdims=True))
        o = jnp.where(hm, _mm(e, vc) / jnp.sum(e, axis=-1, keepdims=True), o)
    o_ref[...] = o.astype(BF16)


def _na(p, bias, t, lc):
    rows = t // GRID_W
    wh = min(NA_WIN_H, rows)
    rs = lambda r: jnp.clip(r - wh // 2, 0, rows - wh)
    ctx_blk = t // lc
    kv = lambda cb: [pl.BlockSpec((GRID_W, MIX_W), functools.partial(lambda r, w, cb: (rs(r) + w, cb), w=w, cb=cb))
                     for w in range(wh)]
    lat = pl.pallas_call(
        functools.partial(_na_kernel, wh=wh),
        grid=(rows,),
        in_specs=([pl.BlockSpec((GRID_W, MIX_W), lambda r: (r, CB_NA_Q))] + kv(CB_NA_K) + kv(CB_NA_V)
                  + [pl.BlockSpec((lc, MIX_W), lambda r: (ctx_blk, CB_NA_K)),
                     pl.BlockSpec((lc, MIX_W), lambda r: (ctx_blk, CB_NA_V)),
                     pl.BlockSpec((1, NA_HEADS, GRID_W, wh * GRID_W),
                                  lambda r: (rs(r) - r + (NA_WIN_H - 1), 0, 0, 0))]),
        out_specs=pl.BlockSpec((GRID_W, MIX_W), lambda r: (r, 0)),
        out_shape=jax.ShapeDtypeStruct((t, MIX_W), BF16),
        compiler_params=_cparams(('parallel',)), name='na_latent',
    )(*([p] * (1 + 2 * wh + 2)), bias)
    cblk = lambda cb: pl.BlockSpec((lc, MIX_W), lambda i: (ctx_blk, cb))
    ctx = pl.pallas_call(
        _na_ctx_kernel,
        grid=(1,),
        in_specs=[cblk(CB_NA_Q), cblk(CB_NA_K), cblk(CB_NA_V)],
        out_specs=pl.BlockSpec((lc, MIX_W), lambda i: (0, 0)),
        out_shape=jax.ShapeDtypeStruct((lc, MIX_W), BF16),
        compiler_params=_cparams(('arbitrary',)), name='na_ctx',
    )(p, p, p)
    return jnp.concatenate([lat, ctx], axis=0)


def _merge_kernel(a0, a1, a2, a3, g0, g1, g2, g3, w_ref, o_ref, wb_ref):
    @pl.when(pl.program_id(1) == 0)
    def _():
        wb_ref[...] = w_ref[...].astype(BF16)
    acc = None
    for n, (a_ref, g_ref) in enumerate(((a0, g0), (a1, g1), (a2, g2), (a3, g3))):
        zn = jnp.dot(a_ref[...], wb_ref[n], preferred_element_type=F32) * _sigmoid(g_ref[...])
        acc = zn if acc is None else acc + zn
    o_ref[...] = acc.astype(BF16)


def _merge(ys, p, w_br, lyr):
    tt = p.shape[0]
    bn = 512
    bm = _pick(tt, (768, 384, 128))
    a_s = pl.BlockSpec((bm, MIX_W), lambda j, i: (i, 0))
    gs = [pl.BlockSpec((bm, bn), functools.partial(lambda j, i, n: (i, (GATE_OFF + n * D_MODEL) // bn + j), n=n))
          for n in range(N_BRANCH)]
    return pl.pallas_call(
        _merge_kernel,
        grid=(D_MODEL // bn, tt // bm),
        in_specs=[a_s] * 4 + gs + [pl.BlockSpec((None, N_BRANCH, MIX_W, bn), lambda j, i: (lyr, 0, 0, j))],
        out_specs=pl.BlockSpec((bm, bn), lambda j, i: (i, j)),
        out_shape=jax.ShapeDtypeStruct((tt, D_MODEL), BF16),
        scratch_shapes=[pltpu.VMEM((N_BRANCH, MIX_W, bn), BF16)],
        compiler_params=_cparams(('arbitrary', 'arbitrary')), name='merge',
    )(*ys, p, p, p, p, w_br)


def _reorder_w_in(w):
    off, pos = 0, {}
    for name, width in _REF_SPLITS:
        pos[name] = (off, width)
        off += width
    cols = []
    for name, width in _IN_ORDER:
        if name == 'pad':
            cols.append(jnp.zeros(w.shape[:-1] + (width,), w.dtype))
        else:
            o, wd = pos[name]
            assert wd == width
            cols.append(w[..., o:o + wd])
    return jnp.concatenate(cols, axis=-1)


def _block_diag_ones(width):
    i = jnp.arange(MIX_W) // width
    return (i[:, None] == i[None, :]).astype(BF16)


def _rope_tables(t, lc):
    tok = jnp.arange(t)
    pos = jnp.stack([tok // GRID_W, tok % GRID_W], axis=-1).astype(F32)
    nf = GLA_DK // 4
    inv = ROPE_BASE ** (-jnp.arange(nf, dtype=F32) / nf)
    ang = pos[:, :, None] * inv
    cos = jnp.repeat(jnp.cos(ang), 2, axis=-1).reshape(t, GLA_DK)
    sin = jnp.sin(ang)
    sin = jnp.stack([-sin, sin], axis=-1).reshape(t, GLA_DK)
    reps = 2 * GLA_HEADS
    cos = jnp.concatenate([jnp.tile(cos, (1, reps)), jnp.ones((lc, reps * GLA_DK), F32)], axis=0)
    sin = jnp.concatenate([jnp.tile(sin, (1, reps)), jnp.zeros((lc, reps * GLA_DK), F32)], axis=0)
    return cos, sin


def _pad_rank_rows(w_up, rank, rows):
    out = jnp.zeros((2, rows, w_up.shape[-1]), w_up.dtype)
    for d in range(2):
        out = out.at[d, d * rank:(d + 1) * rank].set(w_up[d])
    return out


def kernel(x, c, ctx, c_ctx, ada_w, ada_b, norm_g, w_in, rw_mu, rw_w0, rw_w_up, rw_a0, rw_a_up, rw_g_up, rw_k_k, rw_k_a, rw_r_k, rw_ln_g, rw_ln_b, gla_a_up, gla_a_b, gla_gn_g, sgu_ln_g, sgu_ln_b, sgu_w, sgu_b, na_rpb, w_br, w_o, ffn_w1, ffn_w3, ffn_w2, moe_router, moe_w1, moe_w3, moe_w2):
    assert x.shape[0] == 1 and x.shape[2] == D_MODEL
    t, lc = x.shape[1], ctx.shape[1]
    depth = ada_w.shape[0]
    assert t % max(lc, SGU_CHUNK) == 0 and lc % SGU_CHUNK == 0 and t % GRID_W == 0
    rows = t // GRID_W
    assert rows >= NA_WIN_H
    wh = NA_WIN_H
    xs = jnp.concatenate([x[0], ctx[0]], axis=0)
    cond8 = jnp.zeros((8, D_MODEL), F32).at[0].set(c[0]).at[1].set(c_ctx)
    e64 = _block_diag_ones(RW_HEAD)
    e128 = _block_diag_ones(GLA_DV)
    cos_t, sin_t = _rope_tables(t, lc)
    row1 = lambda v: v.reshape(1, -1)
    w_in_r = _reorder_w_in(w_in)

    for i in range(depth):
        mods = _adaln(cond8, ada_w, ada_b, i)[0:2].reshape(2, 6, D_MODEL)
        h = _norm_mod_call(xs, norm_g[i, 0], mods, 0, t)
        p = _matmul(h, w_in_r, i, 1024, (768, 384, 128), name='in_proj')
        mu2 = jnp.stack([rw_mu[i, :, 0].reshape(-1), rw_mu[i, :, 1].reshape(-1)])
        rw_y, rw_bonus = _rwkv_scan(
            p, t, lc, mu2, rw_w0[i].reshape(2, 1, MIX_W), _pad_rank_rows(rw_w_up[i], RW_DECAY_RANK, LANE),
            rw_a0[i].reshape(2, 1, MIX_W), _pad_rank_rows(rw_a_up[i], RW_ICLR_RANK, LANE),
            row1(rw_k_k[i]), row1(rw_k_a[i]), row1(rw_r_k[i]), e64)
        gla_o = _gla_scan(p, t, lc, cos_t, sin_t, _pad_rank_rows(gla_a_up[i], GLA_GATE_RANK, LANE),
                          gla_a_b[i].reshape(2, 1, 2 * LANE))
        y_a, y_b = _mix_finish(p, rw_y, rw_bonus, gla_o, rw_g_up[i], row1(rw_ln_g[i]), row1(rw_ln_b[i]),
                               row1(gla_gn_g[i]), e64, e128, t)
        y_s = _sgu(p, row1(sgu_ln_g[i]), row1(sgu_ln_b[i]), sgu_w[i], jnp.repeat(sgu_b[i].T, 64, axis=1))
        y_d = _na(p, _na_bias(na_rpb[i], wh), t, lc)
        z = _merge((y_a, y_b, y_s, y_d), p, w_br, i)
        y = _matmul(z, w_o, i, 1024, (768, 384, 128), name='out_proj')
        xs = _resid_call(xs, y, norm_g[i, 1], mods, 2, t)
        j = i // 2
        if i % 2 == 0:
            h = _norm_mod_call(xs, norm_g[i, 2], mods, 3, t)
            f = _ffn_down(_ffn_up(h, ffn_w1, ffn_w3, j), ffn_w2, j)
            xs = _resid_call(xs, f, norm_g[i, 3], mods, 5, t)
        else:
            h32, route = _norm_mod_call(xs, norm_g[i, 2], mods, 3, t, router=moe_router[j])
            xs = _moe_layer(xs, h32, route, moe_w1, moe_w3, moe_w2, j, norm_g[i, 3], mods, 5, t)
    return xs[0:t][None]
```

```python
import functools

import jax
import jax.numpy as jnp
from jax import lax
from jax.experimental import pallas as pl
from jax.experimental.pallas import tpu as pltpu

F32 = jnp.float32
BF16 = jnp.bfloat16

D_MODEL = 2048
GRID_W = 64
N_BRANCH = 4
MIX_W = D_MODEL // 4
RW_HEAD = 64
RW_HEADS = MIX_W // RW_HEAD
RW_GROUP = 4
RW_DECAY_RANK = 64
RW_ICLR_RANK = 64
RW_GATE_RANK = 128
RW_GN_EPS = 64e-5
GLA_HEADS = 4
GLA_DV = MIX_W // GLA_HEADS
GLA_DK = GLA_DV // 2
GLA_GATE_RANK = 16
GLA_TAU = 16.0
GN_EPS = 1e-5
ROPE_BASE = 10000.0
SGU_GROUPS = MIX_W // 64
SGU_CHUNK = 128
NA_HEAD = 64
NA_HEADS = MIX_W // NA_HEAD
NA_WIN_H = 8
NA_WIN_W = 16
FFN_DIM = 7 * D_MODEL // 2
N_EXPERTS = 8
TOP_K = 2
EPS = 1e-6
NEG_INF = -1e30

RT_G1, RT_G2, RT_I1, RT_I2 = 8, 9, 10, 11
MOE_BM = 512
CHUNK = 64
LANE = 128
VMEM_LIMIT = 56 * 2 ** 20

_IN_ORDER = (
    ('rw_r', 512), ('rw_k', 512), ('rw_v', 512), ('gla_v', 512), ('gla_r', 512),
    ('sgu_u', 512), ('sgu_v', 512), ('na_q', 512), ('na_k', 512), ('na_v', 512),
    ('gla_q', 256), ('gla_k', 256), ('rw_wd', 128), ('rw_ad', 128), ('rw_gd', 128),
    ('gla_ad', 32), ('pad', 96), ('gate', N_BRANCH * D_MODEL),
)
_REF_SPLITS = (
    ('rw_r', MIX_W), ('rw_k', MIX_W), ('rw_v', MIX_W),
    ('rw_wd', 2 * RW_DECAY_RANK), ('rw_ad', 2 * RW_ICLR_RANK), ('rw_gd', RW_GATE_RANK),
    ('gla_q', GLA_HEADS * GLA_DK), ('gla_k', GLA_HEADS * GLA_DK), ('gla_v', MIX_W),
    ('gla_r', MIX_W), ('gla_ad', 2 * GLA_GATE_RANK),
    ('sgu_u', MIX_W), ('sgu_v', MIX_W),
    ('na_q', MIX_W), ('na_k', MIX_W), ('na_v', MIX_W),
    ('gate', N_BRANCH * D_MODEL),
)
N_IN_P = sum(w for _, w in _IN_ORDER)
CB_GLA_V, CB_GLA_R, CB_SGU_U, CB_SGU_V, CB_NA_Q, CB_NA_K, CB_NA_V, CB_GLA_QK, CB_SMALL = 3, 4, 5, 6, 7, 8, 9, 10, 11
GATE_OFF = 6144


def _cparams(sem):
    return pltpu.CompilerParams(dimension_semantics=sem, vmem_limit_bytes=VMEM_LIMIT)


def _pick(m, cands):
    for c in cands:
        if m % c == 0:
            return c
    raise ValueError(f'no tile for {m}')


def _mm(a, b):
    return jnp.dot(a.astype(BF16), b.astype(BF16), preferred_element_type=F32)


def _mm_nt(a, b):
    return lax.dot_general(a.astype(BF16), b.astype(BF16), (((1,), (1,)), ((), ())),
                           preferred_element_type=F32)


def _mm_tn(a, b):
    return lax.dot_general(a.astype(BF16), b.astype(BF16), (((0,), (0,)), ((), ())),
                           preferred_element_type=F32)


def _split2(x):
    hi = x.astype(BF16)
    lo = (x - hi.astype(F32)).astype(BF16)
    return hi, lo


def _mm_xw(x, w_bf):
    hi, lo = _split2(x)
    return (jnp.dot(hi, w_bf, preferred_element_type=F32)
            + jnp.dot(lo, w_bf, preferred_element_type=F32))


def _mm_wx(w_bf, x):
    hi, lo = _split2(x)
    return (jnp.dot(w_bf, hi, preferred_element_type=F32)
            + jnp.dot(w_bf, lo, preferred_element_type=F32))


def _mm3(a, b):
    ah, al = _split2(a)
    bh, bl = _split2(b)
    return (jnp.dot(ah, bh, preferred_element_type=F32)
            + jnp.dot(ah, bl, preferred_element_type=F32)
            + jnp.dot(al, bh, preferred_element_type=F32))


def _sigmoid(x):
    return 1.0 / (1.0 + jnp.exp(-x))


def _softplus(x):
    return jnp.maximum(x, 0.0) + jnp.log1p(jnp.exp(-jnp.abs(x)))


def _gelu(x):
    return 0.5 * x * (1.0 + lax.erf(x * (0.5 ** 0.5)))


def _order_masks(d):
    t = lax.broadcasted_iota(jnp.int32, (CHUNK, CHUNK), 0)
    s = lax.broadcasted_iota(jnp.int32, (CHUNK, CHUNK), 1)
    diff = (t - s) * jnp.where(d == 0, 1, -1)
    return diff > 0, diff >= 0


def _chunk_of(d, n, nl, nc):
    fwd = jnp.where(n < nc, nl + n, n - nc)
    bwd = jnp.where(n < nc, nl + nc - 1 - n, nl - 1 - (n - nc))
    return jnp.where(d == 0, fwd, bwd)


def _ada_kernel(c_ref, w_ref, b_ref, o_ref):
    cnd = c_ref[...]
    a = cnd * _sigmoid(cnd)
    o_ref[...] = _mm(a, w_ref[...]) + b_ref[...]


def _adaln(cond8, ada_w, ada_b, lyr):
    depth, _, n = ada_w.shape
    bn = 1536
    return pl.pallas_call(
        _ada_kernel,
        grid=(n // bn,),
        in_specs=[pl.BlockSpec((8, D_MODEL), lambda j: (0, 0)),
                  pl.BlockSpec((None, D_MODEL, bn), lambda j: (lyr, 0, j)),
                  pl.BlockSpec((None, 1, bn), lambda j: (lyr, 0, j))],
        out_specs=pl.BlockSpec((8, bn), lambda j: (0, j)),
        out_shape=jax.ShapeDtypeStruct((8, n), F32),
        compiler_params=_cparams(('arbitrary',)),
        name='adaln',
    )(cond8, ada_w, ada_b.reshape(depth, 1, n))


def _norm_mod(x, g, m, k0):
    y = x * lax.rsqrt(jnp.mean(x * x, axis=-1, keepdims=True) + EPS) * g
    return y * (1.0 + m[k0 + 1:k0 + 2]) + m[k0:k0 + 1]


def _pack_halves(h):
    n = h.shape[1] // 2
    bits = lax.bitcast_convert_type(h.astype(BF16).astype(F32), jnp.uint32)
    return (bits[:, 0:n] >> 16) | bits[:, n:]


def _unpack_halves(w):
    lo = lax.bitcast_convert_type(w << 16, F32)
    hi = lax.bitcast_convert_type(w & jnp.uint32(0xFFFF0000), F32)
    return jnp.concatenate([lo, hi], axis=1).astype(BF16)


def _norm_mod_kernel(x_ref, g_ref, m_ref, o_ref, *, k0):
    o_ref[...] = _norm_mod(x_ref[...], g_ref[...], m_ref[0], k0).astype(BF16)


def _norm_mod_route_kernel(x_ref, g_ref, m_ref, r_ref, o_ref, gate_ref, *, k0):
    h = _norm_mod(x_ref[...], g_ref[...], m_ref[0], k0)
    o_ref[...] = _pack_halves(h)
    logits = _mm3(h, r_ref[...])
    lane = lax.broadcasted_iota(jnp.int32, logits.shape, 1)
    neg = jnp.float32(-jnp.inf)
    l1 = jnp.where(lane < N_EXPERTS, logits, neg)
    m1 = jnp.max(l1, axis=-1, keepdims=True)
    i1 = jnp.min(jnp.where(l1 == m1, lane, LANE), axis=-1, keepdims=True)
    l2 = jnp.where(lane == i1, neg, l1)
    m2 = jnp.max(l2, axis=-1, keepdims=True)
    i2 = jnp.min(jnp.where(l2 == m2, lane, LANE), axis=-1, keepdims=True)
    e2 = jnp.exp(m2 - m1)
    den = 1.0 + e2
    sel = jnp.where(jnp.logical_or(lane == i1, lane == i2), 1.0, 0.0)
    sel = jnp.where(lane == RT_G1, 1.0 / den, jnp.where(lane == RT_G2, e2 / den, sel))
    gate_ref[...] = jnp.where(lane == RT_I1, i1.astype(F32), jnp.where(lane == RT_I2, i2.astype(F32), sel))


def _row_tile(t, lc):
    return _pick(lc, (256, 128))


def _norm_mod_call(x, g, mods, k0, t, router=None):
    tt = x.shape[0]
    bm = _row_tile(t, tt - t)
    nlt = t // bm
    xs = pl.BlockSpec((bm, D_MODEL), lambda i: (i, 0))
    gs = pl.BlockSpec((1, D_MODEL), lambda i: (0, 0))
    ms = pl.BlockSpec((1, 6, D_MODEL), lambda i: (jnp.where(i >= nlt, 1, 0), 0, 0))
    if router is None:
        return pl.pallas_call(
            functools.partial(_norm_mod_kernel, k0=k0),
            grid=(tt // bm,), in_specs=[xs, gs, ms], out_specs=xs,
            out_shape=jax.ShapeDtypeStruct((tt, D_MODEL), BF16),
            compiler_params=_cparams(('parallel',)), name='norm_mod',
        )(x, g.reshape(1, D_MODEL), mods)
    rpad = jnp.pad(router, ((0, 0), (0, LANE - N_EXPERTS)))
    return pl.pallas_call(
        functools.partial(_norm_mod_route_kernel, k0=k0),
        grid=(tt // bm,),
        in_specs=[xs, gs, ms, pl.BlockSpec((D_MODEL, LANE), lambda i: (0, 0))],
        out_specs=[pl.BlockSpec((bm, D_MODEL // 2), lambda i: (i, 0)), pl.BlockSpec((bm, LANE), lambda i: (i, 0))],
        out_shape=[jax.ShapeDtypeStruct((tt, D_MODEL // 2), jnp.uint32), jax.ShapeDtypeStruct((tt, LANE), F32)],
        compiler_params=_cparams(('parallel',)), name='norm_mod_route',
    )(x, g.reshape(1, D_MODEL), mods, rpad)


def _resid_kernel(x_ref, y_ref, g_ref, m_ref, o_ref, *, kg):
    y = y_ref[...]
    n = y * lax.rsqrt(jnp.mean(y * y, axis=-1, keepdims=True) + EPS) * g_ref[...]
    o_ref[...] = x_ref[...] + m_ref[0][kg:kg + 1] * n


def _resid_call(x, y, g, mods, kg, t):
    tt = x.shape[0]
    bm = _row_tile(t, tt - t)
    nlt = t // bm
    xs = pl.BlockSpec((bm, D_MODEL), lambda i: (i, 0))
    return pl.pallas_call(
        functools.partial(_resid_kernel, kg=kg),
        grid=(tt // bm,),
        in_specs=[xs, xs, pl.BlockSpec((1, D_MODEL), lambda i: (0, 0)),
                  pl.BlockSpec((1, 6, D_MODEL), lambda i: (jnp.where(i >= nlt, 1, 0), 0, 0))],
        out_specs=xs,
        out_shape=jax.ShapeDtypeStruct((tt, D_MODEL), F32),
        compiler_params=_cparams(('parallel',)), name='resid_norm',
    )(x, y, g.reshape(1, D_MODEL), mods)


def _mm_kernel(a_ref, w_ref, o_ref, wb_ref):
    @pl.when(pl.program_id(1) == 0)
    def _():
        wb_ref[...] = w_ref[...].astype(BF16)
    o_ref[...] = jnp.dot(a_ref[...], wb_ref[...], preferred_element_type=F32).astype(o_ref.dtype)


def _matmul(a, w, lyr, bn, bm_cands, out_dtype=F32, name='matmul'):
    m, k = a.shape
    n = w.shape[2]
    bm = _pick(m, bm_cands)
    return pl.pallas_call(
        _mm_kernel,
        grid=(n // bn, m // bm),
        in_specs=[pl.BlockSpec((bm, k), lambda j, i: (i, 0)),
                  pl.BlockSpec((None, k, bn), lambda j, i: (lyr, 0, j))],
        out_specs=pl.BlockSpec((bm, bn), lambda j, i: (i, j)),
        out_shape=jax.ShapeDtypeStruct((m, n), out_dtype),
        scratch_shapes=[pltpu.VMEM((k, bn), BF16)],
        compiler_params=_cparams(('arbitrary', 'arbitrary')), name=name,
    )(a, w)


def _ffn_up_kernel(a_ref, w1_ref, w3_ref, o_ref, w1b, w3b):
    @pl.when(pl.program_id(1) == 0)
    def _():
        w1b[...] = w1_ref[...].astype(BF16)
        w3b[...] = w3_ref[...].astype(BF16)
    a = a_ref[...]
    h1 = jnp.dot(a, w1b[...], preferred_element_type=F32)
    h3 = jnp.dot(a, w3b[...], preferred_element_type=F32)
    o_ref[...] = (h1 * _sigmoid(h1) * h3).astype(BF16)


def _ffn_up(h, w1, w3, e):
    m = h.shape[0]
    bn = 512
    bm = _pick(m, (768, 384, 128))
    ws = pl.BlockSpec((None, D_MODEL, bn), lambda j, i: (e, 0, j))
    return pl.pallas_call(
        _ffn_up_kernel,
        grid=(FFN_DIM // bn, m // bm),
        in_specs=[pl.BlockSpec((bm, D_MODEL), lambda j, i: (i, 0)), ws, ws],
        out_specs=pl.BlockSpec((bm, bn), lambda j, i: (i, j)),
        out_shape=jax.ShapeDtypeStruct((m, FFN_DIM), BF16),
        scratch_shapes=[pltpu.VMEM((D_MODEL, bn), BF16), pltpu.VMEM((D_MODEL, bn), BF16)],
        compiler_params=_cparams(('arbitrary', 'arbitrary')), name='ffn_up',
    )(h, w1, w3)


def _ffn_down_kernel(a_ref, w_ref, o_ref):
    part = jnp.dot(a_ref[...], w_ref[...].astype(BF16), preferred_element_type=F32)

    @pl.when(pl.program_id(1) == 0)
    def _():
        o_ref[...] = part

    @pl.when(pl.program_id(1) > 0)
    def _():
        o_ref[...] += part


def _ffn_down(u, w2, e):
    m = u.shape[0]
    bk = 512
    bm = max(d for d in range(LANE, 1408 + 1, LANE) if m % d == 0)
    return pl.pallas_call(
        _ffn_down_kernel,
        grid=(m // bm, FFN_DIM // bk),
        in_specs=[pl.BlockSpec((bm, bk), lambda i, k: (i, k)),
                  pl.BlockSpec((None, bk, D_MODEL), lambda i, k: (e, k, 0))],
        out_specs=pl.BlockSpec((bm, D_MODEL), lambda i, k: (i, 0)),
        out_shape=jax.ShapeDtypeStruct((m, D_MODEL), F32),
        compiler_params=_cparams(('parallel', 'arbitrary')), name='ffn_down',
    )(u, w2)


def _moe_rank_kernel(r_ref, dest_ref, cnt_ref, carry_ref, tot_ref):
    ph = pl.program_id(0)
    i = pl.program_id(1)
    r = r_ref[...]
    bm = r.shape[0]
    lane = lax.broadcasted_iota(jnp.int32, (1, LANE), 1)
    oh = jnp.where(lane < N_EXPERTS, r, 0.0)
    colsum = jnp.sum(oh, axis=0, keepdims=True)

    @pl.when(jnp.logical_and(ph == 0, i == 0))
    def _():
        carry_ref[...] = jnp.zeros_like(carry_ref)

    @pl.when(jnp.logical_and(ph == 1, i == 0))
    def _():
        tot_ref[...] = carry_ref[...]
        carry_ref[...] = jnp.zeros_like(carry_ref)

    @pl.when(ph == 1)
    def _():
        tot = tot_ref[...]
        padded = jnp.floor((tot + (MOE_BM - 1)) * (1.0 / MOE_BM)) * MOE_BM
        a = lax.broadcasted_iota(jnp.int32, (LANE, LANE), 0)
        b = lax.broadcasted_iota(jnp.int32, (LANE, LANE), 1)
        upper = jnp.where(a < b, 1.0, 0.0).astype(BF16)
        offs = _mm_xw(jnp.broadcast_to(padded, (8, LANE)), upper)[0:1]
        tr = lax.broadcasted_iota(jnp.int32, (bm, bm), 0)
        ts = lax.broadcasted_iota(jnp.int32, (bm, bm), 1)
        before = jnp.dot(jnp.where(ts < tr, 1.0, 0.0).astype(BF16), oh.astype(BF16),
                         preferred_element_type=F32)
        slot = before + carry_ref[...] + offs
        lane_f = lane.astype(F32)
        d1 = jnp.sum(jnp.where(lane_f == r[:, RT_I1:RT_I1 + 1], slot, 0.0), axis=-1, keepdims=True)
        d2 = jnp.sum(jnp.where(lane_f == r[:, RT_I2:RT_I2 + 1], slot, 0.0), axis=-1, keepdims=True)
        dest_ref[...] = jnp.where(lane == 0, d1, jnp.where(lane == 1, d2, 0.0)).astype(jnp.int32)
        cnt_ref[...] = jnp.broadcast_to(tot, (8, LANE))

    carry_ref[...] = carry_ref[...] + colsum


def _moe_rank(route, bm):
    tt = route.shape[0]
    return pl.pallas_call(
        _moe_rank_kernel,
        grid=(2, tt // bm),
        in_specs=[pl.BlockSpec((bm, LANE), lambda ph, i: (i, 0))],
        out_specs=[pl.BlockSpec((bm, LANE), lambda ph, i: (i * ph, 0)),
                   pl.BlockSpec((8, LANE), lambda ph, i: (0, 0))],
        out_shape=[jax.ShapeDtypeStruct((tt, LANE), jnp.int32), jax.ShapeDtypeStruct((8, LANE), F32)],
        scratch_shapes=[pltpu.VMEM((1, LANE), F32), pltpu.VMEM((1, LANE), F32)],
        compiler_params=_cparams(('arbitrary', 'arbitrary')), name='moe_rank',
    )(route)


def _row_copy(src, dst, sem):
    return pltpu.make_async_copy(src, dst, sem)


def _moe_scatter_kernel(dest_ref, h_ref, init_hbm, xs_hbm, sem, *, bm):
    del init_hbm
    base = pl.program_id(0) * bm

    def issue(r, carry):
        t = base + r
        for kk in range(TOP_K):
            _row_copy(h_ref.at[pl.ds(r, 1)], xs_hbm.at[pl.ds(dest_ref[TOP_K * t + kk], 1)], sem).start()
        return carry
    lax.fori_loop(0, bm, issue, 0)

    def drain(r, carry):
        for kk in range(TOP_K):
            _row_copy(h_ref.at[pl.ds(0, 1)], xs_hbm.at[pl.ds(0, 1)], sem).wait()
        return carry
    lax.fori_loop(0, bm, drain, 0)


def _moe_scatter(dest_flat, h32, ns, bm):
    tt, w = h32.shape
    return pl.pallas_call(
        functools.partial(_moe_scatter_kernel, bm=bm),
        grid_spec=pltpu.PrefetchScalarGridSpec(
            num_scalar_prefetch=1, grid=(tt // bm,),
            in_specs=[pl.BlockSpec((bm, w), lambda i, d: (i, 0)), pl.BlockSpec(memory_space=pl.ANY)],
            out_specs=pl.BlockSpec(memory_space=pl.ANY),
            scratch_shapes=[pltpu.SemaphoreType.DMA(())]),
        out_shape=jax.ShapeDtypeStruct((ns, w), h32.dtype),
        input_output_aliases={2: 0},
        compiler_params=_cparams(('arbitrary',)), name='moe_scatter',
    )(dest_flat, h32, jnp.zeros((ns, w), h32.dtype))


def _moe_up_kernel(te_ref, nv_ref, a_ref, w1_ref, w3_ref, o_ref, w1b, w3b):
    m = pl.program_id(1)

    @pl.when(jnp.logical_or(m == 0, te_ref[m] != te_ref[jnp.maximum(m - 1, 0)]))
    def _():
        w1b[...] = w1_ref[...].astype(BF16)
        w3b[...] = w3_ref[...].astype(BF16)

    @pl.when(m < nv_ref[0])
    def _():
        a = _unpack_halves(a_ref[...])
        h1 = jnp.dot(a, w1b[...], preferred_element_type=F32)
        h3 = jnp.dot(a, w3b[...], preferred_element_type=F32)
        o_ref[...] = (h1 * _sigmoid(h1) * h3).astype(BF16)

    @pl.when(m >= nv_ref[0])
    def _():
        o_ref[...] = jnp.zeros_like(o_ref)


def _moe_down_kernel(te_ref, nv_ref, a_ref, w_ref, o_ref, wb_ref, *, split):
    m = pl.program_id(1)
    grp = m // split

    @pl.when(jnp.logical_or(m == 0, te_ref[grp] != te_ref[jnp.maximum((m - 1) // split, 0)]))
    def _():
        wb_ref[...] = w_ref[...].astype(BF16)

    @pl.when(grp < nv_ref[0])
    def _():
        o_ref[...] = jnp.dot(a_ref[...], wb_ref[...], preferred_element_type=F32)

    @pl.when(grp >= nv_ref[0])
    def _():
        o_ref[...] = jnp.zeros_like(o_ref)


def _moe_grouped_ffn(tile_expert, n_valid, xs, w1, w3, w2, lyr):
    ns = xs.shape[0]
    n_tiles = ns // MOE_BM
    bn = 512
    ws = pl.BlockSpec((None, None, D_MODEL, bn), lambda j, m, te, nv: (lyr, te[m], 0, j))
    u = pl.pallas_call(
        _moe_up_kernel,
        grid_spec=pltpu.PrefetchScalarGridSpec(
            num_scalar_prefetch=2, grid=(FFN_DIM // bn, n_tiles),
            in_specs=[pl.BlockSpec((MOE_BM, D_MODEL // 2), lambda j, m, te, nv: (m, 0)), ws, ws],
            out_specs=pl.BlockSpec((MOE_BM, bn), lambda j, m, te, nv: (m, j)),
            scratch_shapes=[pltpu.VMEM((D_MODEL, bn), BF16), pltpu.VMEM((D_MODEL, bn), BF16)]),
        out_shape=jax.ShapeDtypeStruct((ns, FFN_DIM), BF16),
        compiler_params=_cparams(('arbitrary', 'arbitrary')), name='moe_up',
    )(tile_expert, n_valid, xs, w1, w3)
    bn = 512
    split = 2
    bm = MOE_BM // split
    return pl.pallas_call(
        functools.partial(_moe_down_kernel, split=split),
        grid_spec=pltpu.PrefetchScalarGridSpec(
            num_scalar_prefetch=2, grid=(D_MODEL // bn, n_tiles * split),
            in_specs=[pl.BlockSpec((bm, FFN_DIM), lambda j, m, te, nv: (m, 0)),
                      pl.BlockSpec((None, None, FFN_DIM, bn), lambda j, m, te, nv: (lyr, te[m // split], 0, j))],
            out_specs=pl.BlockSpec((bm, bn), lambda j, m, te, nv: (m, j)),
            scratch_shapes=[pltpu.VMEM((FFN_DIM, bn), BF16)]),
        out_shape=jax.ShapeDtypeStruct((ns, D_MODEL), F32),
        compiler_params=_cparams(('arbitrary', 'arbitrary')), name='moe_down',
    )(tile_expert, n_valid, u, w2)


def _moe_combine_kernel(dest_ref, ys_hbm, r_ref, x_ref, g_ref, m_ref, o_ref, buf, sem, *, bm, kg):
    base = pl.program_id(0) * bm

    def issue(r, carry):
        t = base + r
        for kk in range(TOP_K):
            _row_copy(ys_hbm.at[pl.ds(dest_ref[TOP_K * t + kk], 1)], buf.at[kk, pl.ds(r, 1)], sem).start()
        return carry
    lax.fori_loop(0, bm, issue, 0)

    def drain(r, carry):
        for kk in range(TOP_K):
            _row_copy(ys_hbm.at[pl.ds(0, 1)], buf.at[0, pl.ds(0, 1)], sem).wait()
        return carry
    lax.fori_loop(0, bm, drain, 0)

    r = r_ref[...]
    f = r[:, RT_G1:RT_G1 + 1] * buf[0] + r[:, RT_G2:RT_G2 + 1] * buf[1]
    n = f * lax.rsqrt(jnp.mean(f * f, axis=-1, keepdims=True) + EPS) * g_ref[...]
    o_ref[...] = x_ref[...] + m_ref[0][kg:kg + 1] * n


def _moe_combine(dest_flat, ys, route, x, g, mods, kg, t, bm):
    tt = x.shape[0]
    nlt = t // bm
    xs = pl.BlockSpec((bm, D_MODEL), lambda i, d: (i, 0))
    return pl.pallas_call(
        functools.partial(_moe_combine_kernel, bm=bm, kg=kg),
        grid_spec=pltpu.PrefetchScalarGridSpec(
            num_scalar_prefetch=1, grid=(tt // bm,),
            in_specs=[pl.BlockSpec(memory_space=pl.ANY),
                      pl.BlockSpec((bm, LANE), lambda i, d: (i, 0)), xs,
                      pl.BlockSpec((1, D_MODEL), lambda i, d: (0, 0)),
                      pl.BlockSpec((1, 6, D_MODEL), lambda i, d: (jnp.where(i >= nlt, 1, 0), 0, 0))],
            out_specs=xs,
            scratch_shapes=[pltpu.VMEM((TOP_K, bm, D_MODEL), F32), pltpu.SemaphoreType.DMA(())]),
        out_shape=jax.ShapeDtypeStruct((tt, D_MODEL), F32),
        compiler_params=_cparams(('arbitrary',)), name='moe_combine',
    )(dest_flat, ys, route, x, g.reshape(1, D_MODEL), mods)


def _moe_layer(x, h32, route, w1, w3, w2, lyr, g, mods, kg, t):
    tt = x.shape[0]
    bm = _row_tile(t, tt - t)
    dest, cnt = _moe_rank(route, bm)
    n_tiles = -(-TOP_K * tt // MOE_BM) + N_EXPERTS
    ns = n_tiles * MOE_BM
    cum = jnp.cumsum((cnt[0, 0:N_EXPERTS].astype(jnp.int32) + (MOE_BM - 1)) // MOE_BM)
    n_valid = cum[-1]
    tile = jnp.arange(n_tiles, dtype=jnp.int32)
    tile_expert = jnp.searchsorted(cum, jnp.minimum(tile, n_valid - 1), side='right').astype(jnp.int32)
    dest_flat = dest[:, 0:TOP_K].reshape(-1)
    xs32 = _moe_scatter(dest_flat, h32, ns, bm)
    ys = _moe_grouped_ffn(tile_expert, n_valid.reshape(1), xs32, w1, w3, w2, lyr)
    return _moe_combine(dest_flat, ys, route, x, g, mods, kg, t, bm)


def _rwkv_kernel(cur_ref, prv_ref, nxt_ref, sm_ref, mu_ref, w0_ref, wup_ref, a0_ref, aup_ref,
                 kk_ref, ka_ref, rk_ref, e_ref, tri_ref, y_ref, bonus_ref, s_ref, *, nl, nc):
    d = pl.program_id(0)
    n = pl.program_id(1)
    ch = _chunk_of(d, n, nl, nc)

    @pl.when(n == 0)
    def _():
        s_ref[...] = jnp.zeros_like(s_ref)

    x = cur_ref[...]
    first = jnp.logical_or(ch == 0, ch == nl)
    last = jnp.logical_or(ch == nl - 1, ch == nl + nc - 1)
    p_row = jnp.where(first, 0.0, prv_ref[7:8, :])
    n_row = jnp.where(last, 0.0, nxt_ref[0:1, :])
    row = lax.broadcasted_iota(jnp.int32, (CHUNK, 1), 0)
    prev = jnp.where(row == 0, p_row, pltpu.roll(x, 1, axis=0))
    nxt = jnp.where(row == CHUNK - 1, n_row, pltpu.roll(x, CHUNK - 1, axis=0))
    z = x + mu_ref[0:1, :] * (prev - x) + mu_ref[1:2, :] * (nxt - x)
    r, k, v = z[:, 0:MIX_W], z[:, MIX_W:2 * MIX_W], z[:, 2 * MIX_W:3 * MIX_W]

    sm = sm_ref[...]
    e_bf = e_ref[...]
    wd = jnp.tanh(sm[:, 0:128])
    w_log = -_softplus(-(w0_ref[...] + _mm3(wd, wup_ref[...]))) - 0.5
    logw = -jnp.exp(w_log)
    a = _sigmoid(a0_ref[...] + _mm3(sm[:, 128:256], aup_ref[...]))
    kkr = k * kk_ref[...]
    kk = kkr * lax.rsqrt(jnp.maximum(_mm_xw(kkr * kkr, e_bf), 1e-12))
    k_dir = k * (1.0 + (a - 1.0) * ka_ref[...])
    bonus_ref[...] = _mm_xw(r * k_dir * rk_ref[...], e_bf) * v

    m_strict = tri_ref[0]
    m_incl = tri_ref[1]
    eye = m_incl - m_strict
    b_inc = _mm_wx(m_incl[0:CHUNK, 0:CHUNK].astype(BF16), logw)
    b_exc = b_inc - logw
    b_last = jnp.sum(logw, axis=0, keepdims=True)
    beta = kk * a
    ea = -kk * jnp.exp(b_exc)
    er = r * jnp.exp(b_inc)
    ninv = jnp.exp(-b_inc)
    eb = beta * ninv
    ek = k_dir * ninv
    eend = jnp.exp(b_last - b_inc)
    hb = beta * eend
    hk = k_dir * eend
    gam = jnp.exp(b_last)

    gw = RW_GROUP * RW_HEAD
    lane_head = lax.broadcasted_iota(jnp.int32, (1, gw), 1) >> 6

    def spread(xg):
        return jnp.concatenate([jnp.where(lane_head == h, xg, 0.0) for h in range(RW_GROUP)], axis=0)

    for g in range(RW_HEADS // RW_GROUP):
        sl = slice(g * gw, (g + 1) * gw)
        la, lr = spread(ea[:, sl]), spread(er[:, sl])
        rb, rk = spread(eb[:, sl]), spread(ek[:, sl])
        vb = spread(v[:, sl]).astype(BF16)
        amat = _mm_nt(jnp.concatenate([la, lr], axis=0), jnp.concatenate([rb, rk], axis=0))
        m_ab = amat[0:gw, 0:gw] * m_strict
        m_ak = amat[0:gw, gw:] * m_strict
        n_rb = (amat[gw:, 0:gw] * m_incl).astype(BF16)
        n_rk = (amat[gw:, gw:] * m_incl).astype(BF16)
        mp = m_ab
        tinv = eye + m_ab
        for _ in range(5):
            mp = _mm(mp, mp)
            tinv = tinv + _mm(mp, tinv)
        xs_bf = _mm(tinv, jnp.concatenate([la, _mm(m_ak, vb)], axis=1)).astype(BF16)
        qy = _mm(n_rb, xs_bf)
        q_hat = lr + qy[:, 0:gw]
        y_loc = qy[:, gw:] + _mm(n_rk, vb)
        hb_bd = spread(hb[:, sl]).astype(BF16)
        gbt = _mm_tn(xs_bf, hb_bd)
        g_bot = gbt[gw:] + _mm_tn(vb, spread(hk[:, sl]))
        s0 = s_ref[g]
        y_bd = _mm_nt(q_hat, s0) + y_loc
        y_ref[:, sl] = (y_bd[0:CHUNK] + y_bd[CHUNK:2 * CHUNK]
                        + y_bd[2 * CHUNK:3 * CHUNK] + y_bd[3 * CHUNK:4 * CHUNK])
        s_ref[g] = s0 * gam[:, sl] + _mm(s0, gbt[0:gw]) + g_bot


def _rwkv_order_masks():
    i = jnp.arange(RW_GROUP * RW_HEAD)
    same = (i[:, None] // CHUNK) == (i[None, :] // CHUNK)
    diff = (i[:, None] % CHUNK) - (i[None, :] % CHUNK)
    per_dir = [jnp.stack([same & (sg * diff > 0), same & (sg * diff >= 0)]) for sg in (1, -1)]
    return jnp.stack(per_dir).astype(F32)


def _rwkv_scan(p, t, lc, mu2, w0, wup_pad, a0, aup_pad, k_k, k_a, r_k, e64):
    tt = t + lc
    gw = RW_GROUP * RW_HEAD
    nl, nc = t // CHUNK, lc // CHUNK
    nch = nl + nc
    last8 = tt // 8 - 1
    ch = functools.partial(_chunk_of, nl=nl, nc=nc)
    vec = pl.BlockSpec((1, MIX_W), lambda d, n: (0, 0))
    out_s = pl.BlockSpec((None, CHUNK, MIX_W), lambda d, n: (d, ch(d, n), 0))
    return pl.pallas_call(
        functools.partial(_rwkv_kernel, nl=nl, nc=nc),
        grid=(2, nch),
        in_specs=[
            pl.BlockSpec((CHUNK, 3 * MIX_W), lambda d, n: (ch(d, n), 0)),
            pl.BlockSpec((8, 3 * MIX_W), lambda d, n: (jnp.maximum(ch(d, n) * 8 - 1, 0), 0)),
            pl.BlockSpec((8, 3 * MIX_W), lambda d, n: (jnp.minimum(ch(d, n) * 8 + 8, last8), 0)),
            pl.BlockSpec((CHUNK, MIX_W), lambda d, n: (ch(d, n), CB_SMALL)),
            pl.BlockSpec((2, 3 * MIX_W), lambda d, n: (0, 0)),
            pl.BlockSpec((None, 1, MIX_W), lambda d, n: (d, 0, 0)),
            pl.BlockSpec((None, LANE, MIX_W), lambda d, n: (d, 0, 0)),
            pl.BlockSpec((None, 1, MIX_W), lambda d, n: (d, 0, 0)),
            pl.BlockSpec((None, LANE, MIX_W), lambda d, n: (d, 0, 0)),
            vec, vec, vec,
            pl.BlockSpec((MIX_W, MIX_W), lambda d, n: (0, 0)),
            pl.BlockSpec((None, 2, gw, gw), lambda d, n: (d, 0, 0, 0)),
        ],
        out_specs=[out_s, out_s],
        out_shape=[jax.ShapeDtypeStruct((2, tt, MIX_W), F32)] * 2,
        scratch_shapes=[pltpu.VMEM((RW_HEADS // RW_GROUP, gw, gw), F32)],
        compiler_params=_cparams(('arbitrary', 'arbitrary')), name='rwkv_scan',
    )(p, p, p, p, mu2, w0, wup_pad, a0, aup_pad, k_k, k_a, r_k, e64, _rwkv_order_masks())


def _gla_kernel(qk_ref, v_ref, sm_ref, cos_ref, sin_ref, aup_ref, ab_ref, o_ref, s_ref, *, nl, nc):
    d = pl.program_id(0)
    n = pl.program_id(1)

    @pl.when(n == 0)
    def _():
        s_ref[...] = jnp.zeros_like(s_ref)

    qk = qk_ref[...]
    lane = lax.broadcasted_iota(jnp.int32, (CHUNK, 2 * GLA_HEADS * GLA_DK), 1)
    partner = jnp.where((lane & 1) == 0, pltpu.roll(qk, 2 * GLA_HEADS * GLA_DK - 1, axis=1),
                        pltpu.roll(qk, 1, axis=1))
    qk = qk * cos_ref[...] + partner * sin_ref[...]
    hk = GLA_HEADS * GLA_DK
    q = qk[:, 0:hk] * (GLA_DK ** -0.5)
    k = qk[:, hk:]
    v = v_ref[...]
    g = -_softplus(-(_mm3(sm_ref[:, 384:512], aup_ref[...]) + ab_ref[...])) / GLA_TAU
    _, incl = _order_masks(d)
    b = _mm_wx(jnp.where(incl, 1.0, 0.0).astype(BF16), g)
    b_last = jnp.sum(g, axis=0, keepdims=True)
    q_e = q * jnp.exp(b)
    k_e = k * jnp.exp(-b)
    k_end = k * jnp.exp(b_last - b)
    dec = jnp.exp(b_last)
    for h in range(GLA_HEADS):
        sk = slice(h * GLA_DK, (h + 1) * GLA_DK)
        sv = slice(h * GLA_DV, (h + 1) * GLA_DV)
        att = jnp.where(incl, _mm_nt(q_e[:, sk], k_e[:, sk]), 0.0)
        st = s_ref[h]
        o_ref[:, sv] = _mm(att, v[:, sv]) + _mm_nt(q_e[:, sk], st)
        s_ref[h] = st * dec[:, sk] + _mm_tn(v[:, sv], k_end[:, sk])


def _gla_scan(p, t, lc, cos_t, sin_t, aup_pad, a_b):
    tt = t + lc
    nl, nc = t // CHUNK, lc // CHUNK
    ch = functools.partial(_chunk_of, nl=nl, nc=nc)
    blk = lambda cb: pl.BlockSpec((CHUNK, MIX_W), lambda d, n: (ch(d, n), cb))
    return pl.pallas_call(
        functools.partial(_gla_kernel, nl=nl, nc=nc),
        grid=(2, nl + nc),
        in_specs=[blk(CB_GLA_QK), blk(CB_GLA_V), blk(CB_SMALL), blk(0), blk(0),
                  pl.BlockSpec((None, LANE, 2 * LANE), lambda d, n: (d, 0, 0)),
                  pl.BlockSpec((None, 1, 2 * LANE), lambda d, n: (d, 0, 0))],
        out_specs=pl.BlockSpec((None, CHUNK, MIX_W), lambda d, n: (d, ch(d, n), 0)),
        out_shape=jax.ShapeDtypeStruct((2, tt, MIX_W), F32),
        scratch_shapes=[pltpu.VMEM((GLA_HEADS, GLA_DV, GLA_DK), F32)],
        compiler_params=_cparams(('arbitrary', 'arbitrary')), name='gla_scan',
    )(p, p, p, cos_t, sin_t, aup_pad, a_b)


def _head_norm(y, e_bf, width, eps):
    mu = _mm_xw(y, e_bf) * (1.0 / width)
    dl = y - mu
    var = _mm_xw(dl * dl, e_bf) * (1.0 / width)
    return dl * lax.rsqrt(var + eps)


def _mix_finish_kernel(y0_ref, y1_ref, b0_ref, b1_ref, sm_ref, gup_ref, lng_ref, lnb_ref, e64_ref,
                       o0_ref, o1_ref, gr_ref, gng_ref, e128_ref, a_ref, b_ref):
    yn = _head_norm(y0_ref[...] + y1_ref[...], e64_ref[...], RW_HEAD, RW_GN_EPS)
    yn = yn * lng_ref[...] + lnb_ref[...] + b0_ref[...] + b1_ref[...]
    gate = _mm(_sigmoid(sm_ref[:, 256:384]), gup_ref[...])
    a_ref[...] = (yn * gate).astype(BF16)
    on = _head_norm(o0_ref[...] + o1_ref[...], e128_ref[...], GLA_DV, GN_EPS) * gng_ref[...]
    gr = gr_ref[...]
    b_ref[...] = (on * (gr * _sigmoid(gr))).astype(BF16)


def _mix_finish(p, rw_y, rw_bonus, gla_o, g_up, ln_g, ln_b, gn_g, e64, e128, t):
    tt = p.shape[0]
    bm = _row_tile(t, tt - t)
    dblk = lambda d: pl.BlockSpec((None, bm, MIX_W), lambda i: (d, i, 0))
    vec = pl.BlockSpec((1, MIX_W), lambda i: (0, 0))
    mat = pl.BlockSpec((MIX_W, MIX_W), lambda i: (0, 0))
    out = pl.BlockSpec((bm, MIX_W), lambda i: (i, 0))
    return pl.pallas_call(
        _mix_finish_kernel,
        grid=(tt // bm,),
        in_specs=[dblk(0), dblk(1), dblk(0), dblk(1),
                  pl.BlockSpec((bm, MIX_W), lambda i: (i, CB_SMALL)),
                  pl.BlockSpec((RW_GATE_RANK, MIX_W), lambda i: (0, 0)), vec, vec, mat,
                  dblk(0), dblk(1), pl.BlockSpec((bm, MIX_W), lambda i: (i, CB_GLA_R)), vec, mat],
        out_specs=[out, out],
        out_shape=[jax.ShapeDtypeStruct((tt, MIX_W), BF16)] * 2,
        compiler_params=_cparams(('parallel',)), name='mix_finish',
    )(rw_y, rw_y, rw_bonus, rw_bonus, p, g_up, ln_g, ln_b, e64, gla_o, gla_o, p, gn_g, e128)


def _sgu_kernel(u_ref, v_ref, lng_ref, lnb_ref, ws_ref, bs_ref, o_ref):
    u = _gelu(u_ref[...])
    v = _gelu(v_ref[...])
    mu = jnp.mean(v, axis=-1, keepdims=True)
    dl = v - mu
    var = jnp.mean(dl * dl, axis=-1, keepdims=True)
    vn = (dl * lax.rsqrt(var + GN_EPS) * lng_ref[...] + lnb_ref[...]).astype(BF16)
    lane = lax.broadcasted_iota(jnp.int32, (1, MIX_W), 1)
    s = bs_ref[...]
    for g in range(SGU_GROUPS):
        s = s + jnp.where((lane >> 6) == g, jnp.dot(ws_ref[g].astype(BF16), vn, preferred_element_type=F32), 0.0)
    o_ref[...] = (u * s).astype(BF16)


def _sgu(p, ln_g, ln_b, w_s, b_full):
    tt = p.shape[0]
    vec = pl.BlockSpec((1, MIX_W), lambda i: (0, 0))
    return pl.pallas_call(
        _sgu_kernel,
        grid=(tt // SGU_CHUNK,),
        in_specs=[pl.BlockSpec((SGU_CHUNK, MIX_W), lambda i: (i, CB_SGU_U)),
                  pl.BlockSpec((SGU_CHUNK, MIX_W), lambda i: (i, CB_SGU_V)), vec, vec,
                  pl.BlockSpec((SGU_GROUPS, SGU_CHUNK, SGU_CHUNK), lambda i: (0, 0, 0)),
                  pl.BlockSpec((SGU_CHUNK, MIX_W), lambda i: (0, 0))],
        out_specs=pl.BlockSpec((SGU_CHUNK, MIX_W), lambda i: (i, 0)),
        out_shape=jax.ShapeDtypeStruct((tt, MIX_W), BF16),
        compiler_params=_cparams(('parallel',)), name='sgu',
    )(p, p, ln_g, ln_b, w_s, b_full)


def _na_bias_kernel(rpb_ref, o_ref, *, wh):
    h = pl.program_id(0)
    n_dc = 2 * NA_WIN_W - 1
    n_dr = 2 * NA_WIN_H - 1
    shape = (GRID_W, 2 * GRID_W)
    c = lax.broadcasted_iota(jnp.int32, shape, 0)
    lane = lax.broadcasted_iota(jnp.int32, shape, 1)
    x = lane & (GRID_W - 1)
    dc = jnp.clip(x - c + (NA_WIN_W - 1), 0, 2 * NA_WIN_W - 2)
    key = (lane >> 6) * n_dc + dc
    cs = jnp.clip(c - NA_WIN_W // 2, 0, GRID_W - NA_WIN_W)
    ok = jnp.logical_and(x >= cs, x < cs + NA_WIN_W)
    pairs = []
    for dr in range(n_dr - 1):
        base = h * (n_dr * n_dc) + dr * n_dc
        tile = lax.fori_loop(0, 2 * n_dc, lambda j, acc: jnp.where(key == j, rpb_ref[base + j], acc),
                             jnp.zeros(shape, F32))
        pairs.append(jnp.where(ok, tile, NEG_INF))
    for dr0 in range(n_dr - wh + 1):
        for jj in range(wh // 2):
            o_ref[dr0, 0, :, jj * 2 * GRID_W:(jj + 1) * 2 * GRID_W] = pairs[dr0 + 2 * jj]


def _na_bias(rpb, wh):
    n_dr0 = 2 * NA_WIN_H - wh
    return pl.pallas_call(
        functools.partial(_na_bias_kernel, wh=wh),
        grid=(NA_HEADS,),
        in_specs=[pl.BlockSpec(memory_space=pltpu.SMEM)],
        out_specs=pl.BlockSpec((n_dr0, 1, GRID_W, wh * GRID_W), lambda h: (0, h, 0, 0)),
        out_shape=jax.ShapeDtypeStruct((n_dr0, NA_HEADS, GRID_W, wh * GRID_W), F32),
        compiler_params=_cparams(('parallel',)), name='na_bias',
    )(rpb.reshape(-1))


def _na_kernel(*refs, wh):
    q_ref = refs[0]
    k_refs = refs[1:1 + wh]
    v_refs = refs[1 + wh:1 + 2 * wh]
    kc_ref, vc_ref, bias_ref, o_ref = refs[1 + 2 * wh:]
    q = q_ref[...] * (NA_HEAD ** -0.5)
    kw = jnp.concatenate([r[...].astype(BF16) for r in k_refs], axis=0)
    vw = jnp.concatenate([r[...].astype(BF16) for r in v_refs], axis=0)
    kc = kc_ref[...].astype(BF16)
    vc = vc_ref[...].astype(BF16)
    lane = lax.broadcasted_iota(jnp.int32, (1, LANE), 1)
    outs = []
    for pr in range(NA_HEADS // 2):
        sl = slice(pr * LANE, (pr + 1) * LANE)
        qp, kwp, vwp, kcp, vcp = q[:, sl], kw[:, sl], vw[:, sl], kc[:, sl], vc[:, sl]
        o = jnp.zeros((GRID_W, LANE), F32)
        for s in range(2):
            hm = (lane >> 6) == s
            qh = jnp.where(hm, qp, 0.0).astype(BF16)
            sw = _mm_nt(qh, kwp) + bias_ref[0, 2 * pr + s]
            sc = _mm_nt(qh, kcp)
            m = jnp.maximum(jnp.max(sw, axis=-1, keepdims=True), jnp.max(sc, axis=-1, keepdims=True))
            ew = jnp.exp(sw - m)
            ec = jnp.exp(sc - m)
            den = jnp.sum(ew, axis=-1, keepdims=True) + jnp.sum(ec, axis=-1, keepdims=True)
            oh = (_mm(ew, vwp) + _mm(ec, vcp)) / den
            o = jnp.where(hm, oh, o)
        outs.append(o)
    o_ref[...] = jnp.concatenate(outs, axis=1).astype(BF16)


def _na_ctx_kernel(q_ref, k_ref, v_ref, o_ref):
    q = q_ref[...] * (NA_HEAD ** -0.5)
    kc = k_ref[...].astype(BF16)
    vc = v_ref[...].astype(BF16)
    lane = lax.broadcasted_iota(jnp.int32, (1, MIX_W), 1)
    o = jnp.zeros(q.shape, F32)
    for h in range(NA_HEADS):
        hm = (lane >> 6) == h
        s = _mm_nt(jnp.where(hm, q, 0.0), kc)
        e = jnp.exp(s - jnp.max(s, axis=-1, keepdims=True))
        o = jnp.where(hm, _mm(e, vc) / jnp.sum(e, axis=-1, keepdims=True), o)
    o_ref[...] = o.astype(BF16)


def _na(p, bias, t, lc):
    rows = t // GRID_W
    wh = min(NA_WIN_H, rows)
    rs = lambda r: jnp.clip(r - wh // 2, 0, rows - wh)
    ctx_blk = t // lc
    kv = lambda cb: [pl.BlockSpec((GRID_W, MIX_W), functools.partial(lambda r, w, cb: (rs(r) + w, cb), w=w, cb=cb))
                     for w in range(wh)]
    lat = pl.pallas_call(
        functools.partial(_na_kernel, wh=wh),
        grid=(rows,),
        in_specs=([pl.BlockSpec((GRID_W, MIX_W), lambda r: (r, CB_NA_Q))] + kv(CB_NA_K) + kv(CB_NA_V)
                  + [pl.BlockSpec((lc, MIX_W), lambda r: (ctx_blk, CB_NA_K)),
                     pl.BlockSpec((lc, MIX_W), lambda r: (ctx_blk, CB_NA_V)),
                     pl.BlockSpec((1, NA_HEADS, GRID_W, wh * GRID_W),
                                  lambda r: (rs(r) - r + (NA_WIN_H - 1), 0, 0, 0))]),
        out_specs=pl.BlockSpec((GRID_W, MIX_W), lambda r: (r, 0)),
        out_shape=jax.ShapeDtypeStruct((t, MIX_W), BF16),
        compiler_params=_cparams(('parallel',)), name='na_latent',
    )(*([p] * (1 + 2 * wh + 2)), bias)
    cblk = lambda cb: pl.BlockSpec((lc, MIX_W), lambda i: (ctx_blk, cb))
    ctx = pl.pallas_call(
        _na_ctx_kernel,
        grid=(1,),
        in_specs=[cblk(CB_NA_Q), cblk(CB_NA_K), cblk(CB_NA_V)],
        out_specs=pl.BlockSpec((lc, MIX_W), lambda i: (0, 0)),
        out_shape=jax.ShapeDtypeStruct((lc, MIX_W), BF16),
        compiler_params=_cparams(('arbitrary',)), name='na_ctx',
    )(p, p, p)
    return jnp.concatenate([lat, ctx], axis=0)


def _merge_kernel(a0, a1, a2, a3, g0, g1, g2, g3, w_ref, o_ref, wb_ref):
    @pl.when(pl.program_id(1) == 0)
    def _():
        wb_ref[...] = w_ref[...].astype(BF16)
    acc = None
    for n, (a_ref, g_ref) in enumerate(((a0, g0), (a1, g1), (a2, g2), (a3, g3))):
        zn = jnp.dot(a_ref[...], wb_ref[n], preferred_element_type=F32) * _sigmoid(g_ref[...])
        acc = zn if acc is None else acc + zn
    o_ref[...] = acc.astype(BF16)


def _merge(ys, p, w_br, lyr):
    tt = p.shape[0]
    bn = 512
    bm = _pick(tt, (768, 384, 128))
    a_s = pl.BlockSpec((bm, MIX_W), lambda j, i: (i, 0))
    gs = [pl.BlockSpec((bm, bn), functools.partial(lambda j, i, n: (i, (GATE_OFF + n * D_MODEL) // bn + j), n=n))
          for n in range(N_BRANCH)]
    return pl.pallas_call(
        _merge_kernel,
        grid=(D_MODEL // bn, tt // bm),
        in_specs=[a_s] * 4 + gs + [pl.BlockSpec((None, N_BRANCH, MIX_W, bn), lambda j, i: (lyr, 0, 0, j))],
        out_specs=pl.BlockSpec((bm, bn), lambda j, i: (i, j)),
        out_shape=jax.ShapeDtypeStruct((tt, D_MODEL), BF16),
        scratch_shapes=[pltpu.VMEM((N_BRANCH, MIX_W, bn), BF16)],
        compiler_params=_cparams(('arbitrary', 'arbitrary')), name='merge',
    )(*ys, p, p, p, p, w_br)


def _reorder_w_in(w):
    off, pos = 0, {}
    for name, width in _REF_SPLITS:
        pos[name] = (off, width)
        off += width
    cols = []
    for name, width in _IN_ORDER:
        if name == 'pad':
            cols.append(jnp.zeros(w.shape[:-1] + (width,), w.dtype))
        else:
            o, wd = pos[name]
            assert wd == width
            cols.append(w[..., o:o + wd])
    return jnp.concatenate(cols, axis=-1)


def _block_diag_ones(width):
    i = jnp.arange(MIX_W) // width
    return (i[:, None] == i[None, :]).astype(BF16)


def _rope_tables(t, lc):
    tok = jnp.arange(t)
    pos = jnp.stack([tok // GRID_W, tok % GRID_W], axis=-1).astype(F32)
    nf = GLA_DK // 4
    inv = ROPE_BASE ** (-jnp.arange(nf, dtype=F32) / nf)
    ang = pos[:, :, None] * inv
    cos = jnp.repeat(jnp.cos(ang), 2, axis=-1).reshape(t, GLA_DK)
    sin = jnp.sin(ang)
    sin = jnp.stack([-sin, sin], axis=-1).reshape(t, GLA_DK)
    reps = 2 * GLA_HEADS
    cos = jnp.concatenate([jnp.tile(cos, (1, reps)), jnp.ones((lc, reps * GLA_DK), F32)], axis=0)
    sin = jnp.concatenate([jnp.tile(sin, (1, reps)), jnp.zeros((lc, reps * GLA_DK), F32)], axis=0)
    return cos, sin


def _pad_rank_rows(w_up, rank, rows):
    out = jnp.zeros((2, rows, w_up.shape[-1]), w_up.dtype)
    for d in range(2):
        out = out.at[d, d * rank:(d + 1) * rank].set(w_up[d])
    return out


def kernel(x, c, ctx, c_ctx, ada_w, ada_b, norm_g, w_in, rw_mu, rw_w0, rw_w_up, rw_a0, rw_a_up, rw_g_up, rw_k_k, rw_k_a, rw_r_k, rw_ln_g, rw_ln_b, gla_a_up, gla_a_b, gla_gn_g, sgu_ln_g, sgu_ln_b, sgu_w, sgu_b, na_rpb, w_br, w_o, ffn_w1, ffn_w3, ffn_w2, moe_router, moe_w1, moe_w3, moe_w2):
    assert x.shape[0] == 1 and x.shape[2] == D_MODEL
    t, lc = x.shape[1], ctx.shape[1]
    depth = ada_w.shape[0]
    assert t % max(lc, SGU_CHUNK) == 0 and lc % SGU_CHUNK == 0 and t % GRID_W == 0
    rows = t // GRID_W
    assert rows >= NA_WIN_H
    wh = NA_WIN_H
    xs = jnp.concatenate([x[0], ctx[0]], axis=0)
    cond8 = jnp.zeros((8, D_MODEL), F32).at[0].set(c[0]).at[1].set(c_ctx)
    e64 = _block_diag_ones(RW_HEAD)
    e128 = _block_diag_ones(GLA_DV)
    cos_t, sin_t = _rope_tables(t, lc)
    row1 = lambda v: v.reshape(1, -1)
    w_in_r = _reorder_w_in(w_in)

    for i in range(depth):
        mods = _adaln(cond8, ada_w, ada_b, i)[0:2].reshape(2, 6, D_MODEL)
        h = _norm_mod_call(xs, norm_g[i, 0], mods, 0, t)
        p = _matmul(h, w_in_r, i, 1024, (768, 384, 128), name='in_proj')
        mu2 = jnp.stack([rw_mu[i, :, 0].reshape(-1), rw_mu[i, :, 1].reshape(-1)])
        rw_y, rw_bonus = _rwkv_scan(
            p, t, lc, mu2, rw_w0[i].reshape(2, 1, MIX_W), _pad_rank_rows(rw_w_up[i], RW_DECAY_RANK, LANE),
            rw_a0[i].reshape(2, 1, MIX_W), _pad_rank_rows(rw_a_up[i], RW_ICLR_RANK, LANE),
            row1(rw_k_k[i]), row1(rw_k_a[i]), row1(rw_r_k[i]), e64)
        gla_o = _gla_scan(p, t, lc, cos_t, sin_t, _pad_rank_rows(gla_a_up[i], GLA_GATE_RANK, LANE),
                          gla_a_b[i].reshape(2, 1, 2 * LANE))
        y_a, y_b = _mix_finish(p, rw_y, rw_bonus, gla_o, rw_g_up[i], row1(rw_ln_g[i]), row1(rw_ln_b[i]),
                               row1(gla_gn_g[i]), e64, e128, t)
        y_s = _sgu(p, row1(sgu_ln_g[i]), row1(sgu_ln_b[i]), sgu_w[i], jnp.repeat(sgu_b[i].T, 64, axis=1))
        y_d = _na(p, _na_bias(na_rpb[i], wh), t, lc)
        z = _merge((y_a, y_b, y_s, y_d), p, w_br, i)
        y = _matmul(z, w_o, i, 1024, (768, 384, 128), name='out_proj')
        xs = _resid_call(xs, y, norm_g[i, 1], mods, 2, t)
        j = i // 2
        if i % 2 == 0:
            h = _norm_mod_call(xs, norm_g[i, 2], mods, 3, t)
            f = _ffn_down(_ffn_up(h, ffn_w1, ffn_w3, j), ffn_w2, j)
            xs = _resid_call(xs, f, norm_g[i, 3], mods, 5, t)
        else:
            h32, route = _norm_mod_call(xs, norm_g[i, 2], mods, 3, t, router=moe_router[j])
            xs = _moe_layer(xs, h32, route, moe_w1, moe_w3, moe_w2, j, norm_g[i, 3], mods, 5, t)
    return xs[0:t][None]
```

```python
import functools

import jax
import jax.numpy as jnp
from jax import lax
from jax.experimental import pallas as pl
from jax.experimental.pallas import tpu as pltpu

F32 = jnp.float32
BF16 = jnp.bfloat16

D_MODEL = 2048
GRID_W = 64
N_BRANCH = 4
MIX_W = D_MODEL // 4
RW_HEAD = 64
RW_HEADS = MIX_W // RW_HEAD
RW_GROUP = 4
RW_DECAY_RANK = 64
RW_ICLR_RANK = 64
RW_GATE_RANK = 128
RW_GN_EPS = 64e-5
GLA_HEADS = 4
GLA_DV = MIX_W // GLA_HEADS
GLA_DK = GLA_DV // 2
GLA_GATE_RANK = 16
GLA_TAU = 16.0
GN_EPS = 1e-5
ROPE_BASE = 10000.0
SGU_GROUPS = MIX_W // 64
SGU_CHUNK = 128
NA_HEAD = 64
NA_HEADS = MIX_W // NA_HEAD
NA_WIN_H = 8
NA_WIN_W = 16
FFN_DIM = 7 * D_MODEL // 2
N_EXPERTS = 8
TOP_K = 2
EPS = 1e-6
NEG_INF = -1e30

RT_G1, RT_G2, RT_I1, RT_I2 = 8, 9, 10, 11
MOE_BM = 256
CHUNK = 64
LANE = 128
VMEM_LIMIT = 56 * 2 ** 20

IN_TAIL = 3 * MIX_W + 2 * RW_DECAY_RANK + 2 * RW_ICLR_RANK + RW_GATE_RANK + 2 * GLA_HEADS * GLA_DK \
    + 2 * MIX_W + 2 * GLA_GATE_RANK
IN_PAD = -IN_TAIL % LANE
N_IN = IN_TAIL + 5 * MIX_W + N_BRANCH * D_MODEL
N_IN_P = N_IN + IN_PAD
CB384_RW_SMALL = 4
CB128_GLA_QK, CB128_GLA_V, CB128_GLA_R, CB128_GLA_AD = 15, 19, 23, 27
CB_SGU_U, CB_SGU_V, CB_NA_Q, CB_NA_K, CB_NA_V = 7, 8, 9, 10, 11
GATE_OFF = 6144


def _cparams(sem):
    return pltpu.CompilerParams(dimension_semantics=sem, vmem_limit_bytes=VMEM_LIMIT)


def _pick(m, cands):
    for c in cands:
        if m % c == 0:
            return c
    raise ValueError(f'no tile for {m}')


def _mm(a, b):
    return jnp.dot(a.astype(BF16), b.astype(BF16), preferred_element_type=F32)


def _mm_nt(a, b):
    return lax.dot_general(a.astype(BF16), b.astype(BF16), (((1,), (1,)), ((), ())),
                           preferred_element_type=F32)


def _mm_tn(a, b):
    return lax.dot_general(a.astype(BF16), b.astype(BF16), (((0,), (0,)), ((), ())),
                           preferred_element_type=F32)


def _split2(x):
    hi = x.astype(BF16)
    lo = (x - hi.astype(F32)).astype(BF16)
    return hi, lo


def _mm_xw(x, w_bf):
    hi, lo = _split2(x)
    return (jnp.dot(hi, w_bf, preferred_element_type=F32)
            + jnp.dot(lo, w_bf, preferred_element_type=F32))


def _mm_wx(w_bf, x):
    hi, lo = _split2(x)
    return (jnp.dot(w_bf, hi, preferred_element_type=F32)
            + jnp.dot(w_bf, lo, preferred_element_type=F32))


def _mm3(a, b):
    ah, al = _split2(a)
    bh, bl = _split2(b)
    return (jnp.dot(ah, bh, preferred_element_type=F32)
            + jnp.dot(ah, bl, preferred_element_type=F32)
            + jnp.dot(al, bh, preferred_element_type=F32))


def _sigmoid(x):
    return 1.0 / (1.0 + jnp.exp(-x))


def _softplus(x):
    return jnp.maximum(x, 0.0) + jnp.log1p(jnp.exp(-jnp.abs(x)))


def _gelu(x):
    return 0.5 * x * (1.0 + lax.erf(x * (0.5 ** 0.5)))


def _order_masks(d):
    t = lax.broadcasted_iota(jnp.int32, (CHUNK, CHUNK), 0)
    s = lax.broadcasted_iota(jnp.int32, (CHUNK, CHUNK), 1)
    diff = (t - s) * jnp.where(d == 0, 1, -1)
    return diff > 0, diff >= 0


def _chunk_of(d, n, nl, nc):
    fwd = jnp.where(n < nc, nl + n, n - nc)
    bwd = jnp.where(n < nc, nl + nc - 1 - n, nl - 1 - (n - nc))
    return jnp.where(d == 0, fwd, bwd)


def _ada_kernel(c_ref, w_ref, b_ref, o_ref):
    cnd = c_ref[...]
    a = cnd * _sigmoid(cnd)
    o_ref[...] = _mm(a, w_ref[...]) + b_ref[...]


def _adaln(cond8, ada_w, ada_b, lyr):
    depth, _, n = ada_w.shape
    bn = 1536
    return pl.pallas_call(
        _ada_kernel,
        grid=(n // bn,),
        in_specs=[pl.BlockSpec((8, D_MODEL), lambda j: (0, 0)),
                  pl.BlockSpec((None, D_MODEL, bn), lambda j: (lyr, 0, j)),
                  pl.BlockSpec((None, 1, bn), lambda j: (lyr, 0, j))],
        out_specs=pl.BlockSpec((8, bn), lambda j: (0, j)),
        out_shape=jax.ShapeDtypeStruct((8, n), F32),
        compiler_params=_cparams(('arbitrary',)),
        name='adaln',
    )(cond8, ada_w, ada_b.reshape(depth, 1, n))


def _norm_mod(x, g, m, k0):
    y = x * lax.rsqrt(jnp.mean(x * x, axis=-1, keepdims=True) + EPS) * g
    return y * (1.0 + m[k0 + 1:k0 + 2]) + m[k0:k0 + 1]


def _pack_halves(h):
    n = h.shape[1] // 2
    bits = lax.bitcast_convert_type(h.astype(BF16).astype(F32), jnp.uint32)
    return (bits[:, 0:n] >> 16) | bits[:, n:]


def _unpack_halves(w):
    lo = lax.bitcast_convert_type(w << 16, F32)
    hi = lax.bitcast_convert_type(w & jnp.uint32(0xFFFF0000), F32)
    return jnp.concatenate([lo, hi], axis=1).astype(BF16)


def _norm_mod_kernel(x_ref, g_ref, m_ref, o_ref, *, k0):
    o_ref[...] = _norm_mod(x_ref[...], g_ref[...], m_ref[0], k0).astype(BF16)


def _norm_mod_route_kernel(x_ref, g_ref, m_ref, r_ref, o_ref, gate_ref, *, k0):
    h = _norm_mod(x_ref[...], g_ref[...], m_ref[0], k0)
    o_ref[...] = _pack_halves(h)
    logits = _mm3(h, r_ref[...])
    lane = lax.broadcasted_iota(jnp.int32, logits.shape, 1)
    neg = jnp.float32(-jnp.inf)
    l1 = jnp.where(lane < N_EXPERTS, logits, neg)
    m1 = jnp.max(l1, axis=-1, keepdims=True)
    i1 = jnp.min(jnp.where(l1 == m1, lane, LANE), axis=-1, keepdims=True)
    l2 = jnp.where(lane == i1, neg, l1)
    m2 = jnp.max(l2, axis=-1, keepdims=True)
    i2 = jnp.min(jnp.where(l2 == m2, lane, LANE), axis=-1, keepdims=True)
    e2 = jnp.exp(m2 - m1)
    den = 1.0 + e2
    sel = jnp.where(jnp.logical_or(lane == i1, lane == i2), 1.0, 0.0)
    sel = jnp.where(lane == RT_G1, 1.0 / den, jnp.where(lane == RT_G2, e2 / den, sel))
    gate_ref[...] = jnp.where(lane == RT_I1, i1.astype(F32), jnp.where(lane == RT_I2, i2.astype(F32), sel))


def _row_tile(t, lc):
    return _pick(lc, (256, 128))


def _norm_mod_call(x, g, mods, k0, t, router=None):
    tt = x.shape[0]
    bm = _row_tile(t, tt - t)
    nlt = t // bm
    xs = pl.BlockSpec((bm, D_MODEL), lambda i: (i, 0))
    gs = pl.BlockSpec((1, D_MODEL), lambda i: (0, 0))
    ms = pl.BlockSpec((1, 6, D_MODEL), lambda i: (jnp.where(i >= nlt, 1, 0), 0, 0))
    if router is None:
        return pl.pallas_call(
            functools.partial(_norm_mod_kernel, k0=k0),
            grid=(tt // bm,), in_specs=[xs, gs, ms], out_specs=xs,
            out_shape=jax.ShapeDtypeStruct((tt, D_MODEL), BF16),
            compiler_params=_cparams(('parallel',)), name='norm_mod',
        )(x, g.reshape(1, D_MODEL), mods)
    rpad = jnp.pad(router, ((0, 0), (0, LANE - N_EXPERTS)))
    return pl.pallas_call(
        functools.partial(_norm_mod_route_kernel, k0=k0),
        grid=(tt // bm,),
        in_specs=[xs, gs, ms, pl.BlockSpec((D_MODEL, LANE), lambda i: (0, 0))],
        out_specs=[pl.BlockSpec((bm, D_MODEL // 2), lambda i: (i, 0)), pl.BlockSpec((bm, LANE), lambda i: (i, 0))],
        out_shape=[jax.ShapeDtypeStruct((tt, D_MODEL // 2), jnp.uint32), jax.ShapeDtypeStruct((tt, LANE), F32)],
        compiler_params=_cparams(('parallel',)), name='norm_mod_route',
    )(x, g.reshape(1, D_MODEL), mods, rpad)


def _resid_kernel(x_ref, y_ref, g_ref, m_ref, o_ref, *, kg):
    y = y_ref[...]
    n = y * lax.rsqrt(jnp.mean(y * y, axis=-1, keepdims=True) + EPS) * g_ref[...]
    o_ref[...] = x_ref[...] + m_ref[0][kg:kg + 1] * n


def _resid_call(x, y, g, mods, kg, t):
    tt = x.shape[0]
    bm = _row_tile(t, tt - t)
    nlt = t // bm
    xs = pl.BlockSpec((bm, D_MODEL), lambda i: (i, 0))
    return pl.pallas_call(
        functools.partial(_resid_kernel, kg=kg),
        grid=(tt // bm,),
        in_specs=[xs, xs, pl.BlockSpec((1, D_MODEL), lambda i: (0, 0)),
                  pl.BlockSpec((1, 6, D_MODEL), lambda i: (jnp.where(i >= nlt, 1, 0), 0, 0))],
        out_specs=xs,
        out_shape=jax.ShapeDtypeStruct((tt, D_MODEL), F32),
        compiler_params=_cparams(('parallel',)), name='resid_norm',
    )(x, y, g.reshape(1, D_MODEL), mods)


def _mm_kernel(a_ref, w_ref, o_ref, wb_ref):
    @pl.when(pl.program_id(1) == 0)
    def _():
        wb_ref[...] = w_ref[...].astype(BF16)
    o_ref[...] = jnp.dot(a_ref[...], wb_ref[...], preferred_element_type=F32).astype(o_ref.dtype)


def _matmul(a, w, lyr, bn, bm_cands, out_dtype=F32, name='matmul'):
    m, k = a.shape
    n = w.shape[2]
    bm = _pick(m, bm_cands)
    return pl.pallas_call(
        _mm_kernel,
        grid=(n // bn, m // bm),
        in_specs=[pl.BlockSpec((bm, k), lambda j, i: (i, 0)),
                  pl.BlockSpec((None, k, bn), lambda j, i: (lyr, 0, j))],
        out_specs=pl.BlockSpec((bm, bn), lambda j, i: (i, j)),
        out_shape=jax.ShapeDtypeStruct((m, n), out_dtype),
        scratch_shapes=[pltpu.VMEM((k, bn), BF16)],
        compiler_params=_cparams(('arbitrary', 'arbitrary')), name=name,
    )(a, w)


def _in_proj_kernel(a_ref, wp_ref, w_ref, o_ref, wb_ref, *, bn):
    j = pl.program_id(0)
    straddle = IN_TAIL // bn
    cut = IN_TAIL - straddle * bn
    rows = 256

    @pl.when(pl.program_id(1) == 0)
    def _():
        @pl.when(j < straddle)
        def _():
            wb_ref[...] = w_ref[...].astype(BF16)

        @pl.when(j == straddle)
        def _():
            for r0 in range(0, D_MODEL, rows):
                w = w_ref[r0:r0 + rows, :]
                wb_ref[r0:r0 + rows, :] = jnp.concatenate(
                    [w[:, 0:cut + IN_PAD], w[:, cut:bn - IN_PAD]], axis=1).astype(BF16)

        @pl.when(j > straddle)
        def _():
            for r0 in range(0, D_MODEL, rows):
                wb_ref[r0:r0 + rows, :] = jnp.concatenate(
                    [wp_ref[r0:r0 + rows, LANE - IN_PAD:], w_ref[r0:r0 + rows, 0:bn - IN_PAD]], axis=1).astype(BF16)

    o_ref[...] = jnp.dot(a_ref[...], wb_ref[...], preferred_element_type=F32)


def _in_proj(a, w_in, lyr):
    m = a.shape[0]
    bn = 1024
    bm = _pick(m, (768, 384, 128))
    assert (IN_TAIL + IN_PAD) % LANE == 0 and IN_TAIL // bn == (IN_TAIL + IN_PAD - 1) // bn
    per = bn // LANE
    return pl.pallas_call(
        functools.partial(_in_proj_kernel, bn=bn),
        grid=(N_IN_P // bn, m // bm),
        in_specs=[pl.BlockSpec((bm, D_MODEL), lambda j, i: (i, 0)),
                  pl.BlockSpec((None, D_MODEL, LANE), lambda j, i: (lyr, 0, jnp.maximum(j * per - 1, 0))),
                  pl.BlockSpec((None, D_MODEL, bn), lambda j, i: (lyr, 0, j))],
        out_specs=pl.BlockSpec((bm, bn), lambda j, i: (i, j)),
        out_shape=jax.ShapeDtypeStruct((m, N_IN_P), F32),
        scratch_shapes=[pltpu.VMEM((D_MODEL, bn), BF16)],
        compiler_params=_cparams(('arbitrary', 'arbitrary')), name='in_proj',
    )(a, w_in, w_in)


def _ffn_up_kernel(a_ref, w1_ref, w3_ref, o_ref, w1b, w3b):
    @pl.when(pl.program_id(1) == 0)
    def _():
        w1b[...] = w1_ref[...].astype(BF16)
        w3b[...] = w3_ref[...].astype(BF16)
    a = a_ref[...]
    h1 = jnp.dot(a, w1b[...], preferred_element_type=F32)
    h3 = jnp.dot(a, w3b[...], preferred_element_type=F32)
    o_ref[...] = (h1 * _sigmoid(h1) * h3).astype(BF16)


def _ffn_up(h, w1, w3, e):
    m = h.shape[0]
    bn = 512
    bm = _pick(m, (768, 384, 128))
    ws = pl.BlockSpec((None, D_MODEL, bn), lambda j, i: (e, 0, j))
    return pl.pallas_call(
        _ffn_up_kernel,
        grid=(FFN_DIM // bn, m // bm),
        in_specs=[pl.BlockSpec((bm, D_MODEL), lambda j, i: (i, 0)), ws, ws],
        out_specs=pl.BlockSpec((bm, bn), lambda j, i: (i, j)),
        out_shape=jax.ShapeDtypeStruct((m, FFN_DIM), BF16),
        scratch_shapes=[pltpu.VMEM((D_MODEL, bn), BF16), pltpu.VMEM((D_MODEL, bn), BF16)],
        compiler_params=_cparams(('arbitrary', 'arbitrary')), name='ffn_up',
    )(h, w1, w3)


def _ffn_down_kernel(a_ref, w_ref, o_ref):
    part = jnp.dot(a_ref[...], w_ref[...].astype(BF16), preferred_element_type=F32)

    @pl.when(pl.program_id(1) == 0)
    def _():
        o_ref[...] = part

    @pl.when(pl.program_id(1) > 0)
    def _():
        o_ref[...] += part


def _ffn_down(u, w2, e):
    m = u.shape[0]
    bk = 512
    bm = max(d for d in range(LANE, 1408 + 1, LANE) if m % d == 0)
    return pl.pallas_call(
        _ffn_down_kernel,
        grid=(m // bm, FFN_DIM // bk),
        in_specs=[pl.BlockSpec((bm, bk), lambda i, k: (i, k)),
                  pl.BlockSpec((None, bk, D_MODEL), lambda i, k: (e, k, 0))],
        out_specs=pl.BlockSpec((bm, D_MODEL), lambda i, k: (i, 0)),
        out_shape=jax.ShapeDtypeStruct((m, D_MODEL), F32),
        compiler_params=_cparams(('parallel', 'arbitrary')), name='ffn_down',
    )(u, w2)


def _moe_rank_kernel(r_ref, dest_ref, cnt_ref, carry_ref, tot_ref):
    ph = pl.program_id(0)
    i = pl.program_id(1)
    r = r_ref[...]
    bm = r.shape[0]
    lane = lax.broadcasted_iota(jnp.int32, (1, LANE), 1)
    oh = jnp.where(lane < N_EXPERTS, r, 0.0)
    colsum = jnp.sum(oh, axis=0, keepdims=True)

    @pl.when(jnp.logical_and(ph == 0, i == 0))
    def _():
        carry_ref[...] = jnp.zeros_like(carry_ref)

    @pl.when(jnp.logical_and(ph == 1, i == 0))
    def _():
        tot_ref[...] = carry_ref[...]
        carry_ref[...] = jnp.zeros_like(carry_ref)

    @pl.when(ph == 1)
    def _():
        tot = tot_ref[...]
        padded = jnp.floor((tot + (MOE_BM - 1)) * (1.0 / MOE_BM)) * MOE_BM
        a = lax.broadcasted_iota(jnp.int32, (LANE, LANE), 0)
        b = lax.broadcasted_iota(jnp.int32, (LANE, LANE), 1)
        upper = jnp.where(a < b, 1.0, 0.0).astype(BF16)
        offs = _mm_xw(jnp.broadcast_to(padded, (8, LANE)), upper)[0:1]
        tr = lax.broadcasted_iota(jnp.int32, (bm, bm), 0)
        ts = lax.broadcasted_iota(jnp.int32, (bm, bm), 1)
        before = jnp.dot(jnp.where(ts < tr, 1.0, 0.0).astype(BF16), oh.astype(BF16),
                         preferred_element_type=F32)
        slot = before + carry_ref[...] + offs
        lane_f = lane.astype(F32)
        d1 = jnp.sum(jnp.where(lane_f == r[:, RT_I1:RT_I1 + 1], slot, 0.0), axis=-1, keepdims=True)
        d2 = jnp.sum(jnp.where(lane_f == r[:, RT_I2:RT_I2 + 1], slot, 0.0), axis=-1, keepdims=True)
        dest_ref[...] = jnp.where(lane == 0, d1, jnp.where(lane == 1, d2, 0.0)).astype(jnp.int32)
        cnt_ref[...] = jnp.broadcast_to(tot, (8, LANE))

    carry_ref[...] = carry_ref[...] + colsum


def _moe_rank(route, bm):
    tt = route.shape[0]
    return pl.pallas_call(
        _moe_rank_kernel,
        grid=(2, tt // bm),
        in_specs=[pl.BlockSpec((bm, LANE), lambda ph, i: (i, 0))],
        out_specs=[pl.BlockSpec((bm, LANE), lambda ph, i: (i * ph, 0)),
                   pl.BlockSpec((8, LANE), lambda ph, i: (0, 0))],
        out_shape=[jax.ShapeDtypeStruct((tt, LANE), jnp.int32), jax.ShapeDtypeStruct((8, LANE), F32)],
        scratch_shapes=[pltpu.VMEM((1, LANE), F32), pltpu.VMEM((1, LANE), F32)],
        compiler_params=_cparams(('arbitrary', 'arbitrary')), name='moe_rank',
    )(route)


def _row_copy(src, dst, sem):
    return pltpu.make_async_copy(src, dst, sem)


def _moe_scatter_kernel(dest_ref, h_ref, init_hbm, xs_hbm, sem, *, bm):
    del init_hbm
    base = pl.program_id(0) * bm

    def issue(r, carry):
        t = base + r
        for kk in range(TOP_K):
            _row_copy(h_ref.at[pl.ds(r, 1)], xs_hbm.at[pl.ds(dest_ref[TOP_K * t + kk], 1)], sem).start()
        return carry
    lax.fori_loop(0, bm, issue, 0)

    def drain(r, carry):
        for kk in range(TOP_K):
            _row_copy(h_ref.at[pl.ds(0, 1)], xs_hbm.at[pl.ds(0, 1)], sem).wait()
        return carry
    lax.fori_loop(0, bm, drain, 0)


def _moe_scatter(dest_flat, h32, ns, bm):
    tt, w = h32.shape
    return pl.pallas_call(
        functools.partial(_moe_scatter_kernel, bm=bm),
        grid_spec=pltpu.PrefetchScalarGridSpec(
            num_scalar_prefetch=1, grid=(tt // bm,),
            in_specs=[pl.BlockSpec((bm, w), lambda i, d: (i, 0)), pl.BlockSpec(memory_space=pl.ANY)],
            out_specs=pl.BlockSpec(memory_space=pl.ANY),
            scratch_shapes=[pltpu.SemaphoreType.DMA(())]),
        out_shape=jax.ShapeDtypeStruct((ns, w), h32.dtype),
        input_output_aliases={2: 0},
        compiler_params=_cparams(('arbitrary',)), name='moe_scatter',
    )(dest_flat, h32, jnp.zeros((ns, w), h32.dtype))


def _moe_up_kernel(te_ref, nv_ref, a_ref, w1_ref, w3_ref, o_ref, w1b, w3b):
    m = pl.program_id(1)

    @pl.when(jnp.logical_or(m == 0, te_ref[m] != te_ref[jnp.maximum(m - 1, 0)]))
    def _():
        w1b[...] = w1_ref[...].astype(BF16)
        w3b[...] = w3_ref[...].astype(BF16)

    @pl.when(m < nv_ref[0])
    def _():
        a = _unpack_halves(a_ref[...])
        h1 = jnp.dot(a, w1b[...], preferred_element_type=F32)
        h3 = jnp.dot(a, w3b[...], preferred_element_type=F32)
        o_ref[...] = (h1 * _sigmoid(h1) * h3).astype(BF16)

    @pl.when(m >= nv_ref[0])
    def _():
        o_ref[...] = jnp.zeros_like(o_ref)


def _moe_down_kernel(te_ref, nv_ref, a_ref, w_ref, o_ref, wb_ref):
    m = pl.program_id(1)

    @pl.when(jnp.logical_or(m == 0, te_ref[m] != te_ref[jnp.maximum(m - 1, 0)]))
    def _():
        wb_ref[...] = w_ref[...].astype(BF16)

    @pl.when(m < nv_ref[0])
    def _():
        o_ref[...] = jnp.dot(a_ref[...], wb_ref[...], preferred_element_type=F32)

    @pl.when(m >= nv_ref[0])
    def _():
        o_ref[...] = jnp.zeros_like(o_ref)


def _moe_grouped_ffn(tile_expert, n_valid, xs, w1, w3, w2, lyr):
    ns = xs.shape[0]
    n_tiles = ns // MOE_BM
    bn = 1024
    ws = pl.BlockSpec((None, None, D_MODEL, bn), lambda j, m, te, nv: (lyr, te[m], 0, j))
    u = pl.pallas_call(
        _moe_up_kernel,
        grid_spec=pltpu.PrefetchScalarGridSpec(
            num_scalar_prefetch=2, grid=(FFN_DIM // bn, n_tiles),
            in_specs=[pl.BlockSpec((MOE_BM, D_MODEL // 2), lambda j, m, te, nv: (m, 0)), ws, ws],
            out_specs=pl.BlockSpec((MOE_BM, bn), lambda j, m, te, nv: (m, j)),
            scratch_shapes=[pltpu.VMEM((D_MODEL, bn), BF16), pltpu.VMEM((D_MODEL, bn), BF16)]),
        out_shape=jax.ShapeDtypeStruct((ns, FFN_DIM), BF16),
        compiler_params=_cparams(('arbitrary', 'arbitrary')), name='moe_up',
    )(tile_expert, n_valid, xs, w1, w3)
    bn = 512
    return pl.pallas_call(
        _moe_down_kernel,
        grid_spec=pltpu.PrefetchScalarGridSpec(
            num_scalar_prefetch=2, grid=(D_MODEL // bn, n_tiles),
            in_specs=[pl.BlockSpec((MOE_BM, FFN_DIM), lambda j, m, te, nv: (m, 0)),
                      pl.BlockSpec((None, None, FFN_DIM, bn), lambda j, m, te, nv: (lyr, te[m], 0, j))],
            out_specs=pl.BlockSpec((MOE_BM, bn), lambda j, m, te, nv: (m, j)),
            scratch_shapes=[pltpu.VMEM((FFN_DIM, bn), BF16)]),
        out_shape=jax.ShapeDtypeStruct((ns, D_MODEL), F32),
        compiler_params=_cparams(('arbitrary', 'arbitrary')), name='moe_down',
    )(tile_expert, n_valid, u, w2)


def _moe_combine_kernel(dest_ref, ys_hbm, r_ref, x_ref, g_ref, m_ref, o_ref, buf, sem, *, bm, kg):
    base = pl.program_id(0) * bm

    def issue(r, carry):
        t = base + r
        for kk in range(TOP_K):
            _row_copy(ys_hbm.at[pl.ds(dest_ref[TOP_K * t + kk], 1)], buf.at[kk, pl.ds(r, 1)], sem).start()
        return carry
    lax.fori_loop(0, bm, issue, 0)

    def drain(r, carry):
        for kk in range(TOP_K):
            _row_copy(ys_hbm.at[pl.ds(0, 1)], buf.at[0, pl.ds(0, 1)], sem).wait()
        return carry
    lax.fori_loop(0, bm, drain, 0)

    r = r_ref[...]
    f = r[:, RT_G1:RT_G1 + 1] * buf[0] + r[:, RT_G2:RT_G2 + 1] * buf[1]
    n = f * lax.rsqrt(jnp.mean(f * f, axis=-1, keepdims=True) + EPS) * g_ref[...]
    o_ref[...] = x_ref[...] + m_ref[0][kg:kg + 1] * n


def _moe_combine(dest_flat, ys, route, x, g, mods, kg, t, bm):
    tt = x.shape[0]
    nlt = t // bm
    xs = pl.BlockSpec((bm, D_MODEL), lambda i, d: (i, 0))
    return pl.pallas_call(
        functools.partial(_moe_combine_kernel, bm=bm, kg=kg),
        grid_spec=pltpu.PrefetchScalarGridSpec(
            num_scalar_prefetch=1, grid=(tt // bm,),
            in_specs=[pl.BlockSpec(memory_space=pl.ANY),
                      pl.BlockSpec((bm, LANE), lambda i, d: (i, 0)), xs,
                      pl.BlockSpec((1, D_MODEL), lambda i, d: (0, 0)),
                      pl.BlockSpec((1, 6, D_MODEL), lambda i, d: (jnp.where(i >= nlt, 1, 0), 0, 0))],
            out_specs=xs,
            scratch_shapes=[pltpu.VMEM((TOP_K, bm, D_MODEL), F32), pltpu.SemaphoreType.DMA(())]),
        out_shape=jax.ShapeDtypeStruct((tt, D_MODEL), F32),
        compiler_params=_cparams(('arbitrary',)), name='moe_combine',
    )(dest_flat, ys, route, x, g.reshape(1, D_MODEL), mods)


def _moe_layer(x, h32, route, w1, w3, w2, lyr, g, mods, kg, t):
    tt = x.shape[0]
    bm = _row_tile(t, tt - t)
    dest, cnt = _moe_rank(route, bm)
    n_tiles = -(-TOP_K * tt // MOE_BM) + N_EXPERTS
    ns = n_tiles * MOE_BM
    cum = jnp.cumsum((cnt[0, 0:N_EXPERTS].astype(jnp.int32) + (MOE_BM - 1)) // MOE_BM)
    n_valid = cum[-1]
    tile = jnp.arange(n_tiles, dtype=jnp.int32)
    tile_expert = jnp.searchsorted(cum, jnp.minimum(tile, n_valid - 1), side='right').astype(jnp.int32)
    dest_flat = dest[:, 0:TOP_K].reshape(-1)
    xs32 = _moe_scatter(dest_flat, h32, ns, bm)
    ys = _moe_grouped_ffn(tile_expert, n_valid.reshape(1), xs32, w1, w3, w2, lyr)
    return _moe_combine(dest_flat, ys, route, x, g, mods, kg, t, bm)


def _rwkv_kernel(cur_ref, prv_ref, nxt_ref, sm_ref, mu_ref, w0_ref, wup_ref, a0_ref, aup_ref,
                 kk_ref, ka_ref, rk_ref, e_ref, tri_ref, y_ref, bonus_ref, s_ref, *, nl, nc):
    d = pl.program_id(0)
    n = pl.program_id(1)
    ch = _chunk_of(d, n, nl, nc)

    @pl.when(n == 0)
    def _():
        s_ref[...] = jnp.zeros_like(s_ref)

    x = cur_ref[...]
    first = jnp.logical_or(ch == 0, ch == nl)
    last = jnp.logical_or(ch == nl - 1, ch == nl + nc - 1)
    p_row = jnp.where(first, 0.0, prv_ref[7:8, :])
    n_row = jnp.where(last, 0.0, nxt_ref[0:1, :])
    row = lax.broadcasted_iota(jnp.int32, (CHUNK, 1), 0)
    prev = jnp.where(row == 0, p_row, pltpu.roll(x, 1, axis=0))
    nxt = jnp.where(row == CHUNK - 1, n_row, pltpu.roll(x, CHUNK - 1, axis=0))
    z = x + mu_ref[0:1, :] * (prev - x) + mu_ref[1:2, :] * (nxt - x)
    r, k, v = z[:, 0:MIX_W], z[:, MIX_W:2 * MIX_W], z[:, 2 * MIX_W:3 * MIX_W]

    sm = sm_ref[...]
    e_bf = e_ref[...]
    wd = jnp.tanh(sm[:, 0:128])
    w_log = -_softplus(-(w0_ref[...] + _mm3(wd, wup_ref[...]))) - 0.5
    logw = -jnp.exp(w_log)
    a = _sigmoid(a0_ref[...] + _mm3(sm[:, 128:256], aup_ref[...]))
    kkr = k * kk_ref[...]
    kk = kkr * lax.rsqrt(jnp.maximum(_mm_xw(kkr * kkr, e_bf), 1e-12))
    k_dir = k * (1.0 + (a - 1.0) * ka_ref[...])
    bonus_ref[...] = _mm_xw(r * k_dir * rk_ref[...], e_bf) * v

    m_strict = tri_ref[0]
    m_incl = tri_ref[1]
    eye = m_incl - m_strict
    b_inc = _mm_wx(m_incl[0:CHUNK, 0:CHUNK].astype(BF16), logw)
    b_exc = b_inc - logw
    b_last = jnp.sum(logw, axis=0, keepdims=True)
    beta = kk * a
    ea = -kk * jnp.exp(b_exc)
    er = r * jnp.exp(b_inc)
    ninv = jnp.exp(-b_inc)
    eb = beta * ninv
    ek = k_dir * ninv
    eend = jnp.exp(b_last - b_inc)
    hb = beta * eend
    hk = k_dir * eend
    gam = jnp.exp(b_last)

    gw = RW_GROUP * RW_HEAD
    lane_head = lax.broadcasted_iota(jnp.int32, (1, gw), 1) >> 6

    def spread(xg):
        return jnp.concatenate([jnp.where(lane_head == h, xg, 0.0) for h in range(RW_GROUP)], axis=0)

    for g in range(RW_HEADS // RW_GROUP):
        sl = slice(g * gw, (g + 1) * gw)
        la, lr = spread(ea[:, sl]), spread(er[:, sl])
        rb, rk = spread(eb[:, sl]), spread(ek[:, sl])
        vb = spread(v[:, sl]).astype(BF16)
        amat = _mm_nt(jnp.concatenate([la, lr], axis=0), jnp.concatenate([rb, rk], axis=0))
        m_ab = amat[0:gw, 0:gw] * m_strict
        m_ak = amat[0:gw, gw:] * m_strict
        n_rb = (amat[gw:, 0:gw] * m_incl).astype(BF16)
        n_rk = (amat[gw:, gw:] * m_incl).astype(BF16)
        mp = m_ab
        tinv = eye + m_ab
        for _ in range(5):
            mp = _mm(mp, mp)
            tinv = tinv + _mm(mp, tinv)
        xs_bf = _mm(tinv, jnp.concatenate([la, _mm(m_ak, vb)], axis=1)).astype(BF16)
        qy = _mm(n_rb, xs_bf)
        q_hat = lr + qy[:, 0:gw]
        y_loc = qy[:, gw:] + _mm(n_rk, vb)
        hb_bd = spread(hb[:, sl]).astype(BF16)
        gbt = _mm_tn(xs_bf, hb_bd)
        g_bot = gbt[gw:] + _mm_tn(vb, spread(hk[:, sl]))
        s0 = s_ref[g]
        y_bd = _mm_nt(q_hat, s0) + y_loc
        y_ref[:, sl] = (y_bd[0:CHUNK] + y_bd[CHUNK:2 * CHUNK]
                        + y_bd[2 * CHUNK:3 * CHUNK] + y_bd[3 * CHUNK:4 * CHUNK])
        s_ref[g] = s0 * gam[:, sl] + _mm(s0, gbt[0:gw]) + g_bot


def _rwkv_order_masks():
    i = jnp.arange(RW_GROUP * RW_HEAD)
    same = (i[:, None] // CHUNK) == (i[None, :] // CHUNK)
    diff = (i[:, None] % CHUNK) - (i[None, :] % CHUNK)
    per_dir = [jnp.stack([same & (sg * diff > 0), same & (sg * diff >= 0)]) for sg in (1, -1)]
    return jnp.stack(per_dir).astype(F32)


def _rwkv_scan(p, t, lc, mu2, w0, wup_pad, a0, aup_pad, k_k, k_a, r_k, e64):
    tt = t + lc
    gw = RW_GROUP * RW_HEAD
    nl, nc = t // CHUNK, lc // CHUNK
    nch = nl + nc
    last8 = tt // 8 - 1
    ch = functools.partial(_chunk_of, nl=nl, nc=nc)
    vec = pl.BlockSpec((1, MIX_W), lambda d, n: (0, 0))
    out_s = pl.BlockSpec((None, CHUNK, MIX_W), lambda d, n: (d, ch(d, n), 0))
    return pl.pallas_call(
        functools.partial(_rwkv_kernel, nl=nl, nc=nc),
        grid=(2, nch),
        in_specs=[
            pl.BlockSpec((CHUNK, 3 * MIX_W), lambda d, n: (ch(d, n), 0)),
            pl.BlockSpec((8, 3 * MIX_W), lambda d, n: (jnp.maximum(ch(d, n) * 8 - 1, 0), 0)),
            pl.BlockSpec((8, 3 * MIX_W), lambda d, n: (jnp.minimum(ch(d, n) * 8 + 8, last8), 0)),
            pl.BlockSpec((CHUNK, 3 * LANE), lambda d, n: (ch(d, n), CB384_RW_SMALL)),
            pl.BlockSpec((2, 3 * MIX_W), lambda d, n: (0, 0)),
            pl.BlockSpec((None, 1, MIX_W), lambda d, n: (d, 0, 0)),
            pl.BlockSpec((None, LANE, MIX_W), lambda d, n: (d, 0, 0)),
            pl.BlockSpec((None, 1, MIX_W), lambda d, n: (d, 0, 0)),
            pl.BlockSpec((None, LANE, MIX_W), lambda d, n: (d, 0, 0)),
            vec, vec, vec,
            pl.BlockSpec((MIX_W, MIX_W), lambda d, n: (0, 0)),
            pl.BlockSpec((None, 2, gw, gw), lambda d, n: (d, 0, 0, 0)),
        ],
        out_specs=[out_s, out_s],
        out_shape=[jax.ShapeDtypeStruct((2, tt, MIX_W), F32)] * 2,
        scratch_shapes=[pltpu.VMEM((RW_HEADS // RW_GROUP, gw, gw), F32)],
        compiler_params=_cparams(('arbitrary', 'arbitrary')), name='rwkv_scan',
    )(p, p, p, p, mu2, w0, wup_pad, a0, aup_pad, k_k, k_a, r_k, e64, _rwkv_order_masks())


def _gla_kernel(*refs):
    d = pl.program_id(0)
    n = pl.program_id(1)
    ad_ref, cos_ref, sin_ref, aup_ref, ab_ref, o_ref, s_ref = refs[8:]

    @pl.when(n == 0)
    def _():
        s_ref[...] = jnp.zeros_like(s_ref)

    qk = jnp.concatenate([r[...] for r in refs[0:4]], axis=1)
    v = jnp.concatenate([r[...] for r in refs[4:8]], axis=1)
    lane = lax.broadcasted_iota(jnp.int32, (CHUNK, 2 * GLA_HEADS * GLA_DK), 1)
    partner = jnp.where((lane & 1) == 0, pltpu.roll(qk, 2 * GLA_HEADS * GLA_DK - 1, axis=1),
                        pltpu.roll(qk, 1, axis=1))
    qk = qk * cos_ref[...] + partner * sin_ref[...]
    hk = GLA_HEADS * GLA_DK
    q = qk[:, 0:hk] * (GLA_DK ** -0.5)
    k = qk[:, hk:]
    g = -_softplus(-(_mm3(ad_ref[...], aup_ref[...]) + ab_ref[...])) / GLA_TAU
    _, incl = _order_masks(d)
    b = _mm_wx(jnp.where(incl, 1.0, 0.0).astype(BF16), g)
    b_last = jnp.sum(g, axis=0, keepdims=True)
    q_e = q * jnp.exp(b)
    k_e = k * jnp.exp(-b)
    k_end = k * jnp.exp(b_last - b)
    dec = jnp.exp(b_last)
    for h in range(GLA_HEADS):
        sk = slice(h * GLA_DK, (h + 1) * GLA_DK)
        sv = slice(h * GLA_DV, (h + 1) * GLA_DV)
        att = jnp.where(incl, _mm_nt(q_e[:, sk], k_e[:, sk]), 0.0)
        st = s_ref[h]
        o_ref[:, sv] = _mm(att, v[:, sv]) + _mm_nt(q_e[:, sk], st)
        s_ref[h] = st * dec[:, sk] + _mm_tn(v[:, sv], k_end[:, sk])


def _gla_scan(p, t, lc, cos_t, sin_t, aup_pad, a_b):
    tt = t + lc
    nl, nc = t // CHUNK, lc // CHUNK
    ch = functools.partial(_chunk_of, nl=nl, nc=nc)
    blk = lambda cb: pl.BlockSpec((CHUNK, MIX_W), lambda d, n: (ch(d, n), cb))
    lanes = lambda cb, cnt: [pl.BlockSpec((CHUNK, LANE), functools.partial(lambda d, n, c: (ch(d, n), c), c=cb + q))
                             for q in range(cnt)]
    return pl.pallas_call(
        _gla_kernel,
        grid=(2, nl + nc),
        in_specs=(lanes(CB128_GLA_QK, 4) + lanes(CB128_GLA_V, 4) + lanes(CB128_GLA_AD, 1)
                  + [blk(0), blk(0),
                     pl.BlockSpec((None, LANE, 2 * LANE), lambda d, n: (d, 0, 0)),
                     pl.BlockSpec((None, 1, 2 * LANE), lambda d, n: (d, 0, 0))]),
        out_specs=pl.BlockSpec((None, CHUNK, MIX_W), lambda d, n: (d, ch(d, n), 0)),
        out_shape=jax.ShapeDtypeStruct((2, tt, MIX_W), F32),
        scratch_shapes=[pltpu.VMEM((GLA_HEADS, GLA_DV, GLA_DK), F32)],
        compiler_params=_cparams(('arbitrary', 'arbitrary')), name='gla_scan',
    )(*([p] * 9), cos_t, sin_t, aup_pad, a_b)


def _head_norm(y, e_bf, width, eps):
    mu = _mm_xw(y, e_bf) * (1.0 / width)
    dl = y - mu
    var = _mm_xw(dl * dl, e_bf) * (1.0 / width)
    return dl * lax.rsqrt(var + eps)


def _mix_finish_kernel(y0_ref, y1_ref, b0_ref, b1_ref, sm_ref, gup_ref, lng_ref, lnb_ref, e64_ref,
                       o0_ref, o1_ref, gr0_ref, gr1_ref, gr2_ref, gr3_ref, gng_ref, e128_ref, a_ref, b_ref):
    yn = _head_norm(y0_ref[...] + y1_ref[...], e64_ref[...], RW_HEAD, RW_GN_EPS)
    yn = yn * lng_ref[...] + lnb_ref[...] + b0_ref[...] + b1_ref[...]
    gate = _mm(_sigmoid(sm_ref[:, 256:384]), gup_ref[...])
    a_ref[...] = (yn * gate).astype(BF16)
    on = _head_norm(o0_ref[...] + o1_ref[...], e128_ref[...], GLA_DV, GN_EPS) * gng_ref[...]
    gr = jnp.concatenate([gr0_ref[...], gr1_ref[...], gr2_ref[...], gr3_ref[...]], axis=1)
    b_ref[...] = (on * (gr * _sigmoid(gr))).astype(BF16)


def _mix_finish(p, rw_y, rw_bonus, gla_o, g_up, ln_g, ln_b, gn_g, e64, e128, t):
    tt = p.shape[0]
    bm = _row_tile(t, tt - t)
    dblk = lambda d: pl.BlockSpec((None, bm, MIX_W), lambda i: (d, i, 0))
    vec = pl.BlockSpec((1, MIX_W), lambda i: (0, 0))
    mat = pl.BlockSpec((MIX_W, MIX_W), lambda i: (0, 0))
    out = pl.BlockSpec((bm, MIX_W), lambda i: (i, 0))
    return pl.pallas_call(
        _mix_finish_kernel,
        grid=(tt // bm,),
        in_specs=([dblk(0), dblk(1), dblk(0), dblk(1),
                   pl.BlockSpec((bm, 3 * LANE), lambda i: (i, CB384_RW_SMALL)),
                   pl.BlockSpec((RW_GATE_RANK, MIX_W), lambda i: (0, 0)), vec, vec, mat,
                   dblk(0), dblk(1)]
                  + [pl.BlockSpec((bm, LANE), functools.partial(lambda i, c: (i, c), c=CB128_GLA_R + q))
                     for q in range(4)]
                  + [vec, mat]),
        out_specs=[out, out],
        out_shape=[jax.ShapeDtypeStruct((tt, MIX_W), BF16)] * 2,
        compiler_params=_cparams(('parallel',)), name='mix_finish',
    )(rw_y, rw_y, rw_bonus, rw_bonus, p, g_up, ln_g, ln_b, e64, gla_o, gla_o, p, p, p, p, gn_g, e128)


def _sgu_kernel(u_ref, v_ref, lng_ref, lnb_ref, ws_ref, bs_ref, o_ref):
    u = _gelu(u_ref[...])
    v = _gelu(v_ref[...])
    mu = jnp.mean(v, axis=-1, keepdims=True)
    dl = v - mu
    var = jnp.mean(dl * dl, axis=-1, keepdims=True)
    vn = (dl * lax.rsqrt(var + GN_EPS) * lng_ref[...] + lnb_ref[...]).astype(BF16)
    lane = lax.broadcasted_iota(jnp.int32, (1, MIX_W), 1)
    s = bs_ref[...]
    for g in range(SGU_GROUPS):
        s = s + jnp.where((lane >> 6) == g, jnp.dot(ws_ref[g].astype(BF16), vn, preferred_element_type=F32), 0.0)
    o_ref[...] = (u * s).astype(BF16)


def _sgu(p, ln_g, ln_b, w_s, b_full):
    tt = p.shape[0]
    vec = pl.BlockSpec((1, MIX_W), lambda i: (0, 0))
    return pl.pallas_call(
        _sgu_kernel,
        grid=(tt // SGU_CHUNK,),
        in_specs=[pl.BlockSpec((SGU_CHUNK, MIX_W), lambda i: (i, CB_SGU_U)),
                  pl.BlockSpec((SGU_CHUNK, MIX_W), lambda i: (i, CB_SGU_V)), vec, vec,
                  pl.BlockSpec((SGU_GROUPS, SGU_CHUNK, SGU_CHUNK), lambda i: (0, 0, 0)),
                  pl.BlockSpec((SGU_CHUNK, MIX_W), lambda i: (0, 0))],
        out_specs=pl.BlockSpec((SGU_CHUNK, MIX_W), lambda i: (i, 0)),
        out_shape=jax.ShapeDtypeStruct((tt, MIX_W), BF16),
        compiler_params=_cparams(('parallel',)), name='sgu',
    )(p, p, ln_g, ln_b, w_s, b_full)


def _na_bias_kernel(rpb_ref, o_ref, *, wh):
    h = pl.program_id(0)
    n_dc = 2 * NA_WIN_W - 1
    n_dr = 2 * NA_WIN_H - 1
    shape = (GRID_W, 2 * GRID_W)
    c = lax.broadcasted_iota(jnp.int32, shape, 0)
    lane = lax.broadcasted_iota(jnp.int32, shape, 1)
    x = lane & (GRID_W - 1)
    dc = jnp.clip(x - c + (NA_WIN_W - 1), 0, 2 * NA_WIN_W - 2)
    key = (lane >> 6) * n_dc + dc
    cs = jnp.clip(c - NA_WIN_W // 2, 0, GRID_W - NA_WIN_W)
    ok = jnp.logical_and(x >= cs, x < cs + NA_WIN_W)
    pairs = []
    for dr in range(n_dr - 1):
        base = h * (n_dr * n_dc) + dr * n_dc
        tile = lax.fori_loop(0, 2 * n_dc, lambda j, acc: jnp.where(key == j, rpb_ref[base + j], acc),
                             jnp.zeros(shape, F32))
        pairs.append(jnp.where(ok, tile, NEG_INF))
    for dr0 in range(n_dr - wh + 1):
        for jj in range(wh // 2):
            o_ref[dr0, 0, :, jj * 2 * GRID_W:(jj + 1) * 2 * GRID_W] = pairs[dr0 + 2 * jj]


def _na_bias(rpb, wh):
    n_dr0 = 2 * NA_WIN_H - wh
    return pl.pallas_call(
        functools.partial(_na_bias_kernel, wh=wh),
        grid=(NA_HEADS,),
        in_specs=[pl.BlockSpec(memory_space=pltpu.SMEM)],
        out_specs=pl.BlockSpec((n_dr0, 1, GRID_W, wh * GRID_W), lambda h: (0, h, 0, 0)),
        out_shape=jax.ShapeDtypeStruct((n_dr0, NA_HEADS, GRID_W, wh * GRID_W), F32),
        compiler_params=_cparams(('parallel',)), name='na_bias',
    )(rpb.reshape(-1))


def _na_kernel(*refs, wh):
    q_ref = refs[0]
    k_refs = refs[1:1 + wh]
    v_refs = refs[1 + wh:1 + 2 * wh]
    kc_ref, vc_ref, bias_ref, o_ref = refs[1 + 2 * wh:]
    q = q_ref[...] * (NA_HEAD ** -0.5)
    kw = jnp.concatenate([r[...].astype(BF16) for r in k_refs], axis=0)
    vw = jnp.concatenate([r[...].astype(BF16) for r in v_refs], axis=0)
    kc = kc_ref[...].astype(BF16)
    vc = vc_ref[...].astype(BF16)
    lane = lax.broadcasted_iota(jnp.int32, (1, LANE), 1)
    outs = []
    for pr in range(NA_HEADS // 2):
        sl = slice(pr * LANE, (pr + 1) * LANE)
        qp, kwp, vwp, kcp, vcp = q[:, sl], kw[:, sl], vw[:, sl], kc[:, sl], vc[:, sl]
        o = jnp.zeros((GRID_W, LANE), F32)
        for s in range(2):
            hm = (lane >> 6) == s
            qh = jnp.where(hm, qp, 0.0).astype(BF16)
            sw = _mm_nt(qh, kwp) + bias_ref[0, 2 * pr + s]
            sc = _mm_nt(qh, kcp)
            m = jnp.maximum(jnp.max(sw, axis=-1, keepdims=True), jnp.max(sc, axis=-1, keepdims=True))
            ew = jnp.exp(sw - m)
            ec = jnp.exp(sc - m)
            den = jnp.sum(ew, axis=-1, keepdims=True) + jnp.sum(ec, axis=-1, keepdims=True)
            oh = (_mm(ew, vwp) + _mm(ec, vcp)) / den
            o = jnp.where(hm, oh, o)
        outs.append(o)
    o_ref[...] = jnp.concatenate(outs, axis=1).astype(BF16)


def _na_ctx_kernel(q_ref, k_ref, v_ref, o_ref):
    q = q_ref[...] * (NA_HEAD ** -0.5)
    kc = k_ref[...].astype(BF16)
    vc = v_ref[...].astype(BF16)
    lane = lax.broadcasted_iota(jnp.int32, (1, MIX_W), 1)
    o = jnp.zeros(q.shape, F32)
    for h in range(NA_HEADS):
        hm = (lane >> 6) == h
        s = _mm_nt(jnp.where(hm, q, 0.0), kc)
        e = jnp.exp(s - jnp.max(s, axis=-1, keepdims=True))
        o = jnp.where(hm, _mm(e, vc) / jnp.sum(e, axis=-1, keepdims=True), o)
    o_ref[...] = o.astype(BF16)


def _na(p, bias, t, lc):
    rows = t // GRID_W
    wh = min(NA_WIN_H, rows)
    rs = lambda r: jnp.clip(r - wh // 2, 0, rows - wh)
    ctx_blk = t // lc
    kv = lambda cb: [pl.BlockSpec((GRID_W, MIX_W), functools.partial(lambda r, w, cb: (rs(r) + w, cb), w=w, cb=cb))
                     for w in range(wh)]
    lat = pl.pallas_call(
        functools.partial(_na_kernel, wh=wh),
        grid=(rows,),
        in_specs=([pl.BlockSpec((GRID_W, MIX_W), lambda r: (r, CB_NA_Q))] + kv(CB_NA_K) + kv(CB_NA_V)
                  + [pl.BlockSpec((lc, MIX_W), lambda r: (ctx_blk, CB_NA_K)),
                     pl.BlockSpec((lc, MIX_W), lambda r: (ctx_blk, CB_NA_V)),
                     pl.BlockSpec((1, NA_HEADS, GRID_W, wh * GRID_W),
                                  lambda r: (rs(r) - r + (NA_WIN_H - 1), 0, 0, 0))]),
        out_specs=pl.BlockSpec((GRID_W, MIX_W), lambda r: (r, 0)),
        out_shape=jax.ShapeDtypeStruct((t, MIX_W), BF16),
        compiler_params=_cparams(('parallel',)), name='na_latent',
    )(*([p] * (1 + 2 * wh + 2)), bias)
    cblk = lambda cb: pl.BlockSpec((lc, MIX_W), lambda i: (ctx_blk, cb))
    ctx = pl.pallas_call(
        _na_ctx_kernel,
        grid=(1,),
        in_specs=[cblk(CB_NA_Q), cblk(CB_NA_K), cblk(CB_NA_V)],
        out_specs=pl.BlockSpec((lc, MIX_W), lambda i: (0, 0)),
        out_shape=jax.ShapeDtypeStruct((lc, MIX_W), BF16),
        compiler_params=_cparams(('arbitrary',)), name='na_ctx',
    )(p, p, p)
    return jnp.concatenate([lat, ctx], axis=0)


def _merge_kernel(a0, a1, a2, a3, g0, g1, g2, g3, w_ref, o_ref, wb_ref):
    @pl.when(pl.program_id(1) == 0)
    def _():
        wb_ref[...] = w_ref[...].astype(BF16)
    acc = None
    for n, (a_ref, g_ref) in enumerate(((a0, g0), (a1, g1), (a2, g2), (a3, g3))):
        zn = jnp.dot(a_ref[...], wb_ref[n], preferred_element_type=F32) * _sigmoid(g_ref[...])
        acc = zn if acc is None else acc + zn
    o_ref[...] = acc.astype(BF16)


def _merge(ys, p, w_br, lyr):
    tt = p.shape[0]
    bn = 512
    bm = _pick(tt, (768, 384, 128))
    a_s = pl.BlockSpec((bm, MIX_W), lambda j, i: (i, 0))
    gs = [pl.BlockSpec((bm, bn), functools.partial(lambda j, i, n: (i, (GATE_OFF + n * D_MODEL) // bn + j), n=n))
          for n in range(N_BRANCH)]
    return pl.pallas_call(
        _merge_kernel,
        grid=(D_MODEL // bn, tt // bm),
        in_specs=[a_s] * 4 + gs + [pl.BlockSpec((None, N_BRANCH, MIX_W, bn), lambda j, i: (lyr, 0, 0, j))],
        out_specs=pl.BlockSpec((bm, bn), lambda j, i: (i, j)),
        out_shape=jax.ShapeDtypeStruct((tt, D_MODEL), BF16),
        scratch_shapes=[pltpu.VMEM((N_BRANCH, MIX_W, bn), BF16)],
        compiler_params=_cparams(('arbitrary', 'arbitrary')), name='merge',
    )(*ys, p, p, p, p, w_br)


def _block_diag_ones(width):
    i = jnp.arange(MIX_W) // width
    return (i[:, None] == i[None, :]).astype(BF16)


def _rope_tables(t, lc):
    tok = jnp.arange(t)
    pos = jnp.stack([tok // GRID_W, tok % GRID_W], axis=-1).astype(F32)
    nf = GLA_DK // 4
    inv = ROPE_BASE ** (-jnp.arange(nf, dtype=F32) / nf)
    ang = pos[:, :, None] * inv
    cos = jnp.repeat(jnp.cos(ang), 2, axis=-1).reshape(t, GLA_DK)
    sin = jnp.sin(ang)
    sin = jnp.stack([-sin, sin], axis=-1).reshape(t, GLA_DK)
    reps = 2 * GLA_HEADS
    cos = jnp.concatenate([jnp.tile(cos, (1, reps)), jnp.ones((lc, reps * GLA_DK), F32)], axis=0)
    sin = jnp.concatenate([jnp.tile(sin, (1, reps)), jnp.zeros((lc, reps * GLA_DK), F32)], axis=0)
    return cos, sin


def _pad_rank_rows(w_up, rank, rows):
    out = jnp.zeros((2, rows, w_up.shape[-1]), w_up.dtype)
    for d in range(2):
        out = out.at[d, d * rank:(d + 1) * rank].set(w_up[d])
    return out


def kernel(x, c, ctx, c_ctx, ada_w, ada_b, norm_g, w_in, rw_mu, rw_w0, rw_w_up, rw_a0, rw_a_up, rw_g_up, rw_k_k, rw_k_a, rw_r_k, rw_ln_g, rw_ln_b, gla_a_up, gla_a_b, gla_gn_g, sgu_ln_g, sgu_ln_b, sgu_w, sgu_b, na_rpb, w_br, w_o, ffn_w1, ffn_w3, ffn_w2, moe_router, moe_w1, moe_w3, moe_w2):
    assert x.shape[0] == 1 and x.shape[2] == D_MODEL
    t, lc = x.shape[1], ctx.shape[1]
    depth = ada_w.shape[0]
    assert t % max(lc, SGU_CHUNK) == 0 and lc % SGU_CHUNK == 0 and t % GRID_W == 0
    rows = t // GRID_W
    assert rows >= NA_WIN_H
    wh = NA_WIN_H
    xs = jnp.concatenate([x[0], ctx[0]], axis=0)
    cond8 = jnp.zeros((8, D_MODEL), F32).at[0].set(c[0]).at[1].set(c_ctx)
    e64 = _block_diag_ones(RW_HEAD)
    e128 = _block_diag_ones(GLA_DV)
    cos_t, sin_t = _rope_tables(t, lc)
    row1 = lambda v: v.reshape(1, -1)
    assert w_in.shape[2] == N_IN

    for i in range(depth):
        mods = _adaln(cond8, ada_w, ada_b, i)[0:2].reshape(2, 6, D_MODEL)
        h = _norm_mod_call(xs, norm_g[i, 0], mods, 0, t)
        p = _in_proj(h, w_in, i)
        mu2 = jnp.stack([rw_mu[i, :, 0].reshape(-1), rw_mu[i, :, 1].reshape(-1)])
        rw_y, rw_bonus = _rwkv_scan(
            p, t, lc, mu2, rw_w0[i].reshape(2, 1, MIX_W), _pad_rank_rows(rw_w_up[i], RW_DECAY_RANK, LANE),
            rw_a0[i].reshape(2, 1, MIX_W), _pad_rank_rows(rw_a_up[i], RW_ICLR_RANK, LANE),
            row1(rw_k_k[i]), row1(rw_k_a[i]), row1(rw_r_k[i]), e64)
        gla_o = _gla_scan(p, t, lc, cos_t, sin_t, _pad_rank_rows(gla_a_up[i], GLA_GATE_RANK, LANE),
                          gla_a_b[i].reshape(2, 1, 2 * LANE))
        y_a, y_b = _mix_finish(p, rw_y, rw_bonus, gla_o, rw_g_up[i], row1(rw_ln_g[i]), row1(rw_ln_b[i]),
                               row1(gla_gn_g[i]), e64, e128, t)
        y_s = _sgu(p, row1(sgu_ln_g[i]), row1(sgu_ln_b[i]), sgu_w[i], jnp.repeat(sgu_b[i].T, 64, axis=1))
        y_d = _na(p, _na_bias(na_rpb[i], wh), t, lc)
        z = _merge((y_a, y_b, y_s, y_d), p, w_br, i)
        y = _matmul(z, w_o, i, 1024, (768, 384, 128), name='out_proj')
        xs = _resid_call(xs, y, norm_g[i, 1], mods, 2, t)
        j = i // 2
        if i % 2 == 0:
            h = _norm_mod_call(xs, norm_g[i, 2], mods, 3, t)
            f = _ffn_down(_ffn_up(h, ffn_w1, ffn_w3, j), ffn_w2, j)
            xs = _resid_call(xs, f, norm_g[i, 3], mods, 5, t)
        else:
            h32, route = _norm_mod_call(xs, norm_g[i, 2], mods, 3, t, router=moe_router[j])
            xs = _moe_layer(xs, h32, route, moe_w1, moe_w3, moe_w2, j, norm_g[i, 3], mods, 5, t)
    return xs[0:t][None]
```

```python
import functools

import jax
import jax.numpy as jnp
from jax import lax
from jax.experimental import pallas as pl
from jax.experimental.pallas import tpu as pltpu

F32 = jnp.float32
BF16 = jnp.bfloat16

D_MODEL = 2048
GRID_W = 64
N_BRANCH = 4
MIX_W = D_MODEL // 4
RW_HEAD = 64
RW_HEADS = MIX_W // RW_HEAD
RW_GROUP = 4
RW_DECAY_RANK = 64
RW_ICLR_RANK = 64
RW_GATE_RANK = 128
RW_GN_EPS = 64e-5
GLA_HEADS = 4
GLA_DV = MIX_W // GLA_HEADS
GLA_DK = GLA_DV // 2
GLA_GATE_RANK = 16
GLA_TAU = 16.0
GN_EPS = 1e-5
ROPE_BASE = 10000.0
SGU_GROUPS = MIX_W // 64
SGU_CHUNK = 128
NA_HEAD = 64
NA_HEADS = MIX_W // NA_HEAD
NA_WIN_H = 8
NA_WIN_W = 16
FFN_DIM = 7 * D_MODEL // 2
N_EXPERTS = 8
TOP_K = 2
EPS = 1e-6
NEG_INF = -1e30

RT_G1, RT_G2, RT_I1, RT_I2 = 8, 9, 10, 11
MOE_BM = 256
CHUNK = 64
LANE = 128
VMEM_LIMIT = 56 * 2 ** 20

IN_TAIL = 3 * MIX_W + 2 * RW_DECAY_RANK + 2 * RW_ICLR_RANK + RW_GATE_RANK + 2 * GLA_HEADS * GLA_DK \
    + 2 * MIX_W + 2 * GLA_GATE_RANK
IN_PAD = -IN_TAIL % LANE
N_IN = IN_TAIL + 5 * MIX_W + N_BRANCH * D_MODEL
N_IN_P = N_IN + IN_PAD
CB384_RW_SMALL = 4
CB128_GLA_QK, CB128_GLA_V, CB128_GLA_R, CB128_GLA_AD = 15, 19, 23, 27
CB_SGU_U, CB_SGU_V, CB_NA_Q, CB_NA_K, CB_NA_V = 7, 8, 9, 10, 11
GATE_OFF = 6144


def _cparams(sem):
    return pltpu.CompilerParams(dimension_semantics=sem, vmem_limit_bytes=VMEM_LIMIT)


def _pick(m, cands):
    for c in cands:
        if m % c == 0:
            return c
    raise ValueError(f'no tile for {m}')


def _mm(a, b):
    return jnp.dot(a.astype(BF16), b.astype(BF16), preferred_element_type=F32)


def _mm_nt(a, b):
    return lax.dot_general(a.astype(BF16), b.astype(BF16), (((1,), (1,)), ((), ())),
                           preferred_element_type=F32)


def _mm_tn(a, b):
    return lax.dot_general(a.astype(BF16), b.astype(BF16), (((0,), (0,)), ((), ())),
                           preferred_element_type=F32)


def _split2(x):
    hi = x.astype(BF16)
    lo = (x - hi.astype(F32)).astype(BF16)
    return hi, lo


def _mm_xw(x, w_bf):
    hi, lo = _split2(x)
    return (jnp.dot(hi, w_bf, preferred_element_type=F32)
            + jnp.dot(lo, w_bf, preferred_element_type=F32))


def _mm_wx(w_bf, x):
    hi, lo = _split2(x)
    return (jnp.dot(w_bf, hi, preferred_element_type=F32)
            + jnp.dot(w_bf, lo, preferred_element_type=F32))


def _mm3(a, b):
    ah, al = _split2(a)
    bh, bl = _split2(b)
    return (jnp.dot(ah, bh, preferred_element_type=F32)
            + jnp.dot(ah, bl, preferred_element_type=F32)
            + jnp.dot(al, bh, preferred_element_type=F32))


def _sigmoid(x):
    return 1.0 / (1.0 + jnp.exp(-x))


def _softplus(x):
    return jnp.maximum(x, 0.0) + jnp.log1p(jnp.exp(-jnp.abs(x)))


def _gelu(x):
    return 0.5 * x * (1.0 + lax.erf(x * (0.5 ** 0.5)))


def _order_masks(d):
    t = lax.broadcasted_iota(jnp.int32, (CHUNK, CHUNK), 0)
    s = lax.broadcasted_iota(jnp.int32, (CHUNK, CHUNK), 1)
    diff = (t - s) * jnp.where(d == 0, 1, -1)
    return diff > 0, diff >= 0


def _chunk_of(d, n, nl, nc):
    fwd = jnp.where(n < nc, nl + n, n - nc)
    bwd = jnp.where(n < nc, nl + nc - 1 - n, nl - 1 - (n - nc))
    return jnp.where(d == 0, fwd, bwd)


def _ada_kernel(c_ref, w_ref, b_ref, o_ref):
    cnd = c_ref[...]
    a = cnd * _sigmoid(cnd)
    o_ref[...] = _mm(a, w_ref[...]) + b_ref[...]


def _adaln(cond8, ada_w, ada_b, lyr):
    depth, _, n = ada_w.shape
    bn = 1536
    return pl.pallas_call(
        _ada_kernel,
        grid=(n // bn,),
        in_specs=[pl.BlockSpec((8, D_MODEL), lambda j: (0, 0)),
                  pl.BlockSpec((None, D_MODEL, bn), lambda j: (lyr, 0, j)),
                  pl.BlockSpec((None, 1, bn), lambda j: (lyr, 0, j))],
        out_specs=pl.BlockSpec((8, bn), lambda j: (0, j)),
        out_shape=jax.ShapeDtypeStruct((8, n), F32),
        compiler_params=_cparams(('arbitrary',)),
        name='adaln',
    )(cond8, ada_w, ada_b.reshape(depth, 1, n))


def _norm_mod(x, g, m, k0):
    y = x * lax.rsqrt(jnp.mean(x * x, axis=-1, keepdims=True) + EPS) * g
    return y * (1.0 + m[k0 + 1:k0 + 2]) + m[k0:k0 + 1]


def _pack_halves(h):
    n = h.shape[1] // 2
    bits = lax.bitcast_convert_type(h.astype(BF16).astype(F32), jnp.uint32)
    return (bits[:, 0:n] >> 16) | bits[:, n:]


def _unpack_halves(w):
    lo = lax.bitcast_convert_type(w << 16, F32)
    hi = lax.bitcast_convert_type(w & jnp.uint32(0xFFFF0000), F32)
    return jnp.concatenate([lo, hi], axis=1).astype(BF16)


def _norm_mod_kernel(x_ref, g_ref, m_ref, o_ref, *, k0):
    o_ref[...] = _norm_mod(x_ref[...], g_ref[...], m_ref[0], k0).astype(BF16)


def _route_table(h, router):
    logits = _mm3(h, router)
    lane = lax.broadcasted_iota(jnp.int32, logits.shape, 1)
    neg = jnp.float32(-jnp.inf)
    l1 = jnp.where(lane < N_EXPERTS, logits, neg)
    m1 = jnp.max(l1, axis=-1, keepdims=True)
    i1 = jnp.min(jnp.where(l1 == m1, lane, LANE), axis=-1, keepdims=True)
    l2 = jnp.where(lane == i1, neg, l1)
    m2 = jnp.max(l2, axis=-1, keepdims=True)
    i2 = jnp.min(jnp.where(l2 == m2, lane, LANE), axis=-1, keepdims=True)
    e2 = jnp.exp(m2 - m1)
    den = 1.0 + e2
    sel = jnp.where(jnp.logical_or(lane == i1, lane == i2), 1.0, 0.0)
    sel = jnp.where(lane == RT_G1, 1.0 / den, jnp.where(lane == RT_G2, e2 / den, sel))
    return jnp.where(lane == RT_I1, i1.astype(F32), jnp.where(lane == RT_I2, i2.astype(F32), sel))


def _row_tile(t, lc):
    return _pick(lc, (256, 128))


def _norm_mod_call(x, g, mods, k0, t):
    tt = x.shape[0]
    bm = _row_tile(t, tt - t)
    nlt = t // bm
    xs = pl.BlockSpec((bm, D_MODEL), lambda i: (i, 0))
    gs = pl.BlockSpec((1, D_MODEL), lambda i: (0, 0))
    ms = pl.BlockSpec((1, 6, D_MODEL), lambda i: (jnp.where(i >= nlt, 1, 0), 0, 0))
    return pl.pallas_call(
        functools.partial(_norm_mod_kernel, k0=k0),
        grid=(tt // bm,), in_specs=[xs, gs, ms], out_specs=xs,
        out_shape=jax.ShapeDtypeStruct((tt, D_MODEL), BF16),
        compiler_params=_cparams(('parallel',)), name='norm_mod',
    )(x, g.reshape(1, D_MODEL), mods)


def _resid_kernel(x_ref, y_ref, g_ref, m_ref, o_ref, *, kg):
    y = y_ref[...]
    n = y * lax.rsqrt(jnp.mean(y * y, axis=-1, keepdims=True) + EPS) * g_ref[...]
    o_ref[...] = x_ref[...] + m_ref[0][kg:kg + 1] * n


def _resid_call(x, y, g, mods, kg, t):
    tt = x.shape[0]
    bm = _row_tile(t, tt - t)
    nlt = t // bm
    xs = pl.BlockSpec((bm, D_MODEL), lambda i: (i, 0))
    return pl.pallas_call(
        functools.partial(_resid_kernel, kg=kg),
        grid=(tt // bm,),
        in_specs=[xs, xs, pl.BlockSpec((1, D_MODEL), lambda i: (0, 0)),
                  pl.BlockSpec((1, 6, D_MODEL), lambda i: (jnp.where(i >= nlt, 1, 0), 0, 0))],
        out_specs=xs,
        out_shape=jax.ShapeDtypeStruct((tt, D_MODEL), F32),
        compiler_params=_cparams(('parallel',)), name='resid_norm',
    )(x, y, g.reshape(1, D_MODEL), mods)


def _resid_next_kernel(*refs, kg, k0, route):
    x_ref, y_ref, g_ref, m_ref, g2_ref, m2_ref = refs[0:6]
    y = y_ref[...]
    n = y * lax.rsqrt(jnp.mean(y * y, axis=-1, keepdims=True) + EPS) * g_ref[...]
    xn = x_ref[...] + m_ref[0][kg:kg + 1] * n
    h = _norm_mod(xn, g2_ref[...], m2_ref[0], k0)
    if route:
        r_ref, xo_ref, ho_ref, rt_ref = refs[6:]
        ho_ref[...] = _pack_halves(h)
        rt_ref[...] = _route_table(h, r_ref[...])
    else:
        xo_ref, ho_ref = refs[6:]
        ho_ref[...] = h.astype(BF16)
    xo_ref[...] = xn


def _resid_next_call(x, y, g, mods, kg, t, g2, mods2, k0, router=None):
    tt = x.shape[0]
    bm = _row_tile(t, tt - t)
    nlt = t // bm
    xs = pl.BlockSpec((bm, D_MODEL), lambda i: (i, 0))
    gs = pl.BlockSpec((1, D_MODEL), lambda i: (0, 0))
    ms = pl.BlockSpec((1, 6, D_MODEL), lambda i: (jnp.where(i >= nlt, 1, 0), 0, 0))
    in_specs = [xs, xs, gs, ms, gs, ms]
    args = [x, y, g.reshape(1, D_MODEL), mods, g2.reshape(1, D_MODEL), mods2]
    out_specs = [xs, xs]
    out_shape = [jax.ShapeDtypeStruct((tt, D_MODEL), F32), jax.ShapeDtypeStruct((tt, D_MODEL), BF16)]
    if router is not None:
        in_specs.append(pl.BlockSpec((D_MODEL, LANE), lambda i: (0, 0)))
        args.append(jnp.pad(router, ((0, 0), (0, LANE - N_EXPERTS))))
        out_specs = [xs, pl.BlockSpec((bm, D_MODEL // 2), lambda i: (i, 0)), pl.BlockSpec((bm, LANE), lambda i: (i, 0))]
        out_shape = [out_shape[0], jax.ShapeDtypeStruct((tt, D_MODEL // 2), jnp.uint32),
                     jax.ShapeDtypeStruct((tt, LANE), F32)]
    return pl.pallas_call(
        functools.partial(_resid_next_kernel, kg=kg, k0=k0, route=router is not None),
        grid=(tt // bm,), in_specs=in_specs, out_specs=out_specs, out_shape=out_shape,
        compiler_params=_cparams(('parallel',)), name='resid_next',
    )(*args)


def _mm_kernel(a_ref, w_ref, o_ref, wb_ref):
    @pl.when(pl.program_id(1) == 0)
    def _():
        wb_ref[...] = w_ref[...].astype(BF16)
    o_ref[...] = jnp.dot(a_ref[...], wb_ref[...], preferred_element_type=F32).astype(o_ref.dtype)


def _matmul(a, w, lyr, bn, bm_cands, out_dtype=F32, name='matmul'):
    m, k = a.shape
    n = w.shape[2]
    bm = _pick(m, bm_cands)
    return pl.pallas_call(
        _mm_kernel,
        grid=(n // bn, m // bm),
        in_specs=[pl.BlockSpec((bm, k), lambda j, i: (i, 0)),
                  pl.BlockSpec((None, k, bn), lambda j, i: (lyr, 0, j))],
        out_specs=pl.BlockSpec((bm, bn), lambda j, i: (i, j)),
        out_shape=jax.ShapeDtypeStruct((m, n), out_dtype),
        scratch_shapes=[pltpu.VMEM((k, bn), BF16)],
        compiler_params=_cparams(('arbitrary', 'arbitrary')), name=name,
    )(a, w)


def _in_proj_kernel(a_ref, w_ref, o_ref, wb_ref, tail_ref, *, bn):
    j = pl.program_id(0)
    straddle = IN_TAIL // bn
    cut = IN_TAIL - straddle * bn
    rows = 256

    @pl.when(pl.program_id(1) == 0)
    def _():
        @pl.when(j < straddle)
        def _():
            wb_ref[...] = w_ref[...].astype(BF16)

        @pl.when(j == straddle)
        def _():
            for r0 in range(0, D_MODEL, rows):
                w = w_ref[r0:r0 + rows, :]
                wb_ref[r0:r0 + rows, :] = jnp.concatenate(
                    [w[:, 0:cut + IN_PAD], w[:, cut:bn - IN_PAD]], axis=1).astype(BF16)

        @pl.when(j > straddle)
        def _():
            for r0 in range(0, D_MODEL, rows):
                wb_ref[r0:r0 + rows, :] = jnp.concatenate(
                    [tail_ref[r0:r0 + rows, LANE - IN_PAD:], w_ref[r0:r0 + rows, 0:bn - IN_PAD]], axis=1).astype(BF16)

        @pl.when(j >= straddle)
        def _():
            tail_ref[...] = w_ref[:, bn - LANE:]

    o_ref[...] = jnp.dot(a_ref[...], wb_ref[...], preferred_element_type=F32)


def _in_proj(a, w_in, lyr):
    m = a.shape[0]
    bn = 1024
    bm = _pick(m, (768, 384, 128))
    assert (IN_TAIL + IN_PAD) % LANE == 0 and IN_TAIL // bn == (IN_TAIL + IN_PAD - 1) // bn
    return pl.pallas_call(
        functools.partial(_in_proj_kernel, bn=bn),
        grid=(N_IN_P // bn, m // bm),
        in_specs=[pl.BlockSpec((bm, D_MODEL), lambda j, i: (i, 0)),
                  pl.BlockSpec((None, D_MODEL, bn), lambda j, i: (lyr, 0, j))],
        out_specs=pl.BlockSpec((bm, bn), lambda j, i: (i, j)),
        out_shape=jax.ShapeDtypeStruct((m, N_IN_P), F32),
        scratch_shapes=[pltpu.VMEM((D_MODEL, bn), BF16), pltpu.VMEM((D_MODEL, LANE), F32)],
        compiler_params=_cparams(('arbitrary', 'arbitrary')), name='in_proj',
    )(a, w_in)


def _ffn_up_kernel(a_ref, w1_ref, w3_ref, o_ref, w1b, w3b):
    @pl.when(pl.program_id(1) == 0)
    def _():
        w1b[...] = w1_ref[...].astype(BF16)
        w3b[...] = w3_ref[...].astype(BF16)
    a = a_ref[...]
    h1 = jnp.dot(a, w1b[...], preferred_element_type=F32)
    h3 = jnp.dot(a, w3b[...], preferred_element_type=F32)
    o_ref[...] = (h1 * _sigmoid(h1) * h3).astype(BF16)


def _ffn_up(h, w1, w3, e):
    m = h.shape[0]
    bn = 512
    bm = _pick(m, (768, 384, 128))
    ws = pl.BlockSpec((None, D_MODEL, bn), lambda j, i: (e, 0, j))
    return pl.pallas_call(
        _ffn_up_kernel,
        grid=(FFN_DIM // bn, m // bm),
        in_specs=[pl.BlockSpec((bm, D_MODEL), lambda j, i: (i, 0)), ws, ws],
        out_specs=pl.BlockSpec((bm, bn), lambda j, i: (i, j)),
        out_shape=jax.ShapeDtypeStruct((m, FFN_DIM), BF16),
        scratch_shapes=[pltpu.VMEM((D_MODEL, bn), BF16), pltpu.VMEM((D_MODEL, bn), BF16)],
        compiler_params=_cparams(('arbitrary', 'arbitrary')), name='ffn_up',
    )(h, w1, w3)


def _ffn_down_kernel(a_ref, w_ref, o_ref):
    part = jnp.dot(a_ref[...], w_ref[...].astype(BF16), preferred_element_type=F32)

    @pl.when(pl.program_id(1) == 0)
    def _():
        o_ref[...] = part

    @pl.when(pl.program_id(1) > 0)
    def _():
        o_ref[...] += part


def _ffn_down(u, w2, e):
    m = u.shape[0]
    bk = 512
    bm = max(d for d in range(LANE, 1408 + 1, LANE) if m % d == 0)
    return pl.pallas_call(
        _ffn_down_kernel,
        grid=(m // bm, FFN_DIM // bk),
        in_specs=[pl.BlockSpec((bm, bk), lambda i, k: (i, k)),
                  pl.BlockSpec((None, bk, D_MODEL), lambda i, k: (e, k, 0))],
        out_specs=pl.BlockSpec((bm, D_MODEL), lambda i, k: (i, 0)),
        out_shape=jax.ShapeDtypeStruct((m, D_MODEL), F32),
        compiler_params=_cparams(('parallel', 'arbitrary')), name='ffn_down',
    )(u, w2)


def _moe_rank_kernel(r_ref, dest_ref, cnt_ref, carry_ref, tot_ref):
    ph = pl.program_id(0)
    i = pl.program_id(1)
    r = r_ref[...]
    bm = r.shape[0]
    lane = lax.broadcasted_iota(jnp.int32, (1, LANE), 1)
    oh = jnp.where(lane < N_EXPERTS, r, 0.0)
    colsum = jnp.sum(oh, axis=0, keepdims=True)

    @pl.when(jnp.logical_and(ph == 0, i == 0))
    def _():
        carry_ref[...] = jnp.zeros_like(carry_ref)

    @pl.when(jnp.logical_and(ph == 1, i == 0))
    def _():
        tot_ref[...] = carry_ref[...]
        carry_ref[...] = jnp.zeros_like(carry_ref)

    @pl.when(ph == 1)
    def _():
        tot = tot_ref[...]
        padded = jnp.floor((tot + (MOE_BM - 1)) * (1.0 / MOE_BM)) * MOE_BM
        a = lax.broadcasted_iota(jnp.int32, (LANE, LANE), 0)
        b = lax.broadcasted_iota(jnp.int32, (LANE, LANE), 1)
        upper = jnp.where(a < b, 1.0, 0.0).astype(BF16)
        offs = _mm_xw(jnp.broadcast_to(padded, (8, LANE)), upper)[0:1]
        tr = lax.broadcasted_iota(jnp.int32, (bm, bm), 0)
        ts = lax.broadcasted_iota(jnp.int32, (bm, bm), 1)
        before = jnp.dot(jnp.where(ts < tr, 1.0, 0.0).astype(BF16), oh.astype(BF16),
                         preferred_element_type=F32)
        slot = before + carry_ref[...] + offs
        lane_f = lane.astype(F32)
        d1 = jnp.sum(jnp.where(lane_f == r[:, RT_I1:RT_I1 + 1], slot, 0.0), axis=-1, keepdims=True)
        d2 = jnp.sum(jnp.where(lane_f == r[:, RT_I2:RT_I2 + 1], slot, 0.0), axis=-1, keepdims=True)
        dest_ref[...] = jnp.where(lane == 0, d1, jnp.where(lane == 1, d2, 0.0)).astype(jnp.int32)
        cnt_ref[...] = jnp.broadcast_to(tot, (8, LANE))

    carry_ref[...] = carry_ref[...] + colsum


def _moe_rank(route, bm):
    tt = route.shape[0]
    return pl.pallas_call(
        _moe_rank_kernel,
        grid=(2, tt // bm),
        in_specs=[pl.BlockSpec((bm, LANE), lambda ph, i: (i, 0))],
        out_specs=[pl.BlockSpec((bm, LANE), lambda ph, i: (i * ph, 0)),
                   pl.BlockSpec((8, LANE), lambda ph, i: (0, 0))],
        out_shape=[jax.ShapeDtypeStruct((tt, LANE), jnp.int32), jax.ShapeDtypeStruct((8, LANE), F32)],
        scratch_shapes=[pltpu.VMEM((1, LANE), F32), pltpu.VMEM((1, LANE), F32)],
        compiler_params=_cparams(('arbitrary', 'arbitrary')), name='moe_rank',
    )(route)


def _row_copy(src, dst, sem):
    return pltpu.make_async_copy(src, dst, sem)


def _moe_scatter_kernel(dest_ref, h_ref, init_hbm, xs_hbm, sem, *, bm):
    del init_hbm
    base = pl.program_id(0) * bm

    def issue(r, carry):
        t = base + r
        for kk in range(TOP_K):
            _row_copy(h_ref.at[pl.ds(r, 1)], xs_hbm.at[pl.ds(dest_ref[TOP_K * t + kk], 1)], sem).start()
        return carry
    lax.fori_loop(0, bm, issue, 0)

    def drain(r, carry):
        for kk in range(TOP_K):
            _row_copy(h_ref.at[pl.ds(0, 1)], xs_hbm.at[pl.ds(0, 1)], sem).wait()
        return carry
    lax.fori_loop(0, bm, drain, 0)


def _moe_scatter(dest_flat, h32, ns, bm):
    tt, w = h32.shape
    return pl.pallas_call(
        functools.partial(_moe_scatter_kernel, bm=bm),
        grid_spec=pltpu.PrefetchScalarGridSpec(
            num_scalar_prefetch=1, grid=(tt // bm,),
            in_specs=[pl.BlockSpec((bm, w), lambda i, d: (i, 0)), pl.BlockSpec(memory_space=pl.ANY)],
            out_specs=pl.BlockSpec(memory_space=pl.ANY),
            scratch_shapes=[pltpu.SemaphoreType.DMA(())]),
        out_shape=jax.ShapeDtypeStruct((ns, w), h32.dtype),
        input_output_aliases={2: 0},
        compiler_params=_cparams(('arbitrary',)), name='moe_scatter',
    )(dest_flat, h32, jnp.zeros((ns, w), h32.dtype))


def _moe_up_kernel(te_ref, nv_ref, a_ref, w1_ref, w3_ref, o_ref, w1b, w3b):
    m = pl.program_id(1)

    @pl.when(jnp.logical_or(m == 0, te_ref[m] != te_ref[jnp.maximum(m - 1, 0)]))
    def _():
        w1b[...] = w1_ref[...].astype(BF16)
        w3b[...] = w3_ref[...].astype(BF16)

    @pl.when(m < nv_ref[0])
    def _():
        a = _unpack_halves(a_ref[...])
        h1 = jnp.dot(a, w1b[...], preferred_element_type=F32)
        h3 = jnp.dot(a, w3b[...], preferred_element_type=F32)
        o_ref[...] = (h1 * _sigmoid(h1) * h3).astype(BF16)

    @pl.when(m >= nv_ref[0])
    def _():
        o_ref[...] = jnp.zeros_like(o_ref)


def _moe_down_kernel(te_ref, nv_ref, a_ref, w_ref, o_ref, wb_ref):
    m = pl.program_id(1)

    @pl.when(jnp.logical_or(m == 0, te_ref[m] != te_ref[jnp.maximum(m - 1, 0)]))
    def _():
        wb_ref[...] = w_ref[...].astype(BF16)

    @pl.when(m < nv_ref[0])
    def _():
        o_ref[...] = jnp.dot(a_ref[...], wb_ref[...], preferred_element_type=F32)

    @pl.when(m >= nv_ref[0])
    def _():
        o_ref[...] = jnp.zeros_like(o_ref)


def _moe_grouped_ffn(tile_expert, n_valid, xs, w1, w3, w2, lyr):
    ns = xs.shape[0]
    n_tiles = ns // MOE_BM
    bn = 1024
    ws = pl.BlockSpec((None, None, D_MODEL, bn), lambda j, m, te, nv: (lyr, te[m], 0, j))
    u = pl.pallas_call(
        _moe_up_kernel,
        grid_spec=pltpu.PrefetchScalarGridSpec(
            num_scalar_prefetch=2, grid=(FFN_DIM // bn, n_tiles),
            in_specs=[pl.BlockSpec((MOE_BM, D_MODEL // 2), lambda j, m, te, nv: (m, 0)), ws, ws],
            out_specs=pl.BlockSpec((MOE_BM, bn), lambda j, m, te, nv: (m, j)),
            scratch_shapes=[pltpu.VMEM((D_MODEL, bn), BF16), pltpu.VMEM((D_MODEL, bn), BF16)]),
        out_shape=jax.ShapeDtypeStruct((ns, FFN_DIM), BF16),
        compiler_params=_cparams(('arbitrary', 'arbitrary')), name='moe_up',
    )(tile_expert, n_valid, xs, w1, w3)
    bn = 512
    return pl.pallas_call(
        _moe_down_kernel,
        grid_spec=pltpu.PrefetchScalarGridSpec(
            num_scalar_prefetch=2, grid=(D_MODEL // bn, n_tiles),
            in_specs=[pl.BlockSpec((MOE_BM, FFN_DIM), lambda j, m, te, nv: (m, 0)),
                      pl.BlockSpec((None, None, FFN_DIM, bn), lambda j, m, te, nv: (lyr, te[m], 0, j))],
            out_specs=pl.BlockSpec((MOE_BM, bn), lambda j, m, te, nv: (m, j)),
            scratch_shapes=[pltpu.VMEM((FFN_DIM, bn), BF16)]),
        out_shape=jax.ShapeDtypeStruct((ns, D_MODEL), F32),
        compiler_params=_cparams(('arbitrary', 'arbitrary')), name='moe_down',
    )(tile_expert, n_valid, u, w2)


def _moe_combine_kernel(dest_ref, ys_hbm, r_ref, x_ref, g_ref, m_ref, o_ref, buf, sem, *, bm, kg):
    base = pl.program_id(0) * bm

    def issue(r, carry):
        t = base + r
        for kk in range(TOP_K):
            _row_copy(ys_hbm.at[pl.ds(dest_ref[TOP_K * t + kk], 1)], buf.at[kk, pl.ds(r, 1)], sem).start()
        return carry
    lax.fori_loop(0, bm, issue, 0)

    def drain(r, carry):
        for kk in range(TOP_K):
            _row_copy(ys_hbm.at[pl.ds(0, 1)], buf.at[0, pl.ds(0, 1)], sem).wait()
        return carry
    lax.fori_loop(0, bm, drain, 0)

    r = r_ref[...]
    f = r[:, RT_G1:RT_G1 + 1] * buf[0] + r[:, RT_G2:RT_G2 + 1] * buf[1]
    n = f * lax.rsqrt(jnp.mean(f * f, axis=-1, keepdims=True) + EPS) * g_ref[...]
    o_ref[...] = x_ref[...] + m_ref[0][kg:kg + 1] * n


def _moe_combine(dest_flat, ys, route, x, g, mods, kg, t, bm):
    tt = x.shape[0]
    nlt = t // bm
    xs = pl.BlockSpec((bm, D_MODEL), lambda i, d: (i, 0))
    return pl.pallas_call(
        functools.partial(_moe_combine_kernel, bm=bm, kg=kg),
        grid_spec=pltpu.PrefetchScalarGridSpec(
            num_scalar_prefetch=1, grid=(tt // bm,),
            in_specs=[pl.BlockSpec(memory_space=pl.ANY),
                      pl.BlockSpec((bm, LANE), lambda i, d: (i, 0)), xs,
                      pl.BlockSpec((1, D_MODEL), lambda i, d: (0, 0)),
                      pl.BlockSpec((1, 6, D_MODEL), lambda i, d: (jnp.where(i >= nlt, 1, 0), 0, 0))],
            out_specs=xs,
            scratch_shapes=[pltpu.VMEM((TOP_K, bm, D_MODEL), F32), pltpu.SemaphoreType.DMA(())]),
        out_shape=jax.ShapeDtypeStruct((tt, D_MODEL), F32),
        compiler_params=_cparams(('arbitrary',)), name='moe_combine',
    )(dest_flat, ys, route, x, g.reshape(1, D_MODEL), mods)


def _moe_layer(x, h32, route, w1, w3, w2, lyr, g, mods, kg, t):
    tt = x.shape[0]
    bm = _row_tile(t, tt - t)
    dest, cnt = _moe_rank(route, bm)
    n_tiles = -(-TOP_K * tt // MOE_BM) + N_EXPERTS
    ns = n_tiles * MOE_BM
    cum = jnp.cumsum((cnt[0, 0:N_EXPERTS].astype(jnp.int32) + (MOE_BM - 1)) // MOE_BM)
    n_valid = cum[-1]
    tile = jnp.arange(n_tiles, dtype=jnp.int32)
    tile_expert = jnp.searchsorted(cum, jnp.minimum(tile, n_valid - 1), side='right').astype(jnp.int32)
    dest_flat = dest[:, 0:TOP_K].reshape(-1)
    xs32 = _moe_scatter(dest_flat, h32, ns, bm)
    ys = _moe_grouped_ffn(tile_expert, n_valid.reshape(1), xs32, w1, w3, w2, lyr)
    return _moe_combine(dest_flat, ys, route, x, g, mods, kg, t, bm)


def _rwkv_kernel(cur_ref, prv_ref, nxt_ref, sm_ref, mu_ref, w0_ref, wup_ref, a0_ref, aup_ref,
                 kk_ref, ka_ref, rk_ref, e_ref, tri_ref, y_ref, bonus_ref, s_ref, *, nl, nc):
    d = pl.program_id(0)
    n = pl.program_id(1)
    ch = _chunk_of(d, n, nl, nc)

    @pl.when(n == 0)
    def _():
        s_ref[...] = jnp.zeros_like(s_ref)

    x = cur_ref[...]
    first = jnp.logical_or(ch == 0, ch == nl)
    last = jnp.logical_or(ch == nl - 1, ch == nl + nc - 1)
    p_row = jnp.where(first, 0.0, prv_ref[7:8, :])
    n_row = jnp.where(last, 0.0, nxt_ref[0:1, :])
    row = lax.broadcasted_iota(jnp.int32, (CHUNK, 1), 0)
    prev = jnp.where(row == 0, p_row, pltpu.roll(x, 1, axis=0))
    nxt = jnp.where(row == CHUNK - 1, n_row, pltpu.roll(x, CHUNK - 1, axis=0))
    z = x + mu_ref[0:1, :] * (prev - x) + mu_ref[1:2, :] * (nxt - x)
    r, k, v = z[:, 0:MIX_W], z[:, MIX_W:2 * MIX_W], z[:, 2 * MIX_W:3 * MIX_W]

    sm = sm_ref[...]
    e_bf = e_ref[...]
    wd = jnp.tanh(sm[:, 0:128])
    w_log = -_softplus(-(w0_ref[...] + _mm3(wd, wup_ref[...]))) - 0.5
    logw = -jnp.exp(w_log)
    a = _sigmoid(a0_ref[...] + _mm3(sm[:, 128:256], aup_ref[...]))
    kkr = k * kk_ref[...]
    kk = kkr * lax.rsqrt(jnp.maximum(_mm_xw(kkr * kkr, e_bf), 1e-12))
    k_dir = k * (1.0 + (a - 1.0) * ka_ref[...])
    bonus_ref[...] = _mm_xw(r * k_dir * rk_ref[...], e_bf) * v

    m_strict = tri_ref[0]
    m_incl = tri_ref[1]
    eye = m_incl - m_strict
    b_inc = _mm_wx(m_incl[0:CHUNK, 0:CHUNK].astype(BF16), logw)
    b_exc = b_inc - logw
    b_last = jnp.sum(logw, axis=0, keepdims=True)
    beta = kk * a
    ea = -kk * jnp.exp(b_exc)
    er = r * jnp.exp(b_inc)
    ninv = jnp.exp(-b_inc)
    eb = beta * ninv
    ek = k_dir * ninv
    eend = jnp.exp(b_last - b_inc)
    hb = beta * eend
    hk = k_dir * eend
    gam = jnp.exp(b_last)

    gw = RW_GROUP * RW_HEAD
    lane_head = lax.broadcasted_iota(jnp.int32, (1, gw), 1) >> 6

    def spread(xg):
        return jnp.concatenate([jnp.where(lane_head == h, xg, 0.0) for h in range(RW_GROUP)], axis=0)

    for g in range(RW_HEADS // RW_GROUP):
        sl = slice(g * gw, (g + 1) * gw)
        la, lr = spread(ea[:, sl]), spread(er[:, sl])
        rb, rk = spread(eb[:, sl]), spread(ek[:, sl])
        vb = spread(v[:, sl]).astype(BF16)
        amat = _mm_nt(jnp.concatenate([la, lr], axis=0), jnp.concatenate([rb, rk], axis=0))
        m_ab = amat[0:gw, 0:gw] * m_strict
        m_ak = amat[0:gw, gw:] * m_strict
        n_rb = (amat[gw:, 0:gw] * m_incl).astype(BF16)
        n_rk = (amat[gw:, gw:] * m_incl).astype(BF16)
        mp = m_ab
        tinv = eye + m_ab
        for _ in range(5):
            mp = _mm(mp, mp)
            tinv = tinv + _mm(mp, tinv)
        xs_bf = _mm(tinv, jnp.concatenate([la, _mm(m_ak, vb)], axis=1)).astype(BF16)
        qy = _mm(n_rb, xs_bf)
        q_hat = lr + qy[:, 0:gw]
        y_loc = qy[:, gw:] + _mm(n_rk, vb)
        hb_bd = spread(hb[:, sl]).astype(BF16)
        gbt = _mm_tn(xs_bf, hb_bd)
        g_bot = gbt[gw:] + _mm_tn(vb, spread(hk[:, sl]))
        s0 = s_ref[g]
        y_bd = _mm_nt(q_hat, s0) + y_loc
        y_ref[:, sl] = (y_bd[0:CHUNK] + y_bd[CHUNK:2 * CHUNK]
                        + y_bd[2 * CHUNK:3 * CHUNK] + y_bd[3 * CHUNK:4 * CHUNK])
        s_ref[g] = s0 * gam[:, sl] + _mm(s0, gbt[0:gw]) + g_bot


def _rwkv_order_masks():
    i = jnp.arange(RW_GROUP * RW_HEAD)
    same = (i[:, None] // CHUNK) == (i[None, :] // CHUNK)
    diff = (i[:, None] % CHUNK) - (i[None, :] % CHUNK)
    per_dir = [jnp.stack([same & (sg * diff > 0), same & (sg * diff >= 0)]) for sg in (1, -1)]
    return jnp.stack(per_dir).astype(F32)


def _rwkv_scan(p, t, lc, mu2, w0, wup_pad, a0, aup_pad, k_k, k_a, r_k, e64):
    tt = t + lc
    gw = RW_GROUP * RW_HEAD
    nl, nc = t // CHUNK, lc // CHUNK
    nch = nl + nc
    last8 = tt // 8 - 1
    ch = functools.partial(_chunk_of, nl=nl, nc=nc)
    vec = pl.BlockSpec((1, MIX_W), lambda d, n: (0, 0))
    out_s = pl.BlockSpec((None, CHUNK, MIX_W), lambda d, n: (d, ch(d, n), 0))
    return dict(
        in_specs=[
            pl.BlockSpec((CHUNK, 3 * MIX_W), lambda d, n: (ch(d, n), 0)),
            pl.BlockSpec((8, 3 * MIX_W), lambda d, n: (jnp.maximum(ch(d, n) * 8 - 1, 0), 0)),
            pl.BlockSpec((8, 3 * MIX_W), lambda d, n: (jnp.minimum(ch(d, n) * 8 + 8, last8), 0)),
            pl.BlockSpec((CHUNK, 3 * LANE), lambda d, n: (ch(d, n), CB384_RW_SMALL)),
            pl.BlockSpec((2, 3 * MIX_W), lambda d, n: (0, 0)),
            pl.BlockSpec((None, 1, MIX_W), lambda d, n: (d, 0, 0)),
            pl.BlockSpec((None, LANE, MIX_W), lambda d, n: (d, 0, 0)),
            pl.BlockSpec((None, 1, MIX_W), lambda d, n: (d, 0, 0)),
            pl.BlockSpec((None, LANE, MIX_W), lambda d, n: (d, 0, 0)),
            vec, vec, vec,
            pl.BlockSpec((MIX_W, MIX_W), lambda d, n: (0, 0)),
            pl.BlockSpec((None, 2, gw, gw), lambda d, n: (d, 0, 0, 0)),
        ],
        out_specs=[out_s, out_s],
        out_shape=[jax.ShapeDtypeStruct((2, tt, MIX_W), F32)] * 2,
        scratch_shapes=[pltpu.VMEM((RW_HEADS // RW_GROUP, gw, gw), F32)],
        args=[p, p, p, p, mu2, w0, wup_pad, a0, aup_pad, k_k, k_a, r_k, e64, _rwkv_order_masks()])


def _gla_kernel(*refs):
    d = pl.program_id(0)
    n = pl.program_id(1)
    ad_ref, cos_ref, sin_ref, aup_ref, ab_ref, o_ref, s_ref = refs[8:]

    @pl.when(n == 0)
    def _():
        s_ref[...] = jnp.zeros_like(s_ref)

    qk = jnp.concatenate([r[...] for r in refs[0:4]], axis=1)
    v = jnp.concatenate([r[...] for r in refs[4:8]], axis=1)
    lane = lax.broadcasted_iota(jnp.int32, (CHUNK, 2 * GLA_HEADS * GLA_DK), 1)
    partner = jnp.where((lane & 1) == 0, pltpu.roll(qk, 2 * GLA_HEADS * GLA_DK - 1, axis=1),
                        pltpu.roll(qk, 1, axis=1))
    qk = qk * cos_ref[...] + partner * sin_ref[...]
    hk = GLA_HEADS * GLA_DK
    q = qk[:, 0:hk] * (GLA_DK ** -0.5)
    k = qk[:, hk:]
    g = -_softplus(-(_mm3(ad_ref[...], aup_ref[...]) + ab_ref[...])) / GLA_TAU
    _, incl = _order_masks(d)
    b = _mm_wx(jnp.where(incl, 1.0, 0.0).astype(BF16), g)
    b_last = jnp.sum(g, axis=0, keepdims=True)
    q_e = q * jnp.exp(b)
    k_e = k * jnp.exp(-b)
    k_end = k * jnp.exp(b_last - b)
    dec = jnp.exp(b_last)
    for h in range(GLA_HEADS):
        sk = slice(h * GLA_DK, (h + 1) * GLA_DK)
        sv = slice(h * GLA_DV, (h + 1) * GLA_DV)
        att = jnp.where(incl, _mm_nt(q_e[:, sk], k_e[:, sk]), 0.0)
        st = s_ref[h]
        o_ref[:, sv] = _mm(att, v[:, sv]) + _mm_nt(q_e[:, sk], st)
        s_ref[h] = st * dec[:, sk] + _mm_tn(v[:, sv], k_end[:, sk])


def _gla_scan(p, t, lc, cos_t, sin_t, aup_pad, a_b):
    tt = t + lc
    nl, nc = t // CHUNK, lc // CHUNK
    ch = functools.partial(_chunk_of, nl=nl, nc=nc)
    blk = lambda cb: pl.BlockSpec((CHUNK, MIX_W), lambda d, n: (ch(d, n), cb))
    lanes = lambda cb, cnt: [pl.BlockSpec((CHUNK, LANE), functools.partial(lambda d, n, c: (ch(d, n), c), c=cb + q))
                             for q in range(cnt)]
    return dict(
        in_specs=(lanes(CB128_GLA_QK, 4) + lanes(CB128_GLA_V, 4) + lanes(CB128_GLA_AD, 1)
                  + [blk(0), blk(0),
                     pl.BlockSpec((None, LANE, 2 * LANE), lambda d, n: (d, 0, 0)),
                     pl.BlockSpec((None, 1, 2 * LANE), lambda d, n: (d, 0, 0))]),
        out_specs=[pl.BlockSpec((None, CHUNK, MIX_W), lambda d, n: (d, ch(d, n), 0))],
        out_shape=[jax.ShapeDtypeStruct((2, tt, MIX_W), F32)],
        scratch_shapes=[pltpu.VMEM((GLA_HEADS, GLA_DV, GLA_DK), F32)],
        args=[p] * 9 + [cos_t, sin_t, aup_pad, a_b])


def _scan_kernel(*refs, n_a, n_b, nl, nc):
    outs = refs[n_a + n_b:n_a + n_b + 3]
    _rwkv_kernel(*refs[0:n_a], outs[0], outs[1], refs[-2], nl=nl, nc=nc)
    _gla_kernel(*refs[n_a:n_a + n_b], outs[2], refs[-1])


def _scans(rw, gla, t, lc):
    nl, nc = t // CHUNK, lc // CHUNK
    return pl.pallas_call(
        functools.partial(_scan_kernel, n_a=len(rw['in_specs']), n_b=len(gla['in_specs']), nl=nl, nc=nc),
        grid=(2, nl + nc),
        in_specs=rw['in_specs'] + gla['in_specs'],
        out_specs=rw['out_specs'] + gla['out_specs'],
        out_shape=rw['out_shape'] + gla['out_shape'],
        scratch_shapes=rw['scratch_shapes'] + gla['scratch_shapes'],
        compiler_params=_cparams(('arbitrary', 'arbitrary')), name='scans',
    )(*(rw['args'] + gla['args']))


def _head_norm(y, e_bf, width, eps):
    mu = _mm_xw(y, e_bf) * (1.0 / width)
    dl = y - mu
    var = _mm_xw(dl * dl, e_bf) * (1.0 / width)
    return dl * lax.rsqrt(var + eps)


def _mix_finish_kernel(y0_ref, y1_ref, b0_ref, b1_ref, sm_ref, gup_ref, lng_ref, lnb_ref, e64_ref,
                       o0_ref, o1_ref, gr0_ref, gr1_ref, gr2_ref, gr3_ref, gng_ref, e128_ref, a_ref, b_ref):
    yn = _head_norm(y0_ref[...] + y1_ref[...], e64_ref[...], RW_HEAD, RW_GN_EPS)
    yn = yn * lng_ref[...] + lnb_ref[...] + b0_ref[...] + b1_ref[...]
    gate = _mm(_sigmoid(sm_ref[:, 256:384]), gup_ref[...])
    a_ref[...] = (yn * gate).astype(BF16)
    on = _head_norm(o0_ref[...] + o1_ref[...], e128_ref[...], GLA_DV, GN_EPS) * gng_ref[...]
    gr = jnp.concatenate([gr0_ref[...], gr1_ref[...], gr2_ref[...], gr3_ref[...]], axis=1)
    b_ref[...] = (on * (gr * _sigmoid(gr))).astype(BF16)


def _mix_finish(p, rw_y, rw_bonus, gla_o, g_up, ln_g, ln_b, gn_g, e64, e128, t):
    tt = p.shape[0]
    bm = _row_tile(t, tt - t)
    dblk = lambda d: pl.BlockSpec((None, bm, MIX_W), lambda i: (d, i, 0))
    vec = pl.BlockSpec((1, MIX_W), lambda i: (0, 0))
    mat = pl.BlockSpec((MIX_W, MIX_W), lambda i: (0, 0))
    out = pl.BlockSpec((bm, MIX_W), lambda i: (i, 0))
    return pl.pallas_call(
        _mix_finish_kernel,
        grid=(tt // bm,),
        in_specs=([dblk(0), dblk(1), dblk(0), dblk(1),
                   pl.BlockSpec((bm, 3 * LANE), lambda i: (i, CB384_RW_SMALL)),
                   pl.BlockSpec((RW_GATE_RANK, MIX_W), lambda i: (0, 0)), vec, vec, mat,
                   dblk(0), dblk(1)]
                  + [pl.BlockSpec((bm, LANE), functools.partial(lambda i, c: (i, c), c=CB128_GLA_R + q))
                     for q in range(4)]
                  + [vec, mat]),
        out_specs=[out, out],
        out_shape=[jax.ShapeDtypeStruct((tt, MIX_W), BF16)] * 2,
        compiler_params=_cparams(('parallel',)), name='mix_finish',
    )(rw_y, rw_y, rw_bonus, rw_bonus, p, g_up, ln_g, ln_b, e64, gla_o, gla_o, p, p, p, p, gn_g, e128)


def _sgu_kernel(u_ref, v_ref, lng_ref, lnb_ref, ws_ref, bs_ref, o_ref):
    u = _gelu(u_ref[...])
    v = _gelu(v_ref[...])
    mu = jnp.mean(v, axis=-1, keepdims=True)
    dl = v - mu
    var = jnp.mean(dl * dl, axis=-1, keepdims=True)
    vn = (dl * lax.rsqrt(var + GN_EPS) * lng_ref[...] + lnb_ref[...]).astype(BF16)
    lane = lax.broadcasted_iota(jnp.int32, (1, MIX_W), 1)
    s = bs_ref[...]
    for g in range(SGU_GROUPS):
        s = s + jnp.where((lane >> 6) == g, jnp.dot(ws_ref[g].astype(BF16), vn, preferred_element_type=F32), 0.0)
    o_ref[...] = (u * s).astype(BF16)


def _sgu(p, ln_g, ln_b, w_s, b_full):
    tt = p.shape[0]
    vec = pl.BlockSpec((1, MIX_W), lambda i: (0, 0))
    return pl.pallas_call(
        _sgu_kernel,
        grid=(tt // SGU_CHUNK,),
        in_specs=[pl.BlockSpec((SGU_CHUNK, MIX_W), lambda i: (i, CB_SGU_U)),
                  pl.BlockSpec((SGU_CHUNK, MIX_W), lambda i: (i, CB_SGU_V)), vec, vec,
                  pl.BlockSpec((SGU_GROUPS, SGU_CHUNK, SGU_CHUNK), lambda i: (0, 0, 0)),
                  pl.BlockSpec((SGU_CHUNK, MIX_W), lambda i: (0, 0))],
        out_specs=pl.BlockSpec((SGU_CHUNK, MIX_W), lambda i: (i, 0)),
        out_shape=jax.ShapeDtypeStruct((tt, MIX_W), BF16),
        compiler_params=_cparams(('parallel',)), name='sgu',
    )(p, p, ln_g, ln_b, w_s, b_full)


def _na_bias_kernel(rpb_ref, o_ref, *, wh):
    h = pl.program_id(0)
    n_dc = 2 * NA_WIN_W - 1
    n_dr = 2 * NA_WIN_H - 1
    shape = (GRID_W, 2 * GRID_W)
    c = lax.broadcasted_iota(jnp.int32, shape, 0)
    lane = lax.broadcasted_iota(jnp.int32, shape, 1)
    x = lane & (GRID_W - 1)
    dc = jnp.clip(x - c + (NA_WIN_W - 1), 0, 2 * NA_WIN_W - 2)
    key = (lane >> 6) * n_dc + dc
    cs = jnp.clip(c - NA_WIN_W // 2, 0, GRID_W - NA_WIN_W)
    ok = jnp.logical_and(x >= cs, x < cs + NA_WIN_W)
    pairs = []
    for dr in range(n_dr - 1):
        base = h * (n_dr * n_dc) + dr * n_dc
        tile = lax.fori_loop(0, 2 * n_dc, lambda j, acc: jnp.where(key == j, rpb_ref[base + j], acc),
                             jnp.zeros(shape, F32))
        pairs.append(jnp.where(ok, tile, NEG_INF))
    for dr0 in range(n_dr - wh + 1):
        for jj in range(wh // 2):
            o_ref[dr0, 0, :, jj * 2 * GRID_W:(jj + 1) * 2 * GRID_W] = pairs[dr0 + 2 * jj]


def _na_bias(rpb, wh):
    n_dr0 = 2 * NA_WIN_H - wh
    return pl.pallas_call(
        functools.partial(_na_bias_kernel, wh=wh),
        grid=(NA_HEADS,),
        in_specs=[pl.BlockSpec(memory_space=pltpu.SMEM)],
        out_specs=pl.BlockSpec((n_dr0, 1, GRID_W, wh * GRID_W), lambda h: (0, h, 0, 0)),
        out_shape=jax.ShapeDtypeStruct((n_dr0, NA_HEADS, GRID_W, wh * GRID_W), F32),
        compiler_params=_cparams(('parallel',)), name='na_bias',
    )(rpb.reshape(-1))


def _na_kernel(*refs, wh):
    q_ref = refs[0]
    k_refs = refs[1:1 + wh]
    v_refs = refs[1 + wh:1 + 2 * wh]
    kc_ref, vc_ref, bias_ref, o_ref = refs[1 + 2 * wh:]
    q = q_ref[...] * (NA_HEAD ** -0.5)
    kw = jnp.concatenate([r[...].astype(BF16) for r in k_refs], axis=0)
    vw = jnp.concatenate([r[...].astype(BF16) for r in v_refs], axis=0)
    kc = kc_ref[...].astype(BF16)
    vc = vc_ref[...].astype(BF16)
    lane = lax.broadcasted_iota(jnp.int32, (1, LANE), 1)
    outs = []
    for pr in range(NA_HEADS // 2):
        sl = slice(pr * LANE, (pr + 1) * LANE)
        qp, kwp, vwp, kcp, vcp = q[:, sl], kw[:, sl], vw[:, sl], kc[:, sl], vc[:, sl]
        o = jnp.zeros((GRID_W, LANE), F32)
        for s in range(2):
            hm = (lane >> 6) == s
            qh = jnp.where(hm, qp, 0.0).astype(BF16)
            sw = _mm_nt(qh, kwp) + bias_ref[0, 2 * pr + s]
            sc = _mm_nt(qh, kcp)
            m = jnp.maximum(jnp.max(sw, axis=-1, keepdims=True), jnp.max(sc, axis=-1, keepdims=True))
            ew = jnp.exp(sw - m)
            ec = jnp.exp(sc - m)
            den = jnp.sum(ew, axis=-1, keepdims=True) + jnp.sum(ec, axis=-1, keepdims=True)
            oh = (_mm(ew, vwp) + _mm(ec, vcp)) / den
            o = jnp.where(hm, oh, o)
        outs.append(o)
    o_ref[...] = jnp.concatenate(outs, axis=1).astype(BF16)


def _na_ctx_kernel(q_ref, k_ref, v_ref, o_ref):
    q = q_ref[...] * (NA_HEAD ** -0.5)
    kc = k_ref[...].astype(BF16)
    vc = v_ref[...].astype(BF16)
    lane = lax.broadcasted_iota(jnp.int32, (1, MIX_W), 1)
    o = jnp.zeros(q.shape, F32)
    for h in range(NA_HEADS):
        hm = (lane >> 6) == h
        s = _mm_nt(jnp.where(hm, q, 0.0), kc)
        e = jnp.exp(s - jnp.max(s, axis=-1, keepdims=True))
        o = jnp.where(hm, _mm(e, vc) / jnp.sum(e, axis=-1, keepdims=True), o)
    o_ref[...] = o.astype(BF16)


def _na(p, bias, t, lc):
    rows = t // GRID_W
    wh = min(NA_WIN_H, rows)
    rs = lambda r: jnp.clip(r - wh // 2, 0, rows - wh)
    ctx_blk = t // lc
    kv = lambda cb: [pl.BlockSpec((GRID_W, MIX_W), functools.partial(lambda r, w, cb: (rs(r) + w, cb), w=w, cb=cb))
                     for w in range(wh)]
    lat = pl.pallas_call(
        functools.partial(_na_kernel, wh=wh),
        grid=(rows,),
        in_specs=([pl.BlockSpec((GRID_W, MIX_W), lambda r: (r, CB_NA_Q))] + kv(CB_NA_K) + kv(CB_NA_V)
                  + [pl.BlockSpec((lc, MIX_W), lambda r: (ctx_blk, CB_NA_K)),
                     pl.BlockSpec((lc, MIX_W), lambda r: (ctx_blk, CB_NA_V)),
                     pl.BlockSpec((1, NA_HEADS, GRID_W, wh * GRID_W),
                                  lambda r: (rs(r) - r + (NA_WIN_H - 1), 0, 0, 0))]),
        out_specs=pl.BlockSpec((GRID_W, MIX_W), lambda r: (r, 0)),
        out_shape=jax.ShapeDtypeStruct((t, MIX_W), BF16),
        compiler_params=_cparams(('parallel',)), name='na_latent',
    )(*([p] * (1 + 2 * wh + 2)), bias)
    cblk = lambda cb: pl.BlockSpec((lc, MIX_W), lambda i: (ctx_blk, cb))
    ctx = pl.pallas_call(
        _na_ctx_kernel,
        grid=(1,),
        in_specs=[cblk(CB_NA_Q), cblk(CB_NA_K), cblk(CB_NA_V)],
        out_specs=pl.BlockSpec((lc, MIX_W), lambda i: (0, 0)),
        out_shape=jax.ShapeDtypeStruct((lc, MIX_W), BF16),
        compiler_params=_cparams(('arbitrary',)), name='na_ctx',
    )(p, p, p)
    return jnp.concatenate([lat, ctx], axis=0)


def _merge_kernel(a0, a1, a2, a3, g0, g1, g2, g3, w_ref, o_ref, wb_ref):
    @pl.when(pl.program_id(1) == 0)
    def _():
        wb_ref[...] = w_ref[...].astype(BF16)
    acc = None
    for n, (a_ref, g_ref) in enumerate(((a0, g0), (a1, g1), (a2, g2), (a3, g3))):
        zn = jnp.dot(a_ref[...], wb_ref[n], preferred_element_type=F32) * _sigmoid(g_ref[...])
        acc = zn if acc is None else acc + zn
    o_ref[...] = acc.astype(BF16)


def _merge(ys, p, w_br, lyr):
    tt = p.shape[0]
    bn = 512
    bm = _pick(tt, (768, 384, 128))
    a_s = pl.BlockSpec((bm, MIX_W), lambda j, i: (i, 0))
    gs = [pl.BlockSpec((bm, bn), functools.partial(lambda j, i, n: (i, (GATE_OFF + n * D_MODEL) // bn + j), n=n))
          for n in range(N_BRANCH)]
    return pl.pallas_call(
        _merge_kernel,
        grid=(D_MODEL // bn, tt // bm),
        in_specs=[a_s] * 4 + gs + [pl.BlockSpec((None, N_BRANCH, MIX_W, bn), lambda j, i: (lyr, 0, 0, j))],
        out_specs=pl.BlockSpec((bm, bn), lambda j, i: (i, j)),
        out_shape=jax.ShapeDtypeStruct((tt, D_MODEL), BF16),
        scratch_shapes=[pltpu.VMEM((N_BRANCH, MIX_W, bn), BF16)],
        compiler_params=_cparams(('arbitrary', 'arbitrary')), name='merge',
    )(*ys, p, p, p, p, w_br)


def _block_diag_ones(width):
    i = jnp.arange(MIX_W) // width
    return (i[:, None] == i[None, :]).astype(BF16)


def _rope_tables(t, lc):
    tok = jnp.arange(t)
    pos = jnp.stack([tok // GRID_W, tok % GRID_W], axis=-1).astype(F32)
    nf = GLA_DK // 4
    inv = ROPE_BASE ** (-jnp.arange(nf, dtype=F32) / nf)
    ang = pos[:, :, None] * inv
    cos = jnp.repeat(jnp.cos(ang), 2, axis=-1).reshape(t, GLA_DK)
    sin = jnp.sin(ang)
    sin = jnp.stack([-sin, sin], axis=-1).reshape(t, GLA_DK)
    reps = 2 * GLA_HEADS
    cos = jnp.concatenate([jnp.tile(cos, (1, reps)), jnp.ones((lc, reps * GLA_DK), F32)], axis=0)
    sin = jnp.concatenate([jnp.tile(sin, (1, reps)), jnp.zeros((lc, reps * GLA_DK), F32)], axis=0)
    return cos, sin


def _pad_rank_rows(w_up, rank, rows):
    out = jnp.zeros((2, rows, w_up.shape[-1]), w_up.dtype)
    for d in range(2):
        out = out.at[d, d * rank:(d + 1) * rank].set(w_up[d])
    return out


def kernel(x, c, ctx, c_ctx, ada_w, ada_b, norm_g, w_in, rw_mu, rw_w0, rw_w_up, rw_a0, rw_a_up, rw_g_up, rw_k_k, rw_k_a, rw_r_k, rw_ln_g, rw_ln_b, gla_a_up, gla_a_b, gla_gn_g, sgu_ln_g, sgu_ln_b, sgu_w, sgu_b, na_rpb, w_br, w_o, ffn_w1, ffn_w3, ffn_w2, moe_router, moe_w1, moe_w3, moe_w2):
    assert x.shape[0] == 1 and x.shape[2] == D_MODEL
    t, lc = x.shape[1], ctx.shape[1]
    depth = ada_w.shape[0]
    assert t % max(lc, SGU_CHUNK) == 0 and lc % SGU_CHUNK == 0 and t % GRID_W == 0
    rows = t // GRID_W
    assert rows >= NA_WIN_H
    wh = NA_WIN_H
    xs = jnp.concatenate([x[0], ctx[0]], axis=0)
    cond8 = jnp.zeros((8, D_MODEL), F32).at[0].set(c[0]).at[1].set(c_ctx)
    e64 = _block_diag_ones(RW_HEAD)
    e128 = _block_diag_ones(GLA_DV)
    cos_t, sin_t = _rope_tables(t, lc)
    row1 = lambda v: v.reshape(1, -1)
    assert w_in.shape[2] == N_IN

    mods_all = [_adaln(cond8, ada_w, ada_b, i)[0:2].reshape(2, 6, D_MODEL) for i in range(depth)]
    h = _norm_mod_call(xs, norm_g[0, 0], mods_all[0], 0, t)
    for i in range(depth):
        mods = mods_all[i]
        last = i == depth - 1
        p = _in_proj(h, w_in, i)
        mu2 = jnp.stack([rw_mu[i, :, 0].reshape(-1), rw_mu[i, :, 1].reshape(-1)])
        rw_call = _rwkv_scan(
            p, t, lc, mu2, rw_w0[i].reshape(2, 1, MIX_W), _pad_rank_rows(rw_w_up[i], RW_DECAY_RANK, LANE),
            rw_a0[i].reshape(2, 1, MIX_W), _pad_rank_rows(rw_a_up[i], RW_ICLR_RANK, LANE),
            row1(rw_k_k[i]), row1(rw_k_a[i]), row1(rw_r_k[i]), e64)
        gla_call = _gla_scan(p, t, lc, cos_t, sin_t, _pad_rank_rows(gla_a_up[i], GLA_GATE_RANK, LANE),
                             gla_a_b[i].reshape(2, 1, 2 * LANE))
        rw_y, rw_bonus, gla_o = _scans(rw_call, gla_call, t, lc)
        y_a, y_b = _mix_finish(p, rw_y, rw_bonus, gla_o, rw_g_up[i], row1(rw_ln_g[i]), row1(rw_ln_b[i]),
                               row1(gla_gn_g[i]), e64, e128, t)
        y_s = _sgu(p, row1(sgu_ln_g[i]), row1(sgu_ln_b[i]), sgu_w[i], jnp.repeat(sgu_b[i].T, 64, axis=1))
        y_d = _na(p, _na_bias(na_rpb[i], wh), t, lc)
        z = _merge((y_a, y_b, y_s, y_d), p, w_br, i)
        y = _matmul(z, w_o, i, 1024, (768, 384, 128), name='out_proj')
        j = i // 2
        if i % 2 == 0:
            xs, h = _resid_next_call(xs, y, norm_g[i, 1], mods, 2, t, norm_g[i, 2], mods, 3)
            f = _ffn_down(_ffn_up(h, ffn_w1, ffn_w3, j), ffn_w2, j)
            if last:
                xs = _resid_call(xs, f, norm_g[i, 3], mods, 5, t)
            else:
                xs, h = _resid_next_call(xs, f, norm_g[i, 3], mods, 5, t, norm_g[i + 1, 0], mods_all[i + 1], 0)
        else:
            xs, h32, route = _resid_next_call(xs, y, norm_g[i, 1], mods, 2, t, norm_g[i, 2], mods, 3,
                                              router=moe_router[j])
            xs = _moe_layer(xs, h32, route, moe_w1, moe_w3, moe_w2, j, norm_g[i, 3], mods, 5, t)
            if not last:
                h = _norm_mod_call(xs, norm_g[i + 1, 0], mods_all[i + 1], 0, t)
    return xs[0:t][None]
```

```python
import functools

import jax
import jax.numpy as jnp
from jax import lax
from jax.experimental import pallas as pl
from jax.experimental.pallas import tpu as pltpu

F32 = jnp.float32
BF16 = jnp.bfloat16

D_MODEL = 2048
GRID_W = 64
N_BRANCH = 4
MIX_W = D_MODEL // 4
RW_HEAD = 64
RW_HEADS = MIX_W // RW_HEAD
RW_GROUP = 4
RW_DECAY_RANK = 64
RW_ICLR_RANK = 64
RW_GATE_RANK = 128
RW_GN_EPS = 64e-5
GLA_HEADS = 4
GLA_DV = MIX_W // GLA_HEADS
GLA_DK = GLA_DV // 2
GLA_GATE_RANK = 16
GLA_TAU = 16.0
GN_EPS = 1e-5
ROPE_BASE = 10000.0
SGU_GROUPS = MIX_W // 64
SGU_CHUNK = 128
NA_HEAD = 64
NA_HEADS = MIX_W // NA_HEAD
NA_WIN_H = 8
NA_WIN_W = 16
FFN_DIM = 7 * D_MODEL // 2
N_EXPERTS = 8
TOP_K = 2
EPS = 1e-6
NEG_INF = -1e30

RT_G1, RT_G2, RT_I1, RT_I2 = 8, 9, 10, 11
MOE_BM = 256
CHUNK = 64
LANE = 128
VMEM_LIMIT = 56 * 2 ** 20

IN_TAIL = 3 * MIX_W + 2 * RW_DECAY_RANK + 2 * RW_ICLR_RANK + RW_GATE_RANK + 2 * GLA_HEADS * GLA_DK \
    + 2 * MIX_W + 2 * GLA_GATE_RANK
IN_PAD = -IN_TAIL % LANE
N_IN = IN_TAIL + 5 * MIX_W + N_BRANCH * D_MODEL
N_IN_P = N_IN + IN_PAD
CB384_RW_SMALL = 4
CB128_GLA_QK, CB128_GLA_V, CB128_GLA_R, CB128_GLA_AD = 15, 19, 23, 27
CB_SGU_U, CB_SGU_V, CB_NA_Q, CB_NA_K, CB_NA_V = 7, 8, 9, 10, 11
GATE_OFF = 6144


def _cparams(sem):
    return pltpu.CompilerParams(dimension_semantics=sem, vmem_limit_bytes=VMEM_LIMIT)


def _pick(m, cands):
    for c in cands:
        if m % c == 0:
            return c
    raise ValueError(f'no tile for {m}')


def _mm(a, b):
    return jnp.dot(a.astype(BF16), b.astype(BF16), preferred_element_type=F32)


def _mm_nt(a, b):
    return lax.dot_general(a.astype(BF16), b.astype(BF16), (((1,), (1,)), ((), ())),
                           preferred_element_type=F32)


def _mm_tn(a, b):
    return lax.dot_general(a.astype(BF16), b.astype(BF16), (((0,), (0,)), ((), ())),
                           preferred_element_type=F32)


def _split2(x):
    hi = x.astype(BF16)
    lo = (x - hi.astype(F32)).astype(BF16)
    return hi, lo


def _mm_xw(x, w_bf):
    hi, lo = _split2(x)
    return (jnp.dot(hi, w_bf, preferred_element_type=F32)
            + jnp.dot(lo, w_bf, preferred_element_type=F32))


def _mm_wx(w_bf, x):
    hi, lo = _split2(x)
    return (jnp.dot(w_bf, hi, preferred_element_type=F32)
            + jnp.dot(w_bf, lo, preferred_element_type=F32))


def _mm3(a, b):
    ah, al = _split2(a)
    bh, bl = _split2(b)
    return (jnp.dot(ah, bh, preferred_element_type=F32)
            + jnp.dot(ah, bl, preferred_element_type=F32)
            + jnp.dot(al, bh, preferred_element_type=F32))


def _sigmoid(x):
    return 1.0 / (1.0 + jnp.exp(-x))


def _softplus(x):
    return jnp.maximum(x, 0.0) + jnp.log1p(jnp.exp(-jnp.abs(x)))


def _gelu(x):
    return 0.5 * x * (1.0 + lax.erf(x * (0.5 ** 0.5)))


def _order_masks(d):
    t = lax.broadcasted_iota(jnp.int32, (CHUNK, CHUNK), 0)
    s = lax.broadcasted_iota(jnp.int32, (CHUNK, CHUNK), 1)
    diff = (t - s) * jnp.where(d == 0, 1, -1)
    return diff > 0, diff >= 0


def _chunk_of(d, n, nl, nc):
    fwd = jnp.where(n < nc, nl + n, n - nc)
    bwd = jnp.where(n < nc, nl + nc - 1 - n, nl - 1 - (n - nc))
    return jnp.where(d == 0, fwd, bwd)


def _ada_kernel(c_ref, w_ref, b_ref, o_ref):
    cnd = c_ref[...]
    a = cnd * _sigmoid(cnd)
    o_ref[...] = _mm(a, w_ref[...]) + b_ref[...]


def _adaln(cond8, ada_w, ada_b, lyr):
    depth, _, n = ada_w.shape
    bn = 1536
    return pl.pallas_call(
        _ada_kernel,
        grid=(n // bn,),
        in_specs=[pl.BlockSpec((8, D_MODEL), lambda j: (0, 0)),
                  pl.BlockSpec((None, D_MODEL, bn), lambda j: (lyr, 0, j)),
                  pl.BlockSpec((None, 1, bn), lambda j: (lyr, 0, j))],
        out_specs=pl.BlockSpec((8, bn), lambda j: (0, j)),
        out_shape=jax.ShapeDtypeStruct((8, n), F32),
        compiler_params=_cparams(('arbitrary',)),
        name='adaln',
    )(cond8, ada_w, ada_b.reshape(depth, 1, n))


def _norm_mod(x, g, m, k0):
    y = x * lax.rsqrt(jnp.mean(x * x, axis=-1, keepdims=True) + EPS) * g
    return y * (1.0 + m[k0 + 1:k0 + 2]) + m[k0:k0 + 1]


def _pack_halves(h):
    n = h.shape[1] // 2
    bits = lax.bitcast_convert_type(h.astype(BF16).astype(F32), jnp.uint32)
    return (bits[:, 0:n] >> 16) | bits[:, n:]


def _unpack_halves(w):
    lo = lax.bitcast_convert_type(w << 16, F32)
    hi = lax.bitcast_convert_type(w & jnp.uint32(0xFFFF0000), F32)
    return jnp.concatenate([lo, hi], axis=1).astype(BF16)


def _norm_mod_kernel(x_ref, g_ref, m_ref, o_ref, *, k0):
    o_ref[...] = _norm_mod(x_ref[...], g_ref[...], m_ref[0], k0).astype(BF16)


def _route_table(h, router):
    logits = _mm3(h, router)
    lane = lax.broadcasted_iota(jnp.int32, logits.shape, 1)
    neg = jnp.float32(-jnp.inf)
    l1 = jnp.where(lane < N_EXPERTS, logits, neg)
    m1 = jnp.max(l1, axis=-1, keepdims=True)
    i1 = jnp.min(jnp.where(l1 == m1, lane, LANE), axis=-1, keepdims=True)
    l2 = jnp.where(lane == i1, neg, l1)
    m2 = jnp.max(l2, axis=-1, keepdims=True)
    i2 = jnp.min(jnp.where(l2 == m2, lane, LANE), axis=-1, keepdims=True)
    e2 = jnp.exp(m2 - m1)
    den = 1.0 + e2
    sel = jnp.where(jnp.logical_or(lane == i1, lane == i2), 1.0, 0.0)
    sel = jnp.where(lane == RT_G1, 1.0 / den, jnp.where(lane == RT_G2, e2 / den, sel))
    return jnp.where(lane == RT_I1, i1.astype(F32), jnp.where(lane == RT_I2, i2.astype(F32), sel))


def _row_tile(t, lc):
    return _pick(lc, (256, 128))


def _norm_mod_call(x, g, mods, k0, t):
    tt = x.shape[0]
    bm = _row_tile(t, tt - t)
    nlt = t // bm
    xs = pl.BlockSpec((bm, D_MODEL), lambda i: (i, 0))
    gs = pl.BlockSpec((1, D_MODEL), lambda i: (0, 0))
    ms = pl.BlockSpec((1, 6, D_MODEL), lambda i: (jnp.where(i >= nlt, 1, 0), 0, 0))
    return pl.pallas_call(
        functools.partial(_norm_mod_kernel, k0=k0),
        grid=(tt // bm,), in_specs=[xs, gs, ms], out_specs=xs,
        out_shape=jax.ShapeDtypeStruct((tt, D_MODEL), BF16),
        compiler_params=_cparams(('parallel',)), name='norm_mod',
    )(x, g.reshape(1, D_MODEL), mods)


def _resid_kernel(x_ref, y_ref, g_ref, m_ref, o_ref, *, kg):
    y = y_ref[...]
    n = y * lax.rsqrt(jnp.mean(y * y, axis=-1, keepdims=True) + EPS) * g_ref[...]
    o_ref[...] = x_ref[...] + m_ref[0][kg:kg + 1] * n


def _resid_call(x, y, g, mods, kg, t):
    tt = x.shape[0]
    bm = _row_tile(t, tt - t)
    nlt = t // bm
    xs = pl.BlockSpec((bm, D_MODEL), lambda i: (i, 0))
    return pl.pallas_call(
        functools.partial(_resid_kernel, kg=kg),
        grid=(tt // bm,),
        in_specs=[xs, xs, pl.BlockSpec((1, D_MODEL), lambda i: (0, 0)),
                  pl.BlockSpec((1, 6, D_MODEL), lambda i: (jnp.where(i >= nlt, 1, 0), 0, 0))],
        out_specs=xs,
        out_shape=jax.ShapeDtypeStruct((tt, D_MODEL), F32),
        compiler_params=_cparams(('parallel',)), name='resid_norm',
    )(x, y, g.reshape(1, D_MODEL), mods)


def _resid_next_kernel(*refs, kg, k0, route):
    x_ref, y_ref, g_ref, m_ref, g2_ref, m2_ref = refs[0:6]
    y = y_ref[...]
    n = y * lax.rsqrt(jnp.mean(y * y, axis=-1, keepdims=True) + EPS) * g_ref[...]
    xn = x_ref[...] + m_ref[0][kg:kg + 1] * n
    h = _norm_mod(xn, g2_ref[...], m2_ref[0], k0)
    if route:
        r_ref, xo_ref, ho_ref, rt_ref = refs[6:]
        ho_ref[...] = _pack_halves(h)
        rt_ref[...] = _route_table(h, r_ref[...])
    else:
        xo_ref, ho_ref = refs[6:]
        ho_ref[...] = h.astype(BF16)
    xo_ref[...] = xn


def _resid_next_call(x, y, g, mods, kg, t, g2, mods2, k0, router=None):
    tt = x.shape[0]
    bm = _row_tile(t, tt - t)
    nlt = t // bm
    xs = pl.BlockSpec((bm, D_MODEL), lambda i: (i, 0))
    gs = pl.BlockSpec((1, D_MODEL), lambda i: (0, 0))
    ms = pl.BlockSpec((1, 6, D_MODEL), lambda i: (jnp.where(i >= nlt, 1, 0), 0, 0))
    in_specs = [xs, xs, gs, ms, gs, ms]
    args = [x, y, g.reshape(1, D_MODEL), mods, g2.reshape(1, D_MODEL), mods2]
    out_specs = [xs, xs]
    out_shape = [jax.ShapeDtypeStruct((tt, D_MODEL), F32), jax.ShapeDtypeStruct((tt, D_MODEL), BF16)]
    if router is not None:
        in_specs.append(pl.BlockSpec((D_MODEL, LANE), lambda i: (0, 0)))
        args.append(jnp.pad(router, ((0, 0), (0, LANE - N_EXPERTS))))
        out_specs = [xs, pl.BlockSpec((bm, D_MODEL // 2), lambda i: (i, 0)), pl.BlockSpec((bm, LANE), lambda i: (i, 0))]
        out_shape = [out_shape[0], jax.ShapeDtypeStruct((tt, D_MODEL // 2), jnp.uint32),
                     jax.ShapeDtypeStruct((tt, LANE), F32)]
    return pl.pallas_call(
        functools.partial(_resid_next_kernel, kg=kg, k0=k0, route=router is not None),
        grid=(tt // bm,), in_specs=in_specs, out_specs=out_specs, out_shape=out_shape,
        compiler_params=_cparams(('parallel',)), name='resid_next',
    )(*args)


def _mm_kernel(a_ref, w_ref, o_ref, wb_ref):
    @pl.when(pl.program_id(1) == 0)
    def _():
        wb_ref[...] = w_ref[...].astype(BF16)
    o_ref[...] = jnp.dot(a_ref[...], wb_ref[...], preferred_element_type=F32).astype(o_ref.dtype)


def _matmul(a, w, lyr, bn, bm_cands, out_dtype=F32, name='matmul'):
    m, k = a.shape
    n = w.shape[2]
    bm = _pick(m, bm_cands)
    return pl.pallas_call(
        _mm_kernel,
        grid=(n // bn, m // bm),
        in_specs=[pl.BlockSpec((bm, k), lambda j, i: (i, 0)),
                  pl.BlockSpec((None, k, bn), lambda j, i: (lyr, 0, j))],
        out_specs=pl.BlockSpec((bm, bn), lambda j, i: (i, j)),
        out_shape=jax.ShapeDtypeStruct((m, n), out_dtype),
        scratch_shapes=[pltpu.VMEM((k, bn), BF16)],
        compiler_params=_cparams(('arbitrary', 'arbitrary')), name=name,
    )(a, w)


def _in_proj_kernel(a_ref, wt_ref, o_ref, wb_ref, *, bn):
    j = pl.program_id(0)
    straddle = IN_TAIL // bn
    cut = IN_TAIL - straddle * bn

    @pl.when(pl.program_id(1) == 0)
    def _():
        @pl.when(j != straddle)
        def _():
            wb_ref[...] = wt_ref[0].astype(BF16)

        @pl.when(j == straddle)
        def _():
            wb_ref[0:cut + IN_PAD, :] = wt_ref[0, 0:cut + IN_PAD, :].astype(BF16)
            wb_ref[cut + IN_PAD:, :] = wt_ref[0, cut:bn - IN_PAD, :].astype(BF16)

    o_ref[...] = lax.dot_general(a_ref[...], wb_ref[...], (((1,), (1,)), ((), ())), preferred_element_type=F32)


def _in_proj(a, w_in, lyr):
    m = a.shape[0]
    bn = 1024
    bm = _pick(m, (768, 384, 128))
    straddle = IN_TAIL // bn
    assert IN_PAD % 8 == 0 and straddle == (IN_TAIL + IN_PAD - 1) // bn
    w_t = jnp.swapaxes(w_in, 1, 2)
    return pl.pallas_call(
        functools.partial(_in_proj_kernel, bn=bn),
        grid=(N_IN_P // bn, m // bm),
        in_specs=[pl.BlockSpec((bm, D_MODEL), lambda j, i: (i, 0)),
                  pl.BlockSpec((pl.Element(1), pl.Element(bn), pl.Element(D_MODEL)),
                               lambda j, i: (lyr, 8 * (j * (bn // 8) - jnp.where(j > straddle, IN_PAD // 8, 0)), 0))],
        out_specs=pl.BlockSpec((bm, bn), lambda j, i: (i, j)),
        out_shape=jax.ShapeDtypeStruct((m, N_IN_P), F32),
        scratch_shapes=[pltpu.VMEM((bn, D_MODEL), BF16)],
        compiler_params=_cparams(('arbitrary', 'arbitrary')), name='in_proj',
    )(a, w_t)


def _ffn_up_kernel(a_ref, w1_ref, w3_ref, o_ref, w1b, w3b):
    @pl.when(pl.program_id(1) == 0)
    def _():
        w1b[...] = w1_ref[...].astype(BF16)
        w3b[...] = w3_ref[...].astype(BF16)
    a = a_ref[...]
    h1 = jnp.dot(a, w1b[...], preferred_element_type=F32)
    h3 = jnp.dot(a, w3b[...], preferred_element_type=F32)
    o_ref[...] = (h1 * _sigmoid(h1) * h3).astype(BF16)


def _ffn_up(h, w1, w3, e):
    m = h.shape[0]
    bn = 512
    bm = _pick(m, (768, 384, 128))
    ws = pl.BlockSpec((None, D_MODEL, bn), lambda j, i: (e, 0, j))
    return pl.pallas_call(
        _ffn_up_kernel,
        grid=(FFN_DIM // bn, m // bm),
        in_specs=[pl.BlockSpec((bm, D_MODEL), lambda j, i: (i, 0)), ws, ws],
        out_specs=pl.BlockSpec((bm, bn), lambda j, i: (i, j)),
        out_shape=jax.ShapeDtypeStruct((m, FFN_DIM), BF16),
        scratch_shapes=[pltpu.VMEM((D_MODEL, bn), BF16), pltpu.VMEM((D_MODEL, bn), BF16)],
        compiler_params=_cparams(('arbitrary', 'arbitrary')), name='ffn_up',
    )(h, w1, w3)


def _ffn_down_kernel(a_ref, w_ref, o_ref):
    part = jnp.dot(a_ref[...], w_ref[...].astype(BF16), preferred_element_type=F32)

    @pl.when(pl.program_id(1) == 0)
    def _():
        o_ref[...] = part

    @pl.when(pl.program_id(1) > 0)
    def _():
        o_ref[...] += part


def _ffn_down(u, w2, e):
    m = u.shape[0]
    bk = 512
    bm = max(d for d in range(LANE, 1408 + 1, LANE) if m % d == 0)
    return pl.pallas_call(
        _ffn_down_kernel,
        grid=(m // bm, FFN_DIM // bk),
        in_specs=[pl.BlockSpec((bm, bk), lambda i, k: (i, k)),
                  pl.BlockSpec((None, bk, D_MODEL), lambda i, k: (e, k, 0))],
        out_specs=pl.BlockSpec((bm, D_MODEL), lambda i, k: (i, 0)),
        out_shape=jax.ShapeDtypeStruct((m, D_MODEL), F32),
        compiler_params=_cparams(('parallel', 'arbitrary')), name='ffn_down',
    )(u, w2)


def _moe_rank_kernel(r_ref, dest_ref, cnt_ref, carry_ref, tot_ref):
    ph = pl.program_id(0)
    i = pl.program_id(1)
    r = r_ref[...]
    bm = r.shape[0]
    lane = lax.broadcasted_iota(jnp.int32, (1, LANE), 1)
    oh = jnp.where(lane < N_EXPERTS, r, 0.0)
    colsum = jnp.sum(oh, axis=0, keepdims=True)

    @pl.when(jnp.logical_and(ph == 0, i == 0))
    def _():
        carry_ref[...] = jnp.zeros_like(carry_ref)

    @pl.when(jnp.logical_and(ph == 1, i == 0))
    def _():
        tot_ref[...] = carry_ref[...]
        carry_ref[...] = jnp.zeros_like(carry_ref)

    @pl.when(ph == 1)
    def _():
        tot = tot_ref[...]
        padded = jnp.floor((tot + (MOE_BM - 1)) * (1.0 / MOE_BM)) * MOE_BM
        a = lax.broadcasted_iota(jnp.int32, (LANE, LANE), 0)
        b = lax.broadcasted_iota(jnp.int32, (LANE, LANE), 1)
        upper = jnp.where(a < b, 1.0, 0.0).astype(BF16)
        offs = _mm_xw(jnp.broadcast_to(padded, (8, LANE)), upper)[0:1]
        tr = lax.broadcasted_iota(jnp.int32, (bm, bm), 0)
        ts = lax.broadcasted_iota(jnp.int32, (bm, bm), 1)
        before = jnp.dot(jnp.where(ts < tr, 1.0, 0.0).astype(BF16), oh.astype(BF16),
                         preferred_element_type=F32)
        slot = before + carry_ref[...] + offs
        lane_f = lane.astype(F32)
        d1 = jnp.sum(jnp.where(lane_f == r[:, RT_I1:RT_I1 + 1], slot, 0.0), axis=-1, keepdims=True)
        d2 = jnp.sum(jnp.where(lane_f == r[:, RT_I2:RT_I2 + 1], slot, 0.0), axis=-1, keepdims=True)
        dest_ref[...] = jnp.where(lane == 0, d1, jnp.where(lane == 1, d2, 0.0)).astype(jnp.int32)
        cnt_ref[...] = jnp.broadcast_to(tot, (8, LANE))

    carry_ref[...] = carry_ref[...] + colsum


def _moe_rank(route, bm):
    tt = route.shape[0]
    return pl.pallas_call(
        _moe_rank_kernel,
        grid=(2, tt // bm),
        in_specs=[pl.BlockSpec((bm, LANE), lambda ph, i: (i, 0))],
        out_specs=[pl.BlockSpec((bm, LANE), lambda ph, i: (i * ph, 0)),
                   pl.BlockSpec((8, LANE), lambda ph, i: (0, 0))],
        out_shape=[jax.ShapeDtypeStruct((tt, LANE), jnp.int32), jax.ShapeDtypeStruct((8, LANE), F32)],
        scratch_shapes=[pltpu.VMEM((1, LANE), F32), pltpu.VMEM((1, LANE), F32)],
        compiler_params=_cparams(('arbitrary', 'arbitrary')), name='moe_rank',
    )(route)


def _row_copy(src, dst, sem):
    return pltpu.make_async_copy(src, dst, sem)


def _moe_scatter_kernel(dest_ref, h_ref, init_hbm, xs_hbm, sem, *, bm):
    del init_hbm
    base = pl.program_id(0) * bm

    def issue(r, carry):
        t = base + r
        for kk in range(TOP_K):
            _row_copy(h_ref.at[pl.ds(r, 1)], xs_hbm.at[pl.ds(dest_ref[TOP_K * t + kk], 1)], sem).start()
        return carry
    lax.fori_loop(0, bm, issue, 0)

    def drain(r, carry):
        for kk in range(TOP_K):
            _row_copy(h_ref.at[pl.ds(0, 1)], xs_hbm.at[pl.ds(0, 1)], sem).wait()
        return carry
    lax.fori_loop(0, bm, drain, 0)


def _moe_scatter(dest_flat, h32, ns, bm):
    tt, w = h32.shape
    return pl.pallas_call(
        functools.partial(_moe_scatter_kernel, bm=bm),
        grid_spec=pltpu.PrefetchScalarGridSpec(
            num_scalar_prefetch=1, grid=(tt // bm,),
            in_specs=[pl.BlockSpec((bm, w), lambda i, d: (i, 0)), pl.BlockSpec(memory_space=pl.ANY)],
            out_specs=pl.BlockSpec(memory_space=pl.ANY),
            scratch_shapes=[pltpu.SemaphoreType.DMA(())]),
        out_shape=jax.ShapeDtypeStruct((ns, w), h32.dtype),
        input_output_aliases={2: 0},
        compiler_params=_cparams(('arbitrary',)), name='moe_scatter',
    )(dest_flat, h32, jnp.zeros((ns, w), h32.dtype))


def _moe_up_kernel(te_ref, nv_ref, a_ref, w1_ref, w3_ref, o_ref, w1b, w3b):
    m = pl.program_id(1)

    @pl.when(jnp.logical_or(m == 0, te_ref[m] != te_ref[jnp.maximum(m - 1, 0)]))
    def _():
        w1b[...] = w1_ref[...].astype(BF16)
        w3b[...] = w3_ref[...].astype(BF16)

    @pl.when(m < nv_ref[0])
    def _():
        a = _unpack_halves(a_ref[...])
        h1 = jnp.dot(a, w1b[...], preferred_element_type=F32)
        h3 = jnp.dot(a, w3b[...], preferred_element_type=F32)
        o_ref[...] = (h1 * _sigmoid(h1) * h3).astype(BF16)

    @pl.when(m >= nv_ref[0])
    def _():
        o_ref[...] = jnp.zeros_like(o_ref)


def _moe_down_kernel(te_ref, nv_ref, a_ref, w_ref, o_ref, wb_ref):
    m = pl.program_id(1)

    @pl.when(jnp.logical_or(m == 0, te_ref[m] != te_ref[jnp.maximum(m - 1, 0)]))
    def _():
        wb_ref[...] = w_ref[...].astype(BF16)

    @pl.when(m < nv_ref[0])
    def _():
        o_ref[...] = jnp.dot(a_ref[...], wb_ref[...], preferred_element_type=F32)

    @pl.when(m >= nv_ref[0])
    def _():
        o_ref[...] = jnp.zeros_like(o_ref)


def _moe_grouped_ffn(tile_expert, n_valid, xs, w1, w3, w2, lyr):
    ns = xs.shape[0]
    n_tiles = ns // MOE_BM
    bn = 1024
    ws = pl.BlockSpec((None, None, D_MODEL, bn), lambda j, m, te, nv: (lyr, te[m], 0, j))
    u = pl.pallas_call(
        _moe_up_kernel,
        grid_spec=pltpu.PrefetchScalarGridSpec(
            num_scalar_prefetch=2, grid=(FFN_DIM // bn, n_tiles),
            in_specs=[pl.BlockSpec((MOE_BM, D_MODEL // 2), lambda j, m, te, nv: (m, 0)), ws, ws],
            out_specs=pl.BlockSpec((MOE_BM, bn), lambda j, m, te, nv: (m, j)),
            scratch_shapes=[pltpu.VMEM((D_MODEL, bn), BF16), pltpu.VMEM((D_MODEL, bn), BF16)]),
        out_shape=jax.ShapeDtypeStruct((ns, FFN_DIM), BF16),
        compiler_params=_cparams(('arbitrary', 'arbitrary')), name='moe_up',
    )(tile_expert, n_valid, xs, w1, w3)
    bn = 512
    return pl.pallas_call(
        _moe_down_kernel,
        grid_spec=pltpu.PrefetchScalarGridSpec(
            num_scalar_prefetch=2, grid=(D_MODEL // bn, n_tiles),
            in_specs=[pl.BlockSpec((MOE_BM, FFN_DIM), lambda j, m, te, nv: (m, 0)),
                      pl.BlockSpec((None, None, FFN_DIM, bn), lambda j, m, te, nv: (lyr, te[m], 0, j))],
            out_specs=pl.BlockSpec((MOE_BM, bn), lambda j, m, te, nv: (m, j)),
            scratch_shapes=[pltpu.VMEM((FFN_DIM, bn), BF16)]),
        out_shape=jax.ShapeDtypeStruct((ns, D_MODEL), F32),
        compiler_params=_cparams(('arbitrary', 'arbitrary')), name='moe_down',
    )(tile_expert, n_valid, u, w2)


def _moe_combine_kernel(dest_ref, ys_hbm, r_ref, x_ref, g_ref, m_ref, o_ref, buf, sem, *, bm, kg):
    base = pl.program_id(0) * bm

    def issue(r, carry):
        t = base + r
        for kk in range(TOP_K):
            _row_copy(ys_hbm.at[pl.ds(dest_ref[TOP_K * t + kk], 1)], buf.at[kk, pl.ds(r, 1)], sem).start()
        return carry
    lax.fori_loop(0, bm, issue, 0)

    def drain(r, carry):
        for kk in range(TOP_K):
            _row_copy(ys_hbm.at[pl.ds(0, 1)], buf.at[0, pl.ds(0, 1)], sem).wait()
        return carry
    lax.fori_loop(0, bm, drain, 0)

    r = r_ref[...]
    f = r[:, RT_G1:RT_G1 + 1] * buf[0] + r[:, RT_G2:RT_G2 + 1] * buf[1]
    n = f * lax.rsqrt(jnp.mean(f * f, axis=-1, keepdims=True) + EPS) * g_ref[...]
    o_ref[...] = x_ref[...] + m_ref[0][kg:kg + 1] * n


def _moe_combine(dest_flat, ys, route, x, g, mods, kg, t, bm):
    tt = x.shape[0]
    nlt = t // bm
    xs = pl.BlockSpec((bm, D_MODEL), lambda i, d: (i, 0))
    return pl.pallas_call(
        functools.partial(_moe_combine_kernel, bm=bm, kg=kg),
        grid_spec=pltpu.PrefetchScalarGridSpec(
            num_scalar_prefetch=1, grid=(tt // bm,),
            in_specs=[pl.BlockSpec(memory_space=pl.ANY),
                      pl.BlockSpec((bm, LANE), lambda i, d: (i, 0)), xs,
                      pl.BlockSpec((1, D_MODEL), lambda i, d: (0, 0)),
                      pl.BlockSpec((1, 6, D_MODEL), lambda i, d: (jnp.where(i >= nlt, 1, 0), 0, 0))],
            out_specs=xs,
            scratch_shapes=[pltpu.VMEM((TOP_K, bm, D_MODEL), F32), pltpu.SemaphoreType.DMA(())]),
        out_shape=jax.ShapeDtypeStruct((tt, D_MODEL), F32),
        compiler_params=_cparams(('arbitrary',)), name='moe_combine',
    )(dest_flat, ys, route, x, g.reshape(1, D_MODEL), mods)


def _moe_layer(x, h32, route, w1, w3, w2, lyr, g, mods, kg, t):
    tt = x.shape[0]
    bm = _row_tile(t, tt - t)
    dest, cnt = _moe_rank(route, bm)
    n_tiles = -(-TOP_K * tt // MOE_BM) + N_EXPERTS
    ns = n_tiles * MOE_BM
    cum = jnp.cumsum((cnt[0, 0:N_EXPERTS].astype(jnp.int32) + (MOE_BM - 1)) // MOE_BM)
    n_valid = cum[-1]
    tile = jnp.arange(n_tiles, dtype=jnp.int32)
    tile_expert = jnp.searchsorted(cum, jnp.minimum(tile, n_valid - 1), side='right').astype(jnp.int32)
    dest_flat = dest[:, 0:TOP_K].reshape(-1)
    xs32 = _moe_scatter(dest_flat, h32, ns, bm)
    ys = _moe_grouped_ffn(tile_expert, n_valid.reshape(1), xs32, w1, w3, w2, lyr)
    return _moe_combine(dest_flat, ys, route, x, g, mods, kg, t, bm)


def _rwkv_kernel(cur_ref, prv_ref, nxt_ref, sm_ref, mu_ref, w0_ref, wup_ref, a0_ref, aup_ref,
                 kk_ref, ka_ref, rk_ref, e_ref, tri_ref, y_ref, bonus_ref, s_ref, *, nl, nc):
    d = pl.program_id(0)
    n = pl.program_id(1)
    ch = _chunk_of(d, n, nl, nc)

    @pl.when(n == 0)
    def _():
        s_ref[...] = jnp.zeros_like(s_ref)

    x = cur_ref[...]
    first = jnp.logical_or(ch == 0, ch == nl)
    last = jnp.logical_or(ch == nl - 1, ch == nl + nc - 1)
    p_row = jnp.where(first, 0.0, prv_ref[7:8, :])
    n_row = jnp.where(last, 0.0, nxt_ref[0:1, :])
    row = lax.broadcasted_iota(jnp.int32, (CHUNK, 1), 0)
    prev = jnp.where(row == 0, p_row, pltpu.roll(x, 1, axis=0))
    nxt = jnp.where(row == CHUNK - 1, n_row, pltpu.roll(x, CHUNK - 1, axis=0))
    z = x + mu_ref[0:1, :] * (prev - x) + mu_ref[1:2, :] * (nxt - x)
    r, k, v = z[:, 0:MIX_W], z[:, MIX_W:2 * MIX_W], z[:, 2 * MIX_W:3 * MIX_W]

    sm = sm_ref[...]
    e_bf = e_ref[...]
    wd = jnp.tanh(sm[:, 0:128])
    w_log = -_softplus(-(w0_ref[...] + _mm3(wd, wup_ref[...]))) - 0.5
    logw = -jnp.exp(w_log)
    a = _sigmoid(a0_ref[...] + _mm3(sm[:, 128:256], aup_ref[...]))
    kkr = k * kk_ref[...]
    kk = kkr * lax.rsqrt(jnp.maximum(_mm_xw(kkr * kkr, e_bf), 1e-12))
    k_dir = k * (1.0 + (a - 1.0) * ka_ref[...])
    bonus_ref[...] = _mm_xw(r * k_dir * rk_ref[...], e_bf) * v

    m_strict = tri_ref[0]
    m_incl = tri_ref[1]
    eye = m_incl - m_strict
    b_inc = _mm_wx(m_incl[0:CHUNK, 0:CHUNK].astype(BF16), logw)
    b_exc = b_inc - logw
    b_last = jnp.sum(logw, axis=0, keepdims=True)
    beta = kk * a
    ea = -kk * jnp.exp(b_exc)
    er = r * jnp.exp(b_inc)
    ninv = jnp.exp(-b_inc)
    eb = beta * ninv
    ek = k_dir * ninv
    eend = jnp.exp(b_last - b_inc)
    hb = beta * eend
    hk = k_dir * eend
    gam = jnp.exp(b_last)

    gw = RW_GROUP * RW_HEAD
    lane_head = lax.broadcasted_iota(jnp.int32, (1, gw), 1) >> 6

    def spread(xg):
        return jnp.concatenate([jnp.where(lane_head == h, xg, 0.0) for h in range(RW_GROUP)], axis=0)

    for g in range(RW_HEADS // RW_GROUP):
        sl = slice(g * gw, (g + 1) * gw)
        la, lr = spread(ea[:, sl]), spread(er[:, sl])
        rb, rk = spread(eb[:, sl]), spread(ek[:, sl])
        vb = spread(v[:, sl]).astype(BF16)
        amat = _mm_nt(jnp.concatenate([la, lr], axis=0), jnp.concatenate([rb, rk], axis=0))
        m_ab = amat[0:gw, 0:gw] * m_strict
        m_ak = amat[0:gw, gw:] * m_strict
        n_rb = (amat[gw:, 0:gw] * m_incl).astype(BF16)
        n_rk = (amat[gw:, gw:] * m_incl).astype(BF16)
        mp = m_ab
        tinv = eye + m_ab
        for _ in range(5):
            mp = _mm(mp, mp)
            tinv = tinv + _mm(mp, tinv)
        xs_bf = _mm(tinv, jnp.concatenate([la, _mm(m_ak, vb)], axis=1)).astype(BF16)
        qy = _mm(n_rb, xs_bf)
        q_hat = lr + qy[:, 0:gw]
        y_loc = qy[:, gw:] + _mm(n_rk, vb)
        hb_bd = spread(hb[:, sl]).astype(BF16)
        gbt = _mm_tn(xs_bf, hb_bd)
        g_bot = gbt[gw:] + _mm_tn(vb, spread(hk[:, sl]))
        s0 = s_ref[g]
        y_bd = _mm_nt(q_hat, s0) + y_loc
        y_ref[:, sl] = (y_bd[0:CHUNK] + y_bd[CHUNK:2 * CHUNK]
                        + y_bd[2 * CHUNK:3 * CHUNK] + y_bd[3 * CHUNK:4 * CHUNK])
        s_ref[g] = s0 * gam[:, sl] + _mm(s0, gbt[0:gw]) + g_bot


def _rwkv_order_masks():
    i = jnp.arange(RW_GROUP * RW_HEAD)
    same = (i[:, None] // CHUNK) == (i[None, :] // CHUNK)
    diff = (i[:, None] % CHUNK) - (i[None, :] % CHUNK)
    per_dir = [jnp.stack([same & (sg * diff > 0), same & (sg * diff >= 0)]) for sg in (1, -1)]
    return jnp.stack(per_dir).astype(F32)


def _rwkv_scan(p, t, lc, mu2, w0, wup_pad, a0, aup_pad, k_k, k_a, r_k, e64):
    tt = t + lc
    gw = RW_GROUP * RW_HEAD
    nl, nc = t // CHUNK, lc // CHUNK
    nch = nl + nc
    last8 = tt // 8 - 1
    ch = functools.partial(_chunk_of, nl=nl, nc=nc)
    vec = pl.BlockSpec((1, MIX_W), lambda d, n: (0, 0))
    out_s = pl.BlockSpec((None, CHUNK, MIX_W), lambda d, n: (d, ch(d, n), 0))
    return dict(
        in_specs=[
            pl.BlockSpec((CHUNK, 3 * MIX_W), lambda d, n: (ch(d, n), 0)),
            pl.BlockSpec((8, 3 * MIX_W), lambda d, n: (jnp.maximum(ch(d, n) * 8 - 1, 0), 0)),
            pl.BlockSpec((8, 3 * MIX_W), lambda d, n: (jnp.minimum(ch(d, n) * 8 + 8, last8), 0)),
            pl.BlockSpec((CHUNK, 3 * LANE), lambda d, n: (ch(d, n), CB384_RW_SMALL)),
            pl.BlockSpec((2, 3 * MIX_W), lambda d, n: (0, 0)),
            pl.BlockSpec((None, 1, MIX_W), lambda d, n: (d, 0, 0)),
            pl.BlockSpec((None, LANE, MIX_W), lambda d, n: (d, 0, 0)),
            pl.BlockSpec((None, 1, MIX_W), lambda d, n: (d, 0, 0)),
            pl.BlockSpec((None, LANE, MIX_W), lambda d, n: (d, 0, 0)),
            vec, vec, vec,
            pl.BlockSpec((MIX_W, MIX_W), lambda d, n: (0, 0)),
            pl.BlockSpec((None, 2, gw, gw), lambda d, n: (d, 0, 0, 0)),
        ],
        out_specs=[out_s, out_s],
        out_shape=[jax.ShapeDtypeStruct((2, tt, MIX_W), F32)] * 2,
        scratch_shapes=[pltpu.VMEM((RW_HEADS // RW_GROUP, gw, gw), F32)],
        args=[p, p, p, p, mu2, w0, wup_pad, a0, aup_pad, k_k, k_a, r_k, e64, _rwkv_order_masks()])


def _gla_kernel(*refs):
    d = pl.program_id(0)
    n = pl.program_id(1)
    ad_ref, cos_ref, sin_ref, aup_ref, ab_ref, o_ref, s_ref = refs[8:]

    @pl.when(n == 0)
    def _():
        s_ref[...] = jnp.zeros_like(s_ref)

    qk = jnp.concatenate([r[...] for r in refs[0:4]], axis=1)
    v = jnp.concatenate([r[...] for r in refs[4:8]], axis=1)
    lane = lax.broadcasted_iota(jnp.int32, (CHUNK, 2 * GLA_HEADS * GLA_DK), 1)
    partner = jnp.where((lane & 1) == 0, pltpu.roll(qk, 2 * GLA_HEADS * GLA_DK - 1, axis=1),
                        pltpu.roll(qk, 1, axis=1))
    qk = qk * cos_ref[...] + partner * sin_ref[...]
    hk = GLA_HEADS * GLA_DK
    q = qk[:, 0:hk] * (GLA_DK ** -0.5)
    k = qk[:, hk:]
    g = -_softplus(-(_mm3(ad_ref[...], aup_ref[...]) + ab_ref[...])) / GLA_TAU
    _, incl = _order_masks(d)
    b = _mm_wx(jnp.where(incl, 1.0, 0.0).astype(BF16), g)
    b_last = jnp.sum(g, axis=0, keepdims=True)
    q_e = q * jnp.exp(b)
    k_e = k * jnp.exp(-b)
    k_end = k * jnp.exp(b_last - b)
    dec = jnp.exp(b_last)
    for h in range(GLA_HEADS):
        sk = slice(h * GLA_DK, (h + 1) * GLA_DK)
        sv = slice(h * GLA_DV, (h + 1) * GLA_DV)
        att = jnp.where(incl, _mm_nt(q_e[:, sk], k_e[:, sk]), 0.0)
        st = s_ref[h]
        o_ref[:, sv] = _mm(att, v[:, sv]) + _mm_nt(q_e[:, sk], st)
        s_ref[h] = st * dec[:, sk] + _mm_tn(v[:, sv], k_end[:, sk])


def _gla_scan(p, t, lc, cos_t, sin_t, aup_pad, a_b):
    tt = t + lc
    nl, nc = t // CHUNK, lc // CHUNK
    ch = functools.partial(_chunk_of, nl=nl, nc=nc)
    blk = lambda cb: pl.BlockSpec((CHUNK, MIX_W), lambda d, n: (ch(d, n), cb))
    lanes = lambda cb, cnt: [pl.BlockSpec((CHUNK, LANE), functools.partial(lambda d, n, c: (ch(d, n), c), c=cb + q))
                             for q in range(cnt)]
    return dict(
        in_specs=(lanes(CB128_GLA_QK, 4) + lanes(CB128_GLA_V, 4) + lanes(CB128_GLA_AD, 1)
                  + [blk(0), blk(0),
                     pl.BlockSpec((None, LANE, 2 * LANE), lambda d, n: (d, 0, 0)),
                     pl.BlockSpec((None, 1, 2 * LANE), lambda d, n: (d, 0, 0))]),
        out_specs=[pl.BlockSpec((None, CHUNK, MIX_W), lambda d, n: (d, ch(d, n), 0))],
        out_shape=[jax.ShapeDtypeStruct((2, tt, MIX_W), F32)],
        scratch_shapes=[pltpu.VMEM((GLA_HEADS, GLA_DV, GLA_DK), F32)],
        args=[p] * 9 + [cos_t, sin_t, aup_pad, a_b])


def _scan_kernel(*refs, n_a, n_b, nl, nc):
    outs = refs[n_a + n_b:n_a + n_b + 3]
    _rwkv_kernel(*refs[0:n_a], outs[0], outs[1], refs[-2], nl=nl, nc=nc)
    _gla_kernel(*refs[n_a:n_a + n_b], outs[2], refs[-1])


def _scans(rw, gla, t, lc):
    nl, nc = t // CHUNK, lc // CHUNK
    return pl.pallas_call(
        functools.partial(_scan_kernel, n_a=len(rw['in_specs']), n_b=len(gla['in_specs']), nl=nl, nc=nc),
        grid=(2, nl + nc),
        in_specs=rw['in_specs'] + gla['in_specs'],
        out_specs=rw['out_specs'] + gla['out_specs'],
        out_shape=rw['out_shape'] + gla['out_shape'],
        scratch_shapes=rw['scratch_shapes'] + gla['scratch_shapes'],
        compiler_params=_cparams(('arbitrary', 'arbitrary')), name='scans',
    )(*(rw['args'] + gla['args']))


def _head_norm(y, e_bf, width, eps):
    mu = _mm_xw(y, e_bf) * (1.0 / width)
    dl = y - mu
    var = _mm_xw(dl * dl, e_bf) * (1.0 / width)
    return dl * lax.rsqrt(var + eps)


def _mix_finish_kernel(y0_ref, y1_ref, b0_ref, b1_ref, sm_ref, gup_ref, lng_ref, lnb_ref, e64_ref,
                       o0_ref, o1_ref, gr0_ref, gr1_ref, gr2_ref, gr3_ref, gng_ref, e128_ref, a_ref, b_ref):
    yn = _head_norm(y0_ref[...] + y1_ref[...], e64_ref[...], RW_HEAD, RW_GN_EPS)
    yn = yn * lng_ref[...] + lnb_ref[...] + b0_ref[...] + b1_ref[...]
    gate = _mm(_sigmoid(sm_ref[:, 256:384]), gup_ref[...])
    a_ref[...] = (yn * gate).astype(BF16)
    on = _head_norm(o0_ref[...] + o1_ref[...], e128_ref[...], GLA_DV, GN_EPS) * gng_ref[...]
    gr = jnp.concatenate([gr0_ref[...], gr1_ref[...], gr2_ref[...], gr3_ref[...]], axis=1)
    b_ref[...] = (on * (gr * _sigmoid(gr))).astype(BF16)


def _mix_finish(p, rw_y, rw_bonus, gla_o, g_up, ln_g, ln_b, gn_g, e64, e128, t):
    tt = p.shape[0]
    bm = _row_tile(t, tt - t)
    dblk = lambda d: pl.BlockSpec((None, bm, MIX_W), lambda i: (d, i, 0))
    vec = pl.BlockSpec((1, MIX_W), lambda i: (0, 0))
    mat = pl.BlockSpec((MIX_W, MIX_W), lambda i: (0, 0))
    out = pl.BlockSpec((bm, MIX_W), lambda i: (i, 0))
    return pl.pallas_call(
        _mix_finish_kernel,
        grid=(tt // bm,),
        in_specs=([dblk(0), dblk(1), dblk(0), dblk(1),
                   pl.BlockSpec((bm, 3 * LANE), lambda i: (i, CB384_RW_SMALL)),
                   pl.BlockSpec((RW_GATE_RANK, MIX_W), lambda i: (0, 0)), vec, vec, mat,
                   dblk(0), dblk(1)]
                  + [pl.BlockSpec((bm, LANE), functools.partial(lambda i, c: (i, c), c=CB128_GLA_R + q))
                     for q in range(4)]
                  + [vec, mat]),
        out_specs=[out, out],
        out_shape=[jax.ShapeDtypeStruct((tt, MIX_W), BF16)] * 2,
        compiler_params=_cparams(('parallel',)), name='mix_finish',
    )(rw_y, rw_y, rw_bonus, rw_bonus, p, g_up, ln_g, ln_b, e64, gla_o, gla_o, p, p, p, p, gn_g, e128)


def _sgu_kernel(u_ref, v_ref, lng_ref, lnb_ref, ws_ref, bs_ref, o_ref):
    u = _gelu(u_ref[...])
    v = _gelu(v_ref[...])
    mu = jnp.mean(v, axis=-1, keepdims=True)
    dl = v - mu
    var = jnp.mean(dl * dl, axis=-1, keepdims=True)
    vn = (dl * lax.rsqrt(var + GN_EPS) * lng_ref[...] + lnb_ref[...]).astype(BF16)
    lane = lax.broadcasted_iota(jnp.int32, (1, MIX_W), 1)
    s = bs_ref[...]
    for g in range(SGU_GROUPS):
        s = s + jnp.where((lane >> 6) == g, jnp.dot(ws_ref[g].astype(BF16), vn, preferred_element_type=F32), 0.0)
    o_ref[...] = (u * s).astype(BF16)


def _sgu(p, ln_g, ln_b, w_s, b_full):
    tt = p.shape[0]
    vec = pl.BlockSpec((1, MIX_W), lambda i: (0, 0))
    return pl.pallas_call(
        _sgu_kernel,
        grid=(tt // SGU_CHUNK,),
        in_specs=[pl.BlockSpec((SGU_CHUNK, MIX_W), lambda i: (i, CB_SGU_U)),
                  pl.BlockSpec((SGU_CHUNK, MIX_W), lambda i: (i, CB_SGU_V)), vec, vec,
                  pl.BlockSpec((SGU_GROUPS, SGU_CHUNK, SGU_CHUNK), lambda i: (0, 0, 0)),
                  pl.BlockSpec((SGU_CHUNK, MIX_W), lambda i: (0, 0))],
        out_specs=pl.BlockSpec((SGU_CHUNK, MIX_W), lambda i: (i, 0)),
        out_shape=jax.ShapeDtypeStruct((tt, MIX_W), BF16),
        compiler_params=_cparams(('parallel',)), name='sgu',
    )(p, p, ln_g, ln_b, w_s, b_full)


def _na_bias_kernel(rpb_ref, o_ref, *, wh):
    h = pl.program_id(0)
    n_dc = 2 * NA_WIN_W - 1
    n_dr = 2 * NA_WIN_H - 1
    shape = (GRID_W, 2 * GRID_W)
    c = lax.broadcasted_iota(jnp.int32, shape, 0)
    lane = lax.broadcasted_iota(jnp.int32, shape, 1)
    x = lane & (GRID_W - 1)
    dc = jnp.clip(x - c + (NA_WIN_W - 1), 0, 2 * NA_WIN_W - 2)
    key = (lane >> 6) * n_dc + dc
    cs = jnp.clip(c - NA_WIN_W // 2, 0, GRID_W - NA_WIN_W)
    ok = jnp.logical_and(x >= cs, x < cs + NA_WIN_W)
    pairs = []
    for dr in range(n_dr - 1):
        base = h * (n_dr * n_dc) + dr * n_dc
        tile = lax.fori_loop(0, 2 * n_dc, lambda j, acc: jnp.where(key == j, rpb_ref[base + j], acc),
                             jnp.zeros(shape, F32))
        pairs.append(jnp.where(ok, tile, NEG_INF))
    for dr0 in range(n_dr - wh + 1):
        for jj in range(wh // 2):
            o_ref[dr0, 0, :, jj * 2 * GRID_W:(jj + 1) * 2 * GRID_W] = pairs[dr0 + 2 * jj]


def _na_bias(rpb, wh):
    n_dr0 = 2 * NA_WIN_H - wh
    return pl.pallas_call(
        functools.partial(_na_bias_kernel, wh=wh),
        grid=(NA_HEADS,),
        in_specs=[pl.BlockSpec(memory_space=pltpu.SMEM)],
        out_specs=pl.BlockSpec((n_dr0, 1, GRID_W, wh * GRID_W), lambda h: (0, h, 0, 0)),
        out_shape=jax.ShapeDtypeStruct((n_dr0, NA_HEADS, GRID_W, wh * GRID_W), F32),
        compiler_params=_cparams(('parallel',)), name='na_bias',
    )(rpb.reshape(-1))


def _na_kernel(*refs, wh):
    q_ref = refs[0]
    k_refs = refs[1:1 + wh]
    v_refs = refs[1 + wh:1 + 2 * wh]
    kc_ref, vc_ref, bias_ref, o_ref = refs[1 + 2 * wh:]
    q = q_ref[...] * (NA_HEAD ** -0.5)
    kw = jnp.concatenate([r[...].astype(BF16) for r in k_refs], axis=0)
    vw = jnp.concatenate([r[...].astype(BF16) for r in v_refs], axis=0)
    kc = kc_ref[...].astype(BF16)
    vc = vc_ref[...].astype(BF16)
    lane = lax.broadcasted_iota(jnp.int32, (1, LANE), 1)
    outs = []
    for pr in range(NA_HEADS // 2):
        sl = slice(pr * LANE, (pr + 1) * LANE)
        qp, kwp, vwp, kcp, vcp = q[:, sl], kw[:, sl], vw[:, sl], kc[:, sl], vc[:, sl]
        o = jnp.zeros((GRID_W, LANE), F32)
        for s in range(2):
            hm = (lane >> 6) == s
            qh = jnp.where(hm, qp, 0.0).astype(BF16)
            sw = _mm_nt(qh, kwp) + bias_ref[0, 2 * pr + s]
            sc = _mm_nt(qh, kcp)
            m = jnp.maximum(jnp.max(sw, axis=-1, keepdims=True), jnp.max(sc, axis=-1, keepdims=True))
            ew = jnp.exp(sw - m)
            ec = jnp.exp(sc - m)
            den = jnp.sum(ew, axis=-1, keepdims=True) + jnp.sum(ec, axis=-1, keepdims=True)
            oh = (_mm(ew, vwp) + _mm(ec, vcp)) / den
            o = jnp.where(hm, oh, o)
        outs.append(o)
    o_ref[...] = jnp.concatenate(outs, axis=1).astype(BF16)


def _na_ctx_kernel(q_ref, k_ref, v_ref, o_ref):
    q = q_ref[...] * (NA_HEAD ** -0.5)
    kc = k_ref[...].astype(BF16)
    vc = v_ref[...].astype(BF16)
    lane = lax.broadcasted_iota(jnp.int32, (1, MIX_W), 1)
    o = jnp.zeros(q.shape, F32)
    for h in range(NA_HEADS):
        hm = (lane >> 6) == h
        s = _mm_nt(jnp.where(hm, q, 0.0), kc)
        e = jnp.exp(s - jnp.max(s, axis=-1, keepdims=True))
        o = jnp.where(hm, _mm(e, vc) / jnp.sum(e, axis=-1, keepdims=True), o)
    o_ref[...] = o.astype(BF16)


def _na(p, bias, t, lc):
    rows = t // GRID_W
    wh = min(NA_WIN_H, rows)
    rs = lambda r: jnp.clip(r - wh // 2, 0, rows - wh)
    ctx_blk = t // lc
    kv = lambda cb: [pl.BlockSpec((GRID_W, MIX_W), functools.partial(lambda r, w, cb: (rs(r) + w, cb), w=w, cb=cb))
                     for w in range(wh)]
    lat = pl.pallas_call(
        functools.partial(_na_kernel, wh=wh),
        grid=(rows,),
        in_specs=([pl.BlockSpec((GRID_W, MIX_W), lambda r: (r, CB_NA_Q))] + kv(CB_NA_K) + kv(CB_NA_V)
                  + [pl.BlockSpec((lc, MIX_W), lambda r: (ctx_blk, CB_NA_K)),
                     pl.BlockSpec((lc, MIX_W), lambda r: (ctx_blk, CB_NA_V)),
                     pl.BlockSpec((1, NA_HEADS, GRID_W, wh * GRID_W),
                                  lambda r: (rs(r) - r + (NA_WIN_H - 1), 0, 0, 0))]),
        out_specs=pl.BlockSpec((GRID_W, MIX_W), lambda r: (r, 0)),
        out_shape=jax.ShapeDtypeStruct((t, MIX_W), BF16),
        compiler_params=_cparams(('parallel',)), name='na_latent',
    )(*([p] * (1 + 2 * wh + 2)), bias)
    cblk = lambda cb: pl.BlockSpec((lc, MIX_W), lambda i: (ctx_blk, cb))
    ctx = pl.pallas_call(
        _na_ctx_kernel,
        grid=(1,),
        in_specs=[cblk(CB_NA_Q), cblk(CB_NA_K), cblk(CB_NA_V)],
        out_specs=pl.BlockSpec((lc, MIX_W), lambda i: (0, 0)),
        out_shape=jax.ShapeDtypeStruct((lc, MIX_W), BF16),
        compiler_params=_cparams(('arbitrary',)), name='na_ctx',
    )(p, p, p)
    return jnp.concatenate([lat, ctx], axis=0)


def _merge_kernel(a0, a1, a2, a3, g0, g1, g2, g3, w_ref, o_ref, wb_ref):
    @pl.when(pl.program_id(1) == 0)
    def _():
        wb_ref[...] = w_ref[...].astype(BF16)
    acc = None
    for n, (a_ref, g_ref) in enumerate(((a0, g0), (a1, g1), (a2, g2), (a3, g3))):
        zn = jnp.dot(a_ref[...], wb_ref[n], preferred_element_type=F32) * _sigmoid(g_ref[...])
        acc = zn if acc is None else acc + zn
    o_ref[...] = acc.astype(BF16)


def _merge(ys, p, w_br, lyr):
    tt = p.shape[0]
    bn = 512
    bm = _pick(tt, (768, 384, 128))
    a_s = pl.BlockSpec((bm, MIX_W), lambda j, i: (i, 0))
    gs = [pl.BlockSpec((bm, bn), functools.partial(lambda j, i, n: (i, (GATE_OFF + n * D_MODEL) // bn + j), n=n))
          for n in range(N_BRANCH)]
    return pl.pallas_call(
        _merge_kernel,
        grid=(D_MODEL // bn, tt // bm),
        in_specs=[a_s] * 4 + gs + [pl.BlockSpec((None, N_BRANCH, MIX_W, bn), lambda j, i: (lyr, 0, 0, j))],
        out_specs=pl.BlockSpec((bm, bn), lambda j, i: (i, j)),
        out_shape=jax.ShapeDtypeStruct((tt, D_MODEL), BF16),
        scratch_shapes=[pltpu.VMEM((N_BRANCH, MIX_W, bn), BF16)],
        compiler_params=_cparams(('arbitrary', 'arbitrary')), name='merge',
    )(*ys, p, p, p, p, w_br)


def _block_diag_ones(width):
    i = jnp.arange(MIX_W) // width
    return (i[:, None] == i[None, :]).astype(BF16)


def _rope_tables(t, lc):
    tok = jnp.arange(t)
    pos = jnp.stack([tok // GRID_W, tok % GRID_W], axis=-1).astype(F32)
    nf = GLA_DK // 4
    inv = ROPE_BASE ** (-jnp.arange(nf, dtype=F32) / nf)
    ang = pos[:, :, None] * inv
    cos = jnp.repeat(jnp.cos(ang), 2, axis=-1).reshape(t, GLA_DK)
    sin = jnp.sin(ang)
    sin = jnp.stack([-sin, sin], axis=-1).reshape(t, GLA_DK)
    reps = 2 * GLA_HEADS
    cos = jnp.concatenate([jnp.tile(cos, (1, reps)), jnp.ones((lc, reps * GLA_DK), F32)], axis=0)
    sin = jnp.concatenate([jnp.tile(sin, (1, reps)), jnp.zeros((lc, reps * GLA_DK), F32)], axis=0)
    return cos, sin


def _pad_rank_rows(w_up, rank, rows):
    out = jnp.zeros((2, rows, w_up.shape[-1]), w_up.dtype)
    for d in range(2):
        out = out.at[d, d * rank:(d + 1) * rank].set(w_up[d])
    return out


def kernel(x, c, ctx, c_ctx, ada_w, ada_b, norm_g, w_in, rw_mu, rw_w0, rw_w_up, rw_a0, rw_a_up, rw_g_up, rw_k_k, rw_k_a, rw_r_k, rw_ln_g, rw_ln_b, gla_a_up, gla_a_b, gla_gn_g, sgu_ln_g, sgu_ln_b, sgu_w, sgu_b, na_rpb, w_br, w_o, ffn_w1, ffn_w3, ffn_w2, moe_router, moe_w1, moe_w3, moe_w2):
    assert x.shape[0] == 1 and x.shape[2] == D_MODEL
    t, lc = x.shape[1], ctx.shape[1]
    depth = ada_w.shape[0]
    assert t % max(lc, SGU_CHUNK) == 0 and lc % SGU_CHUNK == 0 and t % GRID_W == 0
    rows = t // GRID_W
    assert rows >= NA_WIN_H
    wh = NA_WIN_H
    xs = jnp.concatenate([x[0], ctx[0]], axis=0)
    cond8 = jnp.zeros((8, D_MODEL), F32).at[0].set(c[0]).at[1].set(c_ctx)
    e64 = _block_diag_ones(RW_HEAD)
    e128 = _block_diag_ones(GLA_DV)
    cos_t, sin_t = _rope_tables(t, lc)
    row1 = lambda v: v.reshape(1, -1)
    assert w_in.shape[2] == N_IN

    mods_all = [_adaln(cond8, ada_w, ada_b, i)[0:2].reshape(2, 6, D_MODEL) for i in range(depth)]
    h = _norm_mod_call(xs, norm_g[0, 0], mods_all[0], 0, t)
    for i in range(depth):
        mods = mods_all[i]
        last = i == depth - 1
        p = _in_proj(h, w_in, i)
        mu2 = jnp.stack([rw_mu[i, :, 0].reshape(-1), rw_mu[i, :, 1].reshape(-1)])
        rw_call = _rwkv_scan(
            p, t, lc, mu2, rw_w0[i].reshape(2, 1, MIX_W), _pad_rank_rows(rw_w_up[i], RW_DECAY_RANK, LANE),
            rw_a0[i].reshape(2, 1, MIX_W), _pad_rank_rows(rw_a_up[i], RW_ICLR_RANK, LANE),
            row1(rw_k_k[i]), row1(rw_k_a[i]), row1(rw_r_k[i]), e64)
        gla_call = _gla_scan(p, t, lc, cos_t, sin_t, _pad_rank_rows(gla_a_up[i], GLA_GATE_RANK, LANE),
                             gla_a_b[i].reshape(2, 1, 2 * LANE))
        rw_y, rw_bonus, gla_o = _scans(rw_call, gla_call, t, lc)
        y_a, y_b = _mix_finish(p, rw_y, rw_bonus, gla_o, rw_g_up[i], row1(rw_ln_g[i]), row1(rw_ln_b[i]),
                               row1(gla_gn_g[i]), e64, e128, t)
        y_s = _sgu(p, row1(sgu_ln_g[i]), row1(sgu_ln_b[i]), sgu_w[i], jnp.repeat(sgu_b[i].T, 64, axis=1))
        y_d = _na(p, _na_bias(na_rpb[i], wh), t, lc)
        z = _merge((y_a, y_b, y_s, y_d), p, w_br, i)
        y = _matmul(z, w_o, i, 1024, (768, 384, 128), name='out_proj')
        j = i // 2
        if i % 2 == 0:
            xs, h = _resid_next_call(xs, y, norm_g[i, 1], mods, 2, t, norm_g[i, 2], mods, 3)
            f = _ffn_down(_ffn_up(h, ffn_w1, ffn_w3, j), ffn_w2, j)
            if last:
                xs = _resid_call(xs, f, norm_g[i, 3], mods, 5, t)
            else:
                xs, h = _resid_next_call(xs, f, norm_g[i, 3], mods, 5, t, norm_g[i + 1, 0], mods_all[i + 1], 0)
        else:
            xs, h32, route = _resid_next_call(xs, y, norm_g[i, 1], mods, 2, t, norm_g[i, 2], mods, 3,
                                              router=moe_router[j])
            xs = _moe_layer(xs, h32, route, moe_w1, moe_w3, moe_w2, j, norm_g[i, 3], mods, 5, t)
            if not last:
                h = _norm_mod_call(xs, norm_g[i + 1, 0], mods_all[i + 1], 0, t)
    return xs[0:t][None]
```

```python
import functools

import jax
import jax.numpy as jnp
from jax import lax
from jax.experimental import pallas as pl
from jax.experimental.pallas import tpu as pltpu

F32 = jnp.float32
BF16 = jnp.bfloat16

D_MODEL = 2048
GRID_W = 64
N_BRANCH = 4
MIX_W = D_MODEL // 4
RW_HEAD = 64
RW_HEADS = MIX_W // RW_HEAD
RW_GROUP = 4
RW_DECAY_RANK = 64
RW_ICLR_RANK = 64
RW_GATE_RANK = 128
RW_GN_EPS = 64e-5
GLA_HEADS = 4
GLA_DV = MIX_W // GLA_HEADS
GLA_DK = GLA_DV // 2
GLA_GATE_RANK = 16
GLA_TAU = 16.0
GN_EPS = 1e-5
ROPE_BASE = 10000.0
SGU_GROUPS = MIX_W // 64
SGU_CHUNK = 128
NA_HEAD = 64
NA_HEADS = MIX_W // NA_HEAD
NA_WIN_H = 8
NA_WIN_W = 16
FFN_DIM = 7 * D_MODEL // 2
N_EXPERTS = 8
TOP_K = 2
EPS = 1e-6
NEG_INF = -1e30

RT_G1, RT_G2, RT_I1, RT_I2 = 8, 9, 10, 11
MOE_BM = 256
CHUNK = 64
LANE = 128
VMEM_LIMIT = 56 * 2 ** 20

IN_TAIL = 3 * MIX_W + 2 * RW_DECAY_RANK + 2 * RW_ICLR_RANK + RW_GATE_RANK + 2 * GLA_HEADS * GLA_DK \
    + 2 * MIX_W + 2 * GLA_GATE_RANK
IN_PAD = -IN_TAIL % LANE
N_IN = IN_TAIL + 5 * MIX_W + N_BRANCH * D_MODEL
N_IN_P = N_IN + IN_PAD
CB384_RW_SMALL = 4
CB128_GLA_QK, CB128_GLA_V, CB128_GLA_R, CB128_GLA_AD = 15, 19, 23, 27
CB_SGU_U, CB_SGU_V, CB_NA_Q, CB_NA_K, CB_NA_V = 7, 8, 9, 10, 11
GATE_OFF = 6144


def _cparams(sem):
    return pltpu.CompilerParams(dimension_semantics=sem, vmem_limit_bytes=VMEM_LIMIT)


def _pick(m, cands):
    for c in cands:
        if m % c == 0:
            return c
    raise ValueError(f'no tile for {m}')


def _mm(a, b):
    return jnp.dot(a.astype(BF16), b.astype(BF16), preferred_element_type=F32)


def _mm_nt(a, b):
    return lax.dot_general(a.astype(BF16), b.astype(BF16), (((1,), (1,)), ((), ())),
                           preferred_element_type=F32)


def _mm_tn(a, b):
    return lax.dot_general(a.astype(BF16), b.astype(BF16), (((0,), (0,)), ((), ())),
                           preferred_element_type=F32)


def _split2(x):
    hi = x.astype(BF16)
    lo = (x - hi.astype(F32)).astype(BF16)
    return hi, lo


def _mm_xw(x, w_bf):
    hi, lo = _split2(x)
    return (jnp.dot(hi, w_bf, preferred_element_type=F32)
            + jnp.dot(lo, w_bf, preferred_element_type=F32))


def _mm_wx(w_bf, x):
    hi, lo = _split2(x)
    return (jnp.dot(w_bf, hi, preferred_element_type=F32)
            + jnp.dot(w_bf, lo, preferred_element_type=F32))


def _mm3(a, b):
    ah, al = _split2(a)
    bh, bl = _split2(b)
    return (jnp.dot(ah, bh, preferred_element_type=F32)
            + jnp.dot(ah, bl, preferred_element_type=F32)
            + jnp.dot(al, bh, preferred_element_type=F32))


def _sigmoid(x):
    return 1.0 / (1.0 + jnp.exp(-x))


def _softplus(x):
    return jnp.maximum(x, 0.0) + jnp.log1p(jnp.exp(-jnp.abs(x)))


def _gelu(x):
    return 0.5 * x * (1.0 + lax.erf(x * (0.5 ** 0.5)))


def _order_masks(d):
    t = lax.broadcasted_iota(jnp.int32, (CHUNK, CHUNK), 0)
    s = lax.broadcasted_iota(jnp.int32, (CHUNK, CHUNK), 1)
    diff = (t - s) * jnp.where(d == 0, 1, -1)
    return diff > 0, diff >= 0


def _chunk_of(d, n, nl, nc):
    fwd = jnp.where(n < nc, nl + n, n - nc)
    bwd = jnp.where(n < nc, nl + nc - 1 - n, nl - 1 - (n - nc))
    return jnp.where(d == 0, fwd, bwd)


def _ada_kernel(c_ref, w_ref, b_ref, o_ref):
    cnd = c_ref[...]
    a = cnd * _sigmoid(cnd)
    o_ref[...] = _mm(a, w_ref[...]) + b_ref[...]


def _adaln(cond8, ada_w, ada_b, lyr):
    depth, _, n = ada_w.shape
    bn = 1536
    return pl.pallas_call(
        _ada_kernel,
        grid=(n // bn,),
        in_specs=[pl.BlockSpec((8, D_MODEL), lambda j: (0, 0)),
                  pl.BlockSpec((None, D_MODEL, bn), lambda j: (lyr, 0, j)),
                  pl.BlockSpec((None, 1, bn), lambda j: (lyr, 0, j))],
        out_specs=pl.BlockSpec((8, bn), lambda j: (0, j)),
        out_shape=jax.ShapeDtypeStruct((8, n), F32),
        compiler_params=_cparams(('arbitrary',)),
        name='adaln',
    )(cond8, ada_w, ada_b.reshape(depth, 1, n))


def _norm_mod(x, g, m, k0):
    y = x * lax.rsqrt(jnp.mean(x * x, axis=-1, keepdims=True) + EPS) * g
    return y * (1.0 + m[k0 + 1:k0 + 2]) + m[k0:k0 + 1]


def _pack_halves(h):
    n = h.shape[1] // 2
    bits = lax.bitcast_convert_type(h.astype(BF16).astype(F32), jnp.uint32)
    return (bits[:, 0:n] >> 16) | bits[:, n:]


def _unpack_halves(w):
    lo = lax.bitcast_convert_type(w << 16, F32)
    hi = lax.bitcast_convert_type(w & jnp.uint32(0xFFFF0000), F32)
    return jnp.concatenate([lo, hi], axis=1).astype(BF16)


def _norm_mod_kernel(x_ref, g_ref, m_ref, o_ref, *, k0):
    o_ref[...] = _norm_mod(x_ref[...], g_ref[...], m_ref[0], k0).astype(BF16)


def _route_table(h, router):
    logits = _mm3(h, router)
    lane = lax.broadcasted_iota(jnp.int32, logits.shape, 1)
    neg = jnp.float32(-jnp.inf)
    l1 = jnp.where(lane < N_EXPERTS, logits, neg)
    m1 = jnp.max(l1, axis=-1, keepdims=True)
    i1 = jnp.min(jnp.where(l1 == m1, lane, LANE), axis=-1, keepdims=True)
    l2 = jnp.where(lane == i1, neg, l1)
    m2 = jnp.max(l2, axis=-1, keepdims=True)
    i2 = jnp.min(jnp.where(l2 == m2, lane, LANE), axis=-1, keepdims=True)
    e2 = jnp.exp(m2 - m1)
    den = 1.0 + e2
    sel = jnp.where(jnp.logical_or(lane == i1, lane == i2), 1.0, 0.0)
    sel = jnp.where(lane == RT_G1, 1.0 / den, jnp.where(lane == RT_G2, e2 / den, sel))
    return jnp.where(lane == RT_I1, i1.astype(F32), jnp.where(lane == RT_I2, i2.astype(F32), sel))


def _row_tile(t, lc):
    return _pick(lc, (256, 128))


def _norm_mod_call(x, g, mods, k0, t):
    tt = x.shape[0]
    bm = _row_tile(t, tt - t)
    nlt = t // bm
    xs = pl.BlockSpec((bm, D_MODEL), lambda i: (i, 0))
    gs = pl.BlockSpec((1, D_MODEL), lambda i: (0, 0))
    ms = pl.BlockSpec((1, 6, D_MODEL), lambda i: (jnp.where(i >= nlt, 1, 0), 0, 0))
    return pl.pallas_call(
        functools.partial(_norm_mod_kernel, k0=k0),
        grid=(tt // bm,), in_specs=[xs, gs, ms], out_specs=xs,
        out_shape=jax.ShapeDtypeStruct((tt, D_MODEL), BF16),
        compiler_params=_cparams(('parallel',)), name='norm_mod',
    )(x, g.reshape(1, D_MODEL), mods)


def _resid_kernel(x_ref, y_ref, g_ref, m_ref, o_ref, *, kg):
    y = y_ref[...]
    n = y * lax.rsqrt(jnp.mean(y * y, axis=-1, keepdims=True) + EPS) * g_ref[...]
    o_ref[...] = x_ref[...] + m_ref[0][kg:kg + 1] * n


def _resid_call(x, y, g, mods, kg, t):
    tt = x.shape[0]
    bm = _row_tile(t, tt - t)
    nlt = t // bm
    xs = pl.BlockSpec((bm, D_MODEL), lambda i: (i, 0))
    return pl.pallas_call(
        functools.partial(_resid_kernel, kg=kg),
        grid=(tt // bm,),
        in_specs=[xs, xs, pl.BlockSpec((1, D_MODEL), lambda i: (0, 0)),
                  pl.BlockSpec((1, 6, D_MODEL), lambda i: (jnp.where(i >= nlt, 1, 0), 0, 0))],
        out_specs=xs,
        out_shape=jax.ShapeDtypeStruct((tt, D_MODEL), F32),
        compiler_params=_cparams(('parallel',)), name='resid_norm',
    )(x, y, g.reshape(1, D_MODEL), mods)


def _resid_next_kernel(*refs, kg, k0, route):
    x_ref, y_ref, g_ref, m_ref, g2_ref, m2_ref = refs[0:6]
    y = y_ref[...]
    n = y * lax.rsqrt(jnp.mean(y * y, axis=-1, keepdims=True) + EPS) * g_ref[...]
    xn = x_ref[...] + m_ref[0][kg:kg + 1] * n
    h = _norm_mod(xn, g2_ref[...], m2_ref[0], k0)
    if route:
        r_ref, xo_ref, ho_ref, rt_ref = refs[6:]
        ho_ref[...] = _pack_halves(h)
        rt_ref[...] = _route_table(h, r_ref[...])
    else:
        xo_ref, ho_ref = refs[6:]
        ho_ref[...] = h.astype(BF16)
    xo_ref[...] = xn


def _resid_next_call(x, y, g, mods, kg, t, g2, mods2, k0, router=None):
    tt = x.shape[0]
    bm = _row_tile(t, tt - t)
    nlt = t // bm
    xs = pl.BlockSpec((bm, D_MODEL), lambda i: (i, 0))
    gs = pl.BlockSpec((1, D_MODEL), lambda i: (0, 0))
    ms = pl.BlockSpec((1, 6, D_MODEL), lambda i: (jnp.where(i >= nlt, 1, 0), 0, 0))
    in_specs = [xs, xs, gs, ms, gs, ms]
    args = [x, y, g.reshape(1, D_MODEL), mods, g2.reshape(1, D_MODEL), mods2]
    out_specs = [xs, xs]
    out_shape = [jax.ShapeDtypeStruct((tt, D_MODEL), F32), jax.ShapeDtypeStruct((tt, D_MODEL), BF16)]
    if router is not None:
        in_specs.append(pl.BlockSpec((D_MODEL, LANE), lambda i: (0, 0)))
        args.append(jnp.pad(router, ((0, 0), (0, LANE - N_EXPERTS))))
        out_specs = [xs, pl.BlockSpec((bm, D_MODEL // 2), lambda i: (i, 0)), pl.BlockSpec((bm, LANE), lambda i: (i, 0))]
        out_shape = [out_shape[0], jax.ShapeDtypeStruct((tt, D_MODEL // 2), jnp.uint32),
                     jax.ShapeDtypeStruct((tt, LANE), F32)]
    return pl.pallas_call(
        functools.partial(_resid_next_kernel, kg=kg, k0=k0, route=router is not None),
        grid=(tt // bm,), in_specs=in_specs, out_specs=out_specs, out_shape=out_shape,
        compiler_params=_cparams(('parallel',)), name='resid_next',
    )(*args)


def _mm_kernel(a_ref, w_ref, o_ref, wb_ref):
    @pl.when(pl.program_id(1) == 0)
    def _():
        wb_ref[...] = w_ref[...].astype(BF16)
    o_ref[...] = jnp.dot(a_ref[...], wb_ref[...], preferred_element_type=F32).astype(o_ref.dtype)


def _matmul(a, w, lyr, bn, bm_cands, out_dtype=F32, name='matmul'):
    m, k = a.shape
    n = w.shape[2]
    bm = _pick(m, bm_cands)
    return pl.pallas_call(
        _mm_kernel,
        grid=(n // bn, m // bm),
        in_specs=[pl.BlockSpec((bm, k), lambda j, i: (i, 0)),
                  pl.BlockSpec((None, k, bn), lambda j, i: (lyr, 0, j))],
        out_specs=pl.BlockSpec((bm, bn), lambda j, i: (i, j)),
        out_shape=jax.ShapeDtypeStruct((m, n), out_dtype),
        scratch_shapes=[pltpu.VMEM((k, bn), BF16)],
        compiler_params=_cparams(('arbitrary', 'arbitrary')), name=name,
    )(a, w)


def _in_proj_kernel(a_ref, wt_ref, o_ref, wb_ref, *, bn):
    j = pl.program_id(0)
    straddle = IN_TAIL // bn
    cut = IN_TAIL - straddle * bn

    @pl.when(pl.program_id(1) == 0)
    def _():
        @pl.when(j != straddle)
        def _():
            wb_ref[...] = wt_ref[0].astype(BF16)

        @pl.when(j == straddle)
        def _():
            wb_ref[0:cut + IN_PAD, :] = wt_ref[0, 0:cut + IN_PAD, :].astype(BF16)
            wb_ref[cut + IN_PAD:, :] = wt_ref[0, cut:bn - IN_PAD, :].astype(BF16)

    o_ref[...] = lax.dot_general(a_ref[...], wb_ref[...], (((1,), (1,)), ((), ())), preferred_element_type=F32)


def _in_proj(a, w_in, lyr):
    m = a.shape[0]
    bn = 1024
    bm = _pick(m, (768, 384, 128))
    straddle = IN_TAIL // bn
    assert IN_PAD % 8 == 0 and straddle == (IN_TAIL + IN_PAD - 1) // bn
    w_t = jnp.swapaxes(w_in, 1, 2)
    return pl.pallas_call(
        functools.partial(_in_proj_kernel, bn=bn),
        grid=(N_IN_P // bn, m // bm),
        in_specs=[pl.BlockSpec((bm, D_MODEL), lambda j, i: (i, 0)),
                  pl.BlockSpec((pl.Element(1), pl.Element(bn), pl.Element(D_MODEL)),
                               lambda j, i: (lyr, 8 * (j * (bn // 8) - jnp.where(j > straddle, IN_PAD // 8, 0)), 0))],
        out_specs=pl.BlockSpec((bm, bn), lambda j, i: (i, j)),
        out_shape=jax.ShapeDtypeStruct((m, N_IN_P), F32),
        scratch_shapes=[pltpu.VMEM((bn, D_MODEL), BF16)],
        compiler_params=_cparams(('arbitrary', 'arbitrary')), name='in_proj',
    )(a, w_t)


def _ffn_up_kernel(a_ref, w1_ref, w3_ref, o_ref, w1b, w3b):
    @pl.when(pl.program_id(1) == 0)
    def _():
        w1b[...] = w1_ref[...].astype(BF16)
        w3b[...] = w3_ref[...].astype(BF16)
    a = a_ref[...]
    h1 = jnp.dot(a, w1b[...], preferred_element_type=F32)
    h3 = jnp.dot(a, w3b[...], preferred_element_type=F32)
    o_ref[...] = (h1 * _sigmoid(h1) * h3).astype(BF16)


def _ffn_up(h, w1, w3, e):
    m = h.shape[0]
    bn = 512
    bm = _pick(m, (768, 384, 128))
    ws = pl.BlockSpec((None, D_MODEL, bn), lambda j, i: (e, 0, j))
    return pl.pallas_call(
        _ffn_up_kernel,
        grid=(FFN_DIM // bn, m // bm),
        in_specs=[pl.BlockSpec((bm, D_MODEL), lambda j, i: (i, 0)), ws, ws],
        out_specs=pl.BlockSpec((bm, bn), lambda j, i: (i, j)),
        out_shape=jax.ShapeDtypeStruct((m, FFN_DIM), BF16),
        scratch_shapes=[pltpu.VMEM((D_MODEL, bn), BF16), pltpu.VMEM((D_MODEL, bn), BF16)],
        compiler_params=_cparams(('arbitrary', 'arbitrary')), name='ffn_up',
    )(h, w1, w3)


def _ffn_down_kernel(a_ref, w_ref, o_ref):
    part = jnp.dot(a_ref[...], w_ref[...].astype(BF16), preferred_element_type=F32)

    @pl.when(pl.program_id(1) == 0)
    def _():
        o_ref[...] = part

    @pl.when(pl.program_id(1) > 0)
    def _():
        o_ref[...] += part


def _ffn_down(u, w2, e):
    m = u.shape[0]
    bk = 512
    bm = max(d for d in range(LANE, 1408 + 1, LANE) if m % d == 0)
    return pl.pallas_call(
        _ffn_down_kernel,
        grid=(m // bm, FFN_DIM // bk),
        in_specs=[pl.BlockSpec((bm, bk), lambda i, k: (i, k)),
                  pl.BlockSpec((None, bk, D_MODEL), lambda i, k: (e, k, 0))],
        out_specs=pl.BlockSpec((bm, D_MODEL), lambda i, k: (i, 0)),
        out_shape=jax.ShapeDtypeStruct((m, D_MODEL), F32),
        compiler_params=_cparams(('parallel', 'arbitrary')), name='ffn_down',
    )(u, w2)


def _moe_rank_kernel(r_ref, dest_ref, cnt_ref, carry_ref, tot_ref):
    ph = pl.program_id(0)
    i = pl.program_id(1)
    r = r_ref[...]
    bm = r.shape[0]
    lane = lax.broadcasted_iota(jnp.int32, (1, LANE), 1)
    oh = jnp.where(lane < N_EXPERTS, r, 0.0)
    colsum = jnp.sum(oh, axis=0, keepdims=True)

    @pl.when(jnp.logical_and(ph == 0, i == 0))
    def _():
        carry_ref[...] = jnp.zeros_like(carry_ref)

    @pl.when(jnp.logical_and(ph == 1, i == 0))
    def _():
        tot_ref[...] = carry_ref[...]
        carry_ref[...] = jnp.zeros_like(carry_ref)

    @pl.when(ph == 1)
    def _():
        tot = tot_ref[...]
        padded = jnp.floor((tot + (MOE_BM - 1)) * (1.0 / MOE_BM)) * MOE_BM
        a = lax.broadcasted_iota(jnp.int32, (LANE, LANE), 0)
        b = lax.broadcasted_iota(jnp.int32, (LANE, LANE), 1)
        upper = jnp.where(a < b, 1.0, 0.0).astype(BF16)
        offs = _mm_xw(jnp.broadcast_to(padded, (8, LANE)), upper)[0:1]
        tr = lax.broadcasted_iota(jnp.int32, (bm, bm), 0)
        ts = lax.broadcasted_iota(jnp.int32, (bm, bm), 1)
        before = jnp.dot(jnp.where(ts < tr, 1.0, 0.0).astype(BF16), oh.astype(BF16),
                         preferred_element_type=F32)
        slot = before + carry_ref[...] + offs
        lane_f = lane.astype(F32)
        d1 = jnp.sum(jnp.where(lane_f == r[:, RT_I1:RT_I1 + 1], slot, 0.0), axis=-1, keepdims=True)
        d2 = jnp.sum(jnp.where(lane_f == r[:, RT_I2:RT_I2 + 1], slot, 0.0), axis=-1, keepdims=True)
        dest_ref[...] = jnp.where(lane == 0, d1, jnp.where(lane == 1, d2, 0.0)).astype(jnp.int32)
        cnt_ref[...] = jnp.broadcast_to(tot, (8, LANE))

    carry_ref[...] = carry_ref[...] + colsum


def _moe_rank(route, bm):
    tt = route.shape[0]
    return pl.pallas_call(
        _moe_rank_kernel,
        grid=(2, tt // bm),
        in_specs=[pl.BlockSpec((bm, LANE), lambda ph, i: (i, 0))],
        out_specs=[pl.BlockSpec((bm, LANE), lambda ph, i: (i * ph, 0)),
                   pl.BlockSpec((8, LANE), lambda ph, i: (0, 0))],
        out_shape=[jax.ShapeDtypeStruct((tt, LANE), jnp.int32), jax.ShapeDtypeStruct((8, LANE), F32)],
        scratch_shapes=[pltpu.VMEM((1, LANE), F32), pltpu.VMEM((1, LANE), F32)],
        compiler_params=_cparams(('arbitrary', 'arbitrary')), name='moe_rank',
    )(route)


def _row_copy(src, dst, sem):
    return pltpu.make_async_copy(src, dst, sem)


def _moe_scatter_kernel(dest_ref, h_ref, init_hbm, xs_hbm, sem, *, bm):
    del init_hbm
    base = pl.program_id(0) * bm

    def issue(r, carry):
        t = base + r
        for kk in range(TOP_K):
            _row_copy(h_ref.at[pl.ds(r, 1)], xs_hbm.at[pl.ds(dest_ref[TOP_K * t + kk], 1)], sem).start()
        return carry
    lax.fori_loop(0, bm, issue, 0)

    def drain(r, carry):
        for kk in range(TOP_K):
            _row_copy(h_ref.at[pl.ds(0, 1)], xs_hbm.at[pl.ds(0, 1)], sem).wait()
        return carry
    lax.fori_loop(0, bm, drain, 0)


def _moe_scatter(dest_flat, h32, ns, bm):
    tt, w = h32.shape
    return pl.pallas_call(
        functools.partial(_moe_scatter_kernel, bm=bm),
        grid_spec=pltpu.PrefetchScalarGridSpec(
            num_scalar_prefetch=1, grid=(tt // bm,),
            in_specs=[pl.BlockSpec((bm, w), lambda i, d: (i, 0)), pl.BlockSpec(memory_space=pl.ANY)],
            out_specs=pl.BlockSpec(memory_space=pl.ANY),
            scratch_shapes=[pltpu.SemaphoreType.DMA(())]),
        out_shape=jax.ShapeDtypeStruct((ns, w), h32.dtype),
        input_output_aliases={2: 0},
        compiler_params=_cparams(('arbitrary',)), name='moe_scatter',
    )(dest_flat, h32, jnp.zeros((ns, w), h32.dtype))


def _moe_up_kernel(te_ref, nv_ref, a_ref, w1_ref, w3_ref, o_ref, w1b, w3b):
    m = pl.program_id(1)

    @pl.when(jnp.logical_or(m == 0, te_ref[m] != te_ref[jnp.maximum(m - 1, 0)]))
    def _():
        w1b[...] = w1_ref[...].astype(BF16)
        w3b[...] = w3_ref[...].astype(BF16)

    @pl.when(m < nv_ref[0])
    def _():
        a = _unpack_halves(a_ref[...])
        h1 = jnp.dot(a, w1b[...], preferred_element_type=F32)
        h3 = jnp.dot(a, w3b[...], preferred_element_type=F32)
        o_ref[...] = (h1 * _sigmoid(h1) * h3).astype(BF16)

    @pl.when(m >= nv_ref[0])
    def _():
        o_ref[...] = jnp.zeros_like(o_ref)


def _moe_down_kernel(te_ref, nv_ref, a_ref, w_ref, o_ref, wb_ref):
    m = pl.program_id(1)

    @pl.when(jnp.logical_or(m == 0, te_ref[m] != te_ref[jnp.maximum(m - 1, 0)]))
    def _():
        wb_ref[...] = w_ref[...].astype(BF16)

    @pl.when(m < nv_ref[0])
    def _():
        o_ref[...] = jnp.dot(a_ref[...], wb_ref[...], preferred_element_type=F32)

    @pl.when(m >= nv_ref[0])
    def _():
        o_ref[...] = jnp.zeros_like(o_ref)


def _moe_grouped_ffn(tile_expert, n_valid, xs, w1, w3, w2, lyr):
    ns = xs.shape[0]
    n_tiles = ns // MOE_BM
    bn = 1024
    ws = pl.BlockSpec((None, None, D_MODEL, bn), lambda j, m, te, nv: (lyr, te[m], 0, j))
    u = pl.pallas_call(
        _moe_up_kernel,
        grid_spec=pltpu.PrefetchScalarGridSpec(
            num_scalar_prefetch=2, grid=(FFN_DIM // bn, n_tiles),
            in_specs=[pl.BlockSpec((MOE_BM, D_MODEL // 2), lambda j, m, te, nv: (m, 0)), ws, ws],
            out_specs=pl.BlockSpec((MOE_BM, bn), lambda j, m, te, nv: (m, j)),
            scratch_shapes=[pltpu.VMEM((D_MODEL, bn), BF16), pltpu.VMEM((D_MODEL, bn), BF16)]),
        out_shape=jax.ShapeDtypeStruct((ns, FFN_DIM), BF16),
        compiler_params=_cparams(('arbitrary', 'arbitrary')), name='moe_up',
    )(tile_expert, n_valid, xs, w1, w3)
    bn = 512
    return pl.pallas_call(
        _moe_down_kernel,
        grid_spec=pltpu.PrefetchScalarGridSpec(
            num_scalar_prefetch=2, grid=(D_MODEL // bn, n_tiles),
            in_specs=[pl.BlockSpec((MOE_BM, FFN_DIM), lambda j, m, te, nv: (m, 0)),
                      pl.BlockSpec((None, None, FFN_DIM, bn), lambda j, m, te, nv: (lyr, te[m], 0, j))],
            out_specs=pl.BlockSpec((MOE_BM, bn), lambda j, m, te, nv: (m, j)),
            scratch_shapes=[pltpu.VMEM((FFN_DIM, bn), BF16)]),
        out_shape=jax.ShapeDtypeStruct((ns, D_MODEL), F32),
        compiler_params=_cparams(('arbitrary', 'arbitrary')), name='moe_down',
    )(tile_expert, n_valid, u, w2)


def _moe_combine_kernel(dest_ref, ys_hbm, r_ref, x_ref, g_ref, m_ref, o_ref, buf, sem, *, bm, kg):
    base = pl.program_id(0) * bm

    def issue(r, carry):
        t = base + r
        for kk in range(TOP_K):
            _row_copy(ys_hbm.at[pl.ds(dest_ref[TOP_K * t + kk], 1)], buf.at[kk, pl.ds(r, 1)], sem).start()
        return carry
    lax.fori_loop(0, bm, issue, 0)

    def drain(r, carry):
        for kk in range(TOP_K):
            _row_copy(ys_hbm.at[pl.ds(0, 1)], buf.at[0, pl.ds(0, 1)], sem).wait()
        return carry
    lax.fori_loop(0, bm, drain, 0)

    r = r_ref[...]
    f = r[:, RT_G1:RT_G1 + 1] * buf[0] + r[:, RT_G2:RT_G2 + 1] * buf[1]
    n = f * lax.rsqrt(jnp.mean(f * f, axis=-1, keepdims=True) + EPS) * g_ref[...]
    o_ref[...] = x_ref[...] + m_ref[0][kg:kg + 1] * n


def _moe_combine(dest_flat, ys, route, x, g, mods, kg, t, bm):
    tt = x.shape[0]
    nlt = t // bm
    xs = pl.BlockSpec((bm, D_MODEL), lambda i, d: (i, 0))
    return pl.pallas_call(
        functools.partial(_moe_combine_kernel, bm=bm, kg=kg),
        grid_spec=pltpu.PrefetchScalarGridSpec(
            num_scalar_prefetch=1, grid=(tt // bm,),
            in_specs=[pl.BlockSpec(memory_space=pl.ANY),
                      pl.BlockSpec((bm, LANE), lambda i, d: (i, 0)), xs,
                      pl.BlockSpec((1, D_MODEL), lambda i, d: (0, 0)),
                      pl.BlockSpec((1, 6, D_MODEL), lambda i, d: (jnp.where(i >= nlt, 1, 0), 0, 0))],
            out_specs=xs,
            scratch_shapes=[pltpu.VMEM((TOP_K, bm, D_MODEL), F32), pltpu.SemaphoreType.DMA(())]),
        out_shape=jax.ShapeDtypeStruct((tt, D_MODEL), F32),
        compiler_params=_cparams(('arbitrary',)), name='moe_combine',
    )(dest_flat, ys, route, x, g.reshape(1, D_MODEL), mods)


def _moe_layer(x, h32, route, w1, w3, w2, lyr, g, mods, kg, t):
    tt = x.shape[0]
    bm = _row_tile(t, tt - t)
    dest, cnt = _moe_rank(route, bm)
    n_tiles = -(-TOP_K * tt // MOE_BM) + N_EXPERTS
    ns = n_tiles * MOE_BM
    cum = jnp.cumsum((cnt[0, 0:N_EXPERTS].astype(jnp.int32) + (MOE_BM - 1)) // MOE_BM)
    n_valid = cum[-1]
    tile = jnp.arange(n_tiles, dtype=jnp.int32)
    tile_expert = jnp.searchsorted(cum, jnp.minimum(tile, n_valid - 1), side='right').astype(jnp.int32)
    dest_flat = dest[:, 0:TOP_K].reshape(-1)
    xs32 = _moe_scatter(dest_flat, h32, ns, bm)
    ys = _moe_grouped_ffn(tile_expert, n_valid.reshape(1), xs32, w1, w3, w2, lyr)
    return _moe_combine(dest_flat, ys, route, x, g, mods, kg, t, bm)


def _rwkv_kernel(d, cur_ref, prv_ref, nxt_ref, sm_ref, mu_ref, w0_ref, wup_ref, a0_ref, aup_ref,
                 kk_ref, ka_ref, rk_ref, e_ref, tri_ref, y_ref, bonus_ref, s_ref, *, nl, nc):
    n = pl.program_id(0)
    ch = _chunk_of(d, n, nl, nc)

    @pl.when(n == 0)
    def _():
        s_ref[...] = jnp.zeros_like(s_ref)

    x = cur_ref[...]
    first = jnp.logical_or(ch == 0, ch == nl)
    last = jnp.logical_or(ch == nl - 1, ch == nl + nc - 1)
    p_row = jnp.where(first, 0.0, prv_ref[7:8, :])
    n_row = jnp.where(last, 0.0, nxt_ref[0:1, :])
    row = lax.broadcasted_iota(jnp.int32, (CHUNK, 1), 0)
    prev = jnp.where(row == 0, p_row, pltpu.roll(x, 1, axis=0))
    nxt = jnp.where(row == CHUNK - 1, n_row, pltpu.roll(x, CHUNK - 1, axis=0))
    z = x + mu_ref[0:1, :] * (prev - x) + mu_ref[1:2, :] * (nxt - x)
    r, k, v = z[:, 0:MIX_W], z[:, MIX_W:2 * MIX_W], z[:, 2 * MIX_W:3 * MIX_W]

    sm = sm_ref[...]
    e_bf = e_ref[...]
    wd = jnp.tanh(sm[:, 0:128])
    w_log = -_softplus(-(w0_ref[...] + _mm3(wd, wup_ref[...]))) - 0.5
    logw = -jnp.exp(w_log)
    a = _sigmoid(a0_ref[...] + _mm3(sm[:, 128:256], aup_ref[...]))
    kkr = k * kk_ref[...]
    kk = kkr * lax.rsqrt(jnp.maximum(_mm_xw(kkr * kkr, e_bf), 1e-12))
    k_dir = k * (1.0 + (a - 1.0) * ka_ref[...])
    bonus_ref[...] = _mm_xw(r * k_dir * rk_ref[...], e_bf) * v

    m_strict = tri_ref[0]
    m_incl = tri_ref[1]
    eye = m_incl - m_strict
    b_inc = _mm_wx(m_incl[0:CHUNK, 0:CHUNK].astype(BF16), logw)
    b_exc = b_inc - logw
    b_last = jnp.sum(logw, axis=0, keepdims=True)
    beta = kk * a
    ea = -kk * jnp.exp(b_exc)
    er = r * jnp.exp(b_inc)
    ninv = jnp.exp(-b_inc)
    eb = beta * ninv
    ek = k_dir * ninv
    eend = jnp.exp(b_last - b_inc)
    hb = beta * eend
    hk = k_dir * eend
    gam = jnp.exp(b_last)

    gw = RW_GROUP * RW_HEAD
    lane_head = lax.broadcasted_iota(jnp.int32, (1, gw), 1) >> 6

    def spread(xg):
        return jnp.concatenate([jnp.where(lane_head == h, xg, 0.0) for h in range(RW_GROUP)], axis=0)

    for g in range(RW_HEADS // RW_GROUP):
        sl = slice(g * gw, (g + 1) * gw)
        la, lr = spread(ea[:, sl]), spread(er[:, sl])
        rb, rk = spread(eb[:, sl]), spread(ek[:, sl])
        vb = spread(v[:, sl]).astype(BF16)
        amat = _mm_nt(jnp.concatenate([la, lr], axis=0), jnp.concatenate([rb, rk], axis=0))
        m_ab = amat[0:gw, 0:gw] * m_strict
        m_ak = amat[0:gw, gw:] * m_strict
        n_rb = (amat[gw:, 0:gw] * m_incl).astype(BF16)
        n_rk = (amat[gw:, gw:] * m_incl).astype(BF16)
        mp = m_ab
        tinv = eye + m_ab
        for _ in range(5):
            mp = _mm(mp, mp)
            tinv = tinv + _mm(mp, tinv)
        xs_bf = _mm(tinv, jnp.concatenate([la, _mm(m_ak, vb)], axis=1)).astype(BF16)
        qy = _mm(n_rb, xs_bf)
        q_hat = lr + qy[:, 0:gw]
        y_loc = qy[:, gw:] + _mm(n_rk, vb)
        hb_bd = spread(hb[:, sl]).astype(BF16)
        gbt = _mm_tn(xs_bf, hb_bd)
        g_bot = gbt[gw:] + _mm_tn(vb, spread(hk[:, sl]))
        s0 = s_ref[g]
        y_bd = _mm_nt(q_hat, s0) + y_loc
        y_ref[:, sl] = (y_bd[0:CHUNK] + y_bd[CHUNK:2 * CHUNK]
                        + y_bd[2 * CHUNK:3 * CHUNK] + y_bd[3 * CHUNK:4 * CHUNK])
        s_ref[g] = s0 * gam[:, sl] + _mm(s0, gbt[0:gw]) + g_bot


def _rwkv_order_masks():
    i = jnp.arange(RW_GROUP * RW_HEAD)
    same = (i[:, None] // CHUNK) == (i[None, :] // CHUNK)
    diff = (i[:, None] % CHUNK) - (i[None, :] % CHUNK)
    per_dir = [jnp.stack([same & (sg * diff > 0), same & (sg * diff >= 0)]) for sg in (1, -1)]
    return jnp.stack(per_dir).astype(F32)


def _gla_kernel(d, qk_ref, v_ref, ad_ref, cs_ref, aup_ref, ab_ref, o_ref, s_ref):
    @pl.when(pl.program_id(0) == 0)
    def _():
        s_ref[...] = jnp.zeros_like(s_ref)

    qk = qk_ref[...]
    v = v_ref[...]
    w_qk = 2 * GLA_HEADS * GLA_DK
    lane = lax.broadcasted_iota(jnp.int32, (CHUNK, w_qk), 1)
    partner = jnp.where((lane & 1) == 0, pltpu.roll(qk, w_qk - 1, axis=1), pltpu.roll(qk, 1, axis=1))
    qk = qk * cs_ref[:, 0:w_qk] + partner * cs_ref[:, w_qk:]
    hk = GLA_HEADS * GLA_DK
    q = qk[:, 0:hk] * (GLA_DK ** -0.5)
    k = qk[:, hk:]
    g = -_softplus(-(_mm3(ad_ref[...], aup_ref[...]) + ab_ref[...])) / GLA_TAU
    _, incl = _order_masks(d)
    b = _mm_wx(jnp.where(incl, 1.0, 0.0).astype(BF16), g)
    b_last = jnp.sum(g, axis=0, keepdims=True)
    q_e = q * jnp.exp(b)
    k_e = k * jnp.exp(-b)
    k_end = k * jnp.exp(b_last - b)
    dec = jnp.exp(b_last)
    for h in range(GLA_HEADS):
        sk = slice(h * GLA_DK, (h + 1) * GLA_DK)
        sv = slice(h * GLA_DV, (h + 1) * GLA_DV)
        att = jnp.where(incl, _mm_nt(q_e[:, sk], k_e[:, sk]), 0.0)
        st = s_ref[h]
        o_ref[:, sv] = _mm(att, v[:, sv]) + _mm_nt(q_e[:, sk], st)
        s_ref[h] = st * dec[:, sk] + _mm_tn(v[:, sv], k_end[:, sk])


N_SCAN_SHARED = 5
N_SCAN_DIR = 11
SCAN_COLS = 3 * MIX_W + 3 * LANE + 2 * GLA_HEADS * GLA_DK + 2 * MIX_W + LANE


def _scan_kernel(*refs, nl, nc):
    mu, kk, ka, rk, e64 = refs[0:N_SCAN_SHARED]
    n_in = N_SCAN_SHARED + 2 * N_SCAN_DIR
    for d in range(2):
        pm, prv, nxt, cs, w0, wup, a0, aup, tri, g_aup, g_ab = refs[N_SCAN_SHARED + d * N_SCAN_DIR:
                                                                   N_SCAN_SHARED + (d + 1) * N_SCAN_DIR]
        y, bonus, o = refs[n_in + 3 * d:n_in + 3 * d + 3]
        s_rw, s_gla = refs[n_in + 6 + 2 * d:n_in + 8 + 2 * d]
        c0 = 3 * MIX_W
        c1 = c0 + 3 * LANE
        c2 = c1 + 2 * GLA_HEADS * GLA_DK
        _rwkv_kernel(d, pm.at[:, 0:c0], prv, nxt, pm.at[:, c0:c1], mu, w0, wup, a0, aup, kk, ka, rk, e64, tri,
                     y, bonus, s_rw, nl=nl, nc=nc)
        _gla_kernel(d, pm.at[:, c1:c2], pm.at[:, c2:c2 + MIX_W], pm.at[:, SCAN_COLS - LANE:SCAN_COLS], cs,
                    g_aup, g_ab, o, s_gla)


def _scans(p, t, lc, shared, per_dir):
    tt = t + lc
    gw = RW_GROUP * RW_HEAD
    nl, nc = t // CHUNK, lc // CHUNK
    last8 = tt // 8 - 1
    full = lambda a: pl.BlockSpec(a.shape, lambda n: (0,) * a.ndim)
    in_specs = [full(a) for a in shared]
    args = list(shared)
    out_specs = []
    for d in range(2):
        ch = functools.partial(_chunk_of, d, nl=nl, nc=nc)
        rope, *params = per_dir[d]
        in_specs += [
            pl.BlockSpec((CHUNK, SCAN_COLS), lambda n, ch=ch: (ch(n), 0)),
            pl.BlockSpec((8, 3 * MIX_W), lambda n, ch=ch: (jnp.maximum(ch(n) * 8 - 1, 0), 0)),
            pl.BlockSpec((8, 3 * MIX_W), lambda n, ch=ch: (jnp.minimum(ch(n) * 8 + 8, last8), 0)),
            pl.BlockSpec((CHUNK, rope.shape[1]), lambda n, ch=ch: (ch(n), 0)),
        ] + [full(a) for a in params]
        args += [p, p, p, rope] + params
        out_specs += [pl.BlockSpec((CHUNK, MIX_W), lambda n, ch=ch: (ch(n), 0))] * 3
    assert len(in_specs) == N_SCAN_SHARED + 2 * N_SCAN_DIR
    return pl.pallas_call(
        functools.partial(_scan_kernel, nl=nl, nc=nc),
        grid=(nl + nc,),
        in_specs=in_specs, out_specs=out_specs,
        out_shape=[jax.ShapeDtypeStruct((tt, MIX_W), F32)] * 6,
        scratch_shapes=[pltpu.VMEM((RW_HEADS // RW_GROUP, gw, gw), F32),
                        pltpu.VMEM((GLA_HEADS, GLA_DV, GLA_DK), F32)] * 2,
        compiler_params=_cparams(('arbitrary',)), name='scans',
    )(*args)


def _head_norm(y, e_bf, width, eps):
    mu = _mm_xw(y, e_bf) * (1.0 / width)
    dl = y - mu
    var = _mm_xw(dl * dl, e_bf) * (1.0 / width)
    return dl * lax.rsqrt(var + eps)


def _mix_finish_kernel(y0_ref, y1_ref, b0_ref, b1_ref, sm_ref, gup_ref, lng_ref, lnb_ref, e64_ref,
                       o0_ref, o1_ref, gr0_ref, gr1_ref, gr2_ref, gr3_ref, gng_ref, e128_ref, a_ref, b_ref):
    yn = _head_norm(y0_ref[...] + y1_ref[...], e64_ref[...], RW_HEAD, RW_GN_EPS)
    yn = yn * lng_ref[...] + lnb_ref[...] + b0_ref[...] + b1_ref[...]
    gate = _mm(_sigmoid(sm_ref[:, 256:384]), gup_ref[...])
    a_ref[...] = (yn * gate).astype(BF16)
    on = _head_norm(o0_ref[...] + o1_ref[...], e128_ref[...], GLA_DV, GN_EPS) * gng_ref[...]
    gr = jnp.concatenate([gr0_ref[...], gr1_ref[...], gr2_ref[...], gr3_ref[...]], axis=1)
    b_ref[...] = (on * (gr * _sigmoid(gr))).astype(BF16)


def _mix_finish(p, rw_y, rw_bonus, gla_o, g_up, ln_g, ln_b, gn_g, e64, e128, t):
    tt = p.shape[0]
    bm = _row_tile(t, tt - t)
    vec = pl.BlockSpec((1, MIX_W), lambda i: (0, 0))
    mat = pl.BlockSpec((MIX_W, MIX_W), lambda i: (0, 0))
    out = pl.BlockSpec((bm, MIX_W), lambda i: (i, 0))
    return pl.pallas_call(
        _mix_finish_kernel,
        grid=(tt // bm,),
        in_specs=([out, out, out, out,
                   pl.BlockSpec((bm, 3 * LANE), lambda i: (i, CB384_RW_SMALL)),
                   pl.BlockSpec((RW_GATE_RANK, MIX_W), lambda i: (0, 0)), vec, vec, mat,
                   out, out]
                  + [pl.BlockSpec((bm, LANE), functools.partial(lambda i, c: (i, c), c=CB128_GLA_R + q))
                     for q in range(4)]
                  + [vec, mat]),
        out_specs=[out, out],
        out_shape=[jax.ShapeDtypeStruct((tt, MIX_W), BF16)] * 2,
        compiler_params=_cparams(('parallel',)), name='mix_finish',
    )(*rw_y, *rw_bonus, p, g_up, ln_g, ln_b, e64, *gla_o, p, p, p, p, gn_g, e128)


def _sgu_kernel(u_ref, v_ref, lng_ref, lnb_ref, ws_ref, bs_ref, o_ref):
    u = _gelu(u_ref[...])
    v = _gelu(v_ref[...])
    mu = jnp.mean(v, axis=-1, keepdims=True)
    dl = v - mu
    var = jnp.mean(dl * dl, axis=-1, keepdims=True)
    vn = (dl * lax.rsqrt(var + GN_EPS) * lng_ref[...] + lnb_ref[...]).astype(BF16)
    lane = lax.broadcasted_iota(jnp.int32, (1, MIX_W), 1)
    s = bs_ref[...]
    for g in range(SGU_GROUPS):
        s = s + jnp.where((lane >> 6) == g, jnp.dot(ws_ref[g].astype(BF16), vn, preferred_element_type=F32), 0.0)
    o_ref[...] = (u * s).astype(BF16)


def _sgu(p, ln_g, ln_b, w_s, b_full):
    tt = p.shape[0]
    vec = pl.BlockSpec((1, MIX_W), lambda i: (0, 0))
    return pl.pallas_call(
        _sgu_kernel,
        grid=(tt // SGU_CHUNK,),
        in_specs=[pl.BlockSpec((SGU_CHUNK, MIX_W), lambda i: (i, CB_SGU_U)),
                  pl.BlockSpec((SGU_CHUNK, MIX_W), lambda i: (i, CB_SGU_V)), vec, vec,
                  pl.BlockSpec((SGU_GROUPS, SGU_CHUNK, SGU_CHUNK), lambda i: (0, 0, 0)),
                  pl.BlockSpec((SGU_CHUNK, MIX_W), lambda i: (0, 0))],
        out_specs=pl.BlockSpec((SGU_CHUNK, MIX_W), lambda i: (i, 0)),
        out_shape=jax.ShapeDtypeStruct((tt, MIX_W), BF16),
        compiler_params=_cparams(('parallel',)), name='sgu',
    )(p, p, ln_g, ln_b, w_s, b_full)


def _na_bias_kernel(rpb_ref, o_ref, *, wh):
    h = pl.program_id(0)
    n_dc = 2 * NA_WIN_W - 1
    n_dr = 2 * NA_WIN_H - 1
    shape = (GRID_W, 2 * GRID_W)
    c = lax.broadcasted_iota(jnp.int32, shape, 0)
    lane = lax.broadcasted_iota(jnp.int32, shape, 1)
    x = lane & (GRID_W - 1)
    dc = jnp.clip(x - c + (NA_WIN_W - 1), 0, 2 * NA_WIN_W - 2)
    key = (lane >> 6) * n_dc + dc
    cs = jnp.clip(c - NA_WIN_W // 2, 0, GRID_W - NA_WIN_W)
    ok = jnp.logical_and(x >= cs, x < cs + NA_WIN_W)
    pairs = []
    for dr in range(n_dr - 1):
        base = h * (n_dr * n_dc) + dr * n_dc
        tile = lax.fori_loop(0, 2 * n_dc, lambda j, acc: jnp.where(key == j, rpb_ref[base + j], acc),
                             jnp.zeros(shape, F32))
        pairs.append(jnp.where(ok, tile, NEG_INF))
    for dr0 in range(n_dr - wh + 1):
        for jj in range(wh // 2):
            o_ref[dr0, 0, :, jj * 2 * GRID_W:(jj + 1) * 2 * GRID_W] = pairs[dr0 + 2 * jj]


def _na_bias(rpb, wh):
    n_dr0 = 2 * NA_WIN_H - wh
    return pl.pallas_call(
        functools.partial(_na_bias_kernel, wh=wh),
        grid=(NA_HEADS,),
        in_specs=[pl.BlockSpec(memory_space=pltpu.SMEM)],
        out_specs=pl.BlockSpec((n_dr0, 1, GRID_W, wh * GRID_W), lambda h: (0, h, 0, 0)),
        out_shape=jax.ShapeDtypeStruct((n_dr0, NA_HEADS, GRID_W, wh * GRID_W), F32),
        compiler_params=_cparams(('parallel',)), name='na_bias',
    )(rpb.reshape(-1))


def _na_kernel(*refs, wh):
    q_ref = refs[0]
    k_refs = refs[1:1 + wh]
    v_refs = refs[1 + wh:1 + 2 * wh]
    kc_ref, vc_ref, bias_ref, o_ref = refs[1 + 2 * wh:]
    q = q_ref[...] * (NA_HEAD ** -0.5)
    kw = jnp.concatenate([r[...].astype(BF16) for r in k_refs], axis=0)
    vw = jnp.concatenate([r[...].astype(BF16) for r in v_refs], axis=0)
    kc = kc_ref[...].astype(BF16)
    vc = vc_ref[...].astype(BF16)
    lane = lax.broadcasted_iota(jnp.int32, (1, LANE), 1)
    outs = []
    for pr in range(NA_HEADS // 2):
        sl = slice(pr * LANE, (pr + 1) * LANE)
        qp, kwp, vwp, kcp, vcp = q[:, sl], kw[:, sl], vw[:, sl], kc[:, sl], vc[:, sl]
        o = jnp.zeros((GRID_W, LANE), F32)
        for s in range(2):
            hm = (lane >> 6) == s
            qh = jnp.where(hm, qp, 0.0).astype(BF16)
            sw = _mm_nt(qh, kwp) + bias_ref[0, 2 * pr + s]
            sc = _mm_nt(qh, kcp)
            m = jnp.maximum(jnp.max(sw, axis=-1, keepdims=True), jnp.max(sc, axis=-1, keepdims=True))
            ew = jnp.exp(sw - m)
            ec = jnp.exp(sc - m)
            den = jnp.sum(ew, axis=-1, keepdims=True) + jnp.sum(ec, axis=-1, keepdims=True)
            oh = (_mm(ew, vwp) + _mm(ec, vcp)) / den
            o = jnp.where(hm, oh, o)
        outs.append(o)
    o_ref[...] = jnp.concatenate(outs, axis=1).astype(BF16)


def _na_ctx_kernel(q_ref, k_ref, v_ref, o_ref):
    q = q_ref[...] * (NA_HEAD ** -0.5)
    kc = k_ref[...].astype(BF16)
    vc = v_ref[...].astype(BF16)
    lane = lax.broadcasted_iota(jnp.int32, (1, MIX_W), 1)
    o = jnp.zeros(q.shape, F32)
    for h in range(NA_HEADS):
        hm = (lane >> 6) == h
        s = _mm_nt(jnp.where(hm, q, 0.0), kc)
        e = jnp.exp(s - jnp.max(s, axis=-1, keepdims=True))
        o = jnp.where(hm, _mm(e, vc) / jnp.sum(e, axis=-1, keepdims=True), o)
    o_ref[...] = o.astype(BF16)


def _na(p, bias, t, lc):
    rows = t // GRID_W
    wh = min(NA_WIN_H, rows)
    rs = lambda r: jnp.clip(r - wh // 2, 0, rows - wh)
    ctx_blk = t // lc
    kv = lambda cb: [pl.BlockSpec((GRID_W, MIX_W), functools.partial(lambda r, w, cb: (rs(r) + w, cb), w=w, cb=cb))
                     for w in range(wh)]
    lat = pl.pallas_call(
        functools.partial(_na_kernel, wh=wh),
        grid=(rows,),
        in_specs=([pl.BlockSpec((GRID_W, MIX_W), lambda r: (r, CB_NA_Q))] + kv(CB_NA_K) + kv(CB_NA_V)
                  + [pl.BlockSpec((lc, MIX_W), lambda r: (ctx_blk, CB_NA_K)),
                     pl.BlockSpec((lc, MIX_W), lambda r: (ctx_blk, CB_NA_V)),
                     pl.BlockSpec((1, NA_HEADS, GRID_W, wh * GRID_W),
                                  lambda r: (rs(r) - r + (NA_WIN_H - 1), 0, 0, 0))]),
        out_specs=pl.BlockSpec((GRID_W, MIX_W), lambda r: (r, 0)),
        out_shape=jax.ShapeDtypeStruct((t, MIX_W), BF16),
        compiler_params=_cparams(('parallel',)), name='na_latent',
    )(*([p] * (1 + 2 * wh + 2)), bias)
    cblk = lambda cb: pl.BlockSpec((lc, MIX_W), lambda i: (ctx_blk, cb))
    ctx = pl.pallas_call(
        _na_ctx_kernel,
        grid=(1,),
        in_specs=[cblk(CB_NA_Q), cblk(CB_NA_K), cblk(CB_NA_V)],
        out_specs=pl.BlockSpec((lc, MIX_W), lambda i: (0, 0)),
        out_shape=jax.ShapeDtypeStruct((lc, MIX_W), BF16),
        compiler_params=_cparams(('arbitrary',)), name='na_ctx',
    )(p, p, p)
    return jnp.concatenate([lat, ctx], axis=0)


def _merge_kernel(a0, a1, a2, a3, g0, g1, g2, g3, w_ref, o_ref, wb_ref):
    @pl.when(pl.program_id(1) == 0)
    def _():
        wb_ref[...] = w_ref[...].astype(BF16)
    acc = None
    for n, (a_ref, g_ref) in enumerate(((a0, g0), (a1, g1), (a2, g2), (a3, g3))):
        zn = jnp.dot(a_ref[...], wb_ref[n], preferred_element_type=F32) * _sigmoid(g_ref[...])
        acc = zn if acc is None else acc + zn
    o_ref[...] = acc.astype(BF16)


def _merge(ys, p, w_br, lyr):
    tt = p.shape[0]
    bn = 512
    bm = _pick(tt, (768, 384, 128))
    a_s = pl.BlockSpec((bm, MIX_W), lambda j, i: (i, 0))
    gs = [pl.BlockSpec((bm, bn), functools.partial(lambda j, i, n: (i, (GATE_OFF + n * D_MODEL) // bn + j), n=n))
          for n in range(N_BRANCH)]
    return pl.pallas_call(
        _merge_kernel,
        grid=(D_MODEL // bn, tt // bm),
        in_specs=[a_s] * 4 + gs + [pl.BlockSpec((None, N_BRANCH, MIX_W, bn), lambda j, i: (lyr, 0, 0, j))],
        out_specs=pl.BlockSpec((bm, bn), lambda j, i: (i, j)),
        out_shape=jax.ShapeDtypeStruct((tt, D_MODEL), BF16),
        scratch_shapes=[pltpu.VMEM((N_BRANCH, MIX_W, bn), BF16)],
        compiler_params=_cparams(('arbitrary', 'arbitrary')), name='merge',
    )(*ys, p, p, p, p, w_br)


def _block_diag_ones(width):
    i = jnp.arange(MIX_W) // width
    return (i[:, None] == i[None, :]).astype(BF16)


def _rope_tables(t, lc):
    tok = jnp.arange(t)
    pos = jnp.stack([tok // GRID_W, tok % GRID_W], axis=-1).astype(F32)
    nf = GLA_DK // 4
    inv = ROPE_BASE ** (-jnp.arange(nf, dtype=F32) / nf)
    ang = pos[:, :, None] * inv
    cos = jnp.repeat(jnp.cos(ang), 2, axis=-1).reshape(t, GLA_DK)
    sin = jnp.sin(ang)
    sin = jnp.stack([-sin, sin], axis=-1).reshape(t, GLA_DK)
    reps = 2 * GLA_HEADS
    cos = jnp.concatenate([jnp.tile(cos, (1, reps)), jnp.ones((lc, reps * GLA_DK), F32)], axis=0)
    sin = jnp.concatenate([jnp.tile(sin, (1, reps)), jnp.zeros((lc, reps * GLA_DK), F32)], axis=0)
    return jnp.concatenate([cos, sin], axis=1)


def _pad_rank_rows(w_up, rank, rows):
    out = jnp.zeros((2, rows, w_up.shape[-1]), w_up.dtype)
    for d in range(2):
        out = out.at[d, d * rank:(d + 1) * rank].set(w_up[d])
    return out


def kernel(x, c, ctx, c_ctx, ada_w, ada_b, norm_g, w_in, rw_mu, rw_w0, rw_w_up, rw_a0, rw_a_up, rw_g_up, rw_k_k, rw_k_a, rw_r_k, rw_ln_g, rw_ln_b, gla_a_up, gla_a_b, gla_gn_g, sgu_ln_g, sgu_ln_b, sgu_w, sgu_b, na_rpb, w_br, w_o, ffn_w1, ffn_w3, ffn_w2, moe_router, moe_w1, moe_w3, moe_w2):
    assert x.shape[0] == 1 and x.shape[2] == D_MODEL
    t, lc = x.shape[1], ctx.shape[1]
    depth = ada_w.shape[0]
    assert t % max(lc, SGU_CHUNK) == 0 and lc % SGU_CHUNK == 0 and t % GRID_W == 0
    rows = t // GRID_W
    assert rows >= NA_WIN_H
    wh = NA_WIN_H
    xs = jnp.concatenate([x[0], ctx[0]], axis=0)
    cond8 = jnp.zeros((8, D_MODEL), F32).at[0].set(c[0]).at[1].set(c_ctx)
    e64 = _block_diag_ones(RW_HEAD)
    e128 = _block_diag_ones(GLA_DV)
    rope = _rope_tables(t, lc)
    tri = _rwkv_order_masks()
    row1 = lambda v: v.reshape(1, -1)
    assert w_in.shape[2] == N_IN

    mods_all = [_adaln(cond8, ada_w, ada_b, i)[0:2].reshape(2, 6, D_MODEL) for i in range(depth)]
    h = _norm_mod_call(xs, norm_g[0, 0], mods_all[0], 0, t)
    for i in range(depth):
        mods = mods_all[i]
        last = i == depth - 1
        p = _in_proj(h, w_in, i)
        mu2 = jnp.stack([rw_mu[i, :, 0].reshape(-1), rw_mu[i, :, 1].reshape(-1)])
        w_up = _pad_rank_rows(rw_w_up[i], RW_DECAY_RANK, LANE)
        a_up = _pad_rank_rows(rw_a_up[i], RW_ICLR_RANK, LANE)
        g_up = _pad_rank_rows(gla_a_up[i], GLA_GATE_RANK, LANE)
        per_dir = [[rope, row1(rw_w0[i, d]), w_up[d], row1(rw_a0[i, d]), a_up[d], tri[d], g_up[d],
                    row1(gla_a_b[i, d])] for d in range(2)]
        y0, b0, o0, y1, b1, o1 = _scans(p, t, lc, [mu2, row1(rw_k_k[i]), row1(rw_k_a[i]), row1(rw_r_k[i]), e64],
                                        per_dir)
        rw_y, rw_bonus, gla_o = (y0, y1), (b0, b1), (o0, o1)
        y_a, y_b = _mix_finish(p, rw_y, rw_bonus, gla_o, rw_g_up[i], row1(rw_ln_g[i]), row1(rw_ln_b[i]),
                               row1(gla_gn_g[i]), e64, e128, t)
        y_s = _sgu(p, row1(sgu_ln_g[i]), row1(sgu_ln_b[i]), sgu_w[i], jnp.repeat(sgu_b[i].T, 64, axis=1))
        y_d = _na(p, _na_bias(na_rpb[i], wh), t, lc)
        z = _merge((y_a, y_b, y_s, y_d), p, w_br, i)
        y = _matmul(z, w_o, i, 1024, (768, 384, 128), name='out_proj')
        j = i // 2
        if i % 2 == 0:
            xs, h = _resid_next_call(xs, y, norm_g[i, 1], mods, 2, t, norm_g[i, 2], mods, 3)
            f = _ffn_down(_ffn_up(h, ffn_w1, ffn_w3, j), ffn_w2, j)
            if last:
                xs = _resid_call(xs, f, norm_g[i, 3], mods, 5, t)
            else:
                xs, h = _resid_next_call(xs, f, norm_g[i, 3], mods, 5, t, norm_g[i + 1, 0], mods_all[i + 1], 0)
        else:
            xs, h32, route = _resid_next_call(xs, y, norm_g[i, 1], mods, 2, t, norm_g[i, 2], mods, 3,
                                              router=moe_router[j])
            xs = _moe_layer(xs, h32, route, moe_w1, moe_w3, moe_w2, j, norm_g[i, 3], mods, 5, t)
            if not last:
                h = _norm_mod_call(xs, norm_g[i + 1, 0], mods_all[i + 1], 0, t)
    return xs[0:t][None]
```

```python
import functools

import jax
import jax.numpy as jnp
from jax import lax
from jax.experimental import pallas as pl
from jax.experimental.pallas import tpu as pltpu

F32 = jnp.float32
BF16 = jnp.bfloat16

D_MODEL = 2048
GRID_W = 64
N_BRANCH = 4
MIX_W = D_MODEL // 4
RW_HEAD = 64
RW_HEADS = MIX_W // RW_HEAD
RW_GROUP = 4
RW_DECAY_RANK = 64
RW_ICLR_RANK = 64
RW_GATE_RANK = 128
RW_GN_EPS = 64e-5
GLA_HEADS = 4
GLA_DV = MIX_W // GLA_HEADS
GLA_DK = GLA_DV // 2
GLA_GATE_RANK = 16
GLA_TAU = 16.0
GN_EPS = 1e-5
ROPE_BASE = 10000.0
SGU_GROUPS = MIX_W // 64
SGU_CHUNK = 128
NA_HEAD = 64
NA_HEADS = MIX_W // NA_HEAD
NA_WIN_H = 8
NA_WIN_W = 16
FFN_DIM = 7 * D_MODEL // 2
N_EXPERTS = 8
TOP_K = 2
EPS = 1e-6
NEG_INF = -1e30

RT_G1, RT_G2, RT_I1, RT_I2 = 8, 9, 10, 11
MOE_BM = 256
CHUNK = 64
LANE = 128
VMEM_LIMIT = 56 * 2 ** 20

IN_TAIL = 3 * MIX_W + 2 * RW_DECAY_RANK + 2 * RW_ICLR_RANK + RW_GATE_RANK + 2 * GLA_HEADS * GLA_DK \
    + 2 * MIX_W + 2 * GLA_GATE_RANK
IN_PAD = -IN_TAIL % LANE
N_IN = IN_TAIL + 5 * MIX_W + N_BRANCH * D_MODEL
N_IN_P = N_IN + IN_PAD
CB384_RW_SMALL = 4
CB128_GLA_QK, CB128_GLA_V, CB128_GLA_R, CB128_GLA_AD = 15, 19, 23, 27
CB_SGU_U, CB_SGU_V, CB_NA_Q, CB_NA_K, CB_NA_V = 7, 8, 9, 10, 11
GATE_OFF = 6144


def _cparams(sem):
    return pltpu.CompilerParams(dimension_semantics=sem, vmem_limit_bytes=VMEM_LIMIT)


def _pick(m, cands):
    for c in cands:
        if m % c == 0:
            return c
    raise ValueError(f'no tile for {m}')


def _mm(a, b):
    return jnp.dot(a.astype(BF16), b.astype(BF16), preferred_element_type=F32)


def _mm_nt(a, b):
    return lax.dot_general(a.astype(BF16), b.astype(BF16), (((1,), (1,)), ((), ())),
                           preferred_element_type=F32)


def _mm_tn(a, b):
    return lax.dot_general(a.astype(BF16), b.astype(BF16), (((0,), (0,)), ((), ())),
                           preferred_element_type=F32)


def _split2(x):
    hi = x.astype(BF16)
    lo = (x - hi.astype(F32)).astype(BF16)
    return hi, lo


def _mm_xw(x, w_bf):
    hi, lo = _split2(x)
    return (jnp.dot(hi, w_bf, preferred_element_type=F32)
            + jnp.dot(lo, w_bf, preferred_element_type=F32))


def _mm_wx(w_bf, x):
    hi, lo = _split2(x)
    return (jnp.dot(w_bf, hi, preferred_element_type=F32)
            + jnp.dot(w_bf, lo, preferred_element_type=F32))


def _mm3(a, b):
    ah, al = _split2(a)
    bh, bl = _split2(b)
    return (jnp.dot(ah, bh, preferred_element_type=F32)
            + jnp.dot(ah, bl, preferred_element_type=F32)
            + jnp.dot(al, bh, preferred_element_type=F32))


def _sigmoid(x):
    return 1.0 / (1.0 + jnp.exp(-x))


def _softplus(x):
    return jnp.maximum(x, 0.0) + jnp.log1p(jnp.exp(-jnp.abs(x)))


def _gelu(x):
    return 0.5 * x * (1.0 + lax.erf(x * (0.5 ** 0.5)))


def _order_masks(d):
    t = lax.broadcasted_iota(jnp.int32, (CHUNK, CHUNK), 0)
    s = lax.broadcasted_iota(jnp.int32, (CHUNK, CHUNK), 1)
    diff = (t - s) * jnp.where(d == 0, 1, -1)
    return diff > 0, diff >= 0


def _chunk_of(d, n, nl, nc):
    fwd = jnp.where(n < nc, nl + n, n - nc)
    bwd = jnp.where(n < nc, nl + nc - 1 - n, nl - 1 - (n - nc))
    return jnp.where(d == 0, fwd, bwd)


def _ada_kernel(c_ref, w_ref, b_ref, o_ref):
    cnd = c_ref[...]
    a = cnd * _sigmoid(cnd)
    o_ref[...] = _mm(a, w_ref[...]) + b_ref[...]


def _adaln(cond8, ada_w, ada_b, lyr):
    depth, _, n = ada_w.shape
    bn = 1536
    return pl.pallas_call(
        _ada_kernel,
        grid=(n // bn,),
        in_specs=[pl.BlockSpec((8, D_MODEL), lambda j: (0, 0)),
                  pl.BlockSpec((None, D_MODEL, bn), lambda j: (lyr, 0, j)),
                  pl.BlockSpec((None, 1, bn), lambda j: (lyr, 0, j))],
        out_specs=pl.BlockSpec((8, bn), lambda j: (0, j)),
        out_shape=jax.ShapeDtypeStruct((8, n), F32),
        compiler_params=_cparams(('arbitrary',)),
        name='adaln',
    )(cond8, ada_w, ada_b.reshape(depth, 1, n))


def _norm_mod(x, g, m, k0):
    y = x * lax.rsqrt(jnp.mean(x * x, axis=-1, keepdims=True) + EPS) * g
    return y * (1.0 + m[k0 + 1:k0 + 2]) + m[k0:k0 + 1]


def _pack_halves(h):
    n = h.shape[1] // 2
    bits = lax.bitcast_convert_type(h.astype(BF16).astype(F32), jnp.uint32)
    return (bits[:, 0:n] >> 16) | bits[:, n:]


def _unpack_halves(w):
    lo = lax.bitcast_convert_type(w << 16, F32)
    hi = lax.bitcast_convert_type(w & jnp.uint32(0xFFFF0000), F32)
    return jnp.concatenate([lo, hi], axis=1).astype(BF16)


def _norm_mod_kernel(x_ref, g_ref, m_ref, o_ref, *, k0):
    o_ref[...] = _norm_mod(x_ref[...], g_ref[...], m_ref[0], k0).astype(BF16)


def _route_table(h, router):
    logits = _mm3(h, router)
    lane = lax.broadcasted_iota(jnp.int32, logits.shape, 1)
    neg = jnp.float32(-jnp.inf)
    l1 = jnp.where(lane < N_EXPERTS, logits, neg)
    m1 = jnp.max(l1, axis=-1, keepdims=True)
    i1 = jnp.min(jnp.where(l1 == m1, lane, LANE), axis=-1, keepdims=True)
    l2 = jnp.where(lane == i1, neg, l1)
    m2 = jnp.max(l2, axis=-1, keepdims=True)
    i2 = jnp.min(jnp.where(l2 == m2, lane, LANE), axis=-1, keepdims=True)
    e2 = jnp.exp(m2 - m1)
    den = 1.0 + e2
    sel = jnp.where(jnp.logical_or(lane == i1, lane == i2), 1.0, 0.0)
    sel = jnp.where(lane == RT_G1, 1.0 / den, jnp.where(lane == RT_G2, e2 / den, sel))
    return jnp.where(lane == RT_I1, i1.astype(F32), jnp.where(lane == RT_I2, i2.astype(F32), sel))


def _row_tile(t, lc):
    return _pick(lc, (256, 128))


def _norm_mod_call(x, g, mods, k0, t):
    tt = x.shape[0]
    bm = _row_tile(t, tt - t)
    nlt = t // bm
    xs = pl.BlockSpec((bm, D_MODEL), lambda i: (i, 0))
    gs = pl.BlockSpec((1, D_MODEL), lambda i: (0, 0))
    ms = pl.BlockSpec((1, 6, D_MODEL), lambda i: (jnp.where(i >= nlt, 1, 0), 0, 0))
    return pl.pallas_call(
        functools.partial(_norm_mod_kernel, k0=k0),
        grid=(tt // bm,), in_specs=[xs, gs, ms], out_specs=xs,
        out_shape=jax.ShapeDtypeStruct((tt, D_MODEL), BF16),
        compiler_params=_cparams(('parallel',)), name='norm_mod',
    )(x, g.reshape(1, D_MODEL), mods)


def _resid_kernel(x_ref, y_ref, g_ref, m_ref, o_ref, *, kg):
    y = y_ref[...]
    n = y * lax.rsqrt(jnp.mean(y * y, axis=-1, keepdims=True) + EPS) * g_ref[...]
    o_ref[...] = x_ref[...] + m_ref[0][kg:kg + 1] * n


def _resid_call(x, y, g, mods, kg, t):
    tt = x.shape[0]
    bm = _row_tile(t, tt - t)
    nlt = t // bm
    xs = pl.BlockSpec((bm, D_MODEL), lambda i: (i, 0))
    return pl.pallas_call(
        functools.partial(_resid_kernel, kg=kg),
        grid=(tt // bm,),
        in_specs=[xs, xs, pl.BlockSpec((1, D_MODEL), lambda i: (0, 0)),
                  pl.BlockSpec((1, 6, D_MODEL), lambda i: (jnp.where(i >= nlt, 1, 0), 0, 0))],
        out_specs=xs,
        out_shape=jax.ShapeDtypeStruct((tt, D_MODEL), F32),
        compiler_params=_cparams(('parallel',)), name='resid_norm',
    )(x, y, g.reshape(1, D_MODEL), mods)


def _resid_next_kernel(*refs, kg, k0, route):
    x_ref, y_ref, g_ref, m_ref, g2_ref, m2_ref = refs[0:6]
    y = y_ref[...]
    n = y * lax.rsqrt(jnp.mean(y * y, axis=-1, keepdims=True) + EPS) * g_ref[...]
    xn = x_ref[...] + m_ref[0][kg:kg + 1] * n
    h = _norm_mod(xn, g2_ref[...], m2_ref[0], k0)
    if route:
        r_ref, xo_ref, ho_ref, rt_ref = refs[6:]
        ho_ref[...] = _pack_halves(h)
        rt_ref[...] = _route_table(h, r_ref[...])
    else:
        xo_ref, ho_ref = refs[6:]
        ho_ref[...] = h.astype(BF16)
    xo_ref[...] = xn


def _resid_next_call(x, y, g, mods, kg, t, g2, mods2, k0, router=None):
    tt = x.shape[0]
    bm = _row_tile(t, tt - t)
    nlt = t // bm
    xs = pl.BlockSpec((bm, D_MODEL), lambda i: (i, 0))
    gs = pl.BlockSpec((1, D_MODEL), lambda i: (0, 0))
    ms = pl.BlockSpec((1, 6, D_MODEL), lambda i: (jnp.where(i >= nlt, 1, 0), 0, 0))
    in_specs = [xs, xs, gs, ms, gs, ms]
    args = [x, y, g.reshape(1, D_MODEL), mods, g2.reshape(1, D_MODEL), mods2]
    out_specs = [xs, xs]
    out_shape = [jax.ShapeDtypeStruct((tt, D_MODEL), F32), jax.ShapeDtypeStruct((tt, D_MODEL), BF16)]
    if router is not None:
        in_specs.append(pl.BlockSpec((D_MODEL, LANE), lambda i: (0, 0)))
        args.append(jnp.pad(router, ((0, 0), (0, LANE - N_EXPERTS))))
        out_specs = [xs, pl.BlockSpec((bm, D_MODEL // 2), lambda i: (i, 0)), pl.BlockSpec((bm, LANE), lambda i: (i, 0))]
        out_shape = [out_shape[0], jax.ShapeDtypeStruct((tt, D_MODEL // 2), jnp.uint32),
                     jax.ShapeDtypeStruct((tt, LANE), F32)]
    return pl.pallas_call(
        functools.partial(_resid_next_kernel, kg=kg, k0=k0, route=router is not None),
        grid=(tt // bm,), in_specs=in_specs, out_specs=out_specs, out_shape=out_shape,
        compiler_params=_cparams(('parallel',)), name='resid_next',
    )(*args)


def _mm_kernel(a_ref, w_ref, o_ref, wb_ref):
    @pl.when(pl.program_id(1) == 0)
    def _():
        wb_ref[...] = w_ref[...].astype(BF16)
    o_ref[...] = jnp.dot(a_ref[...], wb_ref[...], preferred_element_type=F32).astype(o_ref.dtype)


def _matmul(a, w, lyr, bn, bm_cands, out_dtype=F32, name='matmul'):
    m, k = a.shape
    n = w.shape[2]
    bm = _pick(m, bm_cands)
    return pl.pallas_call(
        _mm_kernel,
        grid=(n // bn, m // bm),
        in_specs=[pl.BlockSpec((bm, k), lambda j, i: (i, 0)),
                  pl.BlockSpec((None, k, bn), lambda j, i: (lyr, 0, j))],
        out_specs=pl.BlockSpec((bm, bn), lambda j, i: (i, j)),
        out_shape=jax.ShapeDtypeStruct((m, n), out_dtype),
        scratch_shapes=[pltpu.VMEM((k, bn), BF16)],
        compiler_params=_cparams(('arbitrary', 'arbitrary')), name=name,
    )(a, w)


def _in_proj_kernel(a_ref, wt_ref, o_ref, wb_ref, *, bn):
    j = pl.program_id(0)
    straddle = IN_TAIL // bn
    cut = IN_TAIL - straddle * bn

    @pl.when(pl.program_id(1) == 0)
    def _():
        @pl.when(j != straddle)
        def _():
            wb_ref[...] = wt_ref[0].astype(BF16)

        @pl.when(j == straddle)
        def _():
            wb_ref[0:cut + IN_PAD, :] = wt_ref[0, 0:cut + IN_PAD, :].astype(BF16)
            wb_ref[cut + IN_PAD:, :] = wt_ref[0, cut:bn - IN_PAD, :].astype(BF16)

    o_ref[...] = lax.dot_general(a_ref[...], wb_ref[...], (((1,), (1,)), ((), ())), preferred_element_type=F32)


def _in_proj(a, w_in, lyr):
    m = a.shape[0]
    bn = 1024
    bm = _pick(m, (768, 384, 128))
    straddle = IN_TAIL // bn
    assert IN_PAD % 8 == 0 and straddle == (IN_TAIL + IN_PAD - 1) // bn
    w_t = jnp.swapaxes(w_in, 1, 2)
    return pl.pallas_call(
        functools.partial(_in_proj_kernel, bn=bn),
        grid=(N_IN_P // bn, m // bm),
        in_specs=[pl.BlockSpec((bm, D_MODEL), lambda j, i: (i, 0)),
                  pl.BlockSpec((pl.Element(1), pl.Element(bn), pl.Element(D_MODEL)),
                               lambda j, i: (lyr, 8 * (j * (bn // 8) - jnp.where(j > straddle, IN_PAD // 8, 0)), 0))],
        out_specs=pl.BlockSpec((bm, bn), lambda j, i: (i, j)),
        out_shape=jax.ShapeDtypeStruct((m, N_IN_P), F32),
        scratch_shapes=[pltpu.VMEM((bn, D_MODEL), BF16)],
        compiler_params=_cparams(('arbitrary', 'arbitrary')), name='in_proj',
    )(a, w_t)


def _ffn_up_kernel(a_ref, w1_ref, w3_ref, o_ref, w1b, w3b):
    @pl.when(pl.program_id(1) == 0)
    def _():
        w1b[...] = w1_ref[...].astype(BF16)
        w3b[...] = w3_ref[...].astype(BF16)
    a = a_ref[...]
    h1 = jnp.dot(a, w1b[...], preferred_element_type=F32)
    h3 = jnp.dot(a, w3b[...], preferred_element_type=F32)
    o_ref[...] = (h1 * _sigmoid(h1) * h3).astype(BF16)


def _ffn_up(h, w1, w3, e):
    m = h.shape[0]
    bn = 512
    bm = _pick(m, (768, 384, 128))
    ws = pl.BlockSpec((None, D_MODEL, bn), lambda j, i: (e, 0, j))
    return pl.pallas_call(
        _ffn_up_kernel,
        grid=(FFN_DIM // bn, m // bm),
        in_specs=[pl.BlockSpec((bm, D_MODEL), lambda j, i: (i, 0)), ws, ws],
        out_specs=pl.BlockSpec((bm, bn), lambda j, i: (i, j)),
        out_shape=jax.ShapeDtypeStruct((m, FFN_DIM), BF16),
        scratch_shapes=[pltpu.VMEM((D_MODEL, bn), BF16), pltpu.VMEM((D_MODEL, bn), BF16)],
        compiler_params=_cparams(('arbitrary', 'arbitrary')), name='ffn_up',
    )(h, w1, w3)


def _ffn_down_kernel(a_ref, w_ref, o_ref):
    part = jnp.dot(a_ref[...], w_ref[...].astype(BF16), preferred_element_type=F32)

    @pl.when(pl.program_id(1) == 0)
    def _():
        o_ref[...] = part

    @pl.when(pl.program_id(1) > 0)
    def _():
        o_ref[...] += part


def _ffn_down(u, w2, e):
    m = u.shape[0]
    bk = 512
    bm = max(d for d in range(LANE, 1408 + 1, LANE) if m % d == 0)
    return pl.pallas_call(
        _ffn_down_kernel,
        grid=(m // bm, FFN_DIM // bk),
        in_specs=[pl.BlockSpec((bm, bk), lambda i, k: (i, k)),
                  pl.BlockSpec((None, bk, D_MODEL), lambda i, k: (e, k, 0))],
        out_specs=pl.BlockSpec((bm, D_MODEL), lambda i, k: (i, 0)),
        out_shape=jax.ShapeDtypeStruct((m, D_MODEL), F32),
        compiler_params=_cparams(('parallel', 'arbitrary')), name='ffn_down',
    )(u, w2)


def _moe_rank_kernel(r_ref, dest_ref, cnt_ref, carry_ref, tot_ref):
    ph = pl.program_id(0)
    i = pl.program_id(1)
    r = r_ref[...]
    bm = r.shape[0]
    lane = lax.broadcasted_iota(jnp.int32, (1, LANE), 1)
    oh = jnp.where(lane < N_EXPERTS, r, 0.0)
    colsum = jnp.sum(oh, axis=0, keepdims=True)

    @pl.when(jnp.logical_and(ph == 0, i == 0))
    def _():
        carry_ref[...] = jnp.zeros_like(carry_ref)

    @pl.when(jnp.logical_and(ph == 1, i == 0))
    def _():
        tot_ref[...] = carry_ref[...]
        carry_ref[...] = jnp.zeros_like(carry_ref)

    @pl.when(ph == 1)
    def _():
        tot = tot_ref[...]
        padded = jnp.floor((tot + (MOE_BM - 1)) * (1.0 / MOE_BM)) * MOE_BM
        a = lax.broadcasted_iota(jnp.int32, (LANE, LANE), 0)
        b = lax.broadcasted_iota(jnp.int32, (LANE, LANE), 1)
        upper = jnp.where(a < b, 1.0, 0.0).astype(BF16)
        offs = _mm_xw(jnp.broadcast_to(padded, (8, LANE)), upper)[0:1]
        tr = lax.broadcasted_iota(jnp.int32, (bm, bm), 0)
        ts = lax.broadcasted_iota(jnp.int32, (bm, bm), 1)
        before = jnp.dot(jnp.where(ts < tr, 1.0, 0.0).astype(BF16), oh.astype(BF16),
                         preferred_element_type=F32)
        slot = before + carry_ref[...] + offs
        lane_f = lane.astype(F32)
        d1 = jnp.sum(jnp.where(lane_f == r[:, RT_I1:RT_I1 + 1], slot, 0.0), axis=-1, keepdims=True)
        d2 = jnp.sum(jnp.where(lane_f == r[:, RT_I2:RT_I2 + 1], slot, 0.0), axis=-1, keepdims=True)
        dest_ref[...] = jnp.where(lane == 0, d1, jnp.where(lane == 1, d2, 0.0)).astype(jnp.int32)
        cnt_ref[...] = jnp.broadcast_to(tot, (8, LANE))

    carry_ref[...] = carry_ref[...] + colsum


def _moe_rank(route, bm):
    tt = route.shape[0]
    return pl.pallas_call(
        _moe_rank_kernel,
        grid=(2, tt // bm),
        in_specs=[pl.BlockSpec((bm, LANE), lambda ph, i: (i, 0))],
        out_specs=[pl.BlockSpec((bm, LANE), lambda ph, i: (i * ph, 0)),
                   pl.BlockSpec((8, LANE), lambda ph, i: (0, 0))],
        out_shape=[jax.ShapeDtypeStruct((tt, LANE), jnp.int32), jax.ShapeDtypeStruct((8, LANE), F32)],
        scratch_shapes=[pltpu.VMEM((1, LANE), F32), pltpu.VMEM((1, LANE), F32)],
        compiler_params=_cparams(('arbitrary', 'arbitrary')), name='moe_rank',
    )(route)


def _row_copy(src, dst, sem):
    return pltpu.make_async_copy(src, dst, sem)


def _moe_scatter_kernel(dest_ref, h_ref, init_hbm, xs_hbm, sem, *, bm):
    del init_hbm
    base = pl.program_id(0) * bm

    def issue(r, carry):
        t = base + r
        for kk in range(TOP_K):
            _row_copy(h_ref.at[pl.ds(r, 1)], xs_hbm.at[pl.ds(dest_ref[TOP_K * t + kk], 1)], sem).start()
        return carry
    lax.fori_loop(0, bm, issue, 0)

    def drain(r, carry):
        for kk in range(TOP_K):
            _row_copy(h_ref.at[pl.ds(0, 1)], xs_hbm.at[pl.ds(0, 1)], sem).wait()
        return carry
    lax.fori_loop(0, bm, drain, 0)


def _moe_scatter(dest_flat, h32, ns, bm):
    tt, w = h32.shape
    return pl.pallas_call(
        functools.partial(_moe_scatter_kernel, bm=bm),
        grid_spec=pltpu.PrefetchScalarGridSpec(
            num_scalar_prefetch=1, grid=(tt // bm,),
            in_specs=[pl.BlockSpec((bm, w), lambda i, d: (i, 0)), pl.BlockSpec(memory_space=pl.ANY)],
            out_specs=pl.BlockSpec(memory_space=pl.ANY),
            scratch_shapes=[pltpu.SemaphoreType.DMA(())]),
        out_shape=jax.ShapeDtypeStruct((ns, w), h32.dtype),
        input_output_aliases={2: 0},
        compiler_params=_cparams(('arbitrary',)), name='moe_scatter',
    )(dest_flat, h32, jnp.zeros((ns, w), h32.dtype))


def _moe_up_kernel(te_ref, nv_ref, a_ref, w1_ref, w3_ref, o_ref, w1b, w3b):
    m = pl.program_id(1)

    @pl.when(jnp.logical_or(m == 0, te_ref[m] != te_ref[jnp.maximum(m - 1, 0)]))
    def _():
        w1b[...] = w1_ref[...].astype(BF16)
        w3b[...] = w3_ref[...].astype(BF16)

    @pl.when(m < nv_ref[0])
    def _():
        a = _unpack_halves(a_ref[...])
        h1 = jnp.dot(a, w1b[...], preferred_element_type=F32)
        h3 = jnp.dot(a, w3b[...], preferred_element_type=F32)
        o_ref[...] = (h1 * _sigmoid(h1) * h3).astype(BF16)

    @pl.when(m >= nv_ref[0])
    def _():
        o_ref[...] = jnp.zeros_like(o_ref)


def _moe_down_kernel(te_ref, nv_ref, a_ref, w_ref, o_ref, wb_ref):
    m = pl.program_id(1)

    @pl.when(jnp.logical_or(m == 0, te_ref[m] != te_ref[jnp.maximum(m - 1, 0)]))
    def _():
        wb_ref[...] = w_ref[...].astype(BF16)

    @pl.when(m < nv_ref[0])
    def _():
        o_ref[...] = jnp.dot(a_ref[...], wb_ref[...], preferred_element_type=F32)

    @pl.when(m >= nv_ref[0])
    def _():
        o_ref[...] = jnp.zeros_like(o_ref)


def _moe_grouped_ffn(tile_expert, n_valid, xs, w1, w3, w2, lyr):
    ns = xs.shape[0]
    n_tiles = ns // MOE_BM
    bn = 1024
    ws = pl.BlockSpec((None, None, D_MODEL, bn), lambda j, m, te, nv: (lyr, te[m], 0, j))
    u = pl.pallas_call(
        _moe_up_kernel,
        grid_spec=pltpu.PrefetchScalarGridSpec(
            num_scalar_prefetch=2, grid=(FFN_DIM // bn, n_tiles),
            in_specs=[pl.BlockSpec((MOE_BM, D_MODEL // 2), lambda j, m, te, nv: (m, 0)), ws, ws],
            out_specs=pl.BlockSpec((MOE_BM, bn), lambda j, m, te, nv: (m, j)),
            scratch_shapes=[pltpu.VMEM((D_MODEL, bn), BF16), pltpu.VMEM((D_MODEL, bn), BF16)]),
        out_shape=jax.ShapeDtypeStruct((ns, FFN_DIM), BF16),
        compiler_params=_cparams(('arbitrary', 'arbitrary')), name='moe_up',
    )(tile_expert, n_valid, xs, w1, w3)
    bn = 512
    return pl.pallas_call(
        _moe_down_kernel,
        grid_spec=pltpu.PrefetchScalarGridSpec(
            num_scalar_prefetch=2, grid=(D_MODEL // bn, n_tiles),
            in_specs=[pl.BlockSpec((MOE_BM, FFN_DIM), lambda j, m, te, nv: (m, 0)),
                      pl.BlockSpec((None, None, FFN_DIM, bn), lambda j, m, te, nv: (lyr, te[m], 0, j))],
            out_specs=pl.BlockSpec((MOE_BM, bn), lambda j, m, te, nv: (m, j)),
            scratch_shapes=[pltpu.VMEM((FFN_DIM, bn), BF16)]),
        out_shape=jax.ShapeDtypeStruct((ns, D_MODEL), F32),
        compiler_params=_cparams(('arbitrary', 'arbitrary')), name='moe_down',
    )(tile_expert, n_valid, u, w2)


def _moe_combine_kernel(dest_ref, ys_hbm, r_ref, x_ref, g_ref, m_ref, o_ref, buf, sem, *, bm, kg):
    base = pl.program_id(0) * bm

    def issue(r, carry):
        t = base + r
        for kk in range(TOP_K):
            _row_copy(ys_hbm.at[pl.ds(dest_ref[TOP_K * t + kk], 1)], buf.at[kk, pl.ds(r, 1)], sem).start()
        return carry
    lax.fori_loop(0, bm, issue, 0)

    def drain(r, carry):
        for kk in range(TOP_K):
            _row_copy(ys_hbm.at[pl.ds(0, 1)], buf.at[0, pl.ds(0, 1)], sem).wait()
        return carry
    lax.fori_loop(0, bm, drain, 0)

    r = r_ref[...]
    f = r[:, RT_G1:RT_G1 + 1] * buf[0] + r[:, RT_G2:RT_G2 + 1] * buf[1]
    n = f * lax.rsqrt(jnp.mean(f * f, axis=-1, keepdims=True) + EPS) * g_ref[...]
    o_ref[...] = x_ref[...] + m_ref[0][kg:kg + 1] * n


def _moe_combine(dest_flat, ys, route, x, g, mods, kg, t, bm):
    tt = x.shape[0]
    nlt = t // bm
    xs = pl.BlockSpec((bm, D_MODEL), lambda i, d: (i, 0))
    return pl.pallas_call(
        functools.partial(_moe_combine_kernel, bm=bm, kg=kg),
        grid_spec=pltpu.PrefetchScalarGridSpec(
            num_scalar_prefetch=1, grid=(tt // bm,),
            in_specs=[pl.BlockSpec(memory_space=pl.ANY),
                      pl.BlockSpec((bm, LANE), lambda i, d: (i, 0)), xs,
                      pl.BlockSpec((1, D_MODEL), lambda i, d: (0, 0)),
                      pl.BlockSpec((1, 6, D_MODEL), lambda i, d: (jnp.where(i >= nlt, 1, 0), 0, 0))],
            out_specs=xs,
            scratch_shapes=[pltpu.VMEM((TOP_K, bm, D_MODEL), F32), pltpu.SemaphoreType.DMA(())]),
        out_shape=jax.ShapeDtypeStruct((tt, D_MODEL), F32),
        compiler_params=_cparams(('arbitrary',)), name='moe_combine',
    )(dest_flat, ys, route, x, g.reshape(1, D_MODEL), mods)


def _moe_layer(x, h32, route, w1, w3, w2, lyr, g, mods, kg, t):
    tt = x.shape[0]
    bm = _row_tile(t, tt - t)
    dest, cnt = _moe_rank(route, bm)
    n_tiles = -(-TOP_K * tt // MOE_BM) + N_EXPERTS
    ns = n_tiles * MOE_BM
    cum = jnp.cumsum((cnt[0, 0:N_EXPERTS].astype(jnp.int32) + (MOE_BM - 1)) // MOE_BM)
    n_valid = cum[-1]
    tile = jnp.arange(n_tiles, dtype=jnp.int32)
    tile_expert = jnp.searchsorted(cum, jnp.minimum(tile, n_valid - 1), side='right').astype(jnp.int32)
    dest_flat = dest[:, 0:TOP_K].reshape(-1)
    xs32 = _moe_scatter(dest_flat, h32, ns, bm)
    ys = _moe_grouped_ffn(tile_expert, n_valid.reshape(1), xs32, w1, w3, w2, lyr)
    return _moe_combine(dest_flat, ys, route, x, g, mods, kg, t, bm)


def _rwkv_kernel(d, cur_ref, prv_ref, nxt_ref, sm_ref, mu_ref, w0_ref, wup_ref, a0_ref, aup_ref,
                 kk_ref, ka_ref, rk_ref, e_ref, tri_ref, y_ref, bonus_ref, s_ref, *, nl, nc):
    n = pl.program_id(0)
    ch = _chunk_of(d, n, nl, nc)

    @pl.when(n == 0)
    def _():
        s_ref[...] = jnp.zeros_like(s_ref)

    x = cur_ref[...]
    first = jnp.logical_or(ch == 0, ch == nl)
    last = jnp.logical_or(ch == nl - 1, ch == nl + nc - 1)
    p_row = jnp.where(first, 0.0, prv_ref[7:8, :])
    n_row = jnp.where(last, 0.0, nxt_ref[0:1, :])
    row = lax.broadcasted_iota(jnp.int32, (CHUNK, 1), 0)
    prev = jnp.where(row == 0, p_row, pltpu.roll(x, 1, axis=0))
    nxt = jnp.where(row == CHUNK - 1, n_row, pltpu.roll(x, CHUNK - 1, axis=0))
    z = x + mu_ref[0:1, :] * (prev - x) + mu_ref[1:2, :] * (nxt - x)
    r, k, v = z[:, 0:MIX_W], z[:, MIX_W:2 * MIX_W], z[:, 2 * MIX_W:3 * MIX_W]

    sm = sm_ref[...]
    e_bf = e_ref[...]
    wd = jnp.tanh(sm[:, 0:128])
    w_log = -_softplus(-(w0_ref[...] + _mm3(wd, wup_ref[...]))) - 0.5
    logw = -jnp.exp(w_log)
    a = _sigmoid(a0_ref[...] + _mm3(sm[:, 128:256], aup_ref[...]))
    kkr = k * kk_ref[...]
    kk = kkr * lax.rsqrt(jnp.maximum(_mm_xw(kkr * kkr, e_bf), 1e-12))
    k_dir = k * (1.0 + (a - 1.0) * ka_ref[...])
    bonus_ref[...] = _mm_xw(r * k_dir * rk_ref[...], e_bf) * v

    m_strict = tri_ref[0]
    m_incl = tri_ref[1]
    eye = m_incl - m_strict
    b_inc = _mm_wx(m_incl[0:CHUNK, 0:CHUNK].astype(BF16), logw)
    b_exc = b_inc - logw
    b_last = jnp.sum(logw, axis=0, keepdims=True)
    beta = kk * a
    ea = -kk * jnp.exp(b_exc)
    er = r * jnp.exp(b_inc)
    ninv = jnp.exp(-b_inc)
    eb = beta * ninv
    ek = k_dir * ninv
    eend = jnp.exp(b_last - b_inc)
    hb = beta * eend
    hk = k_dir * eend
    gam = jnp.exp(b_last)

    gw = RW_GROUP * RW_HEAD
    lane_head = lax.broadcasted_iota(jnp.int32, (1, gw), 1) >> 6

    def spread(xg):
        return jnp.concatenate([jnp.where(lane_head == h, xg, 0.0) for h in range(RW_GROUP)], axis=0)

    groups = range(RW_HEADS // RW_GROUP)
    pre = []
    yield None
    for g in groups:
        sl = slice(g * gw, (g + 1) * gw)
        la, lr = spread(ea[:, sl]), spread(er[:, sl])
        rb, rk = spread(eb[:, sl]), spread(ek[:, sl])
        vb = spread(v[:, sl]).astype(BF16)
        amat = _mm_nt(jnp.concatenate([la, lr], axis=0), jnp.concatenate([rb, rk], axis=0))
        m_ab = amat[0:gw, 0:gw] * m_strict
        rhs = jnp.concatenate([la, _mm(amat[0:gw, gw:] * m_strict, vb)], axis=1)
        n_rb = (amat[gw:, 0:gw] * m_incl).astype(BF16)
        n_rk = (amat[gw:, gw:] * m_incl).astype(BF16)
        pre.append((lr, vb, m_ab, rhs, n_rb, n_rk))
        yield None

    def finish(g, tinv):
        sl = slice(g * gw, (g + 1) * gw)
        lr, vb, _, rhs, n_rb, n_rk = pre[g]
        xs_bf = _mm(tinv, rhs).astype(BF16)
        yield
        qy = _mm(n_rb, xs_bf)
        q_hat = lr + qy[:, 0:gw]
        y_loc = qy[:, gw:] + _mm(n_rk, vb)
        yield
        hb_bd = spread(hb[:, sl]).astype(BF16)
        gbt = _mm_tn(xs_bf, hb_bd)
        g_bot = gbt[gw:] + _mm_tn(vb, spread(hk[:, sl]))
        yield
        s0 = s_ref[g]
        y_bd = _mm_nt(q_hat, s0) + y_loc
        y_ref[:, sl] = (y_bd[0:CHUNK] + y_bd[CHUNK:2 * CHUNK]
                        + y_bd[2 * CHUNK:3 * CHUNK] + y_bd[3 * CHUNK:4 * CHUNK])
        s_ref[g] = s0 * gam[:, sl] + _mm(s0, gbt[0:gw]) + g_bot

    tinvs = yield [p_[2] for p_ in pre], eye
    chains = [finish(g, tinvs[g]) for g in groups]
    while chains:
        yield None
        chains = [c for c in chains if next(c, _DONE) is not _DONE]


def _unit_lower_inverses(ms, eye):
    tinvs = [eye + m for m in ms]
    for _ in range(5):
        ms = [_mm(m, m) for m in ms]
        tinvs = [tinv + _mm(m, tinv) for m, tinv in zip(ms, tinvs)]
    return tinvs


def _rwkv_order_masks():
    i = jnp.arange(RW_GROUP * RW_HEAD)
    same = (i[:, None] // CHUNK) == (i[None, :] // CHUNK)
    diff = (i[:, None] % CHUNK) - (i[None, :] % CHUNK)
    per_dir = [jnp.stack([same & (sg * diff > 0), same & (sg * diff >= 0)]) for sg in (1, -1)]
    return jnp.stack(per_dir).astype(F32)


def _gla_kernel(d, qk_ref, v_ref, ad_ref, cs_ref, aup_ref, ab_ref, o_ref, s_ref):
    @pl.when(pl.program_id(0) == 0)
    def _():
        s_ref[...] = jnp.zeros_like(s_ref)

    qk = qk_ref[...]
    v = v_ref[...]
    w_qk = 2 * GLA_HEADS * GLA_DK
    lane = lax.broadcasted_iota(jnp.int32, (CHUNK, w_qk), 1)
    partner = jnp.where((lane & 1) == 0, pltpu.roll(qk, w_qk - 1, axis=1), pltpu.roll(qk, 1, axis=1))
    qk = qk * cs_ref[:, 0:w_qk] + partner * cs_ref[:, w_qk:]
    hk = GLA_HEADS * GLA_DK
    q = qk[:, 0:hk] * (GLA_DK ** -0.5)
    k = qk[:, hk:]
    g = -_softplus(-(_mm3(ad_ref[...], aup_ref[...]) + ab_ref[...])) / GLA_TAU
    _, incl = _order_masks(d)
    b = _mm_wx(jnp.where(incl, 1.0, 0.0).astype(BF16), g)
    b_last = jnp.sum(g, axis=0, keepdims=True)
    q_e = q * jnp.exp(b)
    k_e = k * jnp.exp(-b)
    k_end = k * jnp.exp(b_last - b)
    dec = jnp.exp(b_last)
    for h in range(GLA_HEADS):
        yield None
        sk = slice(h * GLA_DK, (h + 1) * GLA_DK)
        sv = slice(h * GLA_DV, (h + 1) * GLA_DV)
        att = jnp.where(incl, _mm_nt(q_e[:, sk], k_e[:, sk]), 0.0)
        st = s_ref[h]
        o_ref[:, sv] = _mm(att, v[:, sv]) + _mm_nt(q_e[:, sk], st)
        s_ref[h] = st * dec[:, sk] + _mm_tn(v[:, sv], k_end[:, sk])


_DONE = object()
N_SCAN_SHARED = 5
N_SCAN_DIR = 11
SCAN_COLS = 3 * MIX_W + 3 * LANE + 2 * GLA_HEADS * GLA_DK + 2 * MIX_W + LANE


def _scan_kernel(*refs, nl, nc):
    mu, kk, ka, rk, e64 = refs[0:N_SCAN_SHARED]
    n_in = N_SCAN_SHARED + 2 * N_SCAN_DIR
    c0 = 3 * MIX_W
    c1 = c0 + 3 * LANE
    c2 = c1 + 2 * GLA_HEADS * GLA_DK
    rws, glas = [], []
    for d in range(2):
        pm, prv, nxt, cs, w0, wup, a0, aup, tri, g_aup, g_ab = refs[N_SCAN_SHARED + d * N_SCAN_DIR:
                                                                   N_SCAN_SHARED + (d + 1) * N_SCAN_DIR]
        y, bonus, o = refs[n_in + 3 * d:n_in + 3 * d + 3]
        s_rw, s_gla = refs[n_in + 6 + 2 * d:n_in + 8 + 2 * d]
        rws.append(_rwkv_kernel(d, pm.at[:, 0:c0], prv, nxt, pm.at[:, c0:c1], mu, w0, wup, a0, aup, kk, ka, rk,
                                e64, tri, y, bonus, s_rw, nl=nl, nc=nc))
        glas.append(_gla_kernel(d, pm.at[:, c1:c2], pm.at[:, c2:c2 + MIX_W], pm.at[:, SCAN_COLS - LANE:SCAN_COLS],
                                cs, g_aup, g_ab, o, s_gla))
    asked = [None, None]
    while None in asked:
        for d in range(2):
            if asked[d] is None:
                asked[d] = next(rws[d])
        glas = [c for c in glas if next(c, _DONE) is not _DONE]
    n_grp = len(asked[0][0])
    tinvs = _unit_lower_inverses(asked[0][0] + asked[1][0], asked[0][1])
    for d in range(2):
        rws[d].send(tinvs[d * n_grp:(d + 1) * n_grp])
    chains = rws + glas
    while chains:
        chains = [c for c in chains if next(c, _DONE) is not _DONE]


def _scans(p, t, lc, shared, per_dir):
    tt = t + lc
    gw = RW_GROUP * RW_HEAD
    nl, nc = t // CHUNK, lc // CHUNK
    last8 = tt // 8 - 1
    full = lambda a: pl.BlockSpec(a.shape, lambda n: (0,) * a.ndim)
    in_specs = [full(a) for a in shared]
    args = list(shared)
    out_specs = []
    for d in range(2):
        ch = functools.partial(_chunk_of, d, nl=nl, nc=nc)
        rope, *params = per_dir[d]
        in_specs += [
            pl.BlockSpec((CHUNK, SCAN_COLS), lambda n, ch=ch: (ch(n), 0)),
            pl.BlockSpec((8, 3 * MIX_W), lambda n, ch=ch: (jnp.maximum(ch(n) * 8 - 1, 0), 0)),
            pl.BlockSpec((8, 3 * MIX_W), lambda n, ch=ch: (jnp.minimum(ch(n) * 8 + 8, last8), 0)),
            pl.BlockSpec((CHUNK, rope.shape[1]), lambda n, ch=ch: (ch(n), 0)),
        ] + [full(a) for a in params]
        args += [p, p, p, rope] + params
        out_specs += [pl.BlockSpec((CHUNK, MIX_W), lambda n, ch=ch: (ch(n), 0))] * 3
    assert len(in_specs) == N_SCAN_SHARED + 2 * N_SCAN_DIR
    return pl.pallas_call(
        functools.partial(_scan_kernel, nl=nl, nc=nc),
        grid=(nl + nc,),
        in_specs=in_specs, out_specs=out_specs,
        out_shape=[jax.ShapeDtypeStruct((tt, MIX_W), F32)] * 6,
        scratch_shapes=[pltpu.VMEM((RW_HEADS // RW_GROUP, gw, gw), F32),
                        pltpu.VMEM((GLA_HEADS, GLA_DV, GLA_DK), F32)] * 2,
        compiler_params=_cparams(('arbitrary',)), name='scans',
    )(*args)


def _head_norm(y, e_bf, width, eps):
    mu = _mm_xw(y, e_bf) * (1.0 / width)
    dl = y - mu
    var = _mm_xw(dl * dl, e_bf) * (1.0 / width)
    return dl * lax.rsqrt(var + eps)


def _mix_finish_kernel(y0_ref, y1_ref, b0_ref, b1_ref, sm_ref, gup_ref, lng_ref, lnb_ref, e64_ref,
                       o0_ref, o1_ref, gr0_ref, gr1_ref, gr2_ref, gr3_ref, gng_ref, e128_ref, a_ref, b_ref):
    yn = _head_norm(y0_ref[...] + y1_ref[...], e64_ref[...], RW_HEAD, RW_GN_EPS)
    yn = yn * lng_ref[...] + lnb_ref[...] + b0_ref[...] + b1_ref[...]
    gate = _mm(_sigmoid(sm_ref[:, 256:384]), gup_ref[...])
    a_ref[...] = (yn * gate).astype(BF16)
    on = _head_norm(o0_ref[...] + o1_ref[...], e128_ref[...], GLA_DV, GN_EPS) * gng_ref[...]
    gr = jnp.concatenate([gr0_ref[...], gr1_ref[...], gr2_ref[...], gr3_ref[...]], axis=1)
    b_ref[...] = (on * (gr * _sigmoid(gr))).astype(BF16)


def _mix_finish(p, rw_y, rw_bonus, gla_o, g_up, ln_g, ln_b, gn_g, e64, e128, t):
    tt = p.shape[0]
    bm = _row_tile(t, tt - t)
    vec = pl.BlockSpec((1, MIX_W), lambda i: (0, 0))
    mat = pl.BlockSpec((MIX_W, MIX_W), lambda i: (0, 0))
    out = pl.BlockSpec((bm, MIX_W), lambda i: (i, 0))
    return pl.pallas_call(
        _mix_finish_kernel,
        grid=(tt // bm,),
        in_specs=([out, out, out, out,
                   pl.BlockSpec((bm, 3 * LANE), lambda i: (i, CB384_RW_SMALL)),
                   pl.BlockSpec((RW_GATE_RANK, MIX_W), lambda i: (0, 0)), vec, vec, mat,
                   out, out]
                  + [pl.BlockSpec((bm, LANE), functools.partial(lambda i, c: (i, c), c=CB128_GLA_R + q))
                     for q in range(4)]
                  + [vec, mat]),
        out_specs=[out, out],
        out_shape=[jax.ShapeDtypeStruct((tt, MIX_W), BF16)] * 2,
        compiler_params=_cparams(('parallel',)), name='mix_finish',
    )(*rw_y, *rw_bonus, p, g_up, ln_g, ln_b, e64, *gla_o, p, p, p, p, gn_g, e128)


def _sgu_kernel(u_ref, v_ref, lng_ref, lnb_ref, ws_ref, bs_ref, o_ref):
    u = _gelu(u_ref[...])
    v = _gelu(v_ref[...])
    mu = jnp.mean(v, axis=-1, keepdims=True)
    dl = v - mu
    var = jnp.mean(dl * dl, axis=-1, keepdims=True)
    vn = (dl * lax.rsqrt(var + GN_EPS) * lng_ref[...] + lnb_ref[...]).astype(BF16)
    lane = lax.broadcasted_iota(jnp.int32, (1, MIX_W), 1)
    s = bs_ref[...]
    for g in range(SGU_GROUPS):
        s = s + jnp.where((lane >> 6) == g, jnp.dot(ws_ref[g].astype(BF16), vn, preferred_element_type=F32), 0.0)
    o_ref[...] = (u * s).astype(BF16)


def _sgu(p, ln_g, ln_b, w_s, b_full):
    tt = p.shape[0]
    vec = pl.BlockSpec((1, MIX_W), lambda i: (0, 0))
    return pl.pallas_call(
        _sgu_kernel,
        grid=(tt // SGU_CHUNK,),
        in_specs=[pl.BlockSpec((SGU_CHUNK, MIX_W), lambda i: (i, CB_SGU_U)),
                  pl.BlockSpec((SGU_CHUNK, MIX_W), lambda i: (i, CB_SGU_V)), vec, vec,
                  pl.BlockSpec((SGU_GROUPS, SGU_CHUNK, SGU_CHUNK), lambda i: (0, 0, 0)),
                  pl.BlockSpec((SGU_CHUNK, MIX_W), lambda i: (0, 0))],
        out_specs=pl.BlockSpec((SGU_CHUNK, MIX_W), lambda i: (i, 0)),
        out_shape=jax.ShapeDtypeStruct((tt, MIX_W), BF16),
        compiler_params=_cparams(('parallel',)), name='sgu',
    )(p, p, ln_g, ln_b, w_s, b_full)


def _na_bias_kernel(rpb_ref, o_ref, *, wh):
    h = pl.program_id(0)
    n_dc = 2 * NA_WIN_W - 1
    n_dr = 2 * NA_WIN_H - 1
    shape = (GRID_W, 2 * GRID_W)
    c = lax.broadcasted_iota(jnp.int32, shape, 0)
    lane = lax.broadcasted_iota(jnp.int32, shape, 1)
    x = lane & (GRID_W - 1)
    dc = jnp.clip(x - c + (NA_WIN_W - 1), 0, 2 * NA_WIN_W - 2)
    key = (lane >> 6) * n_dc + dc
    cs = jnp.clip(c - NA_WIN_W // 2, 0, GRID_W - NA_WIN_W)
    ok = jnp.logical_and(x >= cs, x < cs + NA_WIN_W)
    pairs = []
    for dr in range(n_dr - 1):
        base = h * (n_dr * n_dc) + dr * n_dc
        tile = lax.fori_loop(0, 2 * n_dc, lambda j, acc: jnp.where(key == j, rpb_ref[base + j], acc),
                             jnp.zeros(shape, F32))
        pairs.append(jnp.where(ok, tile, NEG_INF))
    for dr0 in range(n_dr - wh + 1):
        for jj in range(wh // 2):
            o_ref[dr0, 0, :, jj * 2 * GRID_W:(jj + 1) * 2 * GRID_W] = pairs[dr0 + 2 * jj]


def _na_bias(rpb, wh):
    n_dr0 = 2 * NA_WIN_H - wh
    return pl.pallas_call(
        functools.partial(_na_bias_kernel, wh=wh),
        grid=(NA_HEADS,),
        in_specs=[pl.BlockSpec(memory_space=pltpu.SMEM)],
        out_specs=pl.BlockSpec((n_dr0, 1, GRID_W, wh * GRID_W), lambda h: (0, h, 0, 0)),
        out_shape=jax.ShapeDtypeStruct((n_dr0, NA_HEADS, GRID_W, wh * GRID_W), F32),
        compiler_params=_cparams(('parallel',)), name='na_bias',
    )(rpb.reshape(-1))


def _na_kernel(*refs, wh):
    q_ref = refs[0]
    k_refs = refs[1:1 + wh]
    v_refs = refs[1 + wh:1 + 2 * wh]
    kc_ref, vc_ref, bias_ref, o_ref = refs[1 + 2 * wh:]
    q = q_ref[...] * (NA_HEAD ** -0.5)
    kw = jnp.concatenate([r[...].astype(BF16) for r in k_refs], axis=0)
    vw = jnp.concatenate([r[...].astype(BF16) for r in v_refs], axis=0)
    kc = kc_ref[...].astype(BF16)
    vc = vc_ref[...].astype(BF16)
    lane = lax.broadcasted_iota(jnp.int32, (1, LANE), 1)
    heads = [(pr, s) for pr in range(NA_HEADS // 2) for s in range(2)]
    slab = lambda x, pr: x[:, pr * LANE:(pr + 1) * LANE]
    scores = []
    for pr, s in heads:
        qh = jnp.where((lane >> 6) == s, slab(q, pr), 0.0).astype(BF16)
        scores.append((_mm_nt(qh, slab(kw, pr)) + bias_ref[0, 2 * pr + s], _mm_nt(qh, slab(kc, pr))))
    probs = []
    for sw, sc in scores:
        m = jnp.maximum(jnp.max(sw, axis=-1, keepdims=True), jnp.max(sc, axis=-1, keepdims=True))
        ew = jnp.exp(sw - m)
        ec = jnp.exp(sc - m)
        den = jnp.sum(ew, axis=-1, keepdims=True) + jnp.sum(ec, axis=-1, keepdims=True)
        probs.append((ew, ec, den))
    outs = [jnp.zeros((GRID_W, LANE), F32)] * (NA_HEADS // 2)
    for (pr, s), (ew, ec, den) in zip(heads, probs):
        oh = (_mm(ew, slab(vw, pr)) + _mm(ec, slab(vc, pr))) / den
        outs[pr] = jnp.where((lane >> 6) == s, oh, outs[pr])
    o_ref[...] = jnp.concatenate(outs, axis=1).astype(BF16)


def _na_ctx_kernel(q_ref, k_ref, v_ref, o_ref):
    q = q_ref[...] * (NA_HEAD ** -0.5)
    kc = k_ref[...].astype(BF16)
    vc = v_ref[...].astype(BF16)
    lane = lax.broadcasted_iota(jnp.int32, (1, MIX_W), 1)
    o = jnp.zeros(q.shape, F32)
    for h in range(NA_HEADS):
        hm = (lane >> 6) == h
        s = _mm_nt(jnp.where(hm, q, 0.0), kc)
        e = jnp.exp(s - jnp.max(s, axis=-1, keepdims=True))
        o = jnp.where(hm, _mm(e, vc) / jnp.sum(e, axis=-1, keepdims=True), o)
    o_ref[...] = o.astype(BF16)


def _na(p, bias, t, lc):
    rows = t // GRID_W
    wh = min(NA_WIN_H, rows)
    rs = lambda r: jnp.clip(r - wh // 2, 0, rows - wh)
    ctx_blk = t // lc
    kv = lambda cb: [pl.BlockSpec((GRID_W, MIX_W), functools.partial(lambda r, w, cb: (rs(r) + w, cb), w=w, cb=cb))
                     for w in range(wh)]
    lat = pl.pallas_call(
        functools.partial(_na_kernel, wh=wh),
        grid=(rows,),
        in_specs=([pl.BlockSpec((GRID_W, MIX_W), lambda r: (r, CB_NA_Q))] + kv(CB_NA_K) + kv(CB_NA_V)
                  + [pl.BlockSpec((lc, MIX_W), lambda r: (ctx_blk, CB_NA_K)),
                     pl.BlockSpec((lc, MIX_W), lambda r: (ctx_blk, CB_NA_V)),
                     pl.BlockSpec((1, NA_HEADS, GRID_W, wh * GRID_W),
                                  lambda r: (rs(r) - r + (NA_WIN_H - 1), 0, 0, 0))]),
        out_specs=pl.BlockSpec((GRID_W, MIX_W), lambda r: (r, 0)),
        out_shape=jax.ShapeDtypeStruct((t, MIX_W), BF16),
        compiler_params=_cparams(('parallel',)), name='na_latent',
    )(*([p] * (1 + 2 * wh + 2)), bias)
    cblk = lambda cb: pl.BlockSpec((lc, MIX_W), lambda i: (ctx_blk, cb))
    ctx = pl.pallas_call(
        _na_ctx_kernel,
        grid=(1,),
        in_specs=[cblk(CB_NA_Q), cblk(CB_NA_K), cblk(CB_NA_V)],
        out_specs=pl.BlockSpec((lc, MIX_W), lambda i: (0, 0)),
        out_shape=jax.ShapeDtypeStruct((lc, MIX_W), BF16),
        compiler_params=_cparams(('arbitrary',)), name='na_ctx',
    )(p, p, p)
    return jnp.concatenate([lat, ctx], axis=0)


def _merge_kernel(a0, a1, a2, a3, g0, g1, g2, g3, w_ref, o_ref, wb_ref):
    @pl.when(pl.program_id(1) == 0)
    def _():
        wb_ref[...] = w_ref[...].astype(BF16)
    acc = None
    for n, (a_ref, g_ref) in enumerate(((a0, g0), (a1, g1), (a2, g2), (a3, g3))):
        zn = jnp.dot(a_ref[...], wb_ref[n], preferred_element_type=F32) * _sigmoid(g_ref[...])
        acc = zn if acc is None else acc + zn
    o_ref[...] = acc.astype(BF16)


def _merge(ys, p, w_br, lyr):
    tt = p.shape[0]
    bn = 512
    bm = _pick(tt, (768, 384, 128))
    a_s = pl.BlockSpec((bm, MIX_W), lambda j, i: (i, 0))
    gs = [pl.BlockSpec((bm, bn), functools.partial(lambda j, i, n: (i, (GATE_OFF + n * D_MODEL) // bn + j), n=n))
          for n in range(N_BRANCH)]
    return pl.pallas_call(
        _merge_kernel,
        grid=(D_MODEL // bn, tt // bm),
        in_specs=[a_s] * 4 + gs + [pl.BlockSpec((None, N_BRANCH, MIX_W, bn), lambda j, i: (lyr, 0, 0, j))],
        out_specs=pl.BlockSpec((bm, bn), lambda j, i: (i, j)),
        out_shape=jax.ShapeDtypeStruct((tt, D_MODEL), BF16),
        scratch_shapes=[pltpu.VMEM((N_BRANCH, MIX_W, bn), BF16)],
        compiler_params=_cparams(('arbitrary', 'arbitrary')), name='merge',
    )(*ys, p, p, p, p, w_br)


def _block_diag_ones(width):
    i = jnp.arange(MIX_W) // width
    return (i[:, None] == i[None, :]).astype(BF16)


def _rope_tables(t, lc):
    tok = jnp.arange(t)
    pos = jnp.stack([tok // GRID_W, tok % GRID_W], axis=-1).astype(F32)
    nf = GLA_DK // 4
    inv = ROPE_BASE ** (-jnp.arange(nf, dtype=F32) / nf)
    ang = pos[:, :, None] * inv
    cos = jnp.repeat(jnp.cos(ang), 2, axis=-1).reshape(t, GLA_DK)
    sin = jnp.sin(ang)
    sin = jnp.stack([-sin, sin], axis=-1).reshape(t, GLA_DK)
    reps = 2 * GLA_HEADS
    cos = jnp.concatenate([jnp.tile(cos, (1, reps)), jnp.ones((lc, reps * GLA_DK), F32)], axis=0)
    sin = jnp.concatenate([jnp.tile(sin, (1, reps)), jnp.zeros((lc, reps * GLA_DK), F32)], axis=0)
    return jnp.concatenate([cos, sin], axis=1)


def _pad_rank_rows(w_up, rank, rows):
    out = jnp.zeros((2, rows, w_up.shape[-1]), w_up.dtype)
    for d in range(2):
        out = out.at[d, d * rank:(d + 1) * rank].set(w_up[d])
    return out


def kernel(x, c, ctx, c_ctx, ada_w, ada_b, norm_g, w_in, rw_mu, rw_w0, rw_w_up, rw_a0, rw_a_up, rw_g_up, rw_k_k, rw_k_a, rw_r_k, rw_ln_g, rw_ln_b, gla_a_up, gla_a_b, gla_gn_g, sgu_ln_g, sgu_ln_b, sgu_w, sgu_b, na_rpb, w_br, w_o, ffn_w1, ffn_w3, ffn_w2, moe_router, moe_w1, moe_w3, moe_w2):
    assert x.shape[0] == 1 and x.shape[2] == D_MODEL
    t, lc = x.shape[1], ctx.shape[1]
    depth = ada_w.shape[0]
    assert t % max(lc, SGU_CHUNK) == 0 and lc % SGU_CHUNK == 0 and t % GRID_W == 0
    rows = t // GRID_W
    assert rows >= NA_WIN_H
    wh = NA_WIN_H
    xs = jnp.concatenate([x[0], ctx[0]], axis=0)
    cond8 = jnp.zeros((8, D_MODEL), F32).at[0].set(c[0]).at[1].set(c_ctx)
    e64 = _block_diag_ones(RW_HEAD)
    e128 = _block_diag_ones(GLA_DV)
    rope = _rope_tables(t, lc)
    tri = _rwkv_order_masks()
    row1 = lambda v: v.reshape(1, -1)
    assert w_in.shape[2] == N_IN

    mods_all = [_adaln(cond8, ada_w, ada_b, i)[0:2].reshape(2, 6, D_MODEL) for i in range(depth)]
    h = _norm_mod_call(xs, norm_g[0, 0], mods_all[0], 0, t)
    for i in range(depth):
        mods = mods_all[i]
        last = i == depth - 1
        p = _in_proj(h, w_in, i)
        mu2 = jnp.stack([rw_mu[i, :, 0].reshape(-1), rw_mu[i, :, 1].reshape(-1)])
        w_up = _pad_rank_rows(rw_w_up[i], RW_DECAY_RANK, LANE)
        a_up = _pad_rank_rows(rw_a_up[i], RW_ICLR_RANK, LANE)
        g_up = _pad_rank_rows(gla_a_up[i], GLA_GATE_RANK, LANE)
        per_dir = [[rope, row1(rw_w0[i, d]), w_up[d], row1(rw_a0[i, d]), a_up[d], tri[d], g_up[d],
                    row1(gla_a_b[i, d])] for d in range(2)]
        y0, b0, o0, y1, b1, o1 = _scans(p, t, lc, [mu2, row1(rw_k_k[i]), row1(rw_k_a[i]), row1(rw_r_k[i]), e64],
                                        per_dir)
        rw_y, rw_bonus, gla_o = (y0, y1), (b0, b1), (o0, o1)
        y_a, y_b = _mix_finish(p, rw_y, rw_bonus, gla_o, rw_g_up[i], row1(rw_ln_g[i]), row1(rw_ln_b[i]),
                               row1(gla_gn_g[i]), e64, e128, t)
        y_s = _sgu(p, row1(sgu_ln_g[i]), row1(sgu_ln_b[i]), sgu_w[i], jnp.repeat(sgu_b[i].T, 64, axis=1))
        y_d = _na(p, _na_bias(na_rpb[i], wh), t, lc)
        z = _merge((y_a, y_b, y_s, y_d), p, w_br, i)
        y = _matmul(z, w_o, i, 1024, (768, 384, 128), name='out_proj')
        j = i // 2
        if i % 2 == 0:
            xs, h = _resid_next_call(xs, y, norm_g[i, 1], mods, 2, t, norm_g[i, 2], mods, 3)
            f = _ffn_down(_ffn_up(h, ffn_w1, ffn_w3, j), ffn_w2, j)
            if last:
                xs = _resid_call(xs, f, norm_g[i, 3], mods, 5, t)
            else:
                xs, h = _resid_next_call(xs, f, norm_g[i, 3], mods, 5, t, norm_g[i + 1, 0], mods_all[i + 1], 0)
        else:
            xs, h32, route = _resid_next_call(xs, y, norm_g[i, 1], mods, 2, t, norm_g[i, 2], mods, 3,
                                              router=moe_router[j])
            xs = _moe_layer(xs, h32, route, moe_w1, moe_w3, moe_w2, j, norm_g[i, 3], mods, 5, t)
            if not last:
                h = _norm_mod_call(xs, norm_g[i + 1, 0], mods_all[i + 1], 0, t)
    return xs[0:t][None]
```

```python
import functools

import jax
import jax.numpy as jnp
from jax import lax
from jax.experimental import pallas as pl
from jax.experimental.pallas import tpu as pltpu

F32 = jnp.float32
BF16 = jnp.bfloat16

D_MODEL = 2048
GRID_W = 64
N_BRANCH = 4
MIX_W = D_MODEL // 4
RW_HEAD = 64
RW_HEADS = MIX_W // RW_HEAD
RW_GROUP = 4
RW_DECAY_RANK = 64
RW_ICLR_RANK = 64
RW_GATE_RANK = 128
RW_GN_EPS = 64e-5
GLA_HEADS = 4
GLA_DV = MIX_W // GLA_HEADS
GLA_DK = GLA_DV // 2
GLA_GATE_RANK = 16
GLA_TAU = 16.0
GN_EPS = 1e-5
ROPE_BASE = 10000.0
SGU_GROUPS = MIX_W // 64
SGU_CHUNK = 128
NA_HEAD = 64
NA_HEADS = MIX_W // NA_HEAD
NA_WIN_H = 8
NA_WIN_W = 16
FFN_DIM = 7 * D_MODEL // 2
N_EXPERTS = 8
TOP_K = 2
EPS = 1e-6
NEG_INF = -1e30

RT_G1, RT_G2, RT_I1, RT_I2 = 8, 9, 10, 11
MOE_BM = 256
CHUNK = 64
LANE = 128
VMEM_LIMIT = 56 * 2 ** 20

IN_TAIL = 3 * MIX_W + 2 * RW_DECAY_RANK + 2 * RW_ICLR_RANK + RW_GATE_RANK + 2 * GLA_HEADS * GLA_DK \
    + 2 * MIX_W + 2 * GLA_GATE_RANK
IN_PAD = -IN_TAIL % LANE
N_IN = IN_TAIL + 5 * MIX_W + N_BRANCH * D_MODEL
N_IN_P = N_IN + IN_PAD
CB384_RW_SMALL = 4
CB128_GLA_R = 23
CB_SGU_U, CB_SGU_V, CB_NA_Q, CB_NA_K, CB_NA_V = 7, 8, 9, 10, 11
GATE_OFF = 6144


def _cparams(sem):
    return pltpu.CompilerParams(dimension_semantics=sem, vmem_limit_bytes=VMEM_LIMIT)


def _pick(m, cands):
    for c in cands:
        if m % c == 0:
            return c
    raise ValueError(f'no tile for {m}')


def _mm(a, b):
    return jnp.dot(a.astype(BF16), b.astype(BF16), preferred_element_type=F32)


def _mm_nt(a, b):
    return lax.dot_general(a.astype(BF16), b.astype(BF16), (((1,), (1,)), ((), ())),
                           preferred_element_type=F32)


def _mm_tn(a, b):
    return lax.dot_general(a.astype(BF16), b.astype(BF16), (((0,), (0,)), ((), ())),
                           preferred_element_type=F32)


def _split2(x):
    hi = x.astype(BF16)
    lo = (x - hi.astype(F32)).astype(BF16)
    return hi, lo


def _mm_xw(x, w_bf):
    hi, lo = _split2(x)
    return (jnp.dot(hi, w_bf, preferred_element_type=F32)
            + jnp.dot(lo, w_bf, preferred_element_type=F32))


def _mm_wx(w_bf, x):
    hi, lo = _split2(x)
    return (jnp.dot(w_bf, hi, preferred_element_type=F32)
            + jnp.dot(w_bf, lo, preferred_element_type=F32))


def _mm3(a, b):
    ah, al = _split2(a)
    bh, bl = _split2(b)
    return (jnp.dot(ah, bh, preferred_element_type=F32)
            + jnp.dot(ah, bl, preferred_element_type=F32)
            + jnp.dot(al, bh, preferred_element_type=F32))


def _sigmoid(x):
    return 1.0 / (1.0 + jnp.exp(-x))


def _softplus(x):
    return jnp.maximum(x, 0.0) + jnp.log1p(jnp.exp(-jnp.abs(x)))


def _gelu(x):
    return 0.5 * x * (1.0 + lax.erf(x * (0.5 ** 0.5)))


def _order_masks(d):
    t = lax.broadcasted_iota(jnp.int32, (CHUNK, CHUNK), 0)
    s = lax.broadcasted_iota(jnp.int32, (CHUNK, CHUNK), 1)
    diff = (t - s) * jnp.where(d == 0, 1, -1)
    return diff > 0, diff >= 0


def _chunk_of(d, n, nl, nc):
    fwd = jnp.where(n < nc, nl + n, n - nc)
    bwd = jnp.where(n < nc, nl + nc - 1 - n, nl - 1 - (n - nc))
    return jnp.where(d == 0, fwd, bwd)


def _ada_kernel(c_ref, w_ref, b_ref, o_ref):
    cnd = c_ref[...]
    a = cnd * _sigmoid(cnd)
    o_ref[...] = _mm(a, w_ref[...]) + b_ref[...]


def _adaln(cond8, ada_w, ada_b, lyr):
    depth, _, n = ada_w.shape
    bn = 1536
    return pl.pallas_call(
        _ada_kernel,
        grid=(n // bn,),
        in_specs=[pl.BlockSpec((8, D_MODEL), lambda j: (0, 0)),
                  pl.BlockSpec((None, D_MODEL, bn), lambda j: (lyr, 0, j)),
                  pl.BlockSpec((None, 1, bn), lambda j: (lyr, 0, j))],
        out_specs=pl.BlockSpec((8, bn), lambda j: (0, j)),
        out_shape=jax.ShapeDtypeStruct((8, n), F32),
        compiler_params=_cparams(('arbitrary',)),
        name='adaln',
    )(cond8, ada_w, ada_b.reshape(depth, 1, n))


def _norm_mod(x, g, m, k0):
    y = x * lax.rsqrt(jnp.mean(x * x, axis=-1, keepdims=True) + EPS) * g
    return y * (1.0 + m[k0 + 1:k0 + 2]) + m[k0:k0 + 1]


def _pack_halves(h):
    n = h.shape[1] // 2
    bits = lax.bitcast_convert_type(h.astype(BF16).astype(F32), jnp.uint32)
    return (bits[:, 0:n] >> 16) | bits[:, n:]


def _unpack_halves(w):
    lo = lax.bitcast_convert_type(w << 16, F32)
    hi = lax.bitcast_convert_type(w & jnp.uint32(0xFFFF0000), F32)
    return jnp.concatenate([lo, hi], axis=1).astype(BF16)


def _norm_mod_kernel(x_ref, g_ref, m_ref, o_ref, *, k0):
    o_ref[...] = _norm_mod(x_ref[...], g_ref[...], m_ref[0], k0).astype(BF16)


def _route_table(h, router):
    logits = _mm3(h, router)
    lane = lax.broadcasted_iota(jnp.int32, logits.shape, 1)
    neg = jnp.float32(-jnp.inf)
    l1 = jnp.where(lane < N_EXPERTS, logits, neg)
    m1 = jnp.max(l1, axis=-1, keepdims=True)
    i1 = jnp.min(jnp.where(l1 == m1, lane, LANE), axis=-1, keepdims=True)
    l2 = jnp.where(lane == i1, neg, l1)
    m2 = jnp.max(l2, axis=-1, keepdims=True)
    i2 = jnp.min(jnp.where(l2 == m2, lane, LANE), axis=-1, keepdims=True)
    e2 = jnp.exp(m2 - m1)
    den = 1.0 + e2
    sel = jnp.where(jnp.logical_or(lane == i1, lane == i2), 1.0, 0.0)
    sel = jnp.where(lane == RT_G1, 1.0 / den, jnp.where(lane == RT_G2, e2 / den, sel))
    return jnp.where(lane == RT_I1, i1.astype(F32), jnp.where(lane == RT_I2, i2.astype(F32), sel))


def _row_tile(t, lc):
    return _pick(lc, (256, 128))


def _norm_mod_call(x, g, mods, k0, t):
    tt = x.shape[0]
    bm = _row_tile(t, tt - t)
    nlt = t // bm
    xs = pl.BlockSpec((bm, D_MODEL), lambda i: (i, 0))
    gs = pl.BlockSpec((1, D_MODEL), lambda i: (0, 0))
    ms = pl.BlockSpec((1, 6, D_MODEL), lambda i: (jnp.where(i >= nlt, 1, 0), 0, 0))
    return pl.pallas_call(
        functools.partial(_norm_mod_kernel, k0=k0),
        grid=(tt // bm,), in_specs=[xs, gs, ms], out_specs=xs,
        out_shape=jax.ShapeDtypeStruct((tt, D_MODEL), BF16),
        compiler_params=_cparams(('parallel',)), name='norm_mod',
    )(x, g.reshape(1, D_MODEL), mods)


def _resid_kernel(x_ref, y_ref, g_ref, m_ref, o_ref, *, kg):
    y = y_ref[...]
    n = y * lax.rsqrt(jnp.mean(y * y, axis=-1, keepdims=True) + EPS) * g_ref[...]
    o_ref[...] = x_ref[...] + m_ref[0][kg:kg + 1] * n


def _resid_call(x, y, g, mods, kg, t, n_out):
    tt = x.shape[0]
    bm = _row_tile(t, tt - t)
    nlt = t // bm
    xs = pl.BlockSpec((bm, D_MODEL), lambda i: (i, 0))
    return pl.pallas_call(
        functools.partial(_resid_kernel, kg=kg),
        grid=(n_out // bm,),
        in_specs=[xs, xs, pl.BlockSpec((1, D_MODEL), lambda i: (0, 0)),
                  pl.BlockSpec((1, 6, D_MODEL), lambda i: (jnp.where(i >= nlt, 1, 0), 0, 0))],
        out_specs=xs,
        out_shape=jax.ShapeDtypeStruct((n_out, D_MODEL), F32),
        compiler_params=_cparams(('parallel',)), name='resid_norm',
    )(x, y, g.reshape(1, D_MODEL), mods)


def _resid_next_kernel(*refs, kg, k0, route):
    x_ref, y_ref, g_ref, m_ref, g2_ref, m2_ref = refs[0:6]
    y = y_ref[...]
    n = y * lax.rsqrt(jnp.mean(y * y, axis=-1, keepdims=True) + EPS) * g_ref[...]
    xn = x_ref[...] + m_ref[0][kg:kg + 1] * n
    h = _norm_mod(xn, g2_ref[...], m2_ref[0], k0)
    if route:
        r_ref, xo_ref, ho_ref, rt_ref = refs[6:]
        ho_ref[...] = _pack_halves(h)
        rt_ref[...] = _route_table(h, r_ref[...])
    else:
        xo_ref, ho_ref = refs[6:]
        ho_ref[...] = h.astype(BF16)
    xo_ref[...] = xn


def _resid_next_call(x, y, g, mods, kg, t, g2, mods2, k0, router=None):
    tt = x.shape[0]
    bm = _row_tile(t, tt - t)
    nlt = t // bm
    xs = pl.BlockSpec((bm, D_MODEL), lambda i: (i, 0))
    gs = pl.BlockSpec((1, D_MODEL), lambda i: (0, 0))
    ms = pl.BlockSpec((1, 6, D_MODEL), lambda i: (jnp.where(i >= nlt, 1, 0), 0, 0))
    in_specs = [xs, xs, gs, ms, gs, ms]
    args = [x, y, g.reshape(1, D_MODEL), mods, g2.reshape(1, D_MODEL), mods2]
    out_specs = [xs, xs]
    out_shape = [jax.ShapeDtypeStruct((tt, D_MODEL), F32), jax.ShapeDtypeStruct((tt, D_MODEL), BF16)]
    if router is not None:
        in_specs.append(pl.BlockSpec((D_MODEL, LANE), lambda i: (0, 0)))
        args.append(jnp.pad(router, ((0, 0), (0, LANE - N_EXPERTS))))
        out_specs = [xs, pl.BlockSpec((bm, D_MODEL // 2), lambda i: (i, 0)), pl.BlockSpec((bm, LANE), lambda i: (i, 0))]
        out_shape = [out_shape[0], jax.ShapeDtypeStruct((tt, D_MODEL // 2), jnp.uint32),
                     jax.ShapeDtypeStruct((tt, LANE), F32)]
    return pl.pallas_call(
        functools.partial(_resid_next_kernel, kg=kg, k0=k0, route=router is not None),
        grid=(tt // bm,), in_specs=in_specs, out_specs=out_specs, out_shape=out_shape,
        compiler_params=_cparams(('parallel',)), name='resid_next',
    )(*args)


def _mm_kernel(a_ref, w_ref, o_ref, wb_ref):
    @pl.when(pl.program_id(1) == 0)
    def _():
        wb_ref[...] = w_ref[...].astype(BF16)
    o_ref[...] = jnp.dot(a_ref[...], wb_ref[...], preferred_element_type=F32).astype(o_ref.dtype)


def _matmul(a, w, lyr, bn, bm_cands, out_dtype=F32, name='matmul'):
    m, k = a.shape
    n = w.shape[2]
    bm = _pick(m, bm_cands)
    return pl.pallas_call(
        _mm_kernel,
        grid=(n // bn, m // bm),
        in_specs=[pl.BlockSpec((bm, k), lambda j, i: (i, 0)),
                  pl.BlockSpec((None, k, bn), lambda j, i: (lyr, 0, j))],
        out_specs=pl.BlockSpec((bm, bn), lambda j, i: (i, j)),
        out_shape=jax.ShapeDtypeStruct((m, n), out_dtype),
        scratch_shapes=[pltpu.VMEM((k, bn), BF16)],
        compiler_params=_cparams(('arbitrary', 'arbitrary')), name=name,
    )(a, w)


def _in_proj_kernel(a_ref, wt_ref, o_ref, wb_ref, *, bn):
    j = pl.program_id(0)
    straddle = IN_TAIL // bn
    cut = IN_TAIL - straddle * bn

    @pl.when(pl.program_id(1) == 0)
    def _():
        @pl.when(j != straddle)
        def _():
            wb_ref[...] = wt_ref[0].astype(BF16)

        @pl.when(j == straddle)
        def _():
            wb_ref[0:cut + IN_PAD, :] = wt_ref[0, 0:cut + IN_PAD, :].astype(BF16)
            wb_ref[cut + IN_PAD:, :] = wt_ref[0, cut:bn - IN_PAD, :].astype(BF16)

    o_ref[...] = lax.dot_general(a_ref[...], wb_ref[...], (((1,), (1,)), ((), ())), preferred_element_type=F32)


def _in_proj(a, w_in, lyr):
    m = a.shape[0]
    bn = 1024
    bm = _pick(m, (768, 384, 128))
    straddle = IN_TAIL // bn
    assert IN_PAD % 8 == 0 and straddle == (IN_TAIL + IN_PAD - 1) // bn
    w_t = jnp.swapaxes(w_in, 1, 2)
    return pl.pallas_call(
        functools.partial(_in_proj_kernel, bn=bn),
        grid=(N_IN_P // bn, m // bm),
        in_specs=[pl.BlockSpec((bm, D_MODEL), lambda j, i: (i, 0)),
                  pl.BlockSpec((pl.Element(1), pl.Element(bn), pl.Element(D_MODEL)),
                               lambda j, i: (lyr, 8 * (j * (bn // 8) - jnp.where(j > straddle, IN_PAD // 8, 0)), 0))],
        out_specs=pl.BlockSpec((bm, bn), lambda j, i: (i, j)),
        out_shape=jax.ShapeDtypeStruct((m, N_IN_P), F32),
        scratch_shapes=[pltpu.VMEM((bn, D_MODEL), BF16)],
        compiler_params=_cparams(('arbitrary', 'arbitrary')), name='in_proj',
    )(a, w_t)


def _ffn_up_kernel(a_ref, w1_ref, w3_ref, o_ref, w1b, w3b):
    @pl.when(pl.program_id(1) == 0)
    def _():
        w1b[...] = w1_ref[...].astype(BF16)
        w3b[...] = w3_ref[...].astype(BF16)
    a = a_ref[...]
    h1 = jnp.dot(a, w1b[...], preferred_element_type=F32)
    h3 = jnp.dot(a, w3b[...], preferred_element_type=F32)
    o_ref[...] = (h1 * _sigmoid(h1) * h3).astype(BF16)


def _ffn_up(h, w1, w3, e):
    m = h.shape[0]
    bn = 512
    bm = _pick(m, (768, 384, 128))
    ws = pl.BlockSpec((None, D_MODEL, bn), lambda j, i: (e, 0, j))
    return pl.pallas_call(
        _ffn_up_kernel,
        grid=(FFN_DIM // bn, m // bm),
        in_specs=[pl.BlockSpec((bm, D_MODEL), lambda j, i: (i, 0)), ws, ws],
        out_specs=pl.BlockSpec((bm, bn), lambda j, i: (i, j)),
        out_shape=jax.ShapeDtypeStruct((m, FFN_DIM), BF16),
        scratch_shapes=[pltpu.VMEM((D_MODEL, bn), BF16), pltpu.VMEM((D_MODEL, bn), BF16)],
        compiler_params=_cparams(('arbitrary', 'arbitrary')), name='ffn_up',
    )(h, w1, w3)


def _ffn_down_kernel(a_ref, w_ref, o_ref):
    part = jnp.dot(a_ref[...], w_ref[...].astype(BF16), preferred_element_type=F32)

    @pl.when(pl.program_id(1) == 0)
    def _():
        o_ref[...] = part

    @pl.when(pl.program_id(1) > 0)
    def _():
        o_ref[...] += part


def _ffn_down(u, w2, e):
    m = u.shape[0]
    bk = 512
    bm = max(d for d in range(LANE, 1408 + 1, LANE) if m % d == 0)
    return pl.pallas_call(
        _ffn_down_kernel,
        grid=(m // bm, FFN_DIM // bk),
        in_specs=[pl.BlockSpec((bm, bk), lambda i, k: (i, k)),
                  pl.BlockSpec((None, bk, D_MODEL), lambda i, k: (e, k, 0))],
        out_specs=pl.BlockSpec((bm, D_MODEL), lambda i, k: (i, 0)),
        out_shape=jax.ShapeDtypeStruct((m, D_MODEL), F32),
        compiler_params=_cparams(('parallel', 'arbitrary')), name='ffn_down',
    )(u, w2)


def _moe_rank_kernel(r_ref, dest_ref, cnt_ref, carry_ref, tot_ref):
    ph = pl.program_id(0)
    i = pl.program_id(1)
    r = r_ref[...]
    bm = r.shape[0]
    lane = lax.broadcasted_iota(jnp.int32, (1, LANE), 1)
    oh = jnp.where(lane < N_EXPERTS, r, 0.0)
    colsum = jnp.sum(oh, axis=0, keepdims=True)

    @pl.when(jnp.logical_and(ph == 0, i == 0))
    def _():
        carry_ref[...] = jnp.zeros_like(carry_ref)

    @pl.when(jnp.logical_and(ph == 1, i == 0))
    def _():
        tot_ref[...] = carry_ref[...]
        carry_ref[...] = jnp.zeros_like(carry_ref)

    @pl.when(ph == 1)
    def _():
        tot = tot_ref[...]
        padded = jnp.floor((tot + (MOE_BM - 1)) * (1.0 / MOE_BM)) * MOE_BM
        a = lax.broadcasted_iota(jnp.int32, (LANE, LANE), 0)
        b = lax.broadcasted_iota(jnp.int32, (LANE, LANE), 1)
        upper = jnp.where(a < b, 1.0, 0.0).astype(BF16)
        offs = _mm_xw(jnp.broadcast_to(padded, (8, LANE)), upper)[0:1]
        tr = lax.broadcasted_iota(jnp.int32, (bm, bm), 0)
        ts = lax.broadcasted_iota(jnp.int32, (bm, bm), 1)
        before = jnp.dot(jnp.where(ts < tr, 1.0, 0.0).astype(BF16), oh.astype(BF16),
                         preferred_element_type=F32)
        slot = before + carry_ref[...] + offs
        lane_f = lane.astype(F32)
        d1 = jnp.sum(jnp.where(lane_f == r[:, RT_I1:RT_I1 + 1], slot, 0.0), axis=-1, keepdims=True)
        d2 = jnp.sum(jnp.where(lane_f == r[:, RT_I2:RT_I2 + 1], slot, 0.0), axis=-1, keepdims=True)
        dest_ref[...] = jnp.where(lane == 0, d1, jnp.where(lane == 1, d2, 0.0)).astype(jnp.int32)
        cnt_ref[...] = jnp.broadcast_to(tot, (8, LANE))

    carry_ref[...] = carry_ref[...] + colsum


def _moe_rank(route, bm):
    tt = route.shape[0]
    return pl.pallas_call(
        _moe_rank_kernel,
        grid=(2, tt // bm),
        in_specs=[pl.BlockSpec((bm, LANE), lambda ph, i: (i, 0))],
        out_specs=[pl.BlockSpec((bm, LANE), lambda ph, i: (i * ph, 0)),
                   pl.BlockSpec((8, LANE), lambda ph, i: (0, 0))],
        out_shape=[jax.ShapeDtypeStruct((tt, LANE), jnp.int32), jax.ShapeDtypeStruct((8, LANE), F32)],
        scratch_shapes=[pltpu.VMEM((1, LANE), F32), pltpu.VMEM((1, LANE), F32)],
        compiler_params=_cparams(('arbitrary', 'arbitrary')), name='moe_rank',
    )(route)


def _row_copy(src, dst, sem):
    return pltpu.make_async_copy(src, dst, sem)


def _moe_scatter_kernel(dest_ref, h_ref, init_hbm, xs_hbm, sem, *, bm):
    del init_hbm
    base = pl.program_id(0) * bm

    def issue(r, carry):
        t = base + r
        for kk in range(TOP_K):
            _row_copy(h_ref.at[pl.ds(r, 1)], xs_hbm.at[pl.ds(dest_ref[TOP_K * t + kk], 1)], sem).start()
        return carry
    lax.fori_loop(0, bm, issue, 0)

    def drain(r, carry):
        for kk in range(TOP_K):
            _row_copy(h_ref.at[pl.ds(0, 1)], xs_hbm.at[pl.ds(0, 1)], sem).wait()
        return carry
    lax.fori_loop(0, bm, drain, 0)


def _moe_scatter(dest_flat, h32, ns, bm):
    tt, w = h32.shape
    return pl.pallas_call(
        functools.partial(_moe_scatter_kernel, bm=bm),
        grid_spec=pltpu.PrefetchScalarGridSpec(
            num_scalar_prefetch=1, grid=(tt // bm,),
            in_specs=[pl.BlockSpec((bm, w), lambda i, d: (i, 0)), pl.BlockSpec(memory_space=pl.ANY)],
            out_specs=pl.BlockSpec(memory_space=pl.ANY),
            scratch_shapes=[pltpu.SemaphoreType.DMA(())]),
        out_shape=jax.ShapeDtypeStruct((ns, w), h32.dtype),
        input_output_aliases={2: 0},
        compiler_params=_cparams(('arbitrary',)), name='moe_scatter',
    )(dest_flat, h32, jnp.zeros((ns, w), h32.dtype))


def _moe_up_kernel(te_ref, nv_ref, a_ref, w1_ref, w3_ref, o_ref, w1b, w3b):
    m = pl.program_id(1)

    @pl.when(jnp.logical_or(m == 0, te_ref[m] != te_ref[jnp.maximum(m - 1, 0)]))
    def _():
        w1b[...] = w1_ref[...].astype(BF16)
        w3b[...] = w3_ref[...].astype(BF16)

    @pl.when(m < nv_ref[0])
    def _():
        a = _unpack_halves(a_ref[...])
        h1 = jnp.dot(a, w1b[...], preferred_element_type=F32)
        h3 = jnp.dot(a, w3b[...], preferred_element_type=F32)
        o_ref[...] = (h1 * _sigmoid(h1) * h3).astype(BF16)

    @pl.when(m >= nv_ref[0])
    def _():
        o_ref[...] = jnp.zeros_like(o_ref)


def _moe_down_kernel(te_ref, nv_ref, a_ref, w_ref, o_ref, wb_ref):
    m = pl.program_id(1)

    @pl.when(jnp.logical_or(m == 0, te_ref[m] != te_ref[jnp.maximum(m - 1, 0)]))
    def _():
        wb_ref[...] = w_ref[...].astype(BF16)

    @pl.when(m < nv_ref[0])
    def _():
        o_ref[...] = jnp.dot(a_ref[...], wb_ref[...], preferred_element_type=F32)

    @pl.when(m >= nv_ref[0])
    def _():
        o_ref[...] = jnp.zeros_like(o_ref)


def _moe_grouped_ffn(tile_expert, n_valid, xs, w1, w3, w2, lyr):
    ns = xs.shape[0]
    n_tiles = ns // MOE_BM
    bn = 1024
    ws = pl.BlockSpec((None, None, D_MODEL, bn), lambda j, m, te, nv: (lyr, te[m], 0, j))
    u = pl.pallas_call(
        _moe_up_kernel,
        grid_spec=pltpu.PrefetchScalarGridSpec(
            num_scalar_prefetch=2, grid=(FFN_DIM // bn, n_tiles),
            in_specs=[pl.BlockSpec((MOE_BM, D_MODEL // 2), lambda j, m, te, nv: (m, 0)), ws, ws],
            out_specs=pl.BlockSpec((MOE_BM, bn), lambda j, m, te, nv: (m, j)),
            scratch_shapes=[pltpu.VMEM((D_MODEL, bn), BF16), pltpu.VMEM((D_MODEL, bn), BF16)]),
        out_shape=jax.ShapeDtypeStruct((ns, FFN_DIM), BF16),
        compiler_params=_cparams(('arbitrary', 'arbitrary')), name='moe_up',
    )(tile_expert, n_valid, xs, w1, w3)
    bn = 512
    return pl.pallas_call(
        _moe_down_kernel,
        grid_spec=pltpu.PrefetchScalarGridSpec(
            num_scalar_prefetch=2, grid=(D_MODEL // bn, n_tiles),
            in_specs=[pl.BlockSpec((MOE_BM, FFN_DIM), lambda j, m, te, nv: (m, 0)),
                      pl.BlockSpec((None, None, FFN_DIM, bn), lambda j, m, te, nv: (lyr, te[m], 0, j))],
            out_specs=pl.BlockSpec((MOE_BM, bn), lambda j, m, te, nv: (m, j)),
            scratch_shapes=[pltpu.VMEM((FFN_DIM, bn), BF16)]),
        out_shape=jax.ShapeDtypeStruct((ns, D_MODEL), F32),
        compiler_params=_cparams(('arbitrary', 'arbitrary')), name='moe_down',
    )(tile_expert, n_valid, u, w2)


def _moe_combine_kernel(dest_ref, ys_hbm, r_ref, x_ref, g_ref, m_ref, o_ref, buf, sem, *, bm, kg):
    base = pl.program_id(0) * bm

    def issue(r, carry):
        t = base + r
        for kk in range(TOP_K):
            _row_copy(ys_hbm.at[pl.ds(dest_ref[TOP_K * t + kk], 1)], buf.at[kk, pl.ds(r, 1)], sem).start()
        return carry
    lax.fori_loop(0, bm, issue, 0)

    def drain(r, carry):
        for kk in range(TOP_K):
            _row_copy(ys_hbm.at[pl.ds(0, 1)], buf.at[0, pl.ds(0, 1)], sem).wait()
        return carry
    lax.fori_loop(0, bm, drain, 0)

    r = r_ref[...]
    f = r[:, RT_G1:RT_G1 + 1] * buf[0] + r[:, RT_G2:RT_G2 + 1] * buf[1]
    n = f * lax.rsqrt(jnp.mean(f * f, axis=-1, keepdims=True) + EPS) * g_ref[...]
    o_ref[...] = x_ref[...] + m_ref[0][kg:kg + 1] * n


def _moe_combine(dest_flat, ys, route, x, g, mods, kg, t, bm, n_out):
    nlt = t // bm
    xs = pl.BlockSpec((bm, D_MODEL), lambda i, d: (i, 0))
    return pl.pallas_call(
        functools.partial(_moe_combine_kernel, bm=bm, kg=kg),
        grid_spec=pltpu.PrefetchScalarGridSpec(
            num_scalar_prefetch=1, grid=(n_out // bm,),
            in_specs=[pl.BlockSpec(memory_space=pl.ANY),
                      pl.BlockSpec((bm, LANE), lambda i, d: (i, 0)), xs,
                      pl.BlockSpec((1, D_MODEL), lambda i, d: (0, 0)),
                      pl.BlockSpec((1, 6, D_MODEL), lambda i, d: (jnp.where(i >= nlt, 1, 0), 0, 0))],
            out_specs=xs,
            scratch_shapes=[pltpu.VMEM((TOP_K, bm, D_MODEL), F32), pltpu.SemaphoreType.DMA(())]),
        out_shape=jax.ShapeDtypeStruct((n_out, D_MODEL), F32),
        compiler_params=_cparams(('arbitrary',)), name='moe_combine',
    )(dest_flat, ys, route, x, g.reshape(1, D_MODEL), mods)


def _moe_layer(x, h32, route, w1, w3, w2, lyr, g, mods, kg, t, n_out):
    tt = x.shape[0]
    bm = _row_tile(t, tt - t)
    dest, cnt = _moe_rank(route, bm)
    n_tiles = -(-TOP_K * tt // MOE_BM) + N_EXPERTS
    ns = n_tiles * MOE_BM
    cum = jnp.cumsum((cnt[0, 0:N_EXPERTS].astype(jnp.int32) + (MOE_BM - 1)) // MOE_BM)
    n_valid = cum[-1]
    tile = jnp.arange(n_tiles, dtype=jnp.int32)
    tile_expert = jnp.searchsorted(cum, jnp.minimum(tile, n_valid - 1), side='right').astype(jnp.int32)
    dest_flat = dest[:, 0:TOP_K].reshape(-1)
    xs32 = _moe_scatter(dest_flat, h32, ns, bm)
    ys = _moe_grouped_ffn(tile_expert, n_valid.reshape(1), xs32, w1, w3, w2, lyr)
    return _moe_combine(dest_flat, ys, route, x, g, mods, kg, t, bm, n_out)


def _rwkv_kernel(d, cur_ref, prv_ref, nxt_ref, sm_ref, mu_ref, w0_ref, wup_ref, a0_ref, aup_ref,
                 kk_ref, ka_ref, rk_ref, e_ref, tri_ref, y_ref, bonus_ref, s_ref, *, nl, nc):
    n = pl.program_id(0)
    ch = _chunk_of(d, n, nl, nc)

    @pl.when(n == 0)
    def _():
        s_ref[...] = jnp.zeros_like(s_ref)

    x = cur_ref[...]
    first = jnp.logical_or(ch == 0, ch == nl)
    last = jnp.logical_or(ch == nl - 1, ch == nl + nc - 1)
    p_row = jnp.where(first, 0.0, prv_ref[7:8, :])
    n_row = jnp.where(last, 0.0, nxt_ref[0:1, :])
    row = lax.broadcasted_iota(jnp.int32, (CHUNK, 1), 0)
    prev = jnp.where(row == 0, p_row, pltpu.roll(x, 1, axis=0))
    nxt = jnp.where(row == CHUNK - 1, n_row, pltpu.roll(x, CHUNK - 1, axis=0))
    z = x + mu_ref[0:1, :] * (prev - x) + mu_ref[1:2, :] * (nxt - x)
    r, k, v = z[:, 0:MIX_W], z[:, MIX_W:2 * MIX_W], z[:, 2 * MIX_W:3 * MIX_W]

    sm = sm_ref[...]
    e_bf = e_ref[...]
    wd = jnp.tanh(sm[:, 0:128])
    w_log = -_softplus(-(w0_ref[...] + _mm3(wd, wup_ref[...]))) - 0.5
    logw = -jnp.exp(w_log)
    a = _sigmoid(a0_ref[...] + _mm3(sm[:, 128:256], aup_ref[...]))
    kkr = k * kk_ref[...]
    kk = kkr * lax.rsqrt(jnp.maximum(_mm_xw(kkr * kkr, e_bf), 1e-12))
    k_dir = k * (1.0 + (a - 1.0) * ka_ref[...])
    bonus_ref[...] = _mm_xw(r * k_dir * rk_ref[...], e_bf) * v

    m_strict = tri_ref[0]
    m_incl = tri_ref[1]
    eye = m_incl - m_strict
    b_inc = _mm_wx(m_incl[0:CHUNK, 0:CHUNK].astype(BF16), logw)
    b_exc = b_inc - logw
    b_last = jnp.sum(logw, axis=0, keepdims=True)
    beta = kk * a
    ea = -kk * jnp.exp(b_exc)
    er = r * jnp.exp(b_inc)
    ninv = jnp.exp(-b_inc)
    eb = beta * ninv
    ek = k_dir * ninv
    eend = jnp.exp(b_last - b_inc)
    hb = beta * eend
    hk = k_dir * eend
    gam = jnp.exp(b_last)

    gw = RW_GROUP * RW_HEAD
    lane_head = lax.broadcasted_iota(jnp.int32, (1, gw), 1) >> 6

    def spread(xg):
        return jnp.concatenate([jnp.where(lane_head == h, xg, 0.0) for h in range(RW_GROUP)], axis=0)

    groups = range(RW_HEADS // RW_GROUP)
    pre = []
    yield None
    for g in groups:
        sl = slice(g * gw, (g + 1) * gw)
        la, lr = spread(ea[:, sl]), spread(er[:, sl])
        rb, rk = spread(eb[:, sl]), spread(ek[:, sl])
        vb = spread(v[:, sl]).astype(BF16)
        amat = _mm_nt(jnp.concatenate([la, lr], axis=0), jnp.concatenate([rb, rk], axis=0))
        m_ab = amat[0:gw, 0:gw] * m_strict
        rhs = jnp.concatenate([la, _mm(amat[0:gw, gw:] * m_strict, vb)], axis=1)
        n_rb = (amat[gw:, 0:gw] * m_incl).astype(BF16)
        n_rk = (amat[gw:, gw:] * m_incl).astype(BF16)
        pre.append((lr, vb, m_ab, rhs, n_rb, n_rk))
        yield None

    def finish(g, tinv):
        sl = slice(g * gw, (g + 1) * gw)
        lr, vb, _, rhs, n_rb, n_rk = pre[g]
        xs_bf = _mm(tinv, rhs).astype(BF16)
        yield
        qy = _mm(n_rb, xs_bf)
        q_hat = lr + qy[:, 0:gw]
        y_loc = qy[:, gw:] + _mm(n_rk, vb)
        yield
        hb_bd = spread(hb[:, sl]).astype(BF16)
        gbt = _mm_tn(xs_bf, hb_bd)
        g_bot = gbt[gw:] + _mm_tn(vb, spread(hk[:, sl]))
        yield
        s0 = s_ref[g]
        y_bd = _mm_nt(q_hat, s0) + y_loc
        y_ref[:, sl] = (y_bd[0:CHUNK] + y_bd[CHUNK:2 * CHUNK]
                        + y_bd[2 * CHUNK:3 * CHUNK] + y_bd[3 * CHUNK:4 * CHUNK])
        s_ref[g] = s0 * gam[:, sl] + _mm(s0, gbt[0:gw]) + g_bot

    tinvs = yield [p_[2] for p_ in pre], eye
    chains = [finish(g, tinvs[g]) for g in groups]
    while chains:
        yield None
        chains = [c for c in chains if next(c, _DONE) is not _DONE]


def _unit_lower_inverses(ms, eye):
    tinvs = [eye + m for m in ms]
    for _ in range(5):
        ms = [_mm(m, m) for m in ms]
        tinvs = [tinv + _mm(m, tinv) for m, tinv in zip(ms, tinvs)]
    return tinvs


def _rwkv_order_masks():
    i = jnp.arange(RW_GROUP * RW_HEAD)
    same = (i[:, None] // CHUNK) == (i[None, :] // CHUNK)
    diff = (i[:, None] % CHUNK) - (i[None, :] % CHUNK)
    per_dir = [jnp.stack([same & (sg * diff > 0), same & (sg * diff >= 0)]) for sg in (1, -1)]
    return jnp.stack(per_dir).astype(F32)


def _gla_kernel(d, qk_ref, v_ref, ad_ref, cs_ref, aup_ref, ab_ref, o_ref, s_ref):
    @pl.when(pl.program_id(0) == 0)
    def _():
        s_ref[...] = jnp.zeros_like(s_ref)

    qk = qk_ref[...]
    v = v_ref[...]
    w_qk = 2 * GLA_HEADS * GLA_DK
    lane = lax.broadcasted_iota(jnp.int32, (CHUNK, w_qk), 1)
    partner = jnp.where((lane & 1) == 0, pltpu.roll(qk, w_qk - 1, axis=1), pltpu.roll(qk, 1, axis=1))
    qk = qk * cs_ref[:, 0:w_qk] + partner * cs_ref[:, w_qk:]
    hk = GLA_HEADS * GLA_DK
    q = qk[:, 0:hk] * (GLA_DK ** -0.5)
    k = qk[:, hk:]
    g = -_softplus(-(_mm3(ad_ref[...], aup_ref[...]) + ab_ref[...])) / GLA_TAU
    _, incl = _order_masks(d)
    b = _mm_wx(jnp.where(incl, 1.0, 0.0).astype(BF16), g)
    b_last = jnp.sum(g, axis=0, keepdims=True)
    q_e = q * jnp.exp(b)
    k_e = k * jnp.exp(-b)
    k_end = k * jnp.exp(b_last - b)
    dec = jnp.exp(b_last)
    for h in range(GLA_HEADS):
        yield None
        sk = slice(h * GLA_DK, (h + 1) * GLA_DK)
        sv = slice(h * GLA_DV, (h + 1) * GLA_DV)
        att = jnp.where(incl, _mm_nt(q_e[:, sk], k_e[:, sk]), 0.0)
        st = s_ref[h]
        o_ref[:, sv] = _mm(att, v[:, sv]) + _mm_nt(q_e[:, sk], st)
        s_ref[h] = st * dec[:, sk] + _mm_tn(v[:, sv], k_end[:, sk])


_DONE = object()
N_SCAN_SHARED = 5
N_SCAN_DIR = 11
SCAN_COLS = 3 * MIX_W + 3 * LANE + 2 * GLA_HEADS * GLA_DK + 2 * MIX_W + LANE


def _scan_kernel(*refs, nl, nc):
    mu, kk, ka, rk, e64 = refs[0:N_SCAN_SHARED]
    n_in = N_SCAN_SHARED + 2 * N_SCAN_DIR
    c0 = 3 * MIX_W
    c1 = c0 + 3 * LANE
    c2 = c1 + 2 * GLA_HEADS * GLA_DK
    rws, glas = [], []
    for d in range(2):
        pm, prv, nxt, cs, w0, wup, a0, aup, tri, g_aup, g_ab = refs[N_SCAN_SHARED + d * N_SCAN_DIR:
                                                                   N_SCAN_SHARED + (d + 1) * N_SCAN_DIR]
        y, bonus, o = refs[n_in + 3 * d:n_in + 3 * d + 3]
        s_rw, s_gla = refs[n_in + 6 + 2 * d:n_in + 8 + 2 * d]
        rws.append(_rwkv_kernel(d, pm.at[:, 0:c0], prv, nxt, pm.at[:, c0:c1], mu, w0, wup, a0, aup, kk, ka, rk,
                                e64, tri, y, bonus, s_rw, nl=nl, nc=nc))
        glas.append(_gla_kernel(d, pm.at[:, c1:c2], pm.at[:, c2:c2 + MIX_W], pm.at[:, SCAN_COLS - LANE:SCAN_COLS],
                                cs, g_aup, g_ab, o, s_gla))
    asked = [None, None]
    while None in asked:
        for d in range(2):
            if asked[d] is None:
                asked[d] = next(rws[d])
        glas = [c for c in glas if next(c, _DONE) is not _DONE]
    n_grp = len(asked[0][0])
    tinvs = _unit_lower_inverses(asked[0][0] + asked[1][0], asked[0][1])
    for d in range(2):
        rws[d].send(tinvs[d * n_grp:(d + 1) * n_grp])
    chains = rws + glas
    while chains:
        chains = [c for c in chains if next(c, _DONE) is not _DONE]


def _scans(p, t, lc, shared, per_dir):
    tt = t + lc
    gw = RW_GROUP * RW_HEAD
    nl, nc = t // CHUNK, lc // CHUNK
    last8 = tt // 8 - 1
    full = lambda a: pl.BlockSpec(a.shape, lambda n: (0,) * a.ndim)
    in_specs = [full(a) for a in shared]
    args = list(shared)
    out_specs = []
    for d in range(2):
        ch = functools.partial(_chunk_of, d, nl=nl, nc=nc)
        rope, *params = per_dir[d]
        in_specs += [
            pl.BlockSpec((CHUNK, SCAN_COLS), lambda n, ch=ch: (ch(n), 0)),
            pl.BlockSpec((8, 3 * MIX_W), lambda n, ch=ch: (jnp.maximum(ch(n) * 8 - 1, 0), 0)),
            pl.BlockSpec((8, 3 * MIX_W), lambda n, ch=ch: (jnp.minimum(ch(n) * 8 + 8, last8), 0)),
            pl.BlockSpec((CHUNK, rope.shape[1]), lambda n, ch=ch: (ch(n), 0)),
        ] + [full(a) for a in params]
        args += [p, p, p, rope] + params
        out_specs += [pl.BlockSpec((CHUNK, MIX_W), lambda n, ch=ch: (ch(n), 0))] * 3
    assert len(in_specs) == N_SCAN_SHARED + 2 * N_SCAN_DIR
    return pl.pallas_call(
        functools.partial(_scan_kernel, nl=nl, nc=nc),
        grid=(nl + nc,),
        in_specs=in_specs, out_specs=out_specs,
        out_shape=[jax.ShapeDtypeStruct((tt, MIX_W), F32)] * 6,
        scratch_shapes=[pltpu.VMEM((RW_HEADS // RW_GROUP, gw, gw), F32),
                        pltpu.VMEM((GLA_HEADS, GLA_DV, GLA_DK), F32)] * 2,
        compiler_params=_cparams(('arbitrary',)), name='scans',
    )(*args)


def _head_norm(y, e_bf, width, eps):
    mu = _mm_xw(y, e_bf) * (1.0 / width)
    dl = y - mu
    var = _mm_xw(dl * dl, e_bf) * (1.0 / width)
    return dl * lax.rsqrt(var + eps)


def _mix_finish_kernel(y0_ref, y1_ref, b0_ref, b1_ref, sm_ref, gup_ref, lng_ref, lnb_ref, e64_ref,
                       o0_ref, o1_ref, gr0_ref, gr1_ref, gr2_ref, gr3_ref, gng_ref, e128_ref, a_ref, b_ref):
    yn = _head_norm(y0_ref[...] + y1_ref[...], e64_ref[...], RW_HEAD, RW_GN_EPS)
    yn = yn * lng_ref[...] + lnb_ref[...] + b0_ref[...] + b1_ref[...]
    gate = _mm(_sigmoid(sm_ref[:, 256:384]), gup_ref[...])
    a_ref[...] = (yn * gate).astype(BF16)
    on = _head_norm(o0_ref[...] + o1_ref[...], e128_ref[...], GLA_DV, GN_EPS) * gng_ref[...]
    gr = jnp.concatenate([gr0_ref[...], gr1_ref[...], gr2_ref[...], gr3_ref[...]], axis=1)
    b_ref[...] = (on * (gr * _sigmoid(gr))).astype(BF16)


def _mix_finish(p, rw_y, rw_bonus, gla_o, g_up, ln_g, ln_b, gn_g, e64, e128, t):
    tt = p.shape[0]
    bm = _row_tile(t, tt - t)
    vec = pl.BlockSpec((1, MIX_W), lambda i: (0, 0))
    mat = pl.BlockSpec((MIX_W, MIX_W), lambda i: (0, 0))
    out = pl.BlockSpec((bm, MIX_W), lambda i: (i, 0))
    return pl.pallas_call(
        _mix_finish_kernel,
        grid=(tt // bm,),
        in_specs=([out, out, out, out,
                   pl.BlockSpec((bm, 3 * LANE), lambda i: (i, CB384_RW_SMALL)),
                   pl.BlockSpec((RW_GATE_RANK, MIX_W), lambda i: (0, 0)), vec, vec, mat,
                   out, out]
                  + [pl.BlockSpec((bm, LANE), functools.partial(lambda i, c: (i, c), c=CB128_GLA_R + q))
                     for q in range(4)]
                  + [vec, mat]),
        out_specs=[out, out],
        out_shape=[jax.ShapeDtypeStruct((tt, MIX_W), BF16)] * 2,
        compiler_params=_cparams(('parallel',)), name='mix_finish',
    )(*rw_y, *rw_bonus, p, g_up, ln_g, ln_b, e64, *gla_o, p, p, p, p, gn_g, e128)


def _sgu_kernel(u_ref, v_ref, lng_ref, lnb_ref, ws_ref, bs_ref, o_ref):
    u = _gelu(u_ref[...])
    v = _gelu(v_ref[...])
    mu = jnp.mean(v, axis=-1, keepdims=True)
    dl = v - mu
    var = jnp.mean(dl * dl, axis=-1, keepdims=True)
    vn = (dl * lax.rsqrt(var + GN_EPS) * lng_ref[...] + lnb_ref[...]).astype(BF16)
    lane = lax.broadcasted_iota(jnp.int32, (1, MIX_W), 1)
    s = bs_ref[...]
    for g in range(SGU_GROUPS):
        s = s + jnp.where((lane >> 6) == g, jnp.dot(ws_ref[g].astype(BF16), vn, preferred_element_type=F32), 0.0)
    o_ref[...] = (u * s).astype(BF16)


def _sgu(p, ln_g, ln_b, w_s, b_full):
    tt = p.shape[0]
    vec = pl.BlockSpec((1, MIX_W), lambda i: (0, 0))
    return pl.pallas_call(
        _sgu_kernel,
        grid=(tt // SGU_CHUNK,),
        in_specs=[pl.BlockSpec((SGU_CHUNK, MIX_W), lambda i: (i, CB_SGU_U)),
                  pl.BlockSpec((SGU_CHUNK, MIX_W), lambda i: (i, CB_SGU_V)), vec, vec,
                  pl.BlockSpec((SGU_GROUPS, SGU_CHUNK, SGU_CHUNK), lambda i: (0, 0, 0)),
                  pl.BlockSpec((SGU_CHUNK, MIX_W), lambda i: (0, 0))],
        out_specs=pl.BlockSpec((SGU_CHUNK, MIX_W), lambda i: (i, 0)),
        out_shape=jax.ShapeDtypeStruct((tt, MIX_W), BF16),
        compiler_params=_cparams(('parallel',)), name='sgu',
    )(p, p, ln_g, ln_b, w_s, b_full)


def _na_bias_kernel(rpb_ref, o_ref, *, wh):
    h = pl.program_id(0)
    n_dc = 2 * NA_WIN_W - 1
    n_dr = 2 * NA_WIN_H - 1
    shape = (GRID_W, 2 * GRID_W)
    c = lax.broadcasted_iota(jnp.int32, shape, 0)
    lane = lax.broadcasted_iota(jnp.int32, shape, 1)
    x = lane & (GRID_W - 1)
    dc = jnp.clip(x - c + (NA_WIN_W - 1), 0, 2 * NA_WIN_W - 2)
    key = (lane >> 6) * n_dc + dc
    cs = jnp.clip(c - NA_WIN_W // 2, 0, GRID_W - NA_WIN_W)
    ok = jnp.logical_and(x >= cs, x < cs + NA_WIN_W)
    pairs = []
    for dr in range(n_dr - 1):
        base = h * (n_dr * n_dc) + dr * n_dc
        tile = lax.fori_loop(0, 2 * n_dc, lambda j, acc: jnp.where(key == j, rpb_ref[base + j], acc),
                             jnp.zeros(shape, F32))
        pairs.append(jnp.where(ok, tile, NEG_INF))
    for dr0 in range(n_dr - wh + 1):
        for jj in range(wh // 2):
            o_ref[dr0, 0, :, jj * 2 * GRID_W:(jj + 1) * 2 * GRID_W] = pairs[dr0 + 2 * jj]


def _na_bias(rpb, wh):
    n_dr0 = 2 * NA_WIN_H - wh
    return pl.pallas_call(
        functools.partial(_na_bias_kernel, wh=wh),
        grid=(NA_HEADS,),
        in_specs=[pl.BlockSpec(memory_space=pltpu.SMEM)],
        out_specs=pl.BlockSpec((n_dr0, 1, GRID_W, wh * GRID_W), lambda h: (0, h, 0, 0)),
        out_shape=jax.ShapeDtypeStruct((n_dr0, NA_HEADS, GRID_W, wh * GRID_W), F32),
        compiler_params=_cparams(('parallel',)), name='na_bias',
    )(rpb.reshape(-1))


def _na_kernel(*refs, wh):
    q_ref = refs[0]
    k_refs = refs[1:1 + wh]
    v_refs = refs[1 + wh:1 + 2 * wh]
    kc_ref, vc_ref, bias_ref, o_ref = refs[1 + 2 * wh:]
    q = q_ref[...] * (NA_HEAD ** -0.5)
    kw = jnp.concatenate([r[...].astype(BF16) for r in k_refs], axis=0)
    vw = jnp.concatenate([r[...].astype(BF16) for r in v_refs], axis=0)
    kc = kc_ref[...].astype(BF16)
    vc = vc_ref[...].astype(BF16)
    lane = lax.broadcasted_iota(jnp.int32, (1, LANE), 1)
    heads = [(pr, s) for pr in range(NA_HEADS // 2) for s in range(2)]
    slab = lambda x, pr: x[:, pr * LANE:(pr + 1) * LANE]
    scores = []
    for pr, s in heads:
        qh = jnp.where((lane >> 6) == s, slab(q, pr), 0.0).astype(BF16)
        scores.append((_mm_nt(qh, slab(kw, pr)) + bias_ref[0, 2 * pr + s], _mm_nt(qh, slab(kc, pr))))
    probs = []
    for sw, sc in scores:
        m = jnp.maximum(jnp.max(sw, axis=-1, keepdims=True), jnp.max(sc, axis=-1, keepdims=True))
        ew = jnp.exp(sw - m)
        ec = jnp.exp(sc - m)
        den = jnp.sum(ew, axis=-1, keepdims=True) + jnp.sum(ec, axis=-1, keepdims=True)
        probs.append((ew, ec, den))
    outs = [jnp.zeros((GRID_W, LANE), F32)] * (NA_HEADS // 2)
    for (pr, s), (ew, ec, den) in zip(heads, probs):
        oh = (_mm(ew, slab(vw, pr)) + _mm(ec, slab(vc, pr))) / den
        outs[pr] = jnp.where((lane >> 6) == s, oh, outs[pr])
    o_ref[...] = jnp.concatenate(outs, axis=1).astype(BF16)


def _na_ctx_kernel(q_ref, k_ref, v_ref, o_ref):
    q = q_ref[...] * (NA_HEAD ** -0.5)
    kc = k_ref[...].astype(BF16)
    vc = v_ref[...].astype(BF16)
    lane = lax.broadcasted_iota(jnp.int32, (1, MIX_W), 1)
    o = jnp.zeros(q.shape, F32)
    for h in range(NA_HEADS):
        hm = (lane >> 6) == h
        s = _mm_nt(jnp.where(hm, q, 0.0), kc)
        e = jnp.exp(s - jnp.max(s, axis=-1, keepdims=True))
        o = jnp.where(hm, _mm(e, vc) / jnp.sum(e, axis=-1, keepdims=True), o)
    o_ref[...] = o.astype(BF16)


def _na(p, bias, t, lc):
    rows = t // GRID_W
    wh = min(NA_WIN_H, rows)
    rs = lambda r: jnp.clip(r - wh // 2, 0, rows - wh)
    ctx_blk = t // lc
    kv = lambda cb: [pl.BlockSpec((GRID_W, MIX_W), functools.partial(lambda r, w, cb: (rs(r) + w, cb), w=w, cb=cb))
                     for w in range(wh)]
    lat = pl.pallas_call(
        functools.partial(_na_kernel, wh=wh),
        grid=(rows,),
        in_specs=([pl.BlockSpec((GRID_W, MIX_W), lambda r: (r, CB_NA_Q))] + kv(CB_NA_K) + kv(CB_NA_V)
                  + [pl.BlockSpec((lc, MIX_W), lambda r: (ctx_blk, CB_NA_K)),
                     pl.BlockSpec((lc, MIX_W), lambda r: (ctx_blk, CB_NA_V)),
                     pl.BlockSpec((1, NA_HEADS, GRID_W, wh * GRID_W),
                                  lambda r: (rs(r) - r + (NA_WIN_H - 1), 0, 0, 0))]),
        out_specs=pl.BlockSpec((GRID_W, MIX_W), lambda r: (r, 0)),
        out_shape=jax.ShapeDtypeStruct((t, MIX_W), BF16),
        compiler_params=_cparams(('parallel',)), name='na_latent',
    )(*([p] * (1 + 2 * wh + 2)), bias)
    cblk = lambda cb: pl.BlockSpec((lc, MIX_W), lambda i: (ctx_blk, cb))
    ctx = pl.pallas_call(
        _na_ctx_kernel,
        grid=(1,),
        in_specs=[cblk(CB_NA_Q), cblk(CB_NA_K), cblk(CB_NA_V)],
        out_specs=pl.BlockSpec((lc, MIX_W), lambda i: (0, 0)),
        out_shape=jax.ShapeDtypeStruct((lc, MIX_W), BF16),
        compiler_params=_cparams(('arbitrary',)), name='na_ctx',
    )(p, p, p)
    return jnp.concatenate([lat, ctx], axis=0)


def _merge_kernel(a0, a1, a2, a3, g0, g1, g2, g3, w_ref, o_ref, wb_ref):
    @pl.when(pl.program_id(1) == 0)
    def _():
        wb_ref[...] = w_ref[...].astype(BF16)
    acc = None
    for n, (a_ref, g_ref) in enumerate(((a0, g0), (a1, g1), (a2, g2), (a3, g3))):
        zn = jnp.dot(a_ref[...], wb_ref[n], preferred_element_type=F32) * _sigmoid(g_ref[...])
        acc = zn if acc is None else acc + zn
    o_ref[...] = acc.astype(BF16)


def _merge(ys, p, w_br, lyr):
    tt = p.shape[0]
    bn = 512
    bm = _pick(tt, (768, 384, 128))
    a_s = pl.BlockSpec((bm, MIX_W), lambda j, i: (i, 0))
    gs = [pl.BlockSpec((bm, bn), functools.partial(lambda j, i, n: (i, (GATE_OFF + n * D_MODEL) // bn + j), n=n))
          for n in range(N_BRANCH)]
    return pl.pallas_call(
        _merge_kernel,
        grid=(D_MODEL // bn, tt // bm),
        in_specs=[a_s] * 4 + gs + [pl.BlockSpec((None, N_BRANCH, MIX_W, bn), lambda j, i: (lyr, 0, 0, j))],
        out_specs=pl.BlockSpec((bm, bn), lambda j, i: (i, j)),
        out_shape=jax.ShapeDtypeStruct((tt, D_MODEL), BF16),
        scratch_shapes=[pltpu.VMEM((N_BRANCH, MIX_W, bn), BF16)],
        compiler_params=_cparams(('arbitrary', 'arbitrary')), name='merge',
    )(*ys, p, p, p, p, w_br)


def _block_diag_ones(width):
    i = jnp.arange(MIX_W) // width
    return (i[:, None] == i[None, :]).astype(BF16)


def _rope_tables(t, lc):
    tok = jnp.arange(t)
    pos = jnp.stack([tok // GRID_W, tok % GRID_W], axis=-1).astype(F32)
    nf = GLA_DK // 4
    inv = ROPE_BASE ** (-jnp.arange(nf, dtype=F32) / nf)
    ang = pos[:, :, None] * inv
    cos = jnp.repeat(jnp.cos(ang), 2, axis=-1).reshape(t, GLA_DK)
    sin = jnp.sin(ang)
    sin = jnp.stack([-sin, sin], axis=-1).reshape(t, GLA_DK)
    reps = 2 * GLA_HEADS
    cos = jnp.concatenate([jnp.tile(cos, (1, reps)), jnp.ones((lc, reps * GLA_DK), F32)], axis=0)
    sin = jnp.concatenate([jnp.tile(sin, (1, reps)), jnp.zeros((lc, reps * GLA_DK), F32)], axis=0)
    return jnp.concatenate([cos, sin], axis=1)


def _pad_rank_rows(w_up, rank, rows):
    out = jnp.zeros((2, rows, w_up.shape[-1]), w_up.dtype)
    for d in range(2):
        out = out.at[d, d * rank:(d + 1) * rank].set(w_up[d])
    return out


def kernel(x, c, ctx, c_ctx, ada_w, ada_b, norm_g, w_in, rw_mu, rw_w0, rw_w_up, rw_a0, rw_a_up, rw_g_up, rw_k_k, rw_k_a, rw_r_k, rw_ln_g, rw_ln_b, gla_a_up, gla_a_b, gla_gn_g, sgu_ln_g, sgu_ln_b, sgu_w, sgu_b, na_rpb, w_br, w_o, ffn_w1, ffn_w3, ffn_w2, moe_router, moe_w1, moe_w3, moe_w2):
    assert x.shape[0] == 1 and x.shape[2] == D_MODEL
    t, lc = x.shape[1], ctx.shape[1]
    depth = ada_w.shape[0]
    assert t % max(lc, SGU_CHUNK) == 0 and lc % SGU_CHUNK == 0 and t % GRID_W == 0
    rows = t // GRID_W
    assert rows >= NA_WIN_H
    wh = NA_WIN_H
    xs = jnp.concatenate([x[0], ctx[0]], axis=0)
    cond8 = jnp.zeros((8, D_MODEL), F32).at[0].set(c[0]).at[1].set(c_ctx)
    e64 = _block_diag_ones(RW_HEAD)
    e128 = _block_diag_ones(GLA_DV)
    rope = _rope_tables(t, lc)
    tri = _rwkv_order_masks()
    row1 = lambda v: v.reshape(1, -1)
    assert w_in.shape[2] == N_IN

    mods_all = [_adaln(cond8, ada_w, ada_b, i)[0:2].reshape(2, 6, D_MODEL) for i in range(depth)]
    h = _norm_mod_call(xs, norm_g[0, 0], mods_all[0], 0, t)
    for i in range(depth):
        mods = mods_all[i]
        last = i == depth - 1
        p = _in_proj(h, w_in, i)
        mu2 = jnp.stack([rw_mu[i, :, 0].reshape(-1), rw_mu[i, :, 1].reshape(-1)])
        w_up = _pad_rank_rows(rw_w_up[i], RW_DECAY_RANK, LANE)
        a_up = _pad_rank_rows(rw_a_up[i], RW_ICLR_RANK, LANE)
        g_up = _pad_rank_rows(gla_a_up[i], GLA_GATE_RANK, LANE)
        per_dir = [[rope, row1(rw_w0[i, d]), w_up[d], row1(rw_a0[i, d]), a_up[d], tri[d], g_up[d],
                    row1(gla_a_b[i, d])] for d in range(2)]
        y0, b0, o0, y1, b1, o1 = _scans(p, t, lc, [mu2, row1(rw_k_k[i]), row1(rw_k_a[i]), row1(rw_r_k[i]), e64],
                                        per_dir)
        rw_y, rw_bonus, gla_o = (y0, y1), (b0, b1), (o0, o1)
        y_a, y_b = _mix_finish(p, rw_y, rw_bonus, gla_o, rw_g_up[i], row1(rw_ln_g[i]), row1(rw_ln_b[i]),
                               row1(gla_gn_g[i]), e64, e128, t)
        y_s = _sgu(p, row1(sgu_ln_g[i]), row1(sgu_ln_b[i]), sgu_w[i], jnp.repeat(sgu_b[i].T, 64, axis=1))
        y_d = _na(p, _na_bias(na_rpb[i], wh), t, lc)
        z = _merge((y_a, y_b, y_s, y_d), p, w_br, i)
        y = _matmul(z, w_o, i, 1024, (768, 384, 128), name='out_proj')
        j = i // 2
        if i % 2 == 0:
            xs, h = _resid_next_call(xs, y, norm_g[i, 1], mods, 2, t, norm_g[i, 2], mods, 3)
            f = _ffn_down(_ffn_up(h, ffn_w1, ffn_w3, j), ffn_w2, j)
            if last:
                xs = _resid_call(xs, f, norm_g[i, 3], mods, 5, t, t)
            else:
                xs, h = _resid_next_call(xs, f, norm_g[i, 3], mods, 5, t, norm_g[i + 1, 0], mods_all[i + 1], 0)
        else:
            xs, h32, route = _resid_next_call(xs, y, norm_g[i, 1], mods, 2, t, norm_g[i, 2], mods, 3,
                                              router=moe_router[j])
            xs = _moe_layer(xs, h32, route, moe_w1, moe_w3, moe_w2, j, norm_g[i, 3], mods, 5, t,
                            t if last else t + lc)
            if not last:
                h = _norm_mod_call(xs, norm_g[i + 1, 0], mods_all[i + 1], 0, t)
    return xs[None]
```

```python
import functools

import jax
import jax.numpy as jnp
from jax import lax
from jax.experimental import pallas as pl
from jax.experimental.pallas import tpu as pltpu

F32 = jnp.float32
BF16 = jnp.bfloat16

D_MODEL = 2048
GRID_W = 64
N_BRANCH = 4
MIX_W = D_MODEL // 4
RW_HEAD = 64
RW_HEADS = MIX_W // RW_HEAD
RW_GROUP = 4
RW_DECAY_RANK = 64
RW_ICLR_RANK = 64
RW_GATE_RANK = 128
RW_GN_EPS = 64e-5
GLA_HEADS = 4
GLA_DV = MIX_W // GLA_HEADS
GLA_DK = GLA_DV // 2
GLA_GATE_RANK = 16
GLA_TAU = 16.0
GN_EPS = 1e-5
ROPE_BASE = 10000.0
SGU_GROUPS = MIX_W // 64
SGU_CHUNK = 128
NA_HEAD = 64
NA_HEADS = MIX_W // NA_HEAD
NA_WIN_H = 8
NA_WIN_W = 16
FFN_DIM = 7 * D_MODEL // 2
N_EXPERTS = 8
TOP_K = 2
EPS = 1e-6
NEG_INF = -1e30

RT_G1, RT_G2, RT_I1, RT_I2 = 8, 9, 10, 11
MOE_BM = 256
CHUNK = 64
LANE = 128
VMEM_LIMIT = 56 * 2 ** 20

IN_TAIL = 3 * MIX_W + 2 * RW_DECAY_RANK + 2 * RW_ICLR_RANK + RW_GATE_RANK + 2 * GLA_HEADS * GLA_DK \
    + 2 * MIX_W + 2 * GLA_GATE_RANK
IN_PAD = -IN_TAIL % LANE
N_IN = IN_TAIL + 5 * MIX_W + N_BRANCH * D_MODEL
N_IN_P = N_IN + IN_PAD
CB384_RW_SMALL = 4
CB128_GLA_R = 23
CB_SGU_U, CB_SGU_V, CB_NA_Q, CB_NA_K, CB_NA_V = 7, 8, 9, 10, 11
GATE_OFF = 6144


def _cparams(sem):
    return pltpu.CompilerParams(dimension_semantics=sem, vmem_limit_bytes=VMEM_LIMIT)


def _pick(m, cands):
    for c in cands:
        if m % c == 0:
            return c
    raise ValueError(f'no tile for {m}')


def _mm(a, b):
    return jnp.dot(a.astype(BF16), b.astype(BF16), preferred_element_type=F32)


def _mm_nt(a, b):
    return lax.dot_general(a.astype(BF16), b.astype(BF16), (((1,), (1,)), ((), ())),
                           preferred_element_type=F32)


def _mm_tn(a, b):
    return lax.dot_general(a.astype(BF16), b.astype(BF16), (((0,), (0,)), ((), ())),
                           preferred_element_type=F32)


def _split2(x):
    hi = x.astype(BF16)
    lo = (x - hi.astype(F32)).astype(BF16)
    return hi, lo


def _mm_xw(x, w_bf):
    hi, lo = _split2(x)
    return (jnp.dot(hi, w_bf, preferred_element_type=F32)
            + jnp.dot(lo, w_bf, preferred_element_type=F32))


def _mm_wx(w_bf, x):
    hi, lo = _split2(x)
    return (jnp.dot(w_bf, hi, preferred_element_type=F32)
            + jnp.dot(w_bf, lo, preferred_element_type=F32))


def _mm3(a, b):
    ah, al = _split2(a)
    bh, bl = _split2(b)
    return (jnp.dot(ah, bh, preferred_element_type=F32)
            + jnp.dot(ah, bl, preferred_element_type=F32)
            + jnp.dot(al, bh, preferred_element_type=F32))


def _sigmoid(x):
    return 1.0 / (1.0 + jnp.exp(-x))


def _softplus(x):
    return jnp.maximum(x, 0.0) + jnp.log1p(jnp.exp(-jnp.abs(x)))


def _gelu(x):
    return 0.5 * x * (1.0 + lax.erf(x * (0.5 ** 0.5)))


def _order_masks(d):
    t = lax.broadcasted_iota(jnp.int32, (CHUNK, CHUNK), 0)
    s = lax.broadcasted_iota(jnp.int32, (CHUNK, CHUNK), 1)
    diff = (t - s) * jnp.where(d == 0, 1, -1)
    return diff > 0, diff >= 0


def _chunk_of(d, n, nl, nc):
    fwd = jnp.where(n < nc, nl + n, n - nc)
    bwd = jnp.where(n < nc, nl + nc - 1 - n, nl - 1 - (n - nc))
    return jnp.where(d == 0, fwd, bwd)


def _ada_kernel(c_ref, w_ref, b_ref, o_ref):
    cnd = c_ref[...]
    a = cnd * _sigmoid(cnd)
    o_ref[...] = _mm(a, w_ref[...]) + b_ref[...]


def _adaln(cond8, ada_w, ada_b, lyr):
    depth, _, n = ada_w.shape
    bn = 1536
    return pl.pallas_call(
        _ada_kernel,
        grid=(n // bn,),
        in_specs=[pl.BlockSpec((8, D_MODEL), lambda j: (0, 0)),
                  pl.BlockSpec((None, D_MODEL, bn), lambda j: (lyr, 0, j)),
                  pl.BlockSpec((None, 1, bn), lambda j: (lyr, 0, j))],
        out_specs=pl.BlockSpec((8, bn), lambda j: (0, j)),
        out_shape=jax.ShapeDtypeStruct((8, n), F32),
        compiler_params=_cparams(('arbitrary',)),
        name='adaln',
    )(cond8, ada_w, ada_b.reshape(depth, 1, n))


def _norm_mod(x, g, m, k0):
    y = x * lax.rsqrt(jnp.mean(x * x, axis=-1, keepdims=True) + EPS) * g
    return y * (1.0 + m[k0 + 1:k0 + 2]) + m[k0:k0 + 1]


def _pack_halves(h):
    n = h.shape[1] // 2
    bits = lax.bitcast_convert_type(h.astype(BF16).astype(F32), jnp.uint32)
    return (bits[:, 0:n] >> 16) | bits[:, n:]


def _unpack_halves(w):
    lo = lax.bitcast_convert_type(w << 16, F32)
    hi = lax.bitcast_convert_type(w & jnp.uint32(0xFFFF0000), F32)
    return jnp.concatenate([lo, hi], axis=1).astype(BF16)


def _norm_mod_kernel(x_ref, g_ref, m_ref, o_ref, *, k0):
    o_ref[...] = _norm_mod(x_ref[...], g_ref[...], m_ref[0], k0).astype(BF16)


def _route_table(h, router):
    logits = _mm3(h, router)
    lane = lax.broadcasted_iota(jnp.int32, logits.shape, 1)
    neg = jnp.float32(-jnp.inf)
    l1 = jnp.where(lane < N_EXPERTS, logits, neg)
    m1 = jnp.max(l1, axis=-1, keepdims=True)
    i1 = jnp.min(jnp.where(l1 == m1, lane, LANE), axis=-1, keepdims=True)
    l2 = jnp.where(lane == i1, neg, l1)
    m2 = jnp.max(l2, axis=-1, keepdims=True)
    i2 = jnp.min(jnp.where(l2 == m2, lane, LANE), axis=-1, keepdims=True)
    e2 = jnp.exp(m2 - m1)
    den = 1.0 + e2
    sel = jnp.where(jnp.logical_or(lane == i1, lane == i2), 1.0, 0.0)
    sel = jnp.where(lane == RT_G1, 1.0 / den, jnp.where(lane == RT_G2, e2 / den, sel))
    return jnp.where(lane == RT_I1, i1.astype(F32), jnp.where(lane == RT_I2, i2.astype(F32), sel))


def _row_tile(t, lc):
    return _pick(lc, (256, 128))


def _norm_mod_call(x, g, mods, k0, t):
    tt = x.shape[0]
    bm = _row_tile(t, tt - t)
    nlt = t // bm
    xs = pl.BlockSpec((bm, D_MODEL), lambda i: (i, 0))
    gs = pl.BlockSpec((1, D_MODEL), lambda i: (0, 0))
    ms = pl.BlockSpec((1, 6, D_MODEL), lambda i: (jnp.where(i >= nlt, 1, 0), 0, 0))
    return pl.pallas_call(
        functools.partial(_norm_mod_kernel, k0=k0),
        grid=(tt // bm,), in_specs=[xs, gs, ms], out_specs=xs,
        out_shape=jax.ShapeDtypeStruct((tt, D_MODEL), BF16),
        compiler_params=_cparams(('parallel',)), name='norm_mod',
    )(x, g.reshape(1, D_MODEL), mods)


def _resid_kernel(x_ref, y_ref, g_ref, m_ref, o_ref, *, kg):
    y = y_ref[...]
    n = y * lax.rsqrt(jnp.mean(y * y, axis=-1, keepdims=True) + EPS) * g_ref[...]
    o_ref[...] = x_ref[...] + m_ref[0][kg:kg + 1] * n


def _resid_call(x, y, g, mods, kg, t, n_out):
    tt = x.shape[0]
    bm = _row_tile(t, tt - t)
    nlt = t // bm
    xs = pl.BlockSpec((bm, D_MODEL), lambda i: (i, 0))
    return pl.pallas_call(
        functools.partial(_resid_kernel, kg=kg),
        grid=(n_out // bm,),
        in_specs=[xs, xs, pl.BlockSpec((1, D_MODEL), lambda i: (0, 0)),
                  pl.BlockSpec((1, 6, D_MODEL), lambda i: (jnp.where(i >= nlt, 1, 0), 0, 0))],
        out_specs=xs,
        out_shape=jax.ShapeDtypeStruct((n_out, D_MODEL), F32),
        compiler_params=_cparams(('parallel',)), name='resid_norm',
    )(x, y, g.reshape(1, D_MODEL), mods)


def _resid_next_kernel(*refs, kg, k0, route):
    x_ref, y_ref, g_ref, m_ref, g2_ref, m2_ref = refs[0:6]
    y = y_ref[...]
    n = y * lax.rsqrt(jnp.mean(y * y, axis=-1, keepdims=True) + EPS) * g_ref[...]
    xn = x_ref[...] + m_ref[0][kg:kg + 1] * n
    h = _norm_mod(xn, g2_ref[...], m2_ref[0], k0)
    if route:
        r_ref, xo_ref, ho_ref, rt_ref = refs[6:]
        ho_ref[...] = _pack_halves(h)
        rt_ref[...] = _route_table(h, r_ref[...])
    else:
        xo_ref, ho_ref = refs[6:]
        ho_ref[...] = h.astype(BF16)
    xo_ref[...] = xn


def _resid_next_call(x, y, g, mods, kg, t, g2, mods2, k0, router=None):
    tt = x.shape[0]
    bm = _row_tile(t, tt - t)
    nlt = t // bm
    xs = pl.BlockSpec((bm, D_MODEL), lambda i: (i, 0))
    gs = pl.BlockSpec((1, D_MODEL), lambda i: (0, 0))
    ms = pl.BlockSpec((1, 6, D_MODEL), lambda i: (jnp.where(i >= nlt, 1, 0), 0, 0))
    in_specs = [xs, xs, gs, ms, gs, ms]
    args = [x, y, g.reshape(1, D_MODEL), mods, g2.reshape(1, D_MODEL), mods2]
    out_specs = [xs, xs]
    out_shape = [jax.ShapeDtypeStruct((tt, D_MODEL), F32), jax.ShapeDtypeStruct((tt, D_MODEL), BF16)]
    if router is not None:
        in_specs.append(pl.BlockSpec((D_MODEL, LANE), lambda i: (0, 0)))
        args.append(jnp.pad(router, ((0, 0), (0, LANE - N_EXPERTS))))
        out_specs = [xs, pl.BlockSpec((bm, D_MODEL // 2), lambda i: (i, 0)), pl.BlockSpec((bm, LANE), lambda i: (i, 0))]
        out_shape = [out_shape[0], jax.ShapeDtypeStruct((tt, D_MODEL // 2), jnp.uint32),
                     jax.ShapeDtypeStruct((tt, LANE), F32)]
    return pl.pallas_call(
        functools.partial(_resid_next_kernel, kg=kg, k0=k0, route=router is not None),
        grid=(tt // bm,), in_specs=in_specs, out_specs=out_specs, out_shape=out_shape,
        compiler_params=_cparams(('parallel',)), name='resid_next',
    )(*args)


def _mm_kernel(a_ref, w_ref, o_ref, wb_ref):
    @pl.when(pl.program_id(1) == 0)
    def _():
        wb_ref[...] = w_ref[...].astype(BF16)
    o_ref[...] = jnp.dot(a_ref[...], wb_ref[...], preferred_element_type=F32).astype(o_ref.dtype)


def _matmul(a, w, lyr, bn, bm_cands, out_dtype=F32, name='matmul'):
    m, k = a.shape
    n = w.shape[2]
    bm = _pick(m, bm_cands)
    return pl.pallas_call(
        _mm_kernel,
        grid=(n // bn, m // bm),
        in_specs=[pl.BlockSpec((bm, k), lambda j, i: (i, 0)),
                  pl.BlockSpec((None, k, bn), lambda j, i: (lyr, 0, j))],
        out_specs=pl.BlockSpec((bm, bn), lambda j, i: (i, j)),
        out_shape=jax.ShapeDtypeStruct((m, n), out_dtype),
        scratch_shapes=[pltpu.VMEM((k, bn), BF16)],
        compiler_params=_cparams(('arbitrary', 'arbitrary')), name=name,
    )(a, w)


def _in_proj_kernel(a_ref, wt_ref, o_ref, wb_ref, *, bn):
    j = pl.program_id(0)
    straddle = IN_TAIL // bn
    cut = IN_TAIL - straddle * bn

    @pl.when(pl.program_id(1) == 0)
    def _():
        @pl.when(j != straddle)
        def _():
            wb_ref[...] = wt_ref[0].astype(BF16)

        @pl.when(j == straddle)
        def _():
            wb_ref[0:cut + IN_PAD, :] = wt_ref[0, 0:cut + IN_PAD, :].astype(BF16)
            wb_ref[cut + IN_PAD:, :] = wt_ref[0, cut:bn - IN_PAD, :].astype(BF16)

    o_ref[...] = lax.dot_general(a_ref[...], wb_ref[...], (((1,), (1,)), ((), ())), preferred_element_type=F32)


def _in_proj(a, w_in, lyr):
    m = a.shape[0]
    bn = 1024
    bm = _pick(m, (768, 384, 128))
    straddle = IN_TAIL // bn
    assert IN_PAD % 8 == 0 and straddle == (IN_TAIL + IN_PAD - 1) // bn
    w_t = jnp.swapaxes(w_in, 1, 2)
    return pl.pallas_call(
        functools.partial(_in_proj_kernel, bn=bn),
        grid=(N_IN_P // bn, m // bm),
        in_specs=[pl.BlockSpec((bm, D_MODEL), lambda j, i: (i, 0)),
                  pl.BlockSpec((pl.Element(1), pl.Element(bn), pl.Element(D_MODEL)),
                               lambda j, i: (lyr, 8 * (j * (bn // 8) - jnp.where(j > straddle, IN_PAD // 8, 0)), 0))],
        out_specs=pl.BlockSpec((bm, bn), lambda j, i: (i, j)),
        out_shape=jax.ShapeDtypeStruct((m, N_IN_P), F32),
        scratch_shapes=[pltpu.VMEM((bn, D_MODEL), BF16)],
        compiler_params=_cparams(('arbitrary', 'arbitrary')), name='in_proj',
    )(a, w_t)


def _ffn_up_kernel(a_ref, w1_ref, w3_ref, o_ref, w1b, w3b):
    @pl.when(pl.program_id(1) == 0)
    def _():
        w1b[...] = w1_ref[...].astype(BF16)
        w3b[...] = w3_ref[...].astype(BF16)
    a = a_ref[...]
    h1 = jnp.dot(a, w1b[...], preferred_element_type=F32)
    h3 = jnp.dot(a, w3b[...], preferred_element_type=F32)
    o_ref[...] = (h1 * _sigmoid(h1) * h3).astype(BF16)


def _ffn_up(h, w1, w3, e):
    m = h.shape[0]
    bn = 512
    bm = _pick(m, (768, 384, 128))
    ws = pl.BlockSpec((None, D_MODEL, bn), lambda j, i: (e, 0, j))
    return pl.pallas_call(
        _ffn_up_kernel,
        grid=(FFN_DIM // bn, m // bm),
        in_specs=[pl.BlockSpec((bm, D_MODEL), lambda j, i: (i, 0)), ws, ws],
        out_specs=pl.BlockSpec((bm, bn), lambda j, i: (i, j)),
        out_shape=jax.ShapeDtypeStruct((m, FFN_DIM), BF16),
        scratch_shapes=[pltpu.VMEM((D_MODEL, bn), BF16), pltpu.VMEM((D_MODEL, bn), BF16)],
        compiler_params=_cparams(('arbitrary', 'arbitrary')), name='ffn_up',
    )(h, w1, w3)


def _ffn_down_kernel(a_ref, w_ref, o_ref):
    part = jnp.dot(a_ref[...], w_ref[...].astype(BF16), preferred_element_type=F32)

    @pl.when(pl.program_id(1) == 0)
    def _():
        o_ref[...] = part

    @pl.when(pl.program_id(1) > 0)
    def _():
        o_ref[...] += part


def _ffn_down(u, w2, e):
    m = u.shape[0]
    bk = 512
    bm = max(d for d in range(LANE, 1408 + 1, LANE) if m % d == 0)
    return pl.pallas_call(
        _ffn_down_kernel,
        grid=(m // bm, FFN_DIM // bk),
        in_specs=[pl.BlockSpec((bm, bk), lambda i, k: (i, k)),
                  pl.BlockSpec((None, bk, D_MODEL), lambda i, k: (e, k, 0))],
        out_specs=pl.BlockSpec((bm, D_MODEL), lambda i, k: (i, 0)),
        out_shape=jax.ShapeDtypeStruct((m, D_MODEL), F32),
        compiler_params=_cparams(('parallel', 'arbitrary')), name='ffn_down',
    )(u, w2)


def _moe_rank_kernel(r_ref, dest_ref, cnt_ref, carry_ref, tot_ref):
    ph = pl.program_id(0)
    i = pl.program_id(1)
    r = r_ref[...]
    bm = r.shape[0]
    lane = lax.broadcasted_iota(jnp.int32, (1, LANE), 1)
    oh = jnp.where(lane < N_EXPERTS, r, 0.0)
    colsum = jnp.sum(oh, axis=0, keepdims=True)

    @pl.when(jnp.logical_and(ph == 0, i == 0))
    def _():
        carry_ref[...] = jnp.zeros_like(carry_ref)

    @pl.when(jnp.logical_and(ph == 1, i == 0))
    def _():
        tot_ref[...] = carry_ref[...]
        carry_ref[...] = jnp.zeros_like(carry_ref)

    @pl.when(ph == 1)
    def _():
        tot = tot_ref[...]
        padded = jnp.floor((tot + (MOE_BM - 1)) * (1.0 / MOE_BM)) * MOE_BM
        a = lax.broadcasted_iota(jnp.int32, (LANE, LANE), 0)
        b = lax.broadcasted_iota(jnp.int32, (LANE, LANE), 1)
        upper = jnp.where(a < b, 1.0, 0.0).astype(BF16)
        offs = _mm_xw(jnp.broadcast_to(padded, (8, LANE)), upper)[0:1]
        tr = lax.broadcasted_iota(jnp.int32, (bm, bm), 0)
        ts = lax.broadcasted_iota(jnp.int32, (bm, bm), 1)
        before = jnp.dot(jnp.where(ts < tr, 1.0, 0.0).astype(BF16), oh.astype(BF16),
                         preferred_element_type=F32)
        slot = before + carry_ref[...] + offs
        lane_f = lane.astype(F32)
        d1 = jnp.sum(jnp.where(lane_f == r[:, RT_I1:RT_I1 + 1], slot, 0.0), axis=-1, keepdims=True)
        d2 = jnp.sum(jnp.where(lane_f == r[:, RT_I2:RT_I2 + 1], slot, 0.0), axis=-1, keepdims=True)
        dest_ref[...] = jnp.where(lane == 0, d1, jnp.where(lane == 1, d2, 0.0)).astype(jnp.int32)
        cnt_ref[...] = jnp.broadcast_to(tot, (8, LANE))

    carry_ref[...] = carry_ref[...] + colsum


def _moe_rank(route, bm):
    tt = route.shape[0]
    return pl.pallas_call(
        _moe_rank_kernel,
        grid=(2, tt // bm),
        in_specs=[pl.BlockSpec((bm, LANE), lambda ph, i: (i, 0))],
        out_specs=[pl.BlockSpec((bm, LANE), lambda ph, i: (i * ph, 0)),
                   pl.BlockSpec((8, LANE), lambda ph, i: (0, 0))],
        out_shape=[jax.ShapeDtypeStruct((tt, LANE), jnp.int32), jax.ShapeDtypeStruct((8, LANE), F32)],
        scratch_shapes=[pltpu.VMEM((1, LANE), F32), pltpu.VMEM((1, LANE), F32)],
        compiler_params=_cparams(('arbitrary', 'arbitrary')), name='moe_rank',
    )(route)


def _row_copy(src, dst, sem):
    return pltpu.make_async_copy(src, dst, sem)


def _moe_scatter_kernel(dest_ref, h_ref, init_hbm, xs_hbm, sem, *, bm):
    del init_hbm
    base = pl.program_id(0) * bm

    def issue(r, carry):
        t = base + r
        for kk in range(TOP_K):
            _row_copy(h_ref.at[pl.ds(r, 1)], xs_hbm.at[pl.ds(dest_ref[TOP_K * t + kk], 1)], sem).start(priority=kk)
        return carry
    lax.fori_loop(0, bm, issue, 0)

    def drain(r, carry):
        for kk in range(TOP_K):
            _row_copy(h_ref.at[pl.ds(0, 1)], xs_hbm.at[pl.ds(0, 1)], sem).wait()
        return carry
    lax.fori_loop(0, bm, drain, 0)


def _moe_scatter(dest_flat, h32, ns, bm):
    tt, w = h32.shape
    return pl.pallas_call(
        functools.partial(_moe_scatter_kernel, bm=bm),
        grid_spec=pltpu.PrefetchScalarGridSpec(
            num_scalar_prefetch=1, grid=(tt // bm,),
            in_specs=[pl.BlockSpec((bm, w), lambda i, d: (i, 0)), pl.BlockSpec(memory_space=pl.ANY)],
            out_specs=pl.BlockSpec(memory_space=pl.ANY),
            scratch_shapes=[pltpu.SemaphoreType.DMA(())]),
        out_shape=jax.ShapeDtypeStruct((ns, w), h32.dtype),
        input_output_aliases={2: 0},
        compiler_params=_cparams(('arbitrary',)), name='moe_scatter',
    )(dest_flat, h32, jnp.zeros((ns, w), h32.dtype))


def _moe_up_kernel(te_ref, nv_ref, a_ref, w1_ref, w3_ref, o_ref, w1b, w3b):
    m = pl.program_id(1)

    @pl.when(jnp.logical_or(m == 0, te_ref[m] != te_ref[jnp.maximum(m - 1, 0)]))
    def _():
        w1b[...] = w1_ref[...].astype(BF16)
        w3b[...] = w3_ref[...].astype(BF16)

    @pl.when(m < nv_ref[0])
    def _():
        a = _unpack_halves(a_ref[...])
        h1 = jnp.dot(a, w1b[...], preferred_element_type=F32)
        h3 = jnp.dot(a, w3b[...], preferred_element_type=F32)
        o_ref[...] = (h1 * _sigmoid(h1) * h3).astype(BF16)

    @pl.when(m >= nv_ref[0])
    def _():
        o_ref[...] = jnp.zeros_like(o_ref)


def _moe_down_kernel(te_ref, nv_ref, a_ref, w_ref, o_ref, wb_ref):
    m = pl.program_id(1)

    @pl.when(jnp.logical_or(m == 0, te_ref[m] != te_ref[jnp.maximum(m - 1, 0)]))
    def _():
        wb_ref[...] = w_ref[...].astype(BF16)

    @pl.when(m < nv_ref[0])
    def _():
        o_ref[...] = jnp.dot(a_ref[...], wb_ref[...], preferred_element_type=F32)

    @pl.when(m >= nv_ref[0])
    def _():
        o_ref[...] = jnp.zeros_like(o_ref)


def _moe_grouped_ffn(tile_expert, n_valid, xs, w1, w3, w2, lyr):
    ns = xs.shape[0]
    n_tiles = ns // MOE_BM
    bn = 1024
    ws = pl.BlockSpec((None, None, D_MODEL, bn), lambda j, m, te, nv: (lyr, te[m], 0, j))
    u = pl.pallas_call(
        _moe_up_kernel,
        grid_spec=pltpu.PrefetchScalarGridSpec(
            num_scalar_prefetch=2, grid=(FFN_DIM // bn, n_tiles),
            in_specs=[pl.BlockSpec((MOE_BM, D_MODEL // 2), lambda j, m, te, nv: (m, 0)), ws, ws],
            out_specs=pl.BlockSpec((MOE_BM, bn), lambda j, m, te, nv: (m, j)),
            scratch_shapes=[pltpu.VMEM((D_MODEL, bn), BF16), pltpu.VMEM((D_MODEL, bn), BF16)]),
        out_shape=jax.ShapeDtypeStruct((ns, FFN_DIM), BF16),
        compiler_params=_cparams(('arbitrary', 'arbitrary')), name='moe_up',
    )(tile_expert, n_valid, xs, w1, w3)
    bn = 512
    return pl.pallas_call(
        _moe_down_kernel,
        grid_spec=pltpu.PrefetchScalarGridSpec(
            num_scalar_prefetch=2, grid=(D_MODEL // bn, n_tiles),
            in_specs=[pl.BlockSpec((MOE_BM, FFN_DIM), lambda j, m, te, nv: (m, 0)),
                      pl.BlockSpec((None, None, FFN_DIM, bn), lambda j, m, te, nv: (lyr, te[m], 0, j))],
            out_specs=pl.BlockSpec((MOE_BM, bn), lambda j, m, te, nv: (m, j)),
            scratch_shapes=[pltpu.VMEM((FFN_DIM, bn), BF16)]),
        out_shape=jax.ShapeDtypeStruct((ns, D_MODEL), F32),
        compiler_params=_cparams(('arbitrary', 'arbitrary')), name='moe_down',
    )(tile_expert, n_valid, u, w2)


def _moe_combine_kernel(dest_ref, ys_hbm, r_ref, x_ref, g_ref, m_ref, o_ref, buf, sem, *, bm, kg):
    base = pl.program_id(0) * bm

    def issue(r, carry):
        t = base + r
        for kk in range(TOP_K):
            _row_copy(ys_hbm.at[pl.ds(dest_ref[TOP_K * t + kk], 1)], buf.at[kk, pl.ds(r, 1)], sem).start(priority=kk)
        return carry
    lax.fori_loop(0, bm, issue, 0)

    def drain(r, carry):
        for kk in range(TOP_K):
            _row_copy(ys_hbm.at[pl.ds(0, 1)], buf.at[0, pl.ds(0, 1)], sem).wait()
        return carry
    lax.fori_loop(0, bm, drain, 0)

    r = r_ref[...]
    f = r[:, RT_G1:RT_G1 + 1] * buf[0] + r[:, RT_G2:RT_G2 + 1] * buf[1]
    n = f * lax.rsqrt(jnp.mean(f * f, axis=-1, keepdims=True) + EPS) * g_ref[...]
    o_ref[...] = x_ref[...] + m_ref[0][kg:kg + 1] * n


def _moe_combine(dest_flat, ys, route, x, g, mods, kg, t, bm, n_out):
    nlt = t // bm
    xs = pl.BlockSpec((bm, D_MODEL), lambda i, d: (i, 0))
    return pl.pallas_call(
        functools.partial(_moe_combine_kernel, bm=bm, kg=kg),
        grid_spec=pltpu.PrefetchScalarGridSpec(
            num_scalar_prefetch=1, grid=(n_out // bm,),
            in_specs=[pl.BlockSpec(memory_space=pl.ANY),
                      pl.BlockSpec((bm, LANE), lambda i, d: (i, 0)), xs,
                      pl.BlockSpec((1, D_MODEL), lambda i, d: (0, 0)),
                      pl.BlockSpec((1, 6, D_MODEL), lambda i, d: (jnp.where(i >= nlt, 1, 0), 0, 0))],
            out_specs=xs,
            scratch_shapes=[pltpu.VMEM((TOP_K, bm, D_MODEL), F32), pltpu.SemaphoreType.DMA(())]),
        out_shape=jax.ShapeDtypeStruct((n_out, D_MODEL), F32),
        compiler_params=_cparams(('arbitrary',)), name='moe_combine',
    )(dest_flat, ys, route, x, g.reshape(1, D_MODEL), mods)


def _moe_layer(x, h32, route, w1, w3, w2, lyr, g, mods, kg, t, n_out):
    tt = x.shape[0]
    bm = _row_tile(t, tt - t)
    dest, cnt = _moe_rank(route, bm)
    n_tiles = -(-TOP_K * tt // MOE_BM) + N_EXPERTS
    ns = n_tiles * MOE_BM
    cum = jnp.cumsum((cnt[0, 0:N_EXPERTS].astype(jnp.int32) + (MOE_BM - 1)) // MOE_BM)
    n_valid = cum[-1]
    tile = jnp.arange(n_tiles, dtype=jnp.int32)
    tile_expert = jnp.searchsorted(cum, jnp.minimum(tile, n_valid - 1), side='right').astype(jnp.int32)
    dest_flat = dest[:, 0:TOP_K].reshape(-1)
    xs32 = _moe_scatter(dest_flat, h32, ns, bm)
    ys = _moe_grouped_ffn(tile_expert, n_valid.reshape(1), xs32, w1, w3, w2, lyr)
    return _moe_combine(dest_flat, ys, route, x, g, mods, kg, t, bm, n_out)


def _rwkv_kernel(d, cur_ref, prv_ref, nxt_ref, sm_ref, mu_ref, w0_ref, wup_ref, a0_ref, aup_ref,
                 kk_ref, ka_ref, rk_ref, e_ref, tri_ref, y_ref, bonus_ref, s_ref, *, nl, nc):
    n = pl.program_id(0)
    ch = _chunk_of(d, n, nl, nc)

    @pl.when(n == 0)
    def _():
        s_ref[...] = jnp.zeros_like(s_ref)

    x = cur_ref[...]
    first = jnp.logical_or(ch == 0, ch == nl)
    last = jnp.logical_or(ch == nl - 1, ch == nl + nc - 1)
    p_row = jnp.where(first, 0.0, prv_ref[7:8, :])
    n_row = jnp.where(last, 0.0, nxt_ref[0:1, :])
    row = lax.broadcasted_iota(jnp.int32, (CHUNK, 1), 0)
    prev = jnp.where(row == 0, p_row, pltpu.roll(x, 1, axis=0))
    nxt = jnp.where(row == CHUNK - 1, n_row, pltpu.roll(x, CHUNK - 1, axis=0))
    z = x + mu_ref[0:1, :] * (prev - x) + mu_ref[1:2, :] * (nxt - x)
    r, k, v = z[:, 0:MIX_W], z[:, MIX_W:2 * MIX_W], z[:, 2 * MIX_W:3 * MIX_W]

    sm = sm_ref[...]
    e_bf = e_ref[...]
    wd = jnp.tanh(sm[:, 0:128])
    w_log = -_softplus(-(w0_ref[...] + _mm3(wd, wup_ref[...]))) - 0.5
    logw = -jnp.exp(w_log)
    a = _sigmoid(a0_ref[...] + _mm3(sm[:, 128:256], aup_ref[...]))
    kkr = k * kk_ref[...]
    kk = kkr * lax.rsqrt(jnp.maximum(_mm_xw(kkr * kkr, e_bf), 1e-12))
    k_dir = k * (1.0 + (a - 1.0) * ka_ref[...])
    bonus_ref[...] = _mm_xw(r * k_dir * rk_ref[...], e_bf) * v

    m_strict = tri_ref[0]
    m_incl = tri_ref[1]
    eye = m_incl - m_strict
    b_inc = _mm_wx(m_incl[0:CHUNK, 0:CHUNK].astype(BF16), logw)
    b_exc = b_inc - logw
    b_last = jnp.sum(logw, axis=0, keepdims=True)
    beta = kk * a
    ea = -kk * jnp.exp(b_exc)
    er = r * jnp.exp(b_inc)
    ninv = jnp.exp(-b_inc)
    eb = beta * ninv
    ek = k_dir * ninv
    eend = jnp.exp(b_last - b_inc)
    hb = beta * eend
    hk = k_dir * eend
    gam = jnp.exp(b_last)

    gw = RW_GROUP * RW_HEAD
    lane_head = lax.broadcasted_iota(jnp.int32, (1, gw), 1) >> 6

    def spread(xg):
        return jnp.concatenate([jnp.where(lane_head == h, xg, 0.0) for h in range(RW_GROUP)], axis=0)

    groups = range(RW_HEADS // RW_GROUP)
    pre = []
    yield None
    for g in groups:
        sl = slice(g * gw, (g + 1) * gw)
        la, lr = spread(ea[:, sl]), spread(er[:, sl])
        rb, rk = spread(eb[:, sl]), spread(ek[:, sl])
        vb = spread(v[:, sl]).astype(BF16)
        amat = _mm_nt(jnp.concatenate([la, lr], axis=0), jnp.concatenate([rb, rk], axis=0))
        m_ab = amat[0:gw, 0:gw] * m_strict
        rhs = jnp.concatenate([la, _mm(amat[0:gw, gw:] * m_strict, vb)], axis=1)
        n_rb = (amat[gw:, 0:gw] * m_incl).astype(BF16)
        n_rk = (amat[gw:, gw:] * m_incl).astype(BF16)
        pre.append((lr, vb, m_ab, rhs, n_rb, n_rk))
        yield None

    def finish(g, tinv):
        sl = slice(g * gw, (g + 1) * gw)
        lr, vb, _, rhs, n_rb, n_rk = pre[g]
        xs_bf = _mm(tinv, rhs).astype(BF16)
        yield
        qy = _mm(n_rb, xs_bf)
        q_hat = lr + qy[:, 0:gw]
        y_loc = qy[:, gw:] + _mm(n_rk, vb)
        yield
        hb_bd = spread(hb[:, sl]).astype(BF16)
        gbt = _mm_tn(xs_bf, hb_bd)
        g_bot = gbt[gw:] + _mm_tn(vb, spread(hk[:, sl]))
        yield
        s0 = s_ref[g]
        y_bd = _mm_nt(q_hat, s0) + y_loc
        y_ref[:, sl] = (y_bd[0:CHUNK] + y_bd[CHUNK:2 * CHUNK]
                        + y_bd[2 * CHUNK:3 * CHUNK] + y_bd[3 * CHUNK:4 * CHUNK])
        s_ref[g] = s0 * gam[:, sl] + _mm(s0, gbt[0:gw]) + g_bot

    tinvs = yield [p_[2] for p_ in pre], eye
    chains = [finish(g, tinvs[g]) for g in groups]
    while chains:
        yield None
        chains = [c for c in chains if next(c, _DONE) is not _DONE]


def _unit_lower_inverses(ms, eye):
    tinvs = [eye + m for m in ms]
    for _ in range(5):
        ms = [_mm(m, m) for m in ms]
        tinvs = [tinv + _mm(m, tinv) for m, tinv in zip(ms, tinvs)]
    return tinvs


def _rwkv_order_masks():
    i = jnp.arange(RW_GROUP * RW_HEAD)
    same = (i[:, None] // CHUNK) == (i[None, :] // CHUNK)
    diff = (i[:, None] % CHUNK) - (i[None, :] % CHUNK)
    per_dir = [jnp.stack([same & (sg * diff > 0), same & (sg * diff >= 0)]) for sg in (1, -1)]
    return jnp.stack(per_dir).astype(F32)


def _gla_kernel(d, qk_ref, v_ref, ad_ref, cs_ref, aup_ref, ab_ref, o_ref, s_ref):
    @pl.when(pl.program_id(0) == 0)
    def _():
        s_ref[...] = jnp.zeros_like(s_ref)

    qk = qk_ref[...]
    v = v_ref[...]
    w_qk = 2 * GLA_HEADS * GLA_DK
    lane = lax.broadcasted_iota(jnp.int32, (CHUNK, w_qk), 1)
    partner = jnp.where((lane & 1) == 0, pltpu.roll(qk, w_qk - 1, axis=1), pltpu.roll(qk, 1, axis=1))
    qk = qk * cs_ref[:, 0:w_qk] + partner * cs_ref[:, w_qk:]
    hk = GLA_HEADS * GLA_DK
    q = qk[:, 0:hk] * (GLA_DK ** -0.5)
    k = qk[:, hk:]
    g = -_softplus(-(_mm3(ad_ref[...], aup_ref[...]) + ab_ref[...])) / GLA_TAU
    _, incl = _order_masks(d)
    b = _mm_wx(jnp.where(incl, 1.0, 0.0).astype(BF16), g)
    b_last = jnp.sum(g, axis=0, keepdims=True)
    q_e = q * jnp.exp(b)
    k_e = k * jnp.exp(-b)
    k_end = k * jnp.exp(b_last - b)
    dec = jnp.exp(b_last)
    for h in range(GLA_HEADS):
        yield None
        sk = slice(h * GLA_DK, (h + 1) * GLA_DK)
        sv = slice(h * GLA_DV, (h + 1) * GLA_DV)
        att = jnp.where(incl, _mm_nt(q_e[:, sk], k_e[:, sk]), 0.0)
        yield None
        st = s_ref[h]
        o_ref[:, sv] = _mm(att, v[:, sv]) + _mm_nt(q_e[:, sk], st)
        s_ref[h] = st * dec[:, sk] + _mm_tn(v[:, sv], k_end[:, sk])


_DONE = object()
N_SCAN_SHARED = 5
N_SCAN_DIR = 11
SCAN_COLS = 3 * MIX_W + 3 * LANE + 2 * GLA_HEADS * GLA_DK + 2 * MIX_W + LANE


def _scan_kernel(*refs, nl, nc):
    mu, kk, ka, rk, e64 = refs[0:N_SCAN_SHARED]
    n_in = N_SCAN_SHARED + 2 * N_SCAN_DIR
    c0 = 3 * MIX_W
    c1 = c0 + 3 * LANE
    c2 = c1 + 2 * GLA_HEADS * GLA_DK
    rws, glas = [], []
    for d in range(2):
        pm, prv, nxt, cs, w0, wup, a0, aup, tri, g_aup, g_ab = refs[N_SCAN_SHARED + d * N_SCAN_DIR:
                                                                   N_SCAN_SHARED + (d + 1) * N_SCAN_DIR]
        y, bonus, o = refs[n_in + 3 * d:n_in + 3 * d + 3]
        s_rw, s_gla = refs[n_in + 6 + 2 * d:n_in + 8 + 2 * d]
        rws.append(_rwkv_kernel(d, pm.at[:, 0:c0], prv, nxt, pm.at[:, c0:c1], mu, w0, wup, a0, aup, kk, ka, rk,
                                e64, tri, y, bonus, s_rw, nl=nl, nc=nc))
        glas.append(_gla_kernel(d, pm.at[:, c1:c2], pm.at[:, c2:c2 + MIX_W], pm.at[:, SCAN_COLS - LANE:SCAN_COLS],
                                cs, g_aup, g_ab, o, s_gla))
    asked = [None, None]
    while None in asked:
        for d in range(2):
            if asked[d] is None:
                asked[d] = next(rws[d])
        glas = [c for c in glas if next(c, _DONE) is not _DONE]
    n_grp = len(asked[0][0])
    tinvs = _unit_lower_inverses(asked[0][0] + asked[1][0], asked[0][1])
    for d in range(2):
        rws[d].send(tinvs[d * n_grp:(d + 1) * n_grp])
    chains = rws + glas
    while chains:
        chains = [c for c in chains if next(c, _DONE) is not _DONE]


def _scans(p, t, lc, shared, per_dir):
    tt = t + lc
    gw = RW_GROUP * RW_HEAD
    nl, nc = t // CHUNK, lc // CHUNK
    last8 = tt // 8 - 1
    full = lambda a: pl.BlockSpec(a.shape, lambda n: (0,) * a.ndim)
    in_specs = [full(a) for a in shared]
    args = list(shared)
    out_specs = []
    for d in range(2):
        ch = functools.partial(_chunk_of, d, nl=nl, nc=nc)
        rope, *params = per_dir[d]
        in_specs += [
            pl.BlockSpec((CHUNK, SCAN_COLS), lambda n, ch=ch: (ch(n), 0)),
            pl.BlockSpec((8, 3 * MIX_W), lambda n, ch=ch: (jnp.maximum(ch(n) * 8 - 1, 0), 0)),
            pl.BlockSpec((8, 3 * MIX_W), lambda n, ch=ch: (jnp.minimum(ch(n) * 8 + 8, last8), 0)),
            pl.BlockSpec((CHUNK, rope.shape[1]), lambda n, ch=ch: (ch(n), 0)),
        ] + [full(a) for a in params]
        args += [p, p, p, rope] + params
        out_specs += [pl.BlockSpec((CHUNK, MIX_W), lambda n, ch=ch: (ch(n), 0))] * 3
    assert len(in_specs) == N_SCAN_SHARED + 2 * N_SCAN_DIR
    return pl.pallas_call(
        functools.partial(_scan_kernel, nl=nl, nc=nc),
        grid=(nl + nc,),
        in_specs=in_specs, out_specs=out_specs,
        out_shape=[jax.ShapeDtypeStruct((tt, MIX_W), F32)] * 6,
        scratch_shapes=[pltpu.VMEM((RW_HEADS // RW_GROUP, gw, gw), F32),
                        pltpu.VMEM((GLA_HEADS, GLA_DV, GLA_DK), F32)] * 2,
        compiler_params=_cparams(('arbitrary',)), name='scans',
    )(*args)


def _head_norm(y, e_bf, width, eps):
    mu = _mm_xw(y, e_bf) * (1.0 / width)
    dl = y - mu
    var = _mm_xw(dl * dl, e_bf) * (1.0 / width)
    return dl * lax.rsqrt(var + eps)


def _mix_finish_kernel(y0_ref, y1_ref, b0_ref, b1_ref, sm_ref, gup_ref, lng_ref, lnb_ref, e64_ref,
                       o0_ref, o1_ref, gr0_ref, gr1_ref, gr2_ref, gr3_ref, gng_ref, e128_ref, a_ref, b_ref):
    yn = _head_norm(y0_ref[...] + y1_ref[...], e64_ref[...], RW_HEAD, RW_GN_EPS)
    yn = yn * lng_ref[...] + lnb_ref[...] + b0_ref[...] + b1_ref[...]
    gate = _mm(_sigmoid(sm_ref[:, 256:384]), gup_ref[...])
    a_ref[...] = (yn * gate).astype(BF16)
    on = _head_norm(o0_ref[...] + o1_ref[...], e128_ref[...], GLA_DV, GN_EPS) * gng_ref[...]
    gr = jnp.concatenate([gr0_ref[...], gr1_ref[...], gr2_ref[...], gr3_ref[...]], axis=1)
    b_ref[...] = (on * (gr * _sigmoid(gr))).astype(BF16)


def _mix_finish(p, rw_y, rw_bonus, gla_o, g_up, ln_g, ln_b, gn_g, e64, e128, t):
    tt = p.shape[0]
    bm = _row_tile(t, tt - t)
    vec = pl.BlockSpec((1, MIX_W), lambda i: (0, 0))
    mat = pl.BlockSpec((MIX_W, MIX_W), lambda i: (0, 0))
    out = pl.BlockSpec((bm, MIX_W), lambda i: (i, 0))
    return pl.pallas_call(
        _mix_finish_kernel,
        grid=(tt // bm,),
        in_specs=([out, out, out, out,
                   pl.BlockSpec((bm, 3 * LANE), lambda i: (i, CB384_RW_SMALL)),
                   pl.BlockSpec((RW_GATE_RANK, MIX_W), lambda i: (0, 0)), vec, vec, mat,
                   out, out]
                  + [pl.BlockSpec((bm, LANE), functools.partial(lambda i, c: (i, c), c=CB128_GLA_R + q))
                     for q in range(4)]
                  + [vec, mat]),
        out_specs=[out, out],
        out_shape=[jax.ShapeDtypeStruct((tt, MIX_W), BF16)] * 2,
        compiler_params=_cparams(('parallel',)), name='mix_finish',
    )(*rw_y, *rw_bonus, p, g_up, ln_g, ln_b, e64, *gla_o, p, p, p, p, gn_g, e128)


def _sgu_kernel(u_ref, v_ref, lng_ref, lnb_ref, ws_ref, bs_ref, o_ref):
    u = _gelu(u_ref[...])
    v = _gelu(v_ref[...])
    mu = jnp.mean(v, axis=-1, keepdims=True)
    dl = v - mu
    var = jnp.mean(dl * dl, axis=-1, keepdims=True)
    vn = (dl * lax.rsqrt(var + GN_EPS) * lng_ref[...] + lnb_ref[...]).astype(BF16)
    lane = lax.broadcasted_iota(jnp.int32, (1, MIX_W), 1)
    s = bs_ref[...]
    for g in range(SGU_GROUPS):
        s = s + jnp.where((lane >> 6) == g, jnp.dot(ws_ref[g].astype(BF16), vn, preferred_element_type=F32), 0.0)
    o_ref[...] = (u * s).astype(BF16)


def _sgu(p, ln_g, ln_b, w_s, b_full):
    tt = p.shape[0]
    vec = pl.BlockSpec((1, MIX_W), lambda i: (0, 0))
    return pl.pallas_call(
        _sgu_kernel,
        grid=(tt // SGU_CHUNK,),
        in_specs=[pl.BlockSpec((SGU_CHUNK, MIX_W), lambda i: (i, CB_SGU_U)),
                  pl.BlockSpec((SGU_CHUNK, MIX_W), lambda i: (i, CB_SGU_V)), vec, vec,
                  pl.BlockSpec((SGU_GROUPS, SGU_CHUNK, SGU_CHUNK), lambda i: (0, 0, 0)),
                  pl.BlockSpec((SGU_CHUNK, MIX_W), lambda i: (0, 0))],
        out_specs=pl.BlockSpec((SGU_CHUNK, MIX_W), lambda i: (i, 0)),
        out_shape=jax.ShapeDtypeStruct((tt, MIX_W), BF16),
        compiler_params=_cparams(('parallel',)), name='sgu',
    )(p, p, ln_g, ln_b, w_s, b_full)


def _na_bias_kernel(rpb_ref, o_ref, *, wh):
    h = pl.program_id(0)
    n_dc = 2 * NA_WIN_W - 1
    n_dr = 2 * NA_WIN_H - 1
    shape = (GRID_W, 2 * GRID_W)
    c = lax.broadcasted_iota(jnp.int32, shape, 0)
    lane = lax.broadcasted_iota(jnp.int32, shape, 1)
    x = lane & (GRID_W - 1)
    dc = jnp.clip(x - c + (NA_WIN_W - 1), 0, 2 * NA_WIN_W - 2)
    key = (lane >> 6) * n_dc + dc
    cs = jnp.clip(c - NA_WIN_W // 2, 0, GRID_W - NA_WIN_W)
    ok = jnp.logical_and(x >= cs, x < cs + NA_WIN_W)
    pairs = []
    for dr in range(n_dr - 1):
        base = h * (n_dr * n_dc) + dr * n_dc
        tile = lax.fori_loop(0, 2 * n_dc, lambda j, acc: jnp.where(key == j, rpb_ref[base + j], acc),
                             jnp.zeros(shape, F32))
        pairs.append(jnp.where(ok, tile, NEG_INF))
    for dr0 in range(n_dr - wh + 1):
        for jj in range(wh // 2):
            o_ref[dr0, 0, :, jj * 2 * GRID_W:(jj + 1) * 2 * GRID_W] = pairs[dr0 + 2 * jj]


def _na_bias(rpb, wh):
    n_dr0 = 2 * NA_WIN_H - wh
    return pl.pallas_call(
        functools.partial(_na_bias_kernel, wh=wh),
        grid=(NA_HEADS,),
        in_specs=[pl.BlockSpec(memory_space=pltpu.SMEM)],
        out_specs=pl.BlockSpec((n_dr0, 1, GRID_W, wh * GRID_W), lambda h: (0, h, 0, 0)),
        out_shape=jax.ShapeDtypeStruct((n_dr0, NA_HEADS, GRID_W, wh * GRID_W), F32),
        compiler_params=_cparams(('parallel',)), name='na_bias',
    )(rpb.reshape(-1))


def _na_kernel(*refs, wh):
    q_ref = refs[0]
    k_refs = refs[1:1 + wh]
    v_refs = refs[1 + wh:1 + 2 * wh]
    kc_ref, vc_ref, bias_ref, o_ref = refs[1 + 2 * wh:]
    q = q_ref[...] * (NA_HEAD ** -0.5)
    kw = jnp.concatenate([r[...].astype(BF16) for r in k_refs], axis=0)
    vw = jnp.concatenate([r[...].astype(BF16) for r in v_refs], axis=0)
    kc = kc_ref[...].astype(BF16)
    vc = vc_ref[...].astype(BF16)
    lane = lax.broadcasted_iota(jnp.int32, (1, LANE), 1)
    heads = [(pr, s) for pr in range(NA_HEADS // 2) for s in range(2)]
    slab = lambda x, pr: x[:, pr * LANE:(pr + 1) * LANE]
    scores = []
    for pr, s in heads:
        qh = jnp.where((lane >> 6) == s, slab(q, pr), 0.0).astype(BF16)
        scores.append((_mm_nt(qh, slab(kw, pr)) + bias_ref[0, 2 * pr + s], _mm_nt(qh, slab(kc, pr))))
    probs = []
    for sw, sc in scores:
        m = jnp.maximum(jnp.max(sw, axis=-1, keepdims=True), jnp.max(sc, axis=-1, keepdims=True))
        ew = jnp.exp(sw - m)
        ec = jnp.exp(sc - m)
        den = jnp.sum(ew, axis=-1, keepdims=True) + jnp.sum(ec, axis=-1, keepdims=True)
        probs.append((ew, ec, den))
    outs = [jnp.zeros((GRID_W, LANE), F32)] * (NA_HEADS // 2)
    for (pr, s), (ew, ec, den) in zip(heads, probs):
        oh = (_mm(ew, slab(vw, pr)) + _mm(ec, slab(vc, pr))) / den
        outs[pr] = jnp.where((lane >> 6) == s, oh, outs[pr])
    o_ref[...] = jnp.concatenate(outs, axis=1).astype(BF16)


def _na_ctx_kernel(q_ref, k_ref, v_ref, o_ref):
    q = q_ref[...] * (NA_HEAD ** -0.5)
    kc = k_ref[...].astype(BF16)
    vc = v_ref[...].astype(BF16)
    lane = lax.broadcasted_iota(jnp.int32, (1, MIX_W), 1)
    o = jnp.zeros(q.shape, F32)
    for h in range(NA_HEADS):
        hm = (lane >> 6) == h
        s = _mm_nt(jnp.where(hm, q, 0.0), kc)
        e = jnp.exp(s - jnp.max(s, axis=-1, keepdims=True))
        o = jnp.where(hm, _mm(e, vc) / jnp.sum(e, axis=-1, keepdims=True), o)
    o_ref[...] = o.astype(BF16)


def _na(p, bias, t, lc):
    rows = t // GRID_W
    wh = min(NA_WIN_H, rows)
    rs = lambda r: jnp.clip(r - wh // 2, 0, rows - wh)
    ctx_blk = t // lc
    kv = lambda cb: [pl.BlockSpec((GRID_W, MIX_W), functools.partial(lambda r, w, cb: (rs(r) + w, cb), w=w, cb=cb))
                     for w in range(wh)]
    lat = pl.pallas_call(
        functools.partial(_na_kernel, wh=wh),
        grid=(rows,),
        in_specs=([pl.BlockSpec((GRID_W, MIX_W), lambda r: (r, CB_NA_Q))] + kv(CB_NA_K) + kv(CB_NA_V)
                  + [pl.BlockSpec((lc, MIX_W), lambda r: (ctx_blk, CB_NA_K)),
                     pl.BlockSpec((lc, MIX_W), lambda r: (ctx_blk, CB_NA_V)),
                     pl.BlockSpec((1, NA_HEADS, GRID_W, wh * GRID_W),
                                  lambda r: (rs(r) - r + (NA_WIN_H - 1), 0, 0, 0))]),
        out_specs=pl.BlockSpec((GRID_W, MIX_W), lambda r: (r, 0)),
        out_shape=jax.ShapeDtypeStruct((t, MIX_W), BF16),
        compiler_params=_cparams(('parallel',)), name='na_latent',
    )(*([p] * (1 + 2 * wh + 2)), bias)
    cblk = lambda cb: pl.BlockSpec((lc, MIX_W), lambda i: (ctx_blk, cb))
    ctx = pl.pallas_call(
        _na_ctx_kernel,
        grid=(1,),
        in_specs=[cblk(CB_NA_Q), cblk(CB_NA_K), cblk(CB_NA_V)],
        out_specs=pl.BlockSpec((lc, MIX_W), lambda i: (0, 0)),
        out_shape=jax.ShapeDtypeStruct((lc, MIX_W), BF16),
        compiler_params=_cparams(('arbitrary',)), name='na_ctx',
    )(p, p, p)
    return jnp.concatenate([lat, ctx], axis=0)


def _merge_kernel(a0, a1, a2, a3, g0, g1, g2, g3, w_ref, o_ref, wb_ref):
    @pl.when(pl.program_id(1) == 0)
    def _():
        wb_ref[...] = w_ref[...].astype(BF16)
    acc = None
    for n, (a_ref, g_ref) in enumerate(((a0, g0), (a1, g1), (a2, g2), (a3, g3))):
        zn = jnp.dot(a_ref[...], wb_ref[n], preferred_element_type=F32) * _sigmoid(g_ref[...])
        acc = zn if acc is None else acc + zn
    o_ref[...] = acc.astype(BF16)


def _merge(ys, p, w_br, lyr):
    tt = p.shape[0]
    bn = 512
    bm = _pick(tt, (768, 384, 128))
    a_s = pl.BlockSpec((bm, MIX_W), lambda j, i: (i, 0))
    gs = [pl.BlockSpec((bm, bn), functools.partial(lambda j, i, n: (i, (GATE_OFF + n * D_MODEL) // bn + j), n=n))
          for n in range(N_BRANCH)]
    return pl.pallas_call(
        _merge_kernel,
        grid=(D_MODEL // bn, tt // bm),
        in_specs=[a_s] * 4 + gs + [pl.BlockSpec((None, N_BRANCH, MIX_W, bn), lambda j, i: (lyr, 0, 0, j))],
        out_specs=pl.BlockSpec((bm, bn), lambda j, i: (i, j)),
        out_shape=jax.ShapeDtypeStruct((tt, D_MODEL), BF16),
        scratch_shapes=[pltpu.VMEM((N_BRANCH, MIX_W, bn), BF16)],
        compiler_params=_cparams(('arbitrary', 'arbitrary')), name='merge',
    )(*ys, p, p, p, p, w_br)


def _block_diag_ones(width):
    i = jnp.arange(MIX_W) // width
    return (i[:, None] == i[None, :]).astype(BF16)


def _rope_tables(t, lc):
    tok = jnp.arange(t)
    pos = jnp.stack([tok // GRID_W, tok % GRID_W], axis=-1).astype(F32)
    nf = GLA_DK // 4
    inv = ROPE_BASE ** (-jnp.arange(nf, dtype=F32) / nf)
    ang = pos[:, :, None] * inv
    cos = jnp.repeat(jnp.cos(ang), 2, axis=-1).reshape(t, GLA_DK)
    sin = jnp.sin(ang)
    sin = jnp.stack([-sin, sin], axis=-1).reshape(t, GLA_DK)
    reps = 2 * GLA_HEADS
    cos = jnp.concatenate([jnp.tile(cos, (1, reps)), jnp.ones((lc, reps * GLA_DK), F32)], axis=0)
    sin = jnp.concatenate([jnp.tile(sin, (1, reps)), jnp.zeros((lc, reps * GLA_DK), F32)], axis=0)
    return jnp.concatenate([cos, sin], axis=1)


def _pad_rank_rows(w_up, rank, rows):
    out = jnp.zeros((2, rows, w_up.shape[-1]), w_up.dtype)
    for d in range(2):
        out = out.at[d, d * rank:(d + 1) * rank].set(w_up[d])
    return out


def kernel(x, c, ctx, c_ctx, ada_w, ada_b, norm_g, w_in, rw_mu, rw_w0, rw_w_up, rw_a0, rw_a_up, rw_g_up, rw_k_k, rw_k_a, rw_r_k, rw_ln_g, rw_ln_b, gla_a_up, gla_a_b, gla_gn_g, sgu_ln_g, sgu_ln_b, sgu_w, sgu_b, na_rpb, w_br, w_o, ffn_w1, ffn_w3, ffn_w2, moe_router, moe_w1, moe_w3, moe_w2):
    assert x.shape[0] == 1 and x.shape[2] == D_MODEL
    t, lc = x.shape[1], ctx.shape[1]
    depth = ada_w.shape[0]
    assert t % max(lc, SGU_CHUNK) == 0 and lc % SGU_CHUNK == 0 and t % GRID_W == 0
    rows = t // GRID_W
    assert rows >= NA_WIN_H
    wh = NA_WIN_H
    xs = jnp.concatenate([x[0], ctx[0]], axis=0)
    cond8 = jnp.zeros((8, D_MODEL), F32).at[0].set(c[0]).at[1].set(c_ctx)
    e64 = _block_diag_ones(RW_HEAD)
    e128 = _block_diag_ones(GLA_DV)
    rope = _rope_tables(t, lc)
    tri = _rwkv_order_masks()
    row1 = lambda v: v.reshape(1, -1)
    assert w_in.shape[2] == N_IN

    mods_all = [_adaln(cond8, ada_w, ada_b, i)[0:2].reshape(2, 6, D_MODEL) for i in range(depth)]
    h = _norm_mod_call(xs, norm_g[0, 0], mods_all[0], 0, t)
    for i in range(depth):
        mods = mods_all[i]
        last = i == depth - 1
        p = _in_proj(h, w_in, i)
        mu2 = jnp.stack([rw_mu[i, :, 0].reshape(-1), rw_mu[i, :, 1].reshape(-1)])
        w_up = _pad_rank_rows(rw_w_up[i], RW_DECAY_RANK, LANE)
        a_up = _pad_rank_rows(rw_a_up[i], RW_ICLR_RANK, LANE)
        g_up = _pad_rank_rows(gla_a_up[i], GLA_GATE_RANK, LANE)
        per_dir = [[rope, row1(rw_w0[i, d]), w_up[d], row1(rw_a0[i, d]), a_up[d], tri[d], g_up[d],
                    row1(gla_a_b[i, d])] for d in range(2)]
        y0, b0, o0, y1, b1, o1 = _scans(p, t, lc, [mu2, row1(rw_k_k[i]), row1(rw_k_a[i]), row1(rw_r_k[i]), e64],
                                        per_dir)
        rw_y, rw_bonus, gla_o = (y0, y1), (b0, b1), (o0, o1)
        y_a, y_b = _mix_finish(p, rw_y, rw_bonus, gla_o, rw_g_up[i], row1(rw_ln_g[i]), row1(rw_ln_b[i]),
                               row1(gla_gn_g[i]), e64, e128, t)
        y_s = _sgu(p, row1(sgu_ln_g[i]), row1(sgu_ln_b[i]), sgu_w[i], jnp.repeat(sgu_b[i].T, 64, axis=1))
        y_d = _na(p, _na_bias(na_rpb[i], wh), t, lc)
        z = _merge((y_a, y_b, y_s, y_d), p, w_br, i)
        y = _matmul(z, w_o, i, 1024, (768, 384, 128), name='out_proj')
        j = i // 2
        if i % 2 == 0:
            xs, h = _resid_next_call(xs, y, norm_g[i, 1], mods, 2, t, norm_g[i, 2], mods, 3)
            f = _ffn_down(_ffn_up(h, ffn_w1, ffn_w3, j), ffn_w2, j)
            if last:
                xs = _resid_call(xs, f, norm_g[i, 3], mods, 5, t, t)
            else:
                xs, h = _resid_next_call(xs, f, norm_g[i, 3], mods, 5, t, norm_g[i + 1, 0], mods_all[i + 1], 0)
        else:
            xs, h32, route = _resid_next_call(xs, y, norm_g[i, 1], mods, 2, t, norm_g[i, 2], mods, 3,
                                              router=moe_router[j])
            xs = _moe_layer(xs, h32, route, moe_w1, moe_w3, moe_w2, j, norm_g[i, 3], mods, 5, t,
                            t if last else t + lc)
            if not last:
                h = _norm_mod_call(xs, norm_g[i + 1, 0], mods_all[i + 1], 0, t)
    return xs[None]
```
